```python
import jax, jax.numpy as jnp
from jax import lax
import numpy as np

D_MODEL = 2048
BATCH = 2
SEQ = 4096
DEPTH = 1

CHUNK = 64
Q_BLOCK = 128
EPS = 1e-6
MLA_HEADS = 8
MLA_Q_RANK = 512
MLA_KV_RANK = 512
MLA_NOPE = 128
MLA_ROPE = 64
MLA_V = 128
ROPE_THETA = 10000.0
MLA_WIDTH = MLA_HEADS * MLA_V
GLA_HEADS = 4
GLA_DK = 128
GLA_DV = 256
GLA_GATE_RANK = 16
GLA_GATE_NORMALIZER = 16.0
GLA_KEY = GLA_HEADS * GLA_DK
GLA_VAL = GLA_HEADS * GLA_DV
D_FF = 5632
N_MOD = 9
IN_SPLITS = (MLA_Q_RANK, MLA_KV_RANK, MLA_ROPE,
             GLA_KEY, GLA_KEY, GLA_VAL,
             GLA_GATE_RANK, GLA_VAL,
             D_MODEL, D_MODEL)
D_IN = sum(IN_SPLITS)

kernel_name = 'hybrid_mla_gla_macaron_adaln'


def _rms_norm(t, g):
    tf = t.astype(jnp.float32)
    y = tf * lax.rsqrt(jnp.mean(tf * tf, axis=-1, keepdims=True) + EPS)
    return (y * g.astype(jnp.float32)).astype(t.dtype)


def _split_cols(t, sizes):
    out, start = [], 0
    for s in sizes:
        out.append(t[..., start:start + s])
        start += s
    return out


def _rope(t, cos, sin):
    half = t.shape[-1] // 2
    tf = t.astype(jnp.float32)
    t1, t2 = tf[..., :half], tf[..., half:]
    return jnp.concatenate([t1 * cos - t2 * sin, t2 * cos + t1 * sin], axis=-1).astype(t.dtype)


def _swiglu(h, w1, w3, w2):
    return (jax.nn.silu(h @ w1) * (h @ w3)) @ w2


def _mla_branch(q_lat, kv_lat, k_rope_raw, cos, sin, g_q_lat, w_uq, g_qn, g_qr,
                g_kv_lat, w_ukv, g_kn, g_kr):
    B, S, _ = q_lat.shape
    H = MLA_HEADS
    cq = _rms_norm(q_lat, g_q_lat)
    q = (cq @ w_uq).reshape(B, S, H, MLA_NOPE + MLA_ROPE)
    q_nope = _rms_norm(q[..., :MLA_NOPE], g_qn)
    q_pe = _rope(_rms_norm(q[..., MLA_NOPE:], g_qr), cos[:, :, None, :], sin[:, :, None, :])
    ckv = _rms_norm(kv_lat, g_kv_lat)
    kv = (ckv @ w_ukv).reshape(B, S, H, MLA_NOPE + MLA_V)
    k_nope = _rms_norm(kv[..., :MLA_NOPE], g_kn)
    v = kv[..., MLA_NOPE:]
    k_pe = _rope(_rms_norm(k_rope_raw, g_kr), cos, sin)
    k_pe = jnp.broadcast_to(k_pe[:, :, None, :], (B, S, H, MLA_ROPE))
    qf = jnp.concatenate([q_nope, q_pe], axis=-1).transpose(0, 2, 1, 3)
    kf = jnp.concatenate([k_nope, k_pe], axis=-1).transpose(0, 2, 1, 3)
    vf = v.transpose(0, 2, 1, 3)
    scale = (MLA_NOPE + MLA_ROPE) ** -0.5
    n_blocks = S // Q_BLOCK
    q_blocks = qf.reshape(B, H, n_blocks, Q_BLOCK, -1).transpose(2, 0, 1, 3, 4)
    key_chunk = jnp.arange(S) // CHUNK

    def attend(args):
        qb, start = args
        s = jnp.einsum('bhqd,bhkd->bhqk', qb, kf).astype(jnp.float32) * scale
        q_chunk = (start + jnp.arange(Q_BLOCK)) // CHUNK
        mask = key_chunk[None, :] <= q_chunk[:, None]
        s = jnp.where(mask, s, -jnp.inf)
        p = jax.nn.softmax(s, axis=-1).astype(vf.dtype)
        return jnp.einsum('bhqk,bhkd->bhqd', p, vf)

    out = lax.map(attend, (q_blocks, jnp.arange(n_blocks) * Q_BLOCK))
    return out.transpose(1, 0, 3, 2, 4).reshape(B, S, MLA_WIDTH)


def _gla_branch(q, k, v, g_lr, g_out, w_gk_up, b_gk, g_gla):
    B, S, _ = q.shape
    H, N = GLA_HEADS, S // CHUNK
    f32 = jnp.float32
    log_a = jax.nn.log_sigmoid((g_lr @ w_gk_up + b_gk).astype(f32)) / GLA_GATE_NORMALIZER

    def heads(t, d):
        return t.astype(f32).reshape(B, N, CHUNK, H, d).transpose(0, 3, 1, 2, 4)

    qh = heads(q, GLA_DK) * (GLA_DK ** -0.5)
    kh = heads(k, GLA_DK)
    vh = heads(v, GLA_DV)
    b = jnp.cumsum(heads(log_a, GLA_DK), axis=3)
    q_dec = qh * jnp.exp(b)
    k_dec = kh * jnp.exp(-b)
    causal = jnp.tril(jnp.ones((CHUNK, CHUNK), dtype=bool))
    attn = jnp.where(causal, jnp.einsum('bhncd,bhnsd->bhncs', q_dec, k_dec), 0.0)
    o_intra = jnp.einsum('bhncs,bhnsv->bhncv', attn, vh)
    b_last = b[:, :, :, -1:, :]
    chunk_kv = jnp.einsum('bhncd,bhncv->bhndv', kh * jnp.exp(b_last - b), vh)
    decay = jnp.exp(b_last[:, :, :, 0, :])

    def step(state, inp):
        dec, kv = inp
        return state * dec[..., None] + kv, state

    init = jnp.zeros((B, H, GLA_DK, GLA_DV), f32)
    _, states = lax.scan(step, init, (decay.transpose(2, 0, 1, 3), chunk_kv.transpose(2, 0, 1, 3, 4)))
    states = states.transpose(1, 2, 0, 3, 4)
    o = o_intra + jnp.einsum('bhncd,bhndv->bhncv', q_dec, states)
    o = o.transpose(0, 2, 3, 1, 4).reshape(B, S, H, GLA_DV)
    o = _rms_norm(o, g_gla) * jax.nn.silu(g_out.astype(f32).reshape(B, S, H, GLA_DV))
    return o.reshape(B, S, GLA_VAL).astype(q.dtype)


def _dense(k, shape, fan_in, scale=1.0):
    return jax.random.normal(k, shape, jnp.float32) * (scale * fan_in ** -0.5)


def _gain(k, shape):
    return 1.0 + 0.05 * jax.random.normal(k, shape, jnp.float32)


def setup_inputs(seed: int = 0) -> dict:
    key = jax.random.key(seed)
    ks = iter(jax.random.split(key, 32))
    L = DEPTH
    x = jax.random.normal(next(ks), (BATCH, SEQ, D_MODEL), jnp.float32)
    c = jax.random.normal(next(ks), (BATCH, D_MODEL), jnp.float32)
    offsets = jax.random.randint(next(ks), (BATCH, 1), 0, 64, dtype=jnp.int32) * CHUNK
    positions = (offsets + jnp.arange(SEQ, dtype=jnp.int32)[None, :]).astype(jnp.int32)
    return {
        'x': x, 'c': c, 'positions': positions,
        'w_ada': _dense(next(ks), (L, D_MODEL, N_MOD * D_MODEL), D_MODEL, 0.5),
        'b_ada': 0.02 * jax.random.normal(next(ks), (L, N_MOD * D_MODEL), jnp.float32),
        'g_ffn1': _gain(next(ks), (L, D_MODEL)),
        'w1_a': _dense(next(ks), (L, D_MODEL, D_FF), D_MODEL),
        'w3_a': _dense(next(ks), (L, D_MODEL, D_FF), D_MODEL),
        'w2_a': _dense(next(ks), (L, D_FF, D_MODEL), D_FF),
        'g_mix': _gain(next(ks), (L, D_MODEL)),
        'w_in': _dense(next(ks), (L, D_MODEL, D_IN), D_MODEL),
        'g_q_lat': _gain(next(ks), (L, MLA_Q_RANK)),
        'w_uq': _dense(next(ks), (L, MLA_Q_RANK, MLA_HEADS * (MLA_NOPE + MLA_ROPE)), MLA_Q_RANK),
        'g_qn': _gain(next(ks), (L, MLA_NOPE)),
        'g_qr': _gain(next(ks), (L, MLA_ROPE)),
        'g_kv_lat': _gain(next(ks), (L, MLA_KV_RANK)),
        'w_ukv': _dense(next(ks), (L, MLA_KV_RANK, MLA_HEADS * (MLA_NOPE + MLA_V)), MLA_KV_RANK),
        'g_kn': _gain(next(ks), (L, MLA_NOPE)),
        'g_kr': _gain(next(ks), (L, MLA_ROPE)),
        'w_gk_up': _dense(next(ks), (L, GLA_GATE_RANK, GLA_KEY), GLA_GATE_RANK),
        'b_gk': 0.1 * jax.random.normal(next(ks), (L, GLA_KEY), jnp.float32),
        'g_gla': _gain(next(ks), (L, GLA_DV)),
        'w_proj_a': _dense(next(ks), (L, MLA_WIDTH, D_MODEL), MLA_WIDTH),
        'w_proj_b': _dense(next(ks), (L, GLA_VAL, D_MODEL), GLA_VAL),
        'w_out': _dense(next(ks), (L, D_MODEL, D_MODEL), D_MODEL),
        'g_ffn2': _gain(next(ks), (L, D_MODEL)),
        'w1_b': _dense(next(ks), (L, D_MODEL, D_FF), D_MODEL),
        'w3_b': _dense(next(ks), (L, D_MODEL, D_FF), D_MODEL),
        'w2_b': _dense(next(ks), (L, D_FF, D_MODEL), D_FF),
        'g_final': _gain(next(ks), (L, D_MODEL)),
    }


def reference(x, c, positions, w_ada, b_ada, g_ffn1, w1_a, w3_a, w2_a, g_mix, w_in,
              g_q_lat, w_uq, g_qn, g_qr, g_kv_lat, w_ukv, g_kn, g_kr, w_gk_up, b_gk, g_gla,
              w_proj_a, w_proj_b, w_out, g_ffn2, w1_b, w3_b, w2_b, g_final):
    B, S, D = x.shape
    inv_freq = ROPE_THETA ** (-jnp.arange(0, MLA_ROPE, 2, dtype=jnp.float32) / MLA_ROPE)
    ang = positions.astype(jnp.float32)[..., None] * inv_freq
    cos, sin = jnp.cos(ang), jnp.sin(ang)
    for i in range(DEPTH):
        mod = (jax.nn.silu(c) @ w_ada[i] + b_ada[i]).reshape(B, N_MOD, D)
        sh1, sc1, ga1, sh2, sc2, ga2, sh3, sc3, ga3 = [mod[:, j, None, :] for j in range(N_MOD)]
        h = _rms_norm(x, g_ffn1[i]) * (1.0 + sc1) + sh1
        x = x + 0.5 * ga1 * _swiglu(h, w1_a[i], w3_a[i], w2_a[i])
        h = _rms_norm(x, g_mix[i]) * (1.0 + sc2) + sh2
        (q_lat, kv_lat, k_rope_raw, gq, gk, gv, g_lr, g_out, gate_a, gate_b) = _split_cols(h @ w_in[i], IN_SPLITS)
        ya = _mla_branch(q_lat, kv_lat, k_rope_raw, cos, sin, g_q_lat[i], w_uq[i], g_qn[i], g_qr[i],
                         g_kv_lat[i], w_ukv[i], g_kn[i], g_kr[i]) @ w_proj_a[i]
        yb = _gla_branch(gq, gk, gv, g_lr, g_out, w_gk_up[i], b_gk[i], g_gla[i]) @ w_proj_b[i]
        merged = jax.nn.sigmoid(gate_a) * ya + jax.nn.sigmoid(gate_b) * yb
        x = x + ga2 * (merged @ w_out[i])
        h = _rms_norm(x, g_ffn2[i]) * (1.0 + sc3) + sh3
        x = x + 0.5 * ga3 * _swiglu(h, w1_b[i], w3_b[i], w2_b[i])
        x = _rms_norm(x, g_final[i])
    return x
```

```python
import functools

import jax
import jax.numpy as jnp
from jax import lax
from jax.experimental import pallas as pl
from jax.experimental.pallas import tpu as pltpu

F32 = jnp.float32
BF16 = jnp.bfloat16

EPS = 1e-6
CHUNK = 64
MLA_HEADS = 8
MLA_Q_RANK = 512
MLA_KV_RANK = 512
MLA_NOPE = 128
MLA_ROPE = 64
MLA_V = 128
MLA_QK = MLA_NOPE + MLA_ROPE
ROPE_THETA = 10000.0
GLA_HEADS = 4
GLA_DK = 128
GLA_DV = 256
GLA_GATE_RANK = 16
GLA_GATE_NORMALIZER = 16.0
GLA_KEY = GLA_HEADS * GLA_DK
GLA_VAL = GLA_HEADS * GLA_DV
N_MOD = 9

LANES = 128
SUBLANES = 8
VMEM_LIMIT = 56 * 1024 * 1024

COL_QLAT = 0
COL_KVLAT = COL_QLAT + MLA_Q_RANK
COL_GQ = COL_KVLAT + MLA_KV_RANK
COL_GK = COL_GQ + GLA_KEY
COL_GV = COL_GK + GLA_KEY
COL_GOUT = COL_GV + GLA_VAL
COL_GATE_A = COL_GOUT + GLA_VAL
TAIL_W = LANES
TAIL_KROPE = 0
TAIL_GLR = MLA_ROPE


def _params(sem):
    return pltpu.CompilerParams(dimension_semantics=sem, vmem_limit_bytes=VMEM_LIMIT)


def _dot(a, b):
    return jnp.dot(a, b, preferred_element_type=F32)


def _dot_t0(a, b):
    return lax.dot_general(a, b, (((0,), (0,)), ((), ())), preferred_element_type=F32)


def _dot_t1(a, b):
    return lax.dot_general(a, b, (((1,), (1,)), ((), ())), preferred_element_type=F32)


def _split_dot(x, w_bf16):
    hi = x.astype(BF16)
    lo = (x - hi.astype(F32)).astype(BF16)
    return _dot(hi, w_bf16) + _dot(lo, w_bf16)


def _split_dot_t0(x, w_bf16):
    hi = x.astype(BF16)
    lo = (x - hi.astype(F32)).astype(BF16)
    return _dot_t0(hi, w_bf16) + _dot_t0(lo, w_bf16)


def _silu(x):
    return x * jax.nn.sigmoid(x)


def _rms(x, g):
    ms = jnp.mean(x * x, axis=-1, keepdims=True)
    return x * lax.rsqrt(ms + EPS) * g


ADA_TN = 1024
ADA_RC = 64


def _adaln_kernel(ct_ref, w_ref, b_ref, o_ref):
    d, nb = ct_ref.shape
    tn = w_ref.shape[1]

    def body(k, acc):
        rows = pl.ds(pl.multiple_of(k * ADA_RC, ADA_RC), ADA_RC)
        w = w_ref[rows, :]
        s = _silu(ct_ref[rows, :])
        out = []
        for b in range(nb):
            p = w * s[:, b:b + 1]
            out.append(acc[b] + p.reshape(ADA_RC // SUBLANES, SUBLANES, tn).sum(axis=0))
        return tuple(out)

    init = tuple(jnp.zeros((SUBLANES, tn), F32) for _ in range(nb))
    acc = lax.fori_loop(0, d // ADA_RC, body, init)
    rows = [jnp.sum(a, axis=0, keepdims=True) for a in acc]
    o_ref[...] = jnp.concatenate(rows, axis=0) + b_ref[...]


def _adaln(c, w_ada, b_ada):
    nb, d = c.shape
    n = w_ada.shape[1]
    return pl.pallas_call(
        _adaln_kernel,
        out_shape=jax.ShapeDtypeStruct((nb, n), F32),
        grid=(n // ADA_TN,),
        in_specs=[
            pl.BlockSpec((d, nb), lambda j: (0, 0)),
            pl.BlockSpec((d, ADA_TN), lambda j: (0, j)),
            pl.BlockSpec((1, ADA_TN), lambda j: (0, j)),
        ],
        out_specs=pl.BlockSpec((nb, ADA_TN), lambda j: (0, j)),
        compiler_params=_params(("arbitrary",)),
        name="adaln",
    )(c.T, w_ada, b_ada.reshape(1, n))


def _mod_norm(x, g, mod_ref, base):
    shift = mod_ref[0, base:base + 1, :]
    scale = mod_ref[0, base + 1:base + 2, :]
    return _rms(x, g) * (1.0 + scale) + shift


FFN_TM = 512
FFN_TF = 512


def _ffn_kernel(x_ref, mod_ref, g_ref, w1_ref, w3_ref, w2_ref, gf_ref, o_ref, h_ref,
                *, base, final_norm):
    j = pl.program_id(1)

    @pl.when(j == 0)
    def _():
        h_ref[...] = _mod_norm(x_ref[...], g_ref[...], mod_ref, base).astype(BF16)
        o_ref[...] = jnp.zeros_like(o_ref)

    h = h_ref[...]
    a = _dot(h, w1_ref[...])
    b = _dot(h, w3_ref[...])
    p = (_silu(a) * b).astype(BF16)
    o_ref[...] += _dot(p, w2_ref[...])

    @pl.when(j == pl.num_programs(1) - 1)
    def _():
        gate = mod_ref[0, base + 2:base + 3, :]
        y = x_ref[...] + 0.5 * gate * o_ref[...]
        if final_norm:
            y = _rms(y, gf_ref[...])
        o_ref[...] = y


def _ffn(x2d, mod3, g, w1, w3, w2, g_final, *, base, final_norm, seq):
    t, d = x2d.shape
    dff = w1.shape[1]
    tiles_per_batch = seq // FFN_TM
    row = lambda i, j: (i, 0)
    const = lambda i, j: (0, 0)
    return pl.pallas_call(
        functools.partial(_ffn_kernel, base=base, final_norm=final_norm),
        out_shape=jax.ShapeDtypeStruct((t, d), F32),
        grid=(t // FFN_TM, dff // FFN_TF),
        in_specs=[
            pl.BlockSpec((FFN_TM, d), row),
            pl.BlockSpec((1, N_MOD, d), lambda i, j: (i // tiles_per_batch, 0, 0)),
            pl.BlockSpec((1, d), const),
            pl.BlockSpec((d, FFN_TF), lambda i, j: (0, j)),
            pl.BlockSpec((d, FFN_TF), lambda i, j: (0, j)),
            pl.BlockSpec((FFN_TF, d), lambda i, j: (j, 0)),
            pl.BlockSpec((1, d), const),
        ],
        out_specs=pl.BlockSpec((FFN_TM, d), row),
        scratch_shapes=[pltpu.VMEM((FFN_TM, d), BF16)],
        compiler_params=_params(("parallel", "arbitrary")),
        name="ffn_final" if final_norm else "ffn",
    )(x2d, mod3, g, w1, w3, w2, g_final)


INP_TM = 1024
INP_TN = 1024


def _inproj_kernel(x_ref, mod_ref, g_ref, w_ref, wt_ref, o_ref, tail_ref, h_ref, *, base):
    @pl.when(pl.program_id(1) == 0)
    def _():
        h = _mod_norm(x_ref[...], g_ref[...], mod_ref, base).astype(BF16)
        h_ref[...] = h
        tail_ref[...] = _dot(h, wt_ref[...])

    o_ref[...] = _dot(h_ref[...], w_ref[...]).astype(BF16)


def _inproj(x2d, mod3, g, w_main, w_tail, *, base, seq):
    t, d = x2d.shape
    n = w_main.shape[1]
    tiles_per_batch = seq // INP_TM
    return pl.pallas_call(
        functools.partial(_inproj_kernel, base=base),
        out_shape=(jax.ShapeDtypeStruct((t, n), BF16),
                   jax.ShapeDtypeStruct((t, TAIL_W), F32)),
        grid=(t // INP_TM, n // INP_TN),
        in_specs=[
            pl.BlockSpec((INP_TM, d), lambda i, j: (i, 0)),
            pl.BlockSpec((1, N_MOD, d), lambda i, j: (i // tiles_per_batch, 0, 0)),
            pl.BlockSpec((1, d), lambda i, j: (0, 0)),
            pl.BlockSpec((d, INP_TN), lambda i, j: (0, j)),
            pl.BlockSpec((d, TAIL_W), lambda i, j: (0, 0)),
        ],
        out_specs=(pl.BlockSpec((INP_TM, INP_TN), lambda i, j: (i, j)),
                   pl.BlockSpec((INP_TM, TAIL_W), lambda i, j: (i, 0))),
        scratch_shapes=[pltpu.VMEM((INP_TM, d), BF16)],
        compiler_params=_params(("parallel", "arbitrary")),
        name="in_proj",
    )(x2d, mod3, g, w_main, w_tail)


PREP_TM = 256
ROPE_HALF = MLA_ROPE // 2


def _rope_rotate(x, cos, sin_signed):
    width = x.shape[-1]
    lane = lax.broadcasted_iota(jnp.int32, x.shape, 1)
    first_half = (lane % MLA_ROPE) < ROPE_HALF
    partner = jnp.where(first_half,
                        pltpu.roll(x, width - ROPE_HALF, 1),
                        pltpu.roll(x, ROPE_HALF, 1))
    return x * cos + partner * sin_signed


def _prep_kernel(qlat_ref, kvlat_ref, tail_ref, pos_ref, invf_ref, sgn_ref,
                 gql_ref, wuq_ref, gqn_ref, gqr_ref, gkvl_ref, wukv_ref, gkn_ref, gkr_ref,
                 grp_ref, q_ref, k_ref, v_ref):
    scale = MLA_QK ** -0.5
    ang = pos_ref[...].astype(F32) * invf_ref[...]
    cos = jnp.cos(ang)
    sin_signed = jnp.sin(ang) * sgn_ref[...]
    n_rope = MLA_HEADS * MLA_ROPE
    cos_q = jnp.concatenate([cos] * (n_rope // LANES), axis=-1)
    sin_q = jnp.concatenate([sin_signed] * (n_rope // LANES), axis=-1)

    cq = _rms(qlat_ref[...].astype(F32), gql_ref[...]).astype(BF16)
    q = _dot(cq, wuq_ref[...])
    n_nope = MLA_HEADS * MLA_NOPE
    qr = q[:, n_nope:]
    ssq = _split_dot(qr * qr, grp_ref[...])
    qr = qr * lax.rsqrt(ssq * (1.0 / MLA_ROPE) + EPS) * gqr_ref[...]
    qr = _rope_rotate(qr, cos_q, sin_q) * scale

    ckv = _rms(kvlat_ref[...].astype(F32), gkvl_ref[...]).astype(BF16)
    kv = _dot(ckv, wukv_ref[...])
    v_ref[0] = kv[:, n_nope:].astype(BF16)
    kr = tail_ref[:, TAIL_KROPE:TAIL_KROPE + MLA_ROPE]
    kr = _rms(kr, gkr_ref[...])
    kr = _rope_rotate(kr, cos[:, :MLA_ROPE], sin_signed[:, :MLA_ROPE]).astype(BF16)

    for h in range(MLA_HEADS):
        qn = _rms(q[:, h * MLA_NOPE:(h + 1) * MLA_NOPE], gqn_ref[...]) * scale
        q_ref[0, h, :, :MLA_NOPE] = qn.astype(BF16)
        q_ref[0, h, :, MLA_NOPE:] = qr[:, h * MLA_ROPE:(h + 1) * MLA_ROPE].astype(BF16)
        kn = _rms(kv[:, h * MLA_NOPE:(h + 1) * MLA_NOPE], gkn_ref[...])
        k_ref[0, h, :, :MLA_NOPE] = kn.astype(BF16)
        k_ref[0, h, :, MLA_NOPE:] = kr


def _mla_prep(proj, tail, pos2d, consts, weights, *, batch, seq):
    t = proj.shape[0]
    tiles_per_batch = seq // PREP_TM
    row = lambda i: (i, 0)
    const = lambda i: (0, 0)
    (invf, sgn, grp) = consts
    (gql, wuq, gqn, gqr, gkvl, wukv, gkn, gkr) = weights
    head_out = pl.BlockSpec((1, MLA_HEADS, PREP_TM, MLA_QK),
                            lambda i: (i // tiles_per_batch, 0, i % tiles_per_batch, 0))
    full = lambda a: pl.BlockSpec(a.shape, const)
    return pl.pallas_call(
        _prep_kernel,
        out_shape=(jax.ShapeDtypeStruct((batch, MLA_HEADS, seq, MLA_QK), BF16),
                   jax.ShapeDtypeStruct((batch, MLA_HEADS, seq, MLA_QK), BF16),
                   jax.ShapeDtypeStruct((batch, seq, MLA_HEADS * MLA_V), BF16)),
        grid=(t // PREP_TM,),
        in_specs=[
            pl.BlockSpec((PREP_TM, MLA_Q_RANK), lambda i: (i, COL_QLAT // MLA_Q_RANK)),
            pl.BlockSpec((PREP_TM, MLA_KV_RANK), lambda i: (i, COL_KVLAT // MLA_KV_RANK)),
            pl.BlockSpec((PREP_TM, TAIL_W), row),
            pl.BlockSpec((PREP_TM, 1), row),
            full(invf), full(sgn),
            full(gql), full(wuq), full(gqn), full(gqr),
            full(gkvl), full(wukv), full(gkn), full(gkr),
            full(grp),
        ],
        out_specs=(head_out, head_out,
                   pl.BlockSpec((1, PREP_TM, MLA_HEADS * MLA_V),
                                lambda i: (i // tiles_per_batch, i % tiles_per_batch, 0))),
        compiler_params=_params(("parallel",)),
        name="mla_prep",
    )(proj, proj, tail, pos2d, invf, sgn, gql, wuq, gqn, gqr, gkvl, wukv, gkn, gkr, grp)


ATT_T = 512


def _attn_kernel(q_ref, k_ref, v_ref, o_ref, m_ref, l_ref, acc_ref):
    qi = pl.program_id(2)
    q = q_ref[0, 0]

    m_ref[...] = jnp.full(m_ref.shape, -jnp.inf, F32)
    l_ref[...] = jnp.zeros(l_ref.shape, F32)
    acc_ref[...] = jnp.zeros(acc_ref.shape, F32)

    def step(kj, masked):
        rows = pl.ds(pl.multiple_of(kj * ATT_T, ATT_T), ATT_T)
        s = _dot_t1(q, k_ref[0, 0, rows, :])
        if masked:
            r = lax.broadcasted_iota(jnp.int32, s.shape, 0)
            c = lax.broadcasted_iota(jnp.int32, s.shape, 1)
            s = jnp.where((c // CHUNK) <= (r // CHUNK), s, -jnp.inf)
        m_prev = m_ref[...]
        m_new = jnp.maximum(m_prev, jnp.max(s, axis=-1, keepdims=True))
        alpha = jnp.exp(m_prev - m_new)
        p = jnp.exp(s - m_new)
        l_ref[...] = alpha * l_ref[...] + jnp.sum(p, axis=-1, keepdims=True)
        acc_ref[...] = alpha * acc_ref[...] + _dot(p.astype(BF16), v_ref[0, rows, :])
        m_ref[...] = m_new

    def body(kj, carry):
        step(kj, masked=False)
        return carry

    lax.fori_loop(0, qi, body, 0)
    step(qi, masked=True)
    o_ref[0] = (acc_ref[...] / l_ref[...]).astype(BF16)


def _mla_attn(q, k, v):
    batch, heads, seq, _ = q.shape
    return pl.pallas_call(
        _attn_kernel,
        out_shape=jax.ShapeDtypeStruct((batch, seq, heads * MLA_V), BF16),
        grid=(batch, heads, seq // ATT_T),
        in_specs=[
            pl.BlockSpec((1, 1, ATT_T, MLA_QK), lambda b, h, i: (b, h, i, 0)),
            pl.BlockSpec((1, 1, seq, MLA_QK), lambda b, h, i: (b, h, 0, 0)),
            pl.BlockSpec((1, seq, MLA_V), lambda b, h, i: (b, 0, h)),
        ],
        out_specs=pl.BlockSpec((1, ATT_T, MLA_V), lambda b, h, i: (b, i, h)),
        scratch_shapes=[pltpu.VMEM((ATT_T, 1), F32), pltpu.VMEM((ATT_T, 1), F32),
                        pltpu.VMEM((ATT_T, MLA_V), F32)],
        compiler_params=_params(("parallel", "parallel", "arbitrary")),
        name="mla_attn",
    )(q, k, v)


GLA_L = 512
GLA_NC = GLA_L // CHUNK


def _gla_kernel(q_ref, k_ref, v_ref, go_ref, tail_ref, wg_ref, bg_ref, gg_ref,
                o_ref, state_ref):
    @pl.when(pl.program_id(2) == 0)
    def _():
        state_ref[...] = jnp.zeros_like(state_ref)

    wg = wg_ref[...]
    wg_hi = wg.astype(BF16)
    wg_lo = (wg - wg_hi.astype(F32)).astype(BF16)
    tl = tail_ref[...]
    tl_hi = tl.astype(BF16)
    tl_lo = (tl - tl_hi.astype(F32)).astype(BF16)
    z = _dot(tl_hi, wg_hi) + _dot(tl_lo, wg_hi) + _dot(tl_hi, wg_lo) + bg_ref[...]
    log_a = (jnp.minimum(z, 0.0) - jnp.log1p(jnp.exp(-jnp.abs(z)))) * (1.0 / GLA_GATE_NORMALIZER)

    r = lax.broadcasted_iota(jnp.int32, (GLA_L, GLA_L), 0)
    c = lax.broadcasted_iota(jnp.int32, (GLA_L, GLA_L), 1)
    same_chunk = (r // CHUNK) == (c // CHUNK)
    causal = same_chunk & (c <= r)
    la_hi = log_a.astype(BF16)
    la_lo = (log_a - la_hi.astype(F32)).astype(BF16)
    tri = causal.astype(BF16)
    blk = same_chunk.astype(BF16)
    b_cum = _dot(tri, la_hi) + _dot(tri, la_lo)
    b_tot = _dot(blk, la_hi) + _dot(blk, la_lo)

    qf = q_ref[...].astype(F32) * (GLA_DK ** -0.5)
    kf = k_ref[...].astype(F32)
    v = v_ref[...]
    q_dec = (qf * jnp.exp(b_cum)).astype(BF16)
    k_dec = (kf * jnp.exp(-b_cum)).astype(BF16)
    k_end = (kf * jnp.exp(b_tot - b_cum)).astype(BF16)

    attn = jnp.where(causal, _dot_t1(q_dec, k_dec), 0.0).astype(BF16)
    o_intra = _dot(attn, v)

    ones = jnp.ones((CHUNK, GLA_DV), BF16)
    state = state_ref[...]
    outs = []
    for n in range(GLA_NC):
        rows = slice(n * CHUNK, (n + 1) * CHUNK)
        outs.append(o_intra[rows] + _dot(q_dec[rows], state.astype(BF16)))
        kv = _dot_t0(k_end[rows], v[rows])
        dec = jnp.exp(_dot_t0(la_hi[rows], ones) + _dot_t0(la_lo[rows], ones))
        state = state * dec + kv
    state_ref[...] = state

    o = jnp.concatenate(outs, axis=0)
    o = _rms(o, gg_ref[...]) * _silu(go_ref[...].astype(F32))
    o_ref[...] = o.astype(BF16)


def _gla(proj, tail, wg_pad, b_gk, g_gla, *, batch, seq):
    t = proj.shape[0]
    steps = seq // GLA_L
    tok = lambda b, h, i: b * steps + i
    return pl.pallas_call(
        _gla_kernel,
        out_shape=jax.ShapeDtypeStruct((t, GLA_VAL), BF16),
        grid=(batch, GLA_HEADS, steps),
        in_specs=[
            pl.BlockSpec((GLA_L, GLA_DK), lambda b, h, i: (tok(b, h, i), COL_GQ // GLA_DK + h)),
            pl.BlockSpec((GLA_L, GLA_DK), lambda b, h, i: (tok(b, h, i), COL_GK // GLA_DK + h)),
            pl.BlockSpec((GLA_L, GLA_DV), lambda b, h, i: (tok(b, h, i), COL_GV // GLA_DV + h)),
            pl.BlockSpec((GLA_L, GLA_DV), lambda b, h, i: (tok(b, h, i), COL_GOUT // GLA_DV + h)),
            pl.BlockSpec((GLA_L, TAIL_W), lambda b, h, i: (tok(b, h, i), 0)),
            pl.BlockSpec((TAIL_W, GLA_DK), lambda b, h, i: (0, h)),
            pl.BlockSpec((1, GLA_DK), lambda b, h, i: (0, h)),
            pl.BlockSpec((1, GLA_DV), lambda b, h, i: (0, 0)),
        ],
        out_specs=pl.BlockSpec((GLA_L, GLA_DV), lambda b, h, i: (tok(b, h, i), h)),
        scratch_shapes=[pltpu.VMEM((GLA_DK, GLA_DV), F32)],
        compiler_params=_params(("parallel", "parallel", "arbitrary")),
        name="gla",
    )(proj, proj, proj, proj, tail, wg_pad, b_gk, g_gla)


MRG_TM = 512


def _merge_kernel(x_ref, mod_ref, a_ref, b_ref, ga_ref, gb_ref, wa_ref, wb_ref, wo_ref, o_ref,
                  *, base):
    ya = _dot(a_ref[...], wa_ref[...])
    yb = _dot(b_ref[...], wb_ref[...])
    merged = (jax.nn.sigmoid(ga_ref[...].astype(F32)) * ya
              + jax.nn.sigmoid(gb_ref[...].astype(F32)) * yb).astype(BF16)
    gate = mod_ref[0, base + 2:base + 3, :]
    o_ref[...] = x_ref[...] + gate * _dot(merged, wo_ref[...])


def _merge(x2d, mod3, attn_o, gla_o, proj, wa, wb, wo, *, base, seq):
    t, d = x2d.shape
    tiles_per_batch = seq // MRG_TM
    row = lambda i: (i, 0)
    resident = lambda a: pl.BlockSpec(a.shape, lambda i: (0, 0), pipeline_mode=pl.Buffered(1))
    return pl.pallas_call(
        functools.partial(_merge_kernel, base=base),
        out_shape=jax.ShapeDtypeStruct((t, d), F32),
        grid=(t // MRG_TM,),
        in_specs=[
            pl.BlockSpec((MRG_TM, d), row),
            pl.BlockSpec((1, N_MOD, d), lambda i: (i // tiles_per_batch, 0, 0)),
            pl.BlockSpec((MRG_TM, attn_o.shape[1]), row),
            pl.BlockSpec((MRG_TM, gla_o.shape[1]), row),
            pl.BlockSpec((MRG_TM, d), lambda i: (i, COL_GATE_A // d)),
            pl.BlockSpec((MRG_TM, d), lambda i: (i, COL_GATE_A // d + 1)),
            resident(wa), resident(wb), resident(wo),
        ],
        out_specs=pl.BlockSpec((MRG_TM, d), row),
        compiler_params=_params(("parallel",)),
        name="merge",
    )(x2d, mod3, attn_o, gla_o, proj, proj, wa, wb, wo)


def _split_cols(w, sizes):
    out, start = [], 0
    for s in sizes:
        out.append(w[:, start:start + s])
        start += s
    return out


def _layer(x2d, mod3, pos2d, p, *, batch, seq):
    d = x2d.shape[1]
    (wi_qlat, wi_kvlat, wi_krope, wi_gq, wi_gk, wi_gv, wi_glr, wi_gout, wi_ga, wi_gb) = _split_cols(
        p["w_in"], (MLA_Q_RANK, MLA_KV_RANK, MLA_ROPE, GLA_KEY, GLA_KEY, GLA_VAL,
                    GLA_GATE_RANK, GLA_VAL, d, d))
    w_main = jnp.concatenate([wi_qlat, wi_kvlat, wi_gq, wi_gk, wi_gv, wi_gout, wi_ga, wi_gb],
                             axis=1).astype(BF16)
    w_tail = jnp.concatenate(
        [wi_krope, wi_glr, jnp.zeros((d, TAIL_W - MLA_ROPE - GLA_GATE_RANK), F32)], axis=1).astype(BF16)
    wuq = p["w_uq"].reshape(MLA_Q_RANK, MLA_HEADS, MLA_QK)
    wuq = jnp.concatenate([wuq[:, :, :MLA_NOPE].reshape(MLA_Q_RANK, -1),
                           wuq[:, :, MLA_NOPE:].reshape(MLA_Q_RANK, -1)], axis=1).astype(BF16)
    wukv = p["w_ukv"].reshape(MLA_KV_RANK, MLA_HEADS, MLA_NOPE + MLA_V)
    wukv = jnp.concatenate([wukv[:, :, :MLA_NOPE].reshape(MLA_KV_RANK, -1),
                            wukv[:, :, MLA_NOPE:].reshape(MLA_KV_RANK, -1)], axis=1).astype(BF16)
    wg_pad = jnp.zeros((TAIL_W, GLA_KEY), F32).at[TAIL_GLR:TAIL_GLR + GLA_GATE_RANK].set(p["w_gk_up"])

    inv_freq = ROPE_THETA ** (-jnp.arange(0, MLA_ROPE, 2, dtype=F32) / MLA_ROPE)
    invf = jnp.tile(inv_freq, LANES // ROPE_HALF).reshape(1, LANES)
    sgn = jnp.tile(jnp.concatenate([-jnp.ones(ROPE_HALF, F32), jnp.ones(ROPE_HALF, F32)]),
                   LANES // MLA_ROPE).reshape(1, LANES)
    n_rope = MLA_HEADS * MLA_ROPE
    lane = jnp.arange(n_rope)
    grp = (lane[:, None] // MLA_ROPE == lane[None, :] // MLA_ROPE).astype(BF16)

    row = lambda a: a.reshape(1, -1)
    x2d = _ffn(x2d, mod3, row(p["g_ffn1"]), p["w1_a"].astype(BF16), p["w3_a"].astype(BF16),
               p["w2_a"].astype(BF16), row(p["g_final"]), base=0, final_norm=False, seq=seq)
    proj, tail = _inproj(x2d, mod3, row(p["g_mix"]), w_main, w_tail, base=3, seq=seq)
    q, k, v = _mla_prep(
        proj, tail, pos2d, (invf, sgn, grp),
        (row(p["g_q_lat"]), wuq, row(p["g_qn"]), row(jnp.tile(p["g_qr"], MLA_HEADS)),
         row(p["g_kv_lat"]), wukv, row(p["g_kn"]), row(p["g_kr"])),
        batch=batch, seq=seq)
    attn_o = _mla_attn(q, k, v).reshape(batch * seq, MLA_HEADS * MLA_V)
    gla_o = _gla(proj, tail, wg_pad, row(p["b_gk"]), row(p["g_gla"]), batch=batch, seq=seq)
    x2d = _merge(x2d, mod3, attn_o, gla_o, proj, p["w_proj_a"].astype(BF16),
                 p["w_proj_b"].astype(BF16), p["w_out"].astype(BF16), base=3, seq=seq)
    x2d = _ffn(x2d, mod3, row(p["g_ffn2"]), p["w1_b"].astype(BF16), p["w3_b"].astype(BF16),
               p["w2_b"].astype(BF16), row(p["g_final"]), base=6, final_norm=True, seq=seq)
    return x2d


def kernel(x, c, positions, w_ada, b_ada, g_ffn1, w1_a, w3_a, w2_a, g_mix, w_in, g_q_lat, w_uq, g_qn, g_qr, g_kv_lat, w_ukv, g_kn, g_kr, w_gk_up, b_gk, g_gla, w_proj_a, w_proj_b, w_out, g_ffn2, w1_b, w3_b, w2_b, g_final):
    batch, seq, d = x.shape
    depth = w_ada.shape[0]
    assert depth == 1, "the final norm is fused into the last FFN of a single layer"
    names = ("g_ffn1", "w1_a", "w3_a", "w2_a", "g_mix", "w_in", "g_q_lat", "w_uq", "g_qn", "g_qr",
             "g_kv_lat", "w_ukv", "g_kn", "g_kr", "w_gk_up", "b_gk", "g_gla", "w_proj_a", "w_proj_b",
             "w_out", "g_ffn2", "w1_b", "w3_b", "w2_b", "g_final")
    stacked = (g_ffn1, w1_a, w3_a, w2_a, g_mix, w_in, g_q_lat, w_uq, g_qn, g_qr, g_kv_lat, w_ukv,
               g_kn, g_kr, w_gk_up, b_gk, g_gla, w_proj_a, w_proj_b, w_out, g_ffn2, w1_b, w3_b, w2_b,
               g_final)
    x2d = x.reshape(batch * seq, d)
    pos2d = positions.reshape(batch * seq, 1)
    p = {n: a[0] for n, a in zip(names, stacked)}
    mod3 = _adaln(c, w_ada[0], b_ada[0]).reshape(batch, N_MOD, d)
    x2d = _layer(x2d, mod3, pos2d, p, batch=batch, seq=seq)
    return x2d.reshape(batch, seq, d)
```

```python
import functools

import jax
import jax.numpy as jnp
from jax import lax
from jax.experimental import pallas as pl
from jax.experimental.pallas import tpu as pltpu

F32 = jnp.float32
BF16 = jnp.bfloat16

EPS = 1e-6
LOG2E = 1.4426950408889634
CHUNK = 64
MLA_HEADS = 8
MLA_Q_RANK = 512
MLA_KV_RANK = 512
MLA_NOPE = 128
MLA_ROPE = 64
MLA_V = 128
MLA_QK = MLA_NOPE + MLA_ROPE
ROPE_THETA = 10000.0
GLA_HEADS = 4
GLA_DK = 128
GLA_DV = 256
GLA_GATE_RANK = 16
GLA_GATE_NORMALIZER = 16.0
GLA_KEY = GLA_HEADS * GLA_DK
GLA_VAL = GLA_HEADS * GLA_DV
N_MOD = 9

LANES = 128
SUBLANES = 8
VMEM_LIMIT = 56 * 1024 * 1024

COL_QLAT = 0
COL_KVLAT = COL_QLAT + MLA_Q_RANK
COL_GQ = COL_KVLAT + MLA_KV_RANK
COL_GK = COL_GQ + GLA_KEY
COL_GV = COL_GK + GLA_KEY
COL_GOUT = COL_GV + GLA_VAL
COL_GATE_A = COL_GOUT + GLA_VAL
TAIL_W = LANES
TAIL_KROPE = 0
TAIL_GLR = MLA_ROPE


def _params(sem):
    return pltpu.CompilerParams(dimension_semantics=sem, vmem_limit_bytes=VMEM_LIMIT)


def _dot(a, b):
    return jnp.dot(a, b, preferred_element_type=F32)


def _dot_t0(a, b):
    return lax.dot_general(a, b, (((0,), (0,)), ((), ())), preferred_element_type=F32)


def _dot_t1(a, b):
    return lax.dot_general(a, b, (((1,), (1,)), ((), ())), preferred_element_type=F32)


def _split_dot(x, w_bf16):
    hi = x.astype(BF16)
    lo = (x - hi.astype(F32)).astype(BF16)
    return _dot(hi, w_bf16) + _dot(lo, w_bf16)


def _split_dot_t0(x, w_bf16):
    hi = x.astype(BF16)
    lo = (x - hi.astype(F32)).astype(BF16)
    return _dot_t0(hi, w_bf16) + _dot_t0(lo, w_bf16)


def _silu(x):
    return x * jax.nn.sigmoid(x)


def _rms(x, g):
    ms = jnp.mean(x * x, axis=-1, keepdims=True)
    return x * lax.rsqrt(ms + EPS) * g


ADA_TN = 1024
ADA_RC = 64


def _adaln_kernel(ct_ref, w_ref, b_ref, o_ref):
    d, nb = ct_ref.shape
    tn = w_ref.shape[1]

    def body(k, acc):
        rows = pl.ds(pl.multiple_of(k * ADA_RC, ADA_RC), ADA_RC)
        w = w_ref[rows, :]
        s = _silu(ct_ref[rows, :])
        out = []
        for b in range(nb):
            p = w * s[:, b:b + 1]
            out.append(acc[b] + p.reshape(ADA_RC // SUBLANES, SUBLANES, tn).sum(axis=0))
        return tuple(out)

    init = tuple(jnp.zeros((SUBLANES, tn), F32) for _ in range(nb))
    acc = lax.fori_loop(0, d // ADA_RC, body, init)
    rows = [jnp.sum(a, axis=0, keepdims=True) for a in acc]
    o_ref[...] = jnp.concatenate(rows, axis=0) + b_ref[...]


def _adaln(c, w_ada, b_ada):
    nb, d = c.shape
    n = w_ada.shape[1]
    return pl.pallas_call(
        _adaln_kernel,
        out_shape=jax.ShapeDtypeStruct((nb, n), F32),
        grid=(n // ADA_TN,),
        in_specs=[
            pl.BlockSpec((d, nb), lambda j: (0, 0)),
            pl.BlockSpec((d, ADA_TN), lambda j: (0, j)),
            pl.BlockSpec((1, ADA_TN), lambda j: (0, j)),
        ],
        out_specs=pl.BlockSpec((nb, ADA_TN), lambda j: (0, j)),
        compiler_params=_params(("arbitrary",)),
        name="adaln",
    )(c.T, w_ada, b_ada.reshape(1, n))


def _mod_norm(x, g, mod_ref, base):
    shift = mod_ref[0, base:base + 1, :]
    scale = mod_ref[0, base + 1:base + 2, :]
    return _rms(x, g) * (1.0 + scale) + shift


FFN_TM = 512
FFN_TF = 512


def _ffn_kernel(x_ref, mod_ref, g_ref, w1_ref, w3_ref, w2_ref, gf_ref, o_ref, h_ref,
                *, base, final_norm):
    j = pl.program_id(1)

    @pl.when(j == 0)
    def _():
        h_ref[...] = _mod_norm(x_ref[...], g_ref[...], mod_ref, base).astype(BF16)
        o_ref[...] = jnp.zeros_like(o_ref)

    h = h_ref[...]
    a = _dot(h, w1_ref[...])
    b = _dot(h, w3_ref[...])
    p = (_silu(a) * b).astype(BF16)
    o_ref[...] += _dot(p, w2_ref[...])

    @pl.when(j == pl.num_programs(1) - 1)
    def _():
        gate = mod_ref[0, base + 2:base + 3, :]
        y = x_ref[...] + 0.5 * gate * o_ref[...]
        if final_norm:
            y = _rms(y, gf_ref[...])
        o_ref[...] = y


def _ffn(x2d, mod3, g, w1, w3, w2, g_final, *, base, final_norm, seq):
    t, d = x2d.shape
    dff = w1.shape[1]
    tiles_per_batch = seq // FFN_TM
    row = lambda i, j: (i, 0)
    const = lambda i, j: (0, 0)
    return pl.pallas_call(
        functools.partial(_ffn_kernel, base=base, final_norm=final_norm),
        out_shape=jax.ShapeDtypeStruct((t, d), F32),
        grid=(t // FFN_TM, dff // FFN_TF),
        in_specs=[
            pl.BlockSpec((FFN_TM, d), row),
            pl.BlockSpec((1, N_MOD, d), lambda i, j: (i // tiles_per_batch, 0, 0)),
            pl.BlockSpec((1, d), const),
            pl.BlockSpec((d, FFN_TF), lambda i, j: (0, j)),
            pl.BlockSpec((d, FFN_TF), lambda i, j: (0, j)),
            pl.BlockSpec((FFN_TF, d), lambda i, j: (j, 0)),
            pl.BlockSpec((1, d), const),
        ],
        out_specs=pl.BlockSpec((FFN_TM, d), row),
        scratch_shapes=[pltpu.VMEM((FFN_TM, d), BF16)],
        compiler_params=_params(("parallel", "arbitrary")),
        name="ffn_final" if final_norm else "ffn",
    )(x2d, mod3, g, w1, w3, w2, g_final)


IN_SPLITS = (MLA_Q_RANK, MLA_KV_RANK, MLA_ROPE, GLA_KEY, GLA_KEY, GLA_VAL, GLA_GATE_RANK, GLA_VAL,
             2048, 2048)
MAIN_ORDER = (0, 1, 3, 4, 5, 7, 8, 9)
TAIL_ORDER = (2, 6)
RELAYOUT_ROWS = 128


def _segments(order):
    starts = [sum(IN_SPLITS[:i]) for i in range(len(IN_SPLITS))]
    segs, dst = [], 0
    for i in order:
        src, n = starts[i], IN_SPLITS[i]
        if segs and segs[-1][0] + segs[-1][2] == src:
            segs[-1] = (segs[-1][0], segs[-1][1], segs[-1][2] + n)
        else:
            segs.append((src, dst, n))
        dst += n
    return segs, dst


def _relayout_kernel(w_ref, main_ref, tail_ref):
    for src, dst, n in _segments(MAIN_ORDER)[0]:
        main_ref[:, dst:dst + n] = w_ref[0, :, src:src + n].astype(BF16)
    segs, used = _segments(TAIL_ORDER)
    for src, dst, n in segs:
        tail_ref[:, dst:dst + n] = w_ref[0, :, src:src + n].astype(BF16)
    tail_ref[:, used:] = jnp.zeros((tail_ref.shape[0], TAIL_W - used), BF16)


def _relayout_w_in(w_in):
    _, d, d_in = w_in.shape
    n_main = _segments(MAIN_ORDER)[1]
    return pl.pallas_call(
        _relayout_kernel,
        out_shape=(jax.ShapeDtypeStruct((d, n_main), BF16), jax.ShapeDtypeStruct((d, TAIL_W), BF16)),
        grid=(d // RELAYOUT_ROWS,),
        in_specs=[pl.BlockSpec((1, RELAYOUT_ROWS, d_in), lambda i: (0, i, 0))],
        out_specs=(pl.BlockSpec((RELAYOUT_ROWS, n_main), lambda i: (i, 0)),
                   pl.BlockSpec((RELAYOUT_ROWS, TAIL_W), lambda i: (i, 0))),
        compiler_params=_params(("parallel",)),
        name="w_in_layout",
    )(w_in)


INP_TM = 1024
INP_TN = 1024


def _inproj_kernel(x_ref, mod_ref, g_ref, w_ref, wt_ref, o_ref, tail_ref, h_ref, *, base):
    @pl.when(pl.program_id(1) == 0)
    def _():
        h = _mod_norm(x_ref[...], g_ref[...], mod_ref, base).astype(BF16)
        h_ref[...] = h
        tail_ref[...] = _dot(h, wt_ref[...])

    o_ref[...] = _dot(h_ref[...], w_ref[...]).astype(BF16)


def _inproj(x2d, mod3, g, w_main, w_tail, *, base, seq):
    t, d = x2d.shape
    n = w_main.shape[1]
    tiles_per_batch = seq // INP_TM
    return pl.pallas_call(
        functools.partial(_inproj_kernel, base=base),
        out_shape=(jax.ShapeDtypeStruct((t, n), BF16),
                   jax.ShapeDtypeStruct((t, TAIL_W), F32)),
        grid=(t // INP_TM, n // INP_TN),
        in_specs=[
            pl.BlockSpec((INP_TM, d), lambda i, j: (i, 0)),
            pl.BlockSpec((1, N_MOD, d), lambda i, j: (i // tiles_per_batch, 0, 0)),
            pl.BlockSpec((1, d), lambda i, j: (0, 0)),
            pl.BlockSpec((d, INP_TN), lambda i, j: (0, j)),
            pl.BlockSpec((d, TAIL_W), lambda i, j: (0, 0)),
        ],
        out_specs=(pl.BlockSpec((INP_TM, INP_TN), lambda i, j: (i, j)),
                   pl.BlockSpec((INP_TM, TAIL_W), lambda i, j: (i, 0))),
        scratch_shapes=[pltpu.VMEM((INP_TM, d), BF16)],
        compiler_params=_params(("parallel", "arbitrary")),
        name="in_proj",
    )(x2d, mod3, g, w_main, w_tail)


PREP_TM = 256
ROPE_HALF = MLA_ROPE // 2


def _rope_rotate(x, cos, sin_signed):
    width = x.shape[-1]
    lane = lax.broadcasted_iota(jnp.int32, x.shape, 1)
    first_half = (lane % MLA_ROPE) < ROPE_HALF
    partner = jnp.where(first_half,
                        pltpu.roll(x, width - ROPE_HALF, 1),
                        pltpu.roll(x, ROPE_HALF, 1))
    return x * cos + partner * sin_signed


def _prep_kernel(qlat_ref, kvlat_ref, tail_ref, pos_ref, invf_ref, sgn_ref,
                 gql_ref, wuq_ref, gqn_ref, gqr_ref, gkvl_ref, wukv_ref, gkn_ref, gkr_ref,
                 grp_ref, q_ref, k_ref, v_ref):
    scale = MLA_QK ** -0.5 * LOG2E
    ang = pos_ref[...].astype(F32) * invf_ref[...]
    cos = jnp.cos(ang)
    sin_signed = jnp.sin(ang) * sgn_ref[...]
    n_rope = MLA_HEADS * MLA_ROPE
    cos_q = jnp.concatenate([cos] * (n_rope // LANES), axis=-1)
    sin_q = jnp.concatenate([sin_signed] * (n_rope // LANES), axis=-1)

    cq = _rms(qlat_ref[...].astype(F32), gql_ref[...]).astype(BF16)
    q = _dot(cq, wuq_ref[...])
    n_nope = MLA_HEADS * MLA_NOPE
    qr = q[:, n_nope:]
    ssq = _split_dot(qr * qr, grp_ref[...])
    qr = qr * lax.rsqrt(ssq * (1.0 / MLA_ROPE) + EPS) * gqr_ref[...]
    qr = _rope_rotate(qr, cos_q, sin_q) * scale

    ckv = _rms(kvlat_ref[...].astype(F32), gkvl_ref[...]).astype(BF16)
    kv = _dot(ckv, wukv_ref[...])
    v_ref[0] = kv[:, n_nope:].astype(BF16)
    kr = tail_ref[:, TAIL_KROPE:TAIL_KROPE + MLA_ROPE]
    kr = _rms(kr, gkr_ref[...])
    kr = _rope_rotate(kr, cos[:, :MLA_ROPE], sin_signed[:, :MLA_ROPE]).astype(BF16)

    for h in range(MLA_HEADS):
        qn = _rms(q[:, h * MLA_NOPE:(h + 1) * MLA_NOPE], gqn_ref[...]) * scale
        q_ref[0, h, :, :MLA_NOPE] = qn.astype(BF16)
        q_ref[0, h, :, MLA_NOPE:] = qr[:, h * MLA_ROPE:(h + 1) * MLA_ROPE].astype(BF16)
        kn = _rms(kv[:, h * MLA_NOPE:(h + 1) * MLA_NOPE], gkn_ref[...])
        k_ref[0, h, :, :MLA_NOPE] = kn.astype(BF16)
        k_ref[0, h, :, MLA_NOPE:] = kr


def _mla_prep(proj, tail, pos2d, consts, weights, *, batch, seq):
    t = proj.shape[0]
    tiles_per_batch = seq // PREP_TM
    row = lambda i: (i, 0)
    const = lambda i: (0, 0)
    (invf, sgn, grp) = consts
    (gql, wuq, gqn, gqr, gkvl, wukv, gkn, gkr) = weights
    head_out = pl.BlockSpec((1, MLA_HEADS, PREP_TM, MLA_QK),
                            lambda i: (i // tiles_per_batch, 0, i % tiles_per_batch, 0))
    full = lambda a: pl.BlockSpec(a.shape, const)
    return pl.pallas_call(
        _prep_kernel,
        out_shape=(jax.ShapeDtypeStruct((batch, MLA_HEADS, seq, MLA_QK), BF16),
                   jax.ShapeDtypeStruct((batch, MLA_HEADS, seq, MLA_QK), BF16),
                   jax.ShapeDtypeStruct((batch, seq, MLA_HEADS * MLA_V), BF16)),
        grid=(t // PREP_TM,),
        in_specs=[
            pl.BlockSpec((PREP_TM, MLA_Q_RANK), lambda i: (i, COL_QLAT // MLA_Q_RANK)),
            pl.BlockSpec((PREP_TM, MLA_KV_RANK), lambda i: (i, COL_KVLAT // MLA_KV_RANK)),
            pl.BlockSpec((PREP_TM, TAIL_W), row),
            pl.BlockSpec((PREP_TM, 1), row),
            full(invf), full(sgn),
            full(gql), full(wuq), full(gqn), full(gqr),
            full(gkvl), full(wukv), full(gkn), full(gkr),
            full(grp),
        ],
        out_specs=(head_out, head_out,
                   pl.BlockSpec((1, PREP_TM, MLA_HEADS * MLA_V),
                                lambda i: (i // tiles_per_batch, i % tiles_per_batch, 0))),
        compiler_params=_params(("parallel",)),
        name="mla_prep",
    )(proj, proj, tail, pos2d, invf, sgn, gql, wuq, gqn, gqr, gkvl, wukv, gkn, gkr, grp)


ATT_T = 1024
ATT_H = ATT_T // 2


def _attn_kernel(q_ref, k_ref, v_ref, o_ref, m_ref, l_ref, acc_ref):
    qi = pl.program_id(2)
    m_ref[...] = jnp.full(m_ref.shape, -jnp.inf, F32)
    l_ref[...] = jnp.zeros(l_ref.shape, F32)
    acc_ref[...] = jnp.zeros(acc_ref.shape, F32)
    ones = jnp.ones((ATT_T, MLA_V), BF16)

    def update(half, s, v2):
        m_prev = m_ref[half]
        m_next = jnp.maximum(m_prev, jnp.max(s, axis=-1, keepdims=True))
        alpha = jnp.exp2(m_prev - m_next)
        p = jnp.exp2(s - jnp.tile(m_next, (1, s.shape[1] // LANES)))
        pv = _dot(p.astype(BF16), v2)
        acc_ref[half] = alpha * acc_ref[half] + pv[:, :MLA_V]
        l_ref[half] = alpha * l_ref[half] + pv[:, MLA_V:]
        m_ref[half] = m_next

    def tiles(kj):
        rows = pl.ds(pl.multiple_of(kj * ATT_T, ATT_T), ATT_T)
        return k_ref[0, 0, rows, :], jnp.concatenate([v_ref[0, rows, :], ones], axis=-1)

    def body(kj, carry):
        k, v2 = tiles(kj)
        for half in range(2):
            q = q_ref[0, 0, half * ATT_H:(half + 1) * ATT_H, :]
            update(half, _dot_t1(q, k), v2)
        return carry

    lax.fori_loop(0, qi, body, 0)

    k, v2 = tiles(qi)
    r = lax.broadcasted_iota(jnp.int32, (ATT_H, ATT_H), 0)
    c = lax.broadcasted_iota(jnp.int32, (ATT_H, ATT_H), 1)
    visible = (c // CHUNK) <= (r // CHUNK)
    s0 = _dot_t1(q_ref[0, 0, :ATT_H, :], k[:ATT_H])
    update(0, jnp.where(visible, s0, -jnp.inf), v2[:ATT_H])
    s1 = _dot_t1(q_ref[0, 0, ATT_H:, :], k)
    s1 = jnp.concatenate([s1[:, :ATT_H], jnp.where(visible, s1[:, ATT_H:], -jnp.inf)], axis=-1)
    update(1, s1, v2)
    for half in range(2):
        o_ref[0, half * ATT_H:(half + 1) * ATT_H, :] = (acc_ref[half] / l_ref[half]).astype(BF16)


def _mla_attn(q, k, v):
    batch, heads, seq, _ = q.shape
    return pl.pallas_call(
        _attn_kernel,
        out_shape=jax.ShapeDtypeStruct((batch, seq, heads * MLA_V), BF16),
        grid=(batch, heads, seq // ATT_T),
        in_specs=[
            pl.BlockSpec((1, 1, ATT_T, MLA_QK), lambda b, h, i: (b, h, i, 0)),
            pl.BlockSpec((1, 1, seq, MLA_QK), lambda b, h, i: (b, h, 0, 0)),
            pl.BlockSpec((1, seq, MLA_V), lambda b, h, i: (b, 0, h)),
        ],
        out_specs=pl.BlockSpec((1, ATT_T, MLA_V), lambda b, h, i: (b, i, h)),
        scratch_shapes=[pltpu.VMEM((2, ATT_H, LANES), F32), pltpu.VMEM((2, ATT_H, LANES), F32),
                        pltpu.VMEM((2, ATT_H, MLA_V), F32)],
        compiler_params=_params(("parallel", "parallel", "arbitrary")),
        name="mla_attn",
    )(q, k, v)


GLA_L = 512
GLA_NC = GLA_L // CHUNK
GLA_HP = 2


def _gla_head(log_a, q, k, v, go, gg, state_ref, causal, tri):
    la_hi = log_a.astype(BF16)
    la_lo = (log_a - la_hi.astype(F32)).astype(BF16)
    cum = _dot(tri, jnp.concatenate([la_hi, la_lo], axis=-1))
    b_cum = cum[:, :GLA_DK] + cum[:, GLA_DK:]
    b3 = b_cum.reshape(GLA_NC, CHUNK, GLA_DK)
    b_end = b3[:, CHUNK - 1:CHUNK, :]
    b_tot = jnp.broadcast_to(b_end, b3.shape).reshape(GLA_L, GLA_DK)

    qf = q.astype(F32) * (GLA_DK ** -0.5)
    kf = k.astype(F32)
    q_dec = (qf * jnp.exp(b_cum)).astype(BF16)
    k_dec = (kf * jnp.exp(-b_cum)).astype(BF16)
    k_end = (kf * jnp.exp(b_tot - b_cum)).astype(BF16)

    attn = jnp.where(causal, _dot_t1(q_dec, k_dec), 0.0).astype(BF16)
    o_intra = _dot(attn, v)

    decay = jnp.exp(b_end)
    state = state_ref[...]
    outs = []
    for n in range(GLA_NC):
        rows = slice(n * CHUNK, (n + 1) * CHUNK)
        outs.append(o_intra[rows] + _dot_t1(q_dec[rows], state.astype(BF16)))
        state = state * decay[n] + _dot_t0(v[rows], k_end[rows])
    state_ref[...] = state

    o = jnp.concatenate(outs, axis=0)
    return (_rms(o, gg) * _silu(go.astype(F32))).astype(BF16)


def _gla_kernel(q_ref, k_ref, v_ref, go_ref, tail_ref, wg_ref, bg_ref, gg_ref,
                o_ref, state_ref):
    @pl.when(pl.program_id(2) == 0)
    def _():
        state_ref[...] = jnp.zeros_like(state_ref)

    wg = wg_ref[...]
    wg_hi = wg.astype(BF16)
    wg_lo = (wg - wg_hi.astype(F32)).astype(BF16)
    tl = tail_ref[...]
    tl_hi = tl.astype(BF16)
    tl_lo = (tl - tl_hi.astype(F32)).astype(BF16)
    z = _dot(tl_hi, wg_hi) + _dot(tl_lo, wg_hi) + _dot(tl_hi, wg_lo) + bg_ref[...]
    log_a = (jnp.minimum(z, 0.0) - jnp.log1p(jnp.exp(-jnp.abs(z)))) * (1.0 / GLA_GATE_NORMALIZER)

    r = lax.broadcasted_iota(jnp.int32, (GLA_L, GLA_L), 0)
    c = lax.broadcasted_iota(jnp.int32, (GLA_L, GLA_L), 1)
    causal = ((r // CHUNK) == (c // CHUNK)) & (c <= r)
    tri = causal.astype(BF16)

    for hh in range(GLA_HP):
        dk = slice(hh * GLA_DK, (hh + 1) * GLA_DK)
        dv = slice(hh * GLA_DV, (hh + 1) * GLA_DV)
        o_ref[:, dv] = _gla_head(log_a[:, dk], q_ref[:, dk], k_ref[:, dk], v_ref[:, dv],
                                 go_ref[:, dv], gg_ref[...], state_ref.at[hh], causal, tri)


def _gla(proj, tail, wg_pad, b_gk, g_gla, *, batch, seq):
    t = proj.shape[0]
    steps = seq // GLA_L
    tok = lambda b, h, i: b * steps + i
    wk, wv = GLA_HP * GLA_DK, GLA_HP * GLA_DV
    return pl.pallas_call(
        _gla_kernel,
        out_shape=jax.ShapeDtypeStruct((t, GLA_VAL), BF16),
        grid=(batch, GLA_HEADS // GLA_HP, steps),
        in_specs=[
            pl.BlockSpec((GLA_L, wk), lambda b, h, i: (tok(b, h, i), COL_GQ // wk + h)),
            pl.BlockSpec((GLA_L, wk), lambda b, h, i: (tok(b, h, i), COL_GK // wk + h)),
            pl.BlockSpec((GLA_L, wv), lambda b, h, i: (tok(b, h, i), COL_GV // wv + h)),
            pl.BlockSpec((GLA_L, wv), lambda b, h, i: (tok(b, h, i), COL_GOUT // wv + h)),
            pl.BlockSpec((GLA_L, TAIL_W), lambda b, h, i: (tok(b, h, i), 0)),
            pl.BlockSpec((TAIL_W, wk), lambda b, h, i: (0, h)),
            pl.BlockSpec((1, wk), lambda b, h, i: (0, h)),
            pl.BlockSpec((1, GLA_DV), lambda b, h, i: (0, 0)),
        ],
        out_specs=pl.BlockSpec((GLA_L, wv), lambda b, h, i: (tok(b, h, i), h)),
        scratch_shapes=[pltpu.VMEM((GLA_HP, GLA_DV, GLA_DK), F32)],
        compiler_params=_params(("parallel", "parallel", "arbitrary")),
        name="gla",
    )(proj, proj, proj, proj, tail, wg_pad, b_gk, g_gla)


MRG_TM = 512


def _merge_kernel(x_ref, mod_ref, a_ref, b_ref, ga_ref, gb_ref, wa_ref, wb_ref, wo_ref, o_ref,
                  *, base):
    ya = _dot(a_ref[...], wa_ref[...])
    yb = _dot(b_ref[...], wb_ref[...])
    merged = (jax.nn.sigmoid(ga_ref[...].astype(F32)) * ya
              + jax.nn.sigmoid(gb_ref[...].astype(F32)) * yb).astype(BF16)
    gate = mod_ref[0, base + 2:base + 3, :]
    o_ref[...] = x_ref[...] + gate * _dot(merged, wo_ref[...])


def _merge(x2d, mod3, attn_o, gla_o, proj, wa, wb, wo, *, base, seq):
    t, d = x2d.shape
    tiles_per_batch = seq // MRG_TM
    row = lambda i: (i, 0)
    resident = lambda a: pl.BlockSpec(a.shape, lambda i: (0, 0), pipeline_mode=pl.Buffered(1))
    return pl.pallas_call(
        functools.partial(_merge_kernel, base=base),
        out_shape=jax.ShapeDtypeStruct((t, d), F32),
        grid=(t // MRG_TM,),
        in_specs=[
            pl.BlockSpec((MRG_TM, d), row),
            pl.BlockSpec((1, N_MOD, d), lambda i: (i // tiles_per_batch, 0, 0)),
            pl.BlockSpec((MRG_TM, attn_o.shape[1]), row),
            pl.BlockSpec((MRG_TM, gla_o.shape[1]), row),
            pl.BlockSpec((MRG_TM, d), lambda i: (i, COL_GATE_A // d)),
            pl.BlockSpec((MRG_TM, d), lambda i: (i, COL_GATE_A // d + 1)),
            resident(wa), resident(wb), resident(wo),
        ],
        out_specs=pl.BlockSpec((MRG_TM, d), row),
        compiler_params=_params(("parallel",)),
        name="merge",
    )(x2d, mod3, attn_o, gla_o, proj, proj, wa, wb, wo)


def _layer(x2d, mod3, pos2d, p, *, batch, seq):
    d = x2d.shape[1]
    assert IN_SPLITS[-1] == d and sum(IN_SPLITS) == p["w_in"].shape[1]
    w_main, w_tail = _relayout_w_in(p["w_in"][None])
    wuq = p["w_uq"].reshape(MLA_Q_RANK, MLA_HEADS, MLA_QK)
    wuq = jnp.concatenate([wuq[:, :, :MLA_NOPE].reshape(MLA_Q_RANK, -1),
                           wuq[:, :, MLA_NOPE:].reshape(MLA_Q_RANK, -1)], axis=1).astype(BF16)
    wukv = p["w_ukv"].reshape(MLA_KV_RANK, MLA_HEADS, MLA_NOPE + MLA_V)
    wukv = jnp.concatenate([wukv[:, :, :MLA_NOPE].reshape(MLA_KV_RANK, -1),
                            wukv[:, :, MLA_NOPE:].reshape(MLA_KV_RANK, -1)], axis=1).astype(BF16)
    wg_pad = jnp.zeros((TAIL_W, GLA_KEY), F32).at[TAIL_GLR:TAIL_GLR + GLA_GATE_RANK].set(p["w_gk_up"])

    inv_freq = ROPE_THETA ** (-jnp.arange(0, MLA_ROPE, 2, dtype=F32) / MLA_ROPE)
    invf = jnp.tile(inv_freq, LANES // ROPE_HALF).reshape(1, LANES)
    sgn = jnp.tile(jnp.concatenate([-jnp.ones(ROPE_HALF, F32), jnp.ones(ROPE_HALF, F32)]),
                   LANES // MLA_ROPE).reshape(1, LANES)
    n_rope = MLA_HEADS * MLA_ROPE
    lane = jnp.arange(n_rope)
    grp = (lane[:, None] // MLA_ROPE == lane[None, :] // MLA_ROPE).astype(BF16)

    row = lambda a: a.reshape(1, -1)
    x2d = _ffn(x2d, mod3, row(p["g_ffn1"]), p["w1_a"].astype(BF16), p["w3_a"].astype(BF16),
               p["w2_a"].astype(BF16), row(p["g_final"]), base=0, final_norm=False, seq=seq)
    proj, tail = _inproj(x2d, mod3, row(p["g_mix"]), w_main, w_tail, base=3, seq=seq)
    q, k, v = _mla_prep(
        proj, tail, pos2d, (invf, sgn, grp),
        (row(p["g_q_lat"]), wuq, row(p["g_qn"]), row(jnp.tile(p["g_qr"], MLA_HEADS)),
         row(p["g_kv_lat"]), wukv, row(p["g_kn"]), row(p["g_kr"])),
        batch=batch, seq=seq)
    attn_o = _mla_attn(q, k, v).reshape(batch * seq, MLA_HEADS * MLA_V)
    gla_o = _gla(proj, tail, wg_pad, row(p["b_gk"]), row(p["g_gla"]), batch=batch, seq=seq)
    x2d = _merge(x2d, mod3, attn_o, gla_o, proj, p["w_proj_a"].astype(BF16),
                 p["w_proj_b"].astype(BF16), p["w_out"].astype(BF16), base=3, seq=seq)
    x2d = _ffn(x2d, mod3, row(p["g_ffn2"]), p["w1_b"].astype(BF16), p["w3_b"].astype(BF16),
               p["w2_b"].astype(BF16), row(p["g_final"]), base=6, final_norm=True, seq=seq)
    return x2d


def kernel(x, c, positions, w_ada, b_ada, g_ffn1, w1_a, w3_a, w2_a, g_mix, w_in, g_q_lat, w_uq, g_qn, g_qr, g_kv_lat, w_ukv, g_kn, g_kr, w_gk_up, b_gk, g_gla, w_proj_a, w_proj_b, w_out, g_ffn2, w1_b, w3_b, w2_b, g_final):
    batch, seq, d = x.shape
    depth = w_ada.shape[0]
    assert depth == 1, "the final norm is fused into the last FFN of a single layer"
    names = ("g_ffn1", "w1_a", "w3_a", "w2_a", "g_mix", "w_in", "g_q_lat", "w_uq", "g_qn", "g_qr",
             "g_kv_lat", "w_ukv", "g_kn", "g_kr", "w_gk_up", "b_gk", "g_gla", "w_proj_a", "w_proj_b",
             "w_out", "g_ffn2", "w1_b", "w3_b", "w2_b", "g_final")
    stacked = (g_ffn1, w1_a, w3_a, w2_a, g_mix, w_in, g_q_lat, w_uq, g_qn, g_qr, g_kv_lat, w_ukv,
               g_kn, g_kr, w_gk_up, b_gk, g_gla, w_proj_a, w_proj_b, w_out, g_ffn2, w1_b, w3_b, w2_b,
               g_final)
    x2d = x.reshape(batch * seq, d)
    pos2d = positions.reshape(batch * seq, 1)
    p = {n: a[0] for n, a in zip(names, stacked)}
    mod3 = _adaln(c, w_ada[0], b_ada[0]).reshape(batch, N_MOD, d)
    x2d = _layer(x2d, mod3, pos2d, p, batch=batch, seq=seq)
    return x2d.reshape(batch, seq, d)
```

```python
import functools

import jax
import jax.numpy as jnp
from jax import lax
from jax.experimental import pallas as pl
from jax.experimental.pallas import tpu as pltpu

F32 = jnp.float32
BF16 = jnp.bfloat16

EPS = 1e-6
LOG2E = 1.4426950408889634
CHUNK = 64
MLA_HEADS = 8
MLA_Q_RANK = 512
MLA_KV_RANK = 512
MLA_NOPE = 128
MLA_ROPE = 64
MLA_V = 128
MLA_QK = MLA_NOPE + MLA_ROPE
ROPE_THETA = 10000.0
GLA_HEADS = 4
GLA_DK = 128
GLA_DV = 256
GLA_GATE_RANK = 16
GLA_GATE_NORMALIZER = 16.0
GLA_KEY = GLA_HEADS * GLA_DK
GLA_VAL = GLA_HEADS * GLA_DV
N_MOD = 9

LANES = 128
SUBLANES = 8
VMEM_LIMIT = 56 * 1024 * 1024

COL_QLAT = 0
COL_KVLAT = COL_QLAT + MLA_Q_RANK
COL_GQ = COL_KVLAT + MLA_KV_RANK
COL_GK = COL_GQ + GLA_KEY
COL_GV = COL_GK + GLA_KEY
COL_GOUT = COL_GV + GLA_VAL
COL_GATE_A = COL_GOUT + GLA_VAL
TAIL_W = LANES
TAIL_KROPE = 0
TAIL_GLR = MLA_ROPE


def _params(sem):
    return pltpu.CompilerParams(dimension_semantics=sem, vmem_limit_bytes=VMEM_LIMIT)


def _dot(a, b):
    return jnp.dot(a, b, preferred_element_type=F32)


def _dot_t0(a, b):
    return lax.dot_general(a, b, (((0,), (0,)), ((), ())), preferred_element_type=F32)


def _dot_t1(a, b):
    return lax.dot_general(a, b, (((1,), (1,)), ((), ())), preferred_element_type=F32)


def _split_dot(x, w_bf16):
    hi = x.astype(BF16)
    lo = (x - hi.astype(F32)).astype(BF16)
    return _dot(hi, w_bf16) + _dot(lo, w_bf16)


def _split_dot_t0(x, w_bf16):
    hi = x.astype(BF16)
    lo = (x - hi.astype(F32)).astype(BF16)
    return _dot_t0(hi, w_bf16) + _dot_t0(lo, w_bf16)


def _silu(x):
    return x * jax.nn.sigmoid(x)


def _rms(x, g):
    ms = jnp.mean(x * x, axis=-1, keepdims=True)
    return x * lax.rsqrt(ms + EPS) * g


ADA_TN = 1024
ADA_RC = 64


def _adaln_kernel(ct_ref, w_ref, b_ref, o_ref, s_ref):
    d, nb = ct_ref.shape
    tn = w_ref.shape[1]

    @pl.when(pl.program_id(0) == 0)
    def _():
        s = _silu(ct_ref[...])
        for b in range(nb):
            s_ref[b] = jnp.broadcast_to(s[:, b:b + 1], (d, LANES))

    def body(k, acc):
        rows = pl.ds(pl.multiple_of(k * ADA_RC, ADA_RC), ADA_RC)
        out = []
        for b in range(nb):
            sb = s_ref[b, rows, :]
            cols = []
            for col in range(tn // LANES):
                p = w_ref[rows, col * LANES:(col + 1) * LANES] * sb
                cols.append(p.reshape(ADA_RC // SUBLANES, SUBLANES, LANES).sum(axis=0))
            out.append(acc[b] + jnp.concatenate(cols, axis=-1))
        return tuple(out)

    init = tuple(jnp.zeros((SUBLANES, tn), F32) for _ in range(nb))
    acc = lax.fori_loop(0, d // ADA_RC, body, init, unroll=2)
    rows = [jnp.sum(a, axis=0, keepdims=True) for a in acc]
    o_ref[...] = jnp.concatenate(rows, axis=0) + b_ref[...]


def _adaln(c, w_ada, b_ada):
    nb, d = c.shape
    n = w_ada.shape[1]
    return pl.pallas_call(
        _adaln_kernel,
        out_shape=jax.ShapeDtypeStruct((nb, n), F32),
        grid=(n // ADA_TN,),
        in_specs=[
            pl.BlockSpec((d, nb), lambda j: (0, 0)),
            pl.BlockSpec((d, ADA_TN), lambda j: (0, j)),
            pl.BlockSpec((1, ADA_TN), lambda j: (0, j)),
        ],
        out_specs=pl.BlockSpec((nb, ADA_TN), lambda j: (0, j)),
        scratch_shapes=[pltpu.VMEM((nb, d, LANES), F32)],
        compiler_params=_params(("arbitrary",)),
        name="adaln",
    )(c.T, w_ada, b_ada.reshape(1, n))


def _mod_norm(x, g, mod_ref, base):
    shift = mod_ref[0, base:base + 1, :]
    scale = mod_ref[0, base + 1:base + 2, :]
    return _rms(x, g) * (1.0 + scale) + shift


FFN_TM = 512
FFN_TF = 512
FFN_HEAD_TF = 256


def _ffn_step(x_ref, mod_ref, g_ref, gf_ref, w1, w3, w2, o_ref, h_ref, *, base, final_norm):
    j = pl.program_id(1)

    @pl.when(j == 0)
    def _():
        h_ref[...] = _mod_norm(x_ref[...], g_ref[...], mod_ref, base).astype(BF16)
        o_ref[...] = jnp.zeros_like(o_ref)

    h = h_ref[...]
    a = _dot(h, w1)
    b = _dot(h, w3)
    p = (_silu(a) * b).astype(BF16)
    o_ref[...] += _dot(p, w2)

    @pl.when(j == pl.num_programs(1) - 1)
    def _():
        gate = mod_ref[0, base + 2:base + 3, :]
        y = x_ref[...] + 0.5 * gate * o_ref[...]
        if final_norm:
            y = _rms(y, gf_ref[...])
        o_ref[...] = y


def _ffn_head_kernel(x_ref, mod_ref, g_ref, w1_ref, w3_ref, w2_ref, gf_ref,
                     o_ref, w1b_ref, w3b_ref, w2b_ref, h_ref, **kw):
    w1 = w1_ref[0].astype(BF16)
    w3 = w3_ref[0].astype(BF16)
    w2 = w2_ref[0].astype(BF16)
    w1b_ref[...] = w1
    w3b_ref[...] = w3
    w2b_ref[...] = w2
    _ffn_step(x_ref, mod_ref, g_ref, gf_ref, w1, w3, w2, o_ref, h_ref, **kw)


def _ffn_kernel(x_ref, mod_ref, g_ref, w1_ref, w3_ref, w2_ref, gf_ref, y0_ref, o_ref, h_ref, **kw):
    i = pl.program_id(0)

    @pl.when(i > 0)
    def _():
        _ffn_step(x_ref, mod_ref, g_ref, gf_ref, w1_ref[...], w3_ref[...], w2_ref[...],
                  o_ref, h_ref, **kw)

    @pl.when((i == 0) & (pl.program_id(1) == pl.num_programs(1) - 1))
    def _():
        o_ref[...] = y0_ref[...]


def _ffn(x2d, mod3, g, w1, w3, w2, g_final, *, base, final_norm, seq):
    t, d = x2d.shape
    dff = w1.shape[2]
    tiles_per_batch = seq // FFN_TM
    const = lambda i, j: (0, 0)
    kw = dict(base=base, final_norm=final_norm)
    tag = "_final" if final_norm else ""
    common = [
        pl.BlockSpec((1, N_MOD, d), lambda i, j: (i // tiles_per_batch, 0, 0)),
        pl.BlockSpec((1, d), const),
    ]
    hf = FFN_HEAD_TF
    y0, w1b, w3b, w2b = pl.pallas_call(
        functools.partial(_ffn_head_kernel, **kw),
        out_shape=(jax.ShapeDtypeStruct((FFN_TM, d), F32),
                   jax.ShapeDtypeStruct((d, dff), BF16), jax.ShapeDtypeStruct((d, dff), BF16),
                   jax.ShapeDtypeStruct((dff, d), BF16)),
        grid=(1, dff // hf),
        in_specs=[pl.BlockSpec((FFN_TM, d), const)] + common + [
            pl.BlockSpec((1, d, hf), lambda i, j: (0, 0, j)),
            pl.BlockSpec((1, d, hf), lambda i, j: (0, 0, j)),
            pl.BlockSpec((1, hf, d), lambda i, j: (0, j, 0)),
            pl.BlockSpec((1, d), const),
        ],
        out_specs=(pl.BlockSpec((FFN_TM, d), const),
                   pl.BlockSpec((d, hf), lambda i, j: (0, j)),
                   pl.BlockSpec((d, hf), lambda i, j: (0, j)),
                   pl.BlockSpec((hf, d), lambda i, j: (j, 0))),
        scratch_shapes=[pltpu.VMEM((FFN_TM, d), BF16)],
        compiler_params=_params(("arbitrary", "arbitrary")),
        name="ffn_head" + tag,
    )(x2d, mod3, g, w1, w3, w2, g_final)

    wcol = lambda i, j: (0, jnp.where(i == 0, 0, j))
    wrow = lambda i, j: (jnp.where(i == 0, 0, j), 0)
    return pl.pallas_call(
        functools.partial(_ffn_kernel, **kw),
        out_shape=jax.ShapeDtypeStruct((t, d), F32),
        grid=(t // FFN_TM, dff // FFN_TF),
        in_specs=[pl.BlockSpec((FFN_TM, d), lambda i, j: (i, 0))] + common + [
            pl.BlockSpec((d, FFN_TF), wcol),
            pl.BlockSpec((d, FFN_TF), wcol),
            pl.BlockSpec((FFN_TF, d), wrow),
            pl.BlockSpec((1, d), const),
            pl.BlockSpec((FFN_TM, d), const),
        ],
        out_specs=pl.BlockSpec((FFN_TM, d), lambda i, j: (i, 0)),
        scratch_shapes=[pltpu.VMEM((FFN_TM, d), BF16)],
        compiler_params=_params(("arbitrary", "arbitrary")),
        name="ffn" + tag,
    )(x2d, mod3, g, w1b, w3b, w2b, g_final, y0)


IN_SPLITS = (MLA_Q_RANK, MLA_KV_RANK, MLA_ROPE, GLA_KEY, GLA_KEY, GLA_VAL, GLA_GATE_RANK, GLA_VAL,
             2048, 2048)
MAIN_ORDER = (0, 1, 3, 4, 5, 7, 8, 9)
TAIL_ORDER = (2, 6)
RELAYOUT_ROWS = 512


def _segments(order):
    starts = [sum(IN_SPLITS[:i]) for i in range(len(IN_SPLITS))]
    segs, dst = [], 0
    for i in order:
        src, n = starts[i], IN_SPLITS[i]
        if segs and segs[-1][0] + segs[-1][2] == src:
            segs[-1] = (segs[-1][0], segs[-1][1], segs[-1][2] + n)
        else:
            segs.append((src, dst, n))
        dst += n
    return segs, dst


def _relayout_kernel(lo_ref, hi_ref, kr_ref, glr_ref, main_ref, tail_ref):
    r = pl.program_id(0)
    rows = RELAYOUT_ROWS
    for src, dst, n in _segments(MAIN_ORDER)[0]:
        shift = src - dst

        @pl.when((r >= dst // rows) & (r < (dst + n) // rows))
        def _(shift=shift):
            main_ref[:rows - shift, :] = lo_ref[shift:, :].astype(BF16)
            if shift:
                main_ref[rows - shift:, :] = hi_ref[:shift, :].astype(BF16)

    tail_ref[:MLA_ROPE, :] = kr_ref[...].astype(BF16)
    tail_ref[MLA_ROPE:MLA_ROPE + GLA_GATE_RANK, :] = glr_ref[...].astype(BF16)
    pad = TAIL_W - MLA_ROPE - GLA_GATE_RANK
    tail_ref[MLA_ROPE + GLA_GATE_RANK:, :] = jnp.zeros((pad, tail_ref.shape[1]), BF16)


def _relayout_w_in(w_in_t):
    d_in, d = w_in_t.shape
    segs, n_main = _segments(MAIN_ORDER)
    rows = RELAYOUT_ROWS
    assert all(dst % rows == 0 and n % rows == 0 and 0 <= src - dst <= LANES and (src - dst) % 16 == 0
               for src, dst, n in segs)
    (kr_src, _, _), (glr_src, _, _) = _segments(TAIL_ORDER)[0]
    assert kr_src % MLA_ROPE == 0 and glr_src % GLA_GATE_RANK == 0
    return pl.pallas_call(
        _relayout_kernel,
        out_shape=(jax.ShapeDtypeStruct((n_main, d), BF16), jax.ShapeDtypeStruct((TAIL_W, d), BF16)),
        grid=(n_main // rows,),
        in_specs=[pl.BlockSpec((rows, d), lambda r: (r, 0)),
                  pl.BlockSpec((LANES, d), lambda r: ((r + 1) * (rows // LANES), 0)),
                  pl.BlockSpec((MLA_ROPE, d), lambda r: (kr_src // MLA_ROPE, 0)),
                  pl.BlockSpec((GLA_GATE_RANK, d), lambda r: (glr_src // GLA_GATE_RANK, 0))],
        out_specs=(pl.BlockSpec((rows, d), lambda r: (r, 0)),
                   pl.BlockSpec((TAIL_W, d), lambda r: (0, 0))),
        compiler_params=_params(("arbitrary",)),
        name="w_in_layout",
    )(w_in_t, w_in_t, w_in_t, w_in_t)


INP_TM = 1024
INP_TN = 1024


def _inproj_kernel(x_ref, mod_ref, g_ref, w_ref, wt_ref, o_ref, tail_ref, h_ref, *, base):
    @pl.when(pl.program_id(1) == 0)
    def _():
        h = _mod_norm(x_ref[...], g_ref[...], mod_ref, base).astype(BF16)
        h_ref[...] = h
        tail_ref[...] = _dot_t1(h, wt_ref[...])

    o_ref[...] = _dot_t1(h_ref[...], w_ref[...]).astype(BF16)


def _inproj(x2d, mod3, g, w_main, w_tail, *, base, seq):
    t, d = x2d.shape
    n = w_main.shape[0]
    tiles_per_batch = seq // INP_TM
    return pl.pallas_call(
        functools.partial(_inproj_kernel, base=base),
        out_shape=(jax.ShapeDtypeStruct((t, n), BF16),
                   jax.ShapeDtypeStruct((t, TAIL_W), F32)),
        grid=(t // INP_TM, n // INP_TN),
        in_specs=[
            pl.BlockSpec((INP_TM, d), lambda i, j: (i, 0)),
            pl.BlockSpec((1, N_MOD, d), lambda i, j: (i // tiles_per_batch, 0, 0)),
            pl.BlockSpec((1, d), lambda i, j: (0, 0)),
            pl.BlockSpec((INP_TN, d), lambda i, j: (j, 0)),
            pl.BlockSpec((TAIL_W, d), lambda i, j: (0, 0)),
        ],
        out_specs=(pl.BlockSpec((INP_TM, INP_TN), lambda i, j: (i, j)),
                   pl.BlockSpec((INP_TM, TAIL_W), lambda i, j: (i, 0))),
        scratch_shapes=[pltpu.VMEM((INP_TM, d), BF16)],
        compiler_params=_params(("parallel", "arbitrary")),
        name="in_proj",
    )(x2d, mod3, g, w_main, w_tail)


PREP_TM = 256
ROPE_HALF = MLA_ROPE // 2


def _rope_rotate(x, cos, sin_signed):
    width = x.shape[-1]
    lane = lax.broadcasted_iota(jnp.int32, x.shape, 1)
    first_half = (lane % MLA_ROPE) < ROPE_HALF
    partner = jnp.where(first_half,
                        pltpu.roll(x, width - ROPE_HALF, 1),
                        pltpu.roll(x, ROPE_HALF, 1))
    return x * cos + partner * sin_signed


def _prep_kernel(qlat_ref, kvlat_ref, tail_ref, pos_ref, invf_ref, sgn_ref,
                 gql_ref, wuq_ref, gqn_ref, gqr_ref, gkvl_ref, wukv_ref, gkn_ref, gkr_ref,
                 grp_ref, q_ref, k_ref, v_ref):
    scale = MLA_QK ** -0.5 * LOG2E
    ang = pos_ref[...].astype(F32) * invf_ref[...]
    cos = jnp.cos(ang)
    sin_signed = jnp.sin(ang) * sgn_ref[...]
    n_rope = MLA_HEADS * MLA_ROPE
    cos_q = jnp.concatenate([cos] * (n_rope // LANES), axis=-1)
    sin_q = jnp.concatenate([sin_signed] * (n_rope // LANES), axis=-1)

    cq = _rms(qlat_ref[...].astype(F32), gql_ref[...]).astype(BF16)
    q = _dot(cq, wuq_ref[...])
    n_nope = MLA_HEADS * MLA_NOPE
    qr = q[:, n_nope:]
    ssq = _split_dot(qr * qr, grp_ref[...])
    qr = qr * lax.rsqrt(ssq * (1.0 / MLA_ROPE) + EPS) * gqr_ref[...]
    qr = _rope_rotate(qr, cos_q, sin_q) * scale

    ckv = _rms(kvlat_ref[...].astype(F32), gkvl_ref[...]).astype(BF16)
    kv = _dot(ckv, wukv_ref[...])
    v_ref[0] = kv[:, n_nope:].astype(BF16)
    kr = tail_ref[:, TAIL_KROPE:TAIL_KROPE + MLA_ROPE]
    kr = _rms(kr, gkr_ref[...])
    kr = _rope_rotate(kr, cos[:, :MLA_ROPE], sin_signed[:, :MLA_ROPE]).astype(BF16)

    for h in range(MLA_HEADS):
        qn = _rms(q[:, h * MLA_NOPE:(h + 1) * MLA_NOPE], gqn_ref[...]) * scale
        q_ref[0, h, :, :MLA_NOPE] = qn.astype(BF16)
        q_ref[0, h, :, MLA_NOPE:] = qr[:, h * MLA_ROPE:(h + 1) * MLA_ROPE].astype(BF16)
        kn = _rms(kv[:, h * MLA_NOPE:(h + 1) * MLA_NOPE], gkn_ref[...])
        k_ref[0, h, :, :MLA_NOPE] = kn.astype(BF16)
        k_ref[0, h, :, MLA_NOPE:] = kr


def _mla_prep(proj, tail, pos2d, consts, weights, *, batch, seq):
    t = proj.shape[0]
    tiles_per_batch = seq // PREP_TM
    row = lambda i: (i, 0)
    const = lambda i: (0, 0)
    (invf, sgn, grp) = consts
    (gql, wuq, gqn, gqr, gkvl, wukv, gkn, gkr) = weights
    head_out = pl.BlockSpec((1, MLA_HEADS, PREP_TM, MLA_QK),
                            lambda i: (i // tiles_per_batch, 0, i % tiles_per_batch, 0))
    full = lambda a: pl.BlockSpec(a.shape, const)
    return pl.pallas_call(
        _prep_kernel,
        out_shape=(jax.ShapeDtypeStruct((batch, MLA_HEADS, seq, MLA_QK), BF16),
                   jax.ShapeDtypeStruct((batch, MLA_HEADS, seq, MLA_QK), BF16),
                   jax.ShapeDtypeStruct((batch, seq, MLA_HEADS * MLA_V), BF16)),
        grid=(t // PREP_TM,),
        in_specs=[
            pl.BlockSpec((PREP_TM, MLA_Q_RANK), lambda i: (i, COL_QLAT // MLA_Q_RANK)),
            pl.BlockSpec((PREP_TM, MLA_KV_RANK), lambda i: (i, COL_KVLAT // MLA_KV_RANK)),
            pl.BlockSpec((PREP_TM, TAIL_W), row),
            pl.BlockSpec((PREP_TM, 1), row),
            full(invf), full(sgn),
            full(gql), full(wuq), full(gqn), full(gqr),
            full(gkvl), full(wukv), full(gkn), full(gkr),
            full(grp),
        ],
        out_specs=(head_out, head_out,
                   pl.BlockSpec((1, PREP_TM, MLA_HEADS * MLA_V),
                                lambda i: (i // tiles_per_batch, i % tiles_per_batch, 0))),
        compiler_params=_params(("parallel",)),
        name="mla_prep",
    )(proj, proj, tail, pos2d, invf, sgn, gql, wuq, gqn, gqr, gkvl, wukv, gkn, gkr, grp)


ATT_T = 1024
ATT_H = ATT_T // 2


def _attn_kernel(q_ref, k_ref, v_ref, o_ref, m_ref, l_ref, acc_ref):
    qi = pl.program_id(2)
    m_ref[...] = jnp.full(m_ref.shape, -jnp.inf, F32)
    l_ref[...] = jnp.zeros(l_ref.shape, F32)
    acc_ref[...] = jnp.zeros(acc_ref.shape, F32)
    ones = jnp.ones((ATT_T, MLA_V), BF16)

    def update(half, s, v2):
        m_prev = m_ref[half]
        m_next = jnp.maximum(m_prev, jnp.max(s, axis=-1, keepdims=True))
        alpha = jnp.exp2(m_prev - m_next)
        p = jnp.exp2(s - jnp.tile(m_next, (1, s.shape[1] // LANES)))
        pv = _dot(p.astype(BF16), v2)
        acc_ref[half] = alpha * acc_ref[half] + pv[:, :MLA_V]
        l_ref[half] = alpha * l_ref[half] + pv[:, MLA_V:]
        m_ref[half] = m_next

    def tiles(kj):
        rows = pl.ds(pl.multiple_of(kj * ATT_T, ATT_T), ATT_T)
        return k_ref[0, 0, rows, :], jnp.concatenate([v_ref[0, rows, :], ones], axis=-1)

    def body(kj, carry):
        k, v2 = tiles(kj)
        for half in range(2):
            q = q_ref[0, 0, half * ATT_H:(half + 1) * ATT_H, :]
            update(half, _dot_t1(q, k), v2)
        return carry

    lax.fori_loop(0, qi, body, 0)

    k, v2 = tiles(qi)
    r = lax.broadcasted_iota(jnp.int32, (ATT_H, ATT_H), 0)
    c = lax.broadcasted_iota(jnp.int32, (ATT_H, ATT_H), 1)
    visible = (c // CHUNK) <= (r // CHUNK)
    s0 = _dot_t1(q_ref[0, 0, :ATT_H, :], k[:ATT_H])
    update(0, jnp.where(visible, s0, -jnp.inf), v2[:ATT_H])
    s1 = _dot_t1(q_ref[0, 0, ATT_H:, :], k)
    s1 = jnp.concatenate([s1[:, :ATT_H], jnp.where(visible, s1[:, ATT_H:], -jnp.inf)], axis=-1)
    update(1, s1, v2)
    for half in range(2):
        o_ref[0, half * ATT_H:(half + 1) * ATT_H, :] = (acc_ref[half] / l_ref[half]).astype(BF16)


def _mla_attn(q, k, v):
    batch, heads, seq, _ = q.shape
    return pl.pallas_call(
        _attn_kernel,
        out_shape=jax.ShapeDtypeStruct((batch, seq, heads * MLA_V), BF16),
        grid=(batch, heads, seq // ATT_T),
        in_specs=[
            pl.BlockSpec((1, 1, ATT_T, MLA_QK), lambda b, h, i: (b, h, i, 0)),
            pl.BlockSpec((1, 1, seq, MLA_QK), lambda b, h, i: (b, h, 0, 0)),
            pl.BlockSpec((1, seq, MLA_V), lambda b, h, i: (b, 0, h)),
        ],
        out_specs=pl.BlockSpec((1, ATT_T, MLA_V), lambda b, h, i: (b, i, h)),
        scratch_shapes=[pltpu.VMEM((2, ATT_H, LANES), F32), pltpu.VMEM((2, ATT_H, LANES), F32),
                        pltpu.VMEM((2, ATT_H, MLA_V), F32)],
        compiler_params=_params(("parallel", "parallel", "arbitrary")),
        name="mla_attn",
    )(q, k, v)


GLA_L = 512
GLA_NC = GLA_L // CHUNK
GLA_HP = 2


def _gla_head(log_a, q, k, v, go, gg, state_ref, causal, tri):
    la_hi = log_a.astype(BF16)
    la_lo = (log_a - la_hi.astype(F32)).astype(BF16)
    cum = _dot(tri, jnp.concatenate([la_hi, la_lo], axis=-1))
    b_cum = cum[:, :GLA_DK] + cum[:, GLA_DK:]
    b3 = b_cum.reshape(GLA_NC, CHUNK, GLA_DK)
    b_end = b3[:, CHUNK - 1:CHUNK, :]
    b_tot = jnp.broadcast_to(b_end, b3.shape).reshape(GLA_L, GLA_DK)

    qf = q.astype(F32) * (GLA_DK ** -0.5)
    kf = k.astype(F32)
    q_dec = (qf * jnp.exp(b_cum)).astype(BF16)
    k_dec = (kf * jnp.exp(-b_cum)).astype(BF16)
    k_end = (kf * jnp.exp(b_tot - b_cum)).astype(BF16)

    attn = jnp.where(causal, _dot_t1(q_dec, k_dec), 0.0).astype(BF16)
    o_intra = _dot(attn, v)

    decay = jnp.exp(b_end)
    state = state_ref[...]
    outs = []
    for n in range(GLA_NC):
        rows = slice(n * CHUNK, (n + 1) * CHUNK)
        outs.append(o_intra[rows] + _dot_t1(q_dec[rows], state.astype(BF16)))
        state = state * decay[n] + _dot_t0(v[rows], k_end[rows])
    state_ref[...] = state

    o = jnp.concatenate(outs, axis=0)
    return (_rms(o, gg) * _silu(go.astype(F32))).astype(BF16)


def _gla_kernel(q_ref, k_ref, v_ref, go_ref, tail_ref, wg_ref, bg_ref, gg_ref,
                o_ref, state_ref):
    @pl.when(pl.program_id(2) == 0)
    def _():
        state_ref[...] = jnp.zeros_like(state_ref)

    wg = wg_ref[...]
    wg_hi = wg.astype(BF16)
    wg_lo = (wg - wg_hi.astype(F32)).astype(BF16)
    tl = tail_ref[...]
    tl_hi = tl.astype(BF16)
    tl_lo = (tl - tl_hi.astype(F32)).astype(BF16)
    z = _dot(tl_hi, wg_hi) + _dot(tl_lo, wg_hi) + _dot(tl_hi, wg_lo) + bg_ref[...]
    log_a = (jnp.minimum(z, 0.0) - jnp.log1p(jnp.exp(-jnp.abs(z)))) * (1.0 / GLA_GATE_NORMALIZER)

    r = lax.broadcasted_iota(jnp.int32, (GLA_L, GLA_L), 0)
    c = lax.broadcasted_iota(jnp.int32, (GLA_L, GLA_L), 1)
    causal = ((r // CHUNK) == (c // CHUNK)) & (c <= r)
    tri = causal.astype(BF16)

    for hh in range(GLA_HP):
        dk = slice(hh * GLA_DK, (hh + 1) * GLA_DK)
        dv = slice(hh * GLA_DV, (hh + 1) * GLA_DV)
        o_ref[:, dv] = _gla_head(log_a[:, dk], q_ref[:, dk], k_ref[:, dk], v_ref[:, dv],
                                 go_ref[:, dv], gg_ref[...], state_ref.at[hh], causal, tri)


def _gla(proj, tail, wg_pad, b_gk, g_gla, *, batch, seq):
    t = proj.shape[0]
    steps = seq // GLA_L
    tok = lambda b, h, i: b * steps + i
    wk, wv = GLA_HP * GLA_DK, GLA_HP * GLA_DV
    return pl.pallas_call(
        _gla_kernel,
        out_shape=jax.ShapeDtypeStruct((t, GLA_VAL), BF16),
        grid=(batch, GLA_HEADS // GLA_HP, steps),
        in_specs=[
            pl.BlockSpec((GLA_L, wk), lambda b, h, i: (tok(b, h, i), COL_GQ // wk + h)),
            pl.BlockSpec((GLA_L, wk), lambda b, h, i: (tok(b, h, i), COL_GK // wk + h)),
            pl.BlockSpec((GLA_L, wv), lambda b, h, i: (tok(b, h, i), COL_GV // wv + h)),
            pl.BlockSpec((GLA_L, wv), lambda b, h, i: (tok(b, h, i), COL_GOUT // wv + h)),
            pl.BlockSpec((GLA_L, TAIL_W), lambda b, h, i: (tok(b, h, i), 0)),
            pl.BlockSpec((TAIL_W, wk), lambda b, h, i: (0, h)),
            pl.BlockSpec((1, wk), lambda b, h, i: (0, h)),
            pl.BlockSpec((1, GLA_DV), lambda b, h, i: (0, 0)),
        ],
        out_specs=pl.BlockSpec((GLA_L, wv), lambda b, h, i: (tok(b, h, i), h)),
        scratch_shapes=[pltpu.VMEM((GLA_HP, GLA_DV, GLA_DK), F32)],
        compiler_params=_params(("parallel", "parallel", "arbitrary")),
        name="gla",
    )(proj, proj, proj, proj, tail, wg_pad, b_gk, g_gla)


MRG_TM = 512


def _merge_kernel(x_ref, mod_ref, a_ref, b_ref, ga_ref, gb_ref, wa_ref, wb_ref, wo_ref, o_ref,
                  *, base):
    ya = _dot(a_ref[...], wa_ref[...])
    yb = _dot(b_ref[...], wb_ref[...])
    merged = (jax.nn.sigmoid(ga_ref[...].astype(F32)) * ya
              + jax.nn.sigmoid(gb_ref[...].astype(F32)) * yb).astype(BF16)
    gate = mod_ref[0, base + 2:base + 3, :]
    o_ref[...] = x_ref[...] + gate * _dot(merged, wo_ref[...])


def _merge(x2d, mod3, attn_o, gla_o, proj, wa, wb, wo, *, base, seq):
    t, d = x2d.shape
    tiles_per_batch = seq // MRG_TM
    row = lambda i: (i, 0)
    resident = lambda a: pl.BlockSpec(a.shape, lambda i: (0, 0), pipeline_mode=pl.Buffered(1))
    return pl.pallas_call(
        functools.partial(_merge_kernel, base=base),
        out_shape=jax.ShapeDtypeStruct((t, d), F32),
        grid=(t // MRG_TM,),
        in_specs=[
            pl.BlockSpec((MRG_TM, d), row),
            pl.BlockSpec((1, N_MOD, d), lambda i: (i // tiles_per_batch, 0, 0)),
            pl.BlockSpec((MRG_TM, attn_o.shape[1]), row),
            pl.BlockSpec((MRG_TM, gla_o.shape[1]), row),
            pl.BlockSpec((MRG_TM, d), lambda i: (i, COL_GATE_A // d)),
            pl.BlockSpec((MRG_TM, d), lambda i: (i, COL_GATE_A // d + 1)),
            resident(wa), resident(wb), resident(wo),
        ],
        out_specs=pl.BlockSpec((MRG_TM, d), row),
        compiler_params=_params(("parallel",)),
        name="merge",
    )(x2d, mod3, attn_o, gla_o, proj, proj, wa, wb, wo)


def _layer(x2d, mod3, pos2d, p, *, batch, seq):
    d = x2d.shape[1]
    assert IN_SPLITS[-1] == d and sum(IN_SPLITS) == p["w_in"].shape[1]
    w_main, w_tail = _relayout_w_in(p["w_in"].T)
    wuq = p["w_uq"].reshape(MLA_Q_RANK, MLA_HEADS, MLA_QK)
    wuq = jnp.concatenate([wuq[:, :, :MLA_NOPE].reshape(MLA_Q_RANK, -1),
                           wuq[:, :, MLA_NOPE:].reshape(MLA_Q_RANK, -1)], axis=1).astype(BF16)
    wukv = p["w_ukv"].reshape(MLA_KV_RANK, MLA_HEADS, MLA_NOPE + MLA_V)
    wukv = jnp.concatenate([wukv[:, :, :MLA_NOPE].reshape(MLA_KV_RANK, -1),
                            wukv[:, :, MLA_NOPE:].reshape(MLA_KV_RANK, -1)], axis=1).astype(BF16)
    wg_pad = jnp.zeros((TAIL_W, GLA_KEY), F32).at[TAIL_GLR:TAIL_GLR + GLA_GATE_RANK].set(p["w_gk_up"])

    inv_freq = ROPE_THETA ** (-jnp.arange(0, MLA_ROPE, 2, dtype=F32) / MLA_ROPE)
    invf = jnp.tile(inv_freq, LANES // ROPE_HALF).reshape(1, LANES)
    sgn = jnp.tile(jnp.concatenate([-jnp.ones(ROPE_HALF, F32), jnp.ones(ROPE_HALF, F32)]),
                   LANES // MLA_ROPE).reshape(1, LANES)
    n_rope = MLA_HEADS * MLA_ROPE
    lane = jnp.arange(n_rope)
    grp = (lane[:, None] // MLA_ROPE == lane[None, :] // MLA_ROPE).astype(BF16)

    row = lambda a: a.reshape(1, -1)
    x2d = _ffn(x2d, mod3, row(p["g_ffn1"]), p["w1_a"][None], p["w3_a"][None],
               p["w2_a"][None], row(p["g_final"]), base=0, final_norm=False, seq=seq)
    proj, tail = _inproj(x2d, mod3, row(p["g_mix"]), w_main, w_tail, base=3, seq=seq)
    q, k, v = _mla_prep(
        proj, tail, pos2d, (invf, sgn, grp),
        (row(p["g_q_lat"]), wuq, row(p["g_qn"]), row(jnp.tile(p["g_qr"], MLA_HEADS)),
         row(p["g_kv_lat"]), wukv, row(p["g_kn"]), row(p["g_kr"])),
        batch=batch, seq=seq)
    attn_o = _mla_attn(q, k, v).reshape(batch * seq, MLA_HEADS * MLA_V)
    gla_o = _gla(proj, tail, wg_pad, row(p["b_gk"]), row(p["g_gla"]), batch=batch, seq=seq)
    x2d = _merge(x2d, mod3, attn_o, gla_o, proj, p["w_proj_a"].astype(BF16),
                 p["w_proj_b"].astype(BF16), p["w_out"].astype(BF16), base=3, seq=seq)
    x2d = _ffn(x2d, mod3, row(p["g_ffn2"]), p["w1_b"][None], p["w3_b"][None],
               p["w2_b"][None], row(p["g_final"]), base=6, final_norm=True, seq=seq)
    return x2d


def kernel(x, c, positions, w_ada, b_ada, g_ffn1, w1_a, w3_a, w2_a, g_mix, w_in, g_q_lat, w_uq, g_qn, g_qr, g_kv_lat, w_ukv, g_kn, g_kr, w_gk_up, b_gk, g_gla, w_proj_a, w_proj_b, w_out, g_ffn2, w1_b, w3_b, w2_b, g_final):
    batch, seq, d = x.shape
    depth = w_ada.shape[0]
    assert depth == 1, "the final norm is fused into the last FFN of a single layer"
    names = ("g_ffn1", "w1_a", "w3_a", "w2_a", "g_mix", "w_in", "g_q_lat", "w_uq", "g_qn", "g_qr",
             "g_kv_lat", "w_ukv", "g_kn", "g_kr", "w_gk_up", "b_gk", "g_gla", "w_proj_a", "w_proj_b",
             "w_out", "g_ffn2", "w1_b", "w3_b", "w2_b", "g_final")
    stacked = (g_ffn1, w1_a, w3_a, w2_a, g_mix, w_in, g_q_lat, w_uq, g_qn, g_qr, g_kv_lat, w_ukv,
               g_kn, g_kr, w_gk_up, b_gk, g_gla, w_proj_a, w_proj_b, w_out, g_ffn2, w1_b, w3_b, w2_b,
               g_final)
    x2d = x.reshape(batch * seq, d)
    pos2d = positions.reshape(batch * seq, 1)
    p = {n: a[0] for n, a in zip(names, stacked)}
    mod3 = _adaln(c, w_ada[0], b_ada[0]).reshape(batch, N_MOD, d)
    x2d = _layer(x2d, mod3, pos2d, p, batch=batch, seq=seq)
    return x2d.reshape(batch, seq, d)
```

```python
import functools

import jax
import jax.numpy as jnp
from jax import lax
from jax.experimental import pallas as pl
from jax.experimental.pallas import tpu as pltpu

F32 = jnp.float32
BF16 = jnp.bfloat16

EPS = 1e-6
LOG2E = 1.4426950408889634
CHUNK = 64
MLA_HEADS = 8
MLA_Q_RANK = 512
MLA_KV_RANK = 512
MLA_NOPE = 128
MLA_ROPE = 64
MLA_V = 128
MLA_QK = MLA_NOPE + MLA_ROPE
ROPE_THETA = 10000.0
GLA_HEADS = 4
GLA_DK = 128
GLA_DV = 256
GLA_GATE_RANK = 16
GLA_GATE_NORMALIZER = 16.0
GLA_KEY = GLA_HEADS * GLA_DK
GLA_VAL = GLA_HEADS * GLA_DV
N_MOD = 9

LANES = 128
SUBLANES = 8
VMEM_LIMIT = 56 * 1024 * 1024

COL_QLAT = 0
COL_KVLAT = COL_QLAT + MLA_Q_RANK
COL_GQ = COL_KVLAT + MLA_KV_RANK
COL_GK = COL_GQ + GLA_KEY
COL_GV = COL_GK + GLA_KEY
COL_GOUT = COL_GV + GLA_VAL
COL_GATE_A = COL_GOUT + GLA_VAL
TAIL_W = LANES
TAIL_KROPE = 0
TAIL_GLR = MLA_ROPE


def _params(sem):
    return pltpu.CompilerParams(dimension_semantics=sem, vmem_limit_bytes=VMEM_LIMIT)


def _dot(a, b):
    return jnp.dot(a, b, preferred_element_type=F32)


def _dot_t0(a, b):
    return lax.dot_general(a, b, (((0,), (0,)), ((), ())), preferred_element_type=F32)


def _dot_t1(a, b):
    return lax.dot_general(a, b, (((1,), (1,)), ((), ())), preferred_element_type=F32)


def _split_dot(x, w_bf16):
    hi = x.astype(BF16)
    lo = (x - hi.astype(F32)).astype(BF16)
    return _dot(hi, w_bf16) + _dot(lo, w_bf16)


def _split_dot_t0(x, w_bf16):
    hi = x.astype(BF16)
    lo = (x - hi.astype(F32)).astype(BF16)
    return _dot_t0(hi, w_bf16) + _dot_t0(lo, w_bf16)


def _silu(x):
    return x * jax.nn.sigmoid(x)


def _rms(x, g):
    ms = jnp.mean(x * x, axis=-1, keepdims=True)
    return x * lax.rsqrt(ms + EPS) * g


ADA_TN = 1024
ADA_RC = 64


def _adaln_kernel(ct_ref, w_ref, b_ref, o_ref, s_ref):
    d, nb = ct_ref.shape
    tn = w_ref.shape[1]

    @pl.when(pl.program_id(0) == 0)
    def _():
        s = _silu(ct_ref[...])
        for b in range(nb):
            s_ref[b] = jnp.broadcast_to(s[:, b:b + 1], (d, LANES))

    def body(k, acc):
        rows = pl.ds(pl.multiple_of(k * ADA_RC, ADA_RC), ADA_RC)
        out = []
        for b in range(nb):
            sb = s_ref[b, rows, :]
            cols = []
            for col in range(tn // LANES):
                p = w_ref[rows, col * LANES:(col + 1) * LANES] * sb
                cols.append(p.reshape(ADA_RC // SUBLANES, SUBLANES, LANES).sum(axis=0))
            out.append(acc[b] + jnp.concatenate(cols, axis=-1))
        return tuple(out)

    init = tuple(jnp.zeros((SUBLANES, tn), F32) for _ in range(nb))
    acc = lax.fori_loop(0, d // ADA_RC, body, init, unroll=2)
    rows = [jnp.sum(a, axis=0, keepdims=True) for a in acc]
    o_ref[...] = jnp.concatenate(rows, axis=0) + b_ref[...]


def _adaln(c, w_ada, b_ada):
    nb, d = c.shape
    n = w_ada.shape[1]
    return pl.pallas_call(
        _adaln_kernel,
        out_shape=jax.ShapeDtypeStruct((nb, n), F32),
        grid=(n // ADA_TN,),
        in_specs=[
            pl.BlockSpec((d, nb), lambda j: (0, 0)),
            pl.BlockSpec((d, ADA_TN), lambda j: (0, j)),
            pl.BlockSpec((1, ADA_TN), lambda j: (0, j)),
        ],
        out_specs=pl.BlockSpec((nb, ADA_TN), lambda j: (0, j)),
        scratch_shapes=[pltpu.VMEM((nb, d, LANES), F32)],
        compiler_params=_params(("arbitrary",)),
        name="adaln",
    )(c.T, w_ada, b_ada.reshape(1, n))


def _mod_norm(x, g, mod_ref, base):
    shift = mod_ref[0, base:base + 1, :]
    scale = mod_ref[0, base + 1:base + 2, :]
    return _rms(x, g) * (1.0 + scale) + shift


FFN_TM = 512
FFN_TF = 512
FFN_HEAD_TILES = 2
FFN_HEAD_TF = 256


def _ffn_step(x_ref, mod_ref, g_ref, gf_ref, w1, w3, w2, o_ref, h_ref, *, base, final_norm):
    j = pl.program_id(1)

    @pl.when(j == 0)
    def _():
        h_ref[...] = _mod_norm(x_ref[...], g_ref[...], mod_ref, base).astype(BF16)
        o_ref[...] = jnp.zeros_like(o_ref)

    h = h_ref[...]
    a = _dot(h, w1())
    b = _dot(h, w3())
    p = (_silu(a) * b).astype(BF16)
    o_ref[...] += _dot(p, w2())

    @pl.when(j == pl.num_programs(1) - 1)
    def _():
        gate = mod_ref[0, base + 2:base + 3, :]
        y = x_ref[...] + 0.5 * gate * o_ref[...]
        if final_norm:
            y = _rms(y, gf_ref[...])
        o_ref[...] = y


def _ffn_head_kernel(x_ref, mod_ref, g_ref, w1_ref, w3_ref, w2_ref, gf_ref,
                     o_ref, w1b_ref, w3b_ref, w2b_ref, h_ref, **kw):
    def cast(src_ref, dst_ref):
        def load():
            w = src_ref[0].astype(BF16)
            dst_ref[...] = w
            return w
        return load

    _ffn_step(x_ref, mod_ref, g_ref, gf_ref, cast(w1_ref, w1b_ref), cast(w3_ref, w3b_ref),
              cast(w2_ref, w2b_ref), o_ref, h_ref, **kw)


def _ffn_kernel(x_ref, mod_ref, g_ref, w1_ref, w3_ref, w2_ref, gf_ref, y0_ref, o_ref, h_ref, **kw):
    i = pl.program_id(0)

    @pl.when(i >= FFN_HEAD_TILES)
    def _():
        _ffn_step(x_ref, mod_ref, g_ref, gf_ref, lambda: w1_ref[...], lambda: w3_ref[...],
                  lambda: w2_ref[...], o_ref, h_ref, **kw)

    @pl.when((i < FFN_HEAD_TILES) & (pl.program_id(1) == pl.num_programs(1) - 1))
    def _():
        o_ref[...] = y0_ref[...]


def _ffn(x2d, mod3, g, w1, w3, w2, g_final, *, base, final_norm, seq):
    t, d = x2d.shape
    dff = w1.shape[2]
    tiles_per_batch = seq // FFN_TM
    const = lambda i, j: (0, 0)
    kw = dict(base=base, final_norm=final_norm)
    tag = "_final" if final_norm else ""
    common = [
        pl.BlockSpec((1, N_MOD, d), lambda i, j: (i // tiles_per_batch, 0, 0)),
        pl.BlockSpec((1, d), const),
    ]
    hf = FFN_HEAD_TF
    head_rows = FFN_HEAD_TILES * FFN_TM
    assert head_rows <= seq
    y0, w1b, w3b, w2b = pl.pallas_call(
        functools.partial(_ffn_head_kernel, **kw),
        out_shape=(jax.ShapeDtypeStruct((head_rows, d), F32),
                   jax.ShapeDtypeStruct((d, dff), BF16), jax.ShapeDtypeStruct((d, dff), BF16),
                   jax.ShapeDtypeStruct((dff, d), BF16)),
        grid=(1, dff // hf),
        in_specs=[pl.BlockSpec((head_rows, d), const, pipeline_mode=pl.Buffered(1))] + common + [
            pl.BlockSpec((1, d, hf), lambda i, j: (0, 0, j)),
            pl.BlockSpec((1, d, hf), lambda i, j: (0, 0, j)),
            pl.BlockSpec((1, hf, d), lambda i, j: (0, j, 0)),
            pl.BlockSpec((1, d), const),
        ],
        out_specs=(pl.BlockSpec((head_rows, d), const),
                   pl.BlockSpec((d, hf), lambda i, j: (0, j)),
                   pl.BlockSpec((d, hf), lambda i, j: (0, j)),
                   pl.BlockSpec((hf, d), lambda i, j: (j, 0))),
        scratch_shapes=[pltpu.VMEM((head_rows, d), BF16)],
        compiler_params=_params(("arbitrary", "arbitrary")),
        name="ffn_head" + tag,
    )(x2d, mod3, g, w1, w3, w2, g_final)

    wj = lambda i, j: jnp.where(i < FFN_HEAD_TILES, 0, j)
    wcol = lambda i, j: (0, wj(i, j))
    wrow = lambda i, j: (wj(i, j), 0)
    return pl.pallas_call(
        functools.partial(_ffn_kernel, **kw),
        out_shape=jax.ShapeDtypeStruct((t, d), F32),
        grid=(t // FFN_TM, dff // FFN_TF),
        in_specs=[pl.BlockSpec((FFN_TM, d), lambda i, j: (i, 0))] + common + [
            pl.BlockSpec((d, FFN_TF), wcol),
            pl.BlockSpec((d, FFN_TF), wcol),
            pl.BlockSpec((FFN_TF, d), wrow),
            pl.BlockSpec((1, d), const),
            pl.BlockSpec((FFN_TM, d), lambda i, j: (jnp.minimum(i, FFN_HEAD_TILES - 1), 0)),
        ],
        out_specs=pl.BlockSpec((FFN_TM, d), lambda i, j: (i, 0)),
        scratch_shapes=[pltpu.VMEM((FFN_TM, d), BF16)],
        compiler_params=_params(("arbitrary", "arbitrary")),
        name="ffn" + tag,
    )(x2d, mod3, g, w1b, w3b, w2b, g_final, y0)


IN_SPLITS = (MLA_Q_RANK, MLA_KV_RANK, MLA_ROPE, GLA_KEY, GLA_KEY, GLA_VAL, GLA_GATE_RANK, GLA_VAL,
             2048, 2048)
MAIN_ORDER = (0, 1, 3, 4, 5, 7, 8, 9)
TAIL_ORDER = (2, 6)
RELAYOUT_ROWS = 512


def _segments(order):
    starts = [sum(IN_SPLITS[:i]) for i in range(len(IN_SPLITS))]
    segs, dst = [], 0
    for i in order:
        src, n = starts[i], IN_SPLITS[i]
        if segs and segs[-1][0] + segs[-1][2] == src:
            segs[-1] = (segs[-1][0], segs[-1][1], segs[-1][2] + n)
        else:
            segs.append((src, dst, n))
        dst += n
    return segs, dst


def _relayout_kernel(lo_ref, hi_ref, kr_ref, glr_ref, main_ref, tail_ref):
    r = pl.program_id(0)
    rows = RELAYOUT_ROWS
    for src, dst, n in _segments(MAIN_ORDER)[0]:
        shift = src - dst

        @pl.when((r >= dst // rows) & (r < (dst + n) // rows))
        def _(shift=shift):
            main_ref[:rows - shift, :] = lo_ref[shift:, :].astype(BF16)
            if shift:
                main_ref[rows - shift:, :] = hi_ref[:shift, :].astype(BF16)

    tail_ref[:MLA_ROPE, :] = kr_ref[...].astype(BF16)
    tail_ref[MLA_ROPE:MLA_ROPE + GLA_GATE_RANK, :] = glr_ref[...].astype(BF16)
    pad = TAIL_W - MLA_ROPE - GLA_GATE_RANK
    tail_ref[MLA_ROPE + GLA_GATE_RANK:, :] = jnp.zeros((pad, tail_ref.shape[1]), BF16)


def _relayout_w_in(w_in_t):
    d_in, d = w_in_t.shape
    segs, n_main = _segments(MAIN_ORDER)
    rows = RELAYOUT_ROWS
    assert all(dst % rows == 0 and n % rows == 0 and 0 <= src - dst <= LANES and (src - dst) % 16 == 0
               for src, dst, n in segs)
    (kr_src, _, _), (glr_src, _, _) = _segments(TAIL_ORDER)[0]
    assert kr_src % MLA_ROPE == 0 and glr_src % GLA_GATE_RANK == 0
    return pl.pallas_call(
        _relayout_kernel,
        out_shape=(jax.ShapeDtypeStruct((n_main, d), BF16), jax.ShapeDtypeStruct((TAIL_W, d), BF16)),
        grid=(n_main // rows,),
        in_specs=[pl.BlockSpec((rows, d), lambda r: (r, 0)),
                  pl.BlockSpec((LANES, d), lambda r: ((r + 1) * (rows // LANES), 0)),
                  pl.BlockSpec((MLA_ROPE, d), lambda r: (kr_src // MLA_ROPE, 0)),
                  pl.BlockSpec((GLA_GATE_RANK, d), lambda r: (glr_src // GLA_GATE_RANK, 0))],
        out_specs=(pl.BlockSpec((rows, d), lambda r: (r, 0)),
                   pl.BlockSpec((TAIL_W, d), lambda r: (0, 0))),
        compiler_params=_params(("arbitrary",)),
        name="w_in_layout",
    )(w_in_t, w_in_t, w_in_t, w_in_t)


INP_TM = 1024
INP_TN = 1024


def _inproj_kernel(x_ref, mod_ref, g_ref, w_ref, wt_ref, o_ref, tail_ref, h_ref, *, base):
    @pl.when(pl.program_id(1) == 0)
    def _():
        h = _mod_norm(x_ref[...], g_ref[...], mod_ref, base).astype(BF16)
        h_ref[...] = h
        tail_ref[...] = _dot_t1(h, wt_ref[...])

    o_ref[...] = _dot_t1(h_ref[...], w_ref[...]).astype(BF16)


def _inproj(x2d, mod3, g, w_main, w_tail, *, base, seq):
    t, d = x2d.shape
    n = w_main.shape[0]
    tiles_per_batch = seq // INP_TM
    return pl.pallas_call(
        functools.partial(_inproj_kernel, base=base),
        out_shape=(jax.ShapeDtypeStruct((t, n), BF16),
                   jax.ShapeDtypeStruct((t, TAIL_W), F32)),
        grid=(t // INP_TM, n // INP_TN),
        in_specs=[
            pl.BlockSpec((INP_TM, d), lambda i, j: (i, 0)),
            pl.BlockSpec((1, N_MOD, d), lambda i, j: (i // tiles_per_batch, 0, 0)),
            pl.BlockSpec((1, d), lambda i, j: (0, 0)),
            pl.BlockSpec((INP_TN, d), lambda i, j: (j, 0)),
            pl.BlockSpec((TAIL_W, d), lambda i, j: (0, 0)),
        ],
        out_specs=(pl.BlockSpec((INP_TM, INP_TN), lambda i, j: (i, j)),
                   pl.BlockSpec((INP_TM, TAIL_W), lambda i, j: (i, 0))),
        scratch_shapes=[pltpu.VMEM((INP_TM, d), BF16)],
        compiler_params=_params(("parallel", "arbitrary")),
        name="in_proj",
    )(x2d, mod3, g, w_main, w_tail)


PREP_TM = 256
ROPE_HALF = MLA_ROPE // 2


def _rope_rotate(x, cos, sin_signed):
    width = x.shape[-1]
    lane = lax.broadcasted_iota(jnp.int32, x.shape, 1)
    first_half = (lane % MLA_ROPE) < ROPE_HALF
    partner = jnp.where(first_half,
                        pltpu.roll(x, width - ROPE_HALF, 1),
                        pltpu.roll(x, ROPE_HALF, 1))
    return x * cos + partner * sin_signed


def _prep_kernel(qlat_ref, kvlat_ref, tail_ref, pos_ref, invf_ref, sgn_ref,
                 gql_ref, wuq_ref, gqn_ref, gqr_ref, gkvl_ref, wukv_ref, gkn_ref, gkr_ref,
                 grp_ref, q_ref, k_ref, v_ref):
    scale = MLA_QK ** -0.5 * LOG2E
    ang = pos_ref[...].astype(F32) * invf_ref[...]
    cos = jnp.cos(ang)
    sin_signed = jnp.sin(ang) * sgn_ref[...]
    n_rope = MLA_HEADS * MLA_ROPE
    cos_q = jnp.concatenate([cos] * (n_rope // LANES), axis=-1)
    sin_q = jnp.concatenate([sin_signed] * (n_rope // LANES), axis=-1)

    cq = _rms(qlat_ref[...].astype(F32), gql_ref[...]).astype(BF16)
    q = _dot(cq, wuq_ref[...])
    n_nope = MLA_HEADS * MLA_NOPE
    qr = q[:, n_nope:]
    ssq = _split_dot(qr * qr, grp_ref[...])
    qr = qr * lax.rsqrt(ssq * (1.0 / MLA_ROPE) + EPS) * gqr_ref[...]
    qr = _rope_rotate(qr, cos_q, sin_q) * scale

    ckv = _rms(kvlat_ref[...].astype(F32), gkvl_ref[...]).astype(BF16)
    kv = _dot(ckv, wukv_ref[...])
    v_ref[0] = kv[:, n_nope:].astype(BF16)
    kr = tail_ref[:, TAIL_KROPE:TAIL_KROPE + MLA_ROPE]
    kr = _rms(kr, gkr_ref[...])
    kr = _rope_rotate(kr, cos[:, :MLA_ROPE], sin_signed[:, :MLA_ROPE]).astype(BF16)

    for h in range(MLA_HEADS):
        qn = _rms(q[:, h * MLA_NOPE:(h + 1) * MLA_NOPE], gqn_ref[...]) * scale
        q_ref[0, h, :, :MLA_NOPE] = qn.astype(BF16)
        q_ref[0, h, :, MLA_NOPE:] = qr[:, h * MLA_ROPE:(h + 1) * MLA_ROPE].astype(BF16)
        kn = _rms(kv[:, h * MLA_NOPE:(h + 1) * MLA_NOPE], gkn_ref[...])
        k_ref[0, h, :, :MLA_NOPE] = kn.astype(BF16)
        k_ref[0, h, :, MLA_NOPE:] = kr


def _mla_prep(proj, tail, pos2d, consts, weights, *, batch, seq):
    t = proj.shape[0]
    tiles_per_batch = seq // PREP_TM
    row = lambda i: (i, 0)
    const = lambda i: (0, 0)
    (invf, sgn, grp) = consts
    (gql, wuq, gqn, gqr, gkvl, wukv, gkn, gkr) = weights
    head_out = pl.BlockSpec((1, MLA_HEADS, PREP_TM, MLA_QK),
                            lambda i: (i // tiles_per_batch, 0, i % tiles_per_batch, 0))
    full = lambda a: pl.BlockSpec(a.shape, const)
    return pl.pallas_call(
        _prep_kernel,
        out_shape=(jax.ShapeDtypeStruct((batch, MLA_HEADS, seq, MLA_QK), BF16),
                   jax.ShapeDtypeStruct((batch, MLA_HEADS, seq, MLA_QK), BF16),
                   jax.ShapeDtypeStruct((batch, seq, MLA_HEADS * MLA_V), BF16)),
        grid=(t // PREP_TM,),
        in_specs=[
            pl.BlockSpec((PREP_TM, MLA_Q_RANK), lambda i: (i, COL_QLAT // MLA_Q_RANK)),
            pl.BlockSpec((PREP_TM, MLA_KV_RANK), lambda i: (i, COL_KVLAT // MLA_KV_RANK)),
            pl.BlockSpec((PREP_TM, TAIL_W), row),
            pl.BlockSpec((PREP_TM, 1), row),
            full(invf), full(sgn),
            full(gql), full(wuq), full(gqn), full(gqr),
            full(gkvl), full(wukv), full(gkn), full(gkr),
            full(grp),
        ],
        out_specs=(head_out, head_out,
                   pl.BlockSpec((1, PREP_TM, MLA_HEADS * MLA_V),
                                lambda i: (i // tiles_per_batch, i % tiles_per_batch, 0))),
        compiler_params=_params(("parallel",)),
        name="mla_prep",
    )(proj, proj, tail, pos2d, invf, sgn, gql, wuq, gqn, gqr, gkvl, wukv, gkn, gkr, grp)


ATT_T = 1024
ATT_H = ATT_T // 2


def _attn_kernel(q_ref, k_ref, v_ref, o_ref, m_ref, l_ref, acc_ref):
    qi = pl.program_id(2)
    m_ref[...] = jnp.full(m_ref.shape, -jnp.inf, F32)
    l_ref[...] = jnp.zeros(l_ref.shape, F32)
    acc_ref[...] = jnp.zeros(acc_ref.shape, F32)
    ones = jnp.ones((ATT_T, MLA_V), BF16)

    def update(half, s, v2):
        m_prev = m_ref[half]
        m_next = jnp.maximum(m_prev, jnp.max(s, axis=-1, keepdims=True))
        alpha = jnp.exp2(m_prev - m_next)
        p = jnp.exp2(s - jnp.tile(m_next, (1, s.shape[1] // LANES)))
        pv = _dot(p.astype(BF16), v2)
        acc_ref[half] = alpha * acc_ref[half] + pv[:, :MLA_V]
        l_ref[half] = alpha * l_ref[half] + pv[:, MLA_V:]
        m_ref[half] = m_next

    def tiles(kj):
        rows = pl.ds(pl.multiple_of(kj * ATT_T, ATT_T), ATT_T)
        return k_ref[0, 0, rows, :], jnp.concatenate([v_ref[0, rows, :], ones], axis=-1)

    def body(kj, carry):
        k, v2 = tiles(kj)
        for half in range(2):
            q = q_ref[0, 0, half * ATT_H:(half + 1) * ATT_H, :]
            update(half, _dot_t1(q, k), v2)
        return carry

    lax.fori_loop(0, qi, body, 0)

    k, v2 = tiles(qi)
    r = lax.broadcasted_iota(jnp.int32, (ATT_H, ATT_H), 0)
    c = lax.broadcasted_iota(jnp.int32, (ATT_H, ATT_H), 1)
    visible = (c // CHUNK) <= (r // CHUNK)
    s0 = _dot_t1(q_ref[0, 0, :ATT_H, :], k[:ATT_H])
    update(0, jnp.where(visible, s0, -jnp.inf), v2[:ATT_H])
    s1 = _dot_t1(q_ref[0, 0, ATT_H:, :], k)
    s1 = jnp.concatenate([s1[:, :ATT_H], jnp.where(visible, s1[:, ATT_H:], -jnp.inf)], axis=-1)
    update(1, s1, v2)
    for half in range(2):
        o_ref[0, half * ATT_H:(half + 1) * ATT_H, :] = (acc_ref[half] / l_ref[half]).astype(BF16)


def _mla_attn(q, k, v):
    batch, heads, seq, _ = q.shape
    return pl.pallas_call(
        _attn_kernel,
        out_shape=jax.ShapeDtypeStruct((batch, seq, heads * MLA_V), BF16),
        grid=(batch, heads, seq // ATT_T),
        in_specs=[
            pl.BlockSpec((1, 1, ATT_T, MLA_QK), lambda b, h, i: (b, h, i, 0)),
            pl.BlockSpec((1, 1, seq, MLA_QK), lambda b, h, i: (b, h, 0, 0)),
            pl.BlockSpec((1, seq, MLA_V), lambda b, h, i: (b, 0, h)),
        ],
        out_specs=pl.BlockSpec((1, ATT_T, MLA_V), lambda b, h, i: (b, i, h)),
        scratch_shapes=[pltpu.VMEM((2, ATT_H, LANES), F32), pltpu.VMEM((2, ATT_H, LANES), F32),
                        pltpu.VMEM((2, ATT_H, MLA_V), F32)],
        compiler_params=_params(("parallel", "parallel", "arbitrary")),
        name="mla_attn",
    )(q, k, v)


GLA_L = 512
GLA_NC = GLA_L // CHUNK
GLA_HP = 2


def _gla_head(log_a, q, k, v, go, gg, state_ref, causal, tri):
    la_hi = log_a.astype(BF16)
    la_lo = (log_a - la_hi.astype(F32)).astype(BF16)
    cum = _dot(tri, jnp.concatenate([la_hi, la_lo], axis=-1))
    b_cum = cum[:, :GLA_DK] + cum[:, GLA_DK:]
    b3 = b_cum.reshape(GLA_NC, CHUNK, GLA_DK)
    b_end = b3[:, CHUNK - 1:CHUNK, :]
    b_tot = jnp.broadcast_to(b_end, b3.shape).reshape(GLA_L, GLA_DK)

    qf = q.astype(F32) * (GLA_DK ** -0.5)
    kf = k.astype(F32)
    q_dec = (qf * jnp.exp(b_cum)).astype(BF16)
    k_dec = (kf * jnp.exp(-b_cum)).astype(BF16)
    k_end = (kf * jnp.exp(b_tot - b_cum)).astype(BF16)

    attn = jnp.where(causal, _dot_t1(q_dec, k_dec), 0.0).astype(BF16)
    o_intra = _dot(attn, v)

    decay = jnp.exp(b_end)
    state = state_ref[...]
    outs = []
    for n in range(GLA_NC):
        rows = slice(n * CHUNK, (n + 1) * CHUNK)
        outs.append(o_intra[rows] + _dot_t1(q_dec[rows], state.astype(BF16)))
        state = state * decay[n] + _dot_t0(v[rows], k_end[rows])
    state_ref[...] = state

    o = jnp.concatenate(outs, axis=0)
    return (_rms(o, gg) * _silu(go.astype(F32))).astype(BF16)


def _gla_kernel(q_ref, k_ref, v_ref, go_ref, tail_ref, wg_ref, bg_ref, gg_ref,
                o_ref, state_ref):
    @pl.when(pl.program_id(2) == 0)
    def _():
        state_ref[...] = jnp.zeros_like(state_ref)

    wg = wg_ref[...]
    wg_hi = wg.astype(BF16)
    wg_lo = (wg - wg_hi.astype(F32)).astype(BF16)
    tl = tail_ref[...]
    tl_hi = tl.astype(BF16)
    tl_lo = (tl - tl_hi.astype(F32)).astype(BF16)
    z = _dot(tl_hi, wg_hi) + _dot(tl_lo, wg_hi) + _dot(tl_hi, wg_lo) + bg_ref[...]
    log_a = (jnp.minimum(z, 0.0) - jnp.log1p(jnp.exp(-jnp.abs(z)))) * (1.0 / GLA_GATE_NORMALIZER)

    r = lax.broadcasted_iota(jnp.int32, (GLA_L, GLA_L), 0)
    c = lax.broadcasted_iota(jnp.int32, (GLA_L, GLA_L), 1)
    causal = ((r // CHUNK) == (c // CHUNK)) & (c <= r)
    tri = causal.astype(BF16)

    for hh in range(GLA_HP):
        dk = slice(hh * GLA_DK, (hh + 1) * GLA_DK)
        dv = slice(hh * GLA_DV, (hh + 1) * GLA_DV)
        o_ref[:, dv] = _gla_head(log_a[:, dk], q_ref[:, dk], k_ref[:, dk], v_ref[:, dv],
                                 go_ref[:, dv], gg_ref[...], state_ref.at[hh], causal, tri)


def _gla(proj, tail, wg_pad, b_gk, g_gla, *, batch, seq):
    t = proj.shape[0]
    steps = seq // GLA_L
    tok = lambda b, h, i: b * steps + i
    wk, wv = GLA_HP * GLA_DK, GLA_HP * GLA_DV
    return pl.pallas_call(
        _gla_kernel,
        out_shape=jax.ShapeDtypeStruct((t, GLA_VAL), BF16),
        grid=(batch, GLA_HEADS // GLA_HP, steps),
        in_specs=[
            pl.BlockSpec((GLA_L, wk), lambda b, h, i: (tok(b, h, i), COL_GQ // wk + h)),
            pl.BlockSpec((GLA_L, wk), lambda b, h, i: (tok(b, h, i), COL_GK // wk + h)),
            pl.BlockSpec((GLA_L, wv), lambda b, h, i: (tok(b, h, i), COL_GV // wv + h)),
            pl.BlockSpec((GLA_L, wv), lambda b, h, i: (tok(b, h, i), COL_GOUT // wv + h)),
            pl.BlockSpec((GLA_L, TAIL_W), lambda b, h, i: (tok(b, h, i), 0)),
            pl.BlockSpec((TAIL_W, wk), lambda b, h, i: (0, h)),
            pl.BlockSpec((1, wk), lambda b, h, i: (0, h)),
            pl.BlockSpec((1, GLA_DV), lambda b, h, i: (0, 0)),
        ],
        out_specs=pl.BlockSpec((GLA_L, wv), lambda b, h, i: (tok(b, h, i), h)),
        scratch_shapes=[pltpu.VMEM((GLA_HP, GLA_DV, GLA_DK), F32)],
        compiler_params=_params(("parallel", "parallel", "arbitrary")),
        name="gla",
    )(proj, proj, proj, proj, tail, wg_pad, b_gk, g_gla)


MRG_TM = 512


def _merge_kernel(x_ref, mod_ref, a_ref, b_ref, ga_ref, gb_ref, wa_ref, wb_ref, wo_ref, o_ref,
                  *, base):
    ya = _dot(a_ref[...], wa_ref[...])
    yb = _dot(b_ref[...], wb_ref[...])
    merged = (jax.nn.sigmoid(ga_ref[...].astype(F32)) * ya
              + jax.nn.sigmoid(gb_ref[...].astype(F32)) * yb).astype(BF16)
    gate = mod_ref[0, base + 2:base + 3, :]
    o_ref[...] = x_ref[...] + gate * _dot(merged, wo_ref[...])


def _merge(x2d, mod3, attn_o, gla_o, proj, wa, wb, wo, *, base, seq):
    t, d = x2d.shape
    tiles_per_batch = seq // MRG_TM
    row = lambda i: (i, 0)
    resident = lambda a: pl.BlockSpec(a.shape, lambda i: (0, 0), pipeline_mode=pl.Buffered(1))
    return pl.pallas_call(
        functools.partial(_merge_kernel, base=base),
        out_shape=jax.ShapeDtypeStruct((t, d), F32),
        grid=(t // MRG_TM,),
        in_specs=[
            pl.BlockSpec((MRG_TM, d), row),
            pl.BlockSpec((1, N_MOD, d), lambda i: (i // tiles_per_batch, 0, 0)),
            pl.BlockSpec((MRG_TM, attn_o.shape[1]), row),
            pl.BlockSpec((MRG_TM, gla_o.shape[1]), row),
            pl.BlockSpec((MRG_TM, d), lambda i: (i, COL_GATE_A // d)),
            pl.BlockSpec((MRG_TM, d), lambda i: (i, COL_GATE_A // d + 1)),
            resident(wa), resident(wb), resident(wo),
        ],
        out_specs=pl.BlockSpec((MRG_TM, d), row),
        compiler_params=_params(("parallel",)),
        name="merge",
    )(x2d, mod3, attn_o, gla_o, proj, proj, wa, wb, wo)


def _layer(x2d, mod3, pos2d, p, *, batch, seq):
    d = x2d.shape[1]
    assert IN_SPLITS[-1] == d and sum(IN_SPLITS) == p["w_in"].shape[1]
    w_main, w_tail = _relayout_w_in(p["w_in"].T)
    wuq = p["w_uq"].reshape(MLA_Q_RANK, MLA_HEADS, MLA_QK)
    wuq = jnp.concatenate([wuq[:, :, :MLA_NOPE].reshape(MLA_Q_RANK, -1),
                           wuq[:, :, MLA_NOPE:].reshape(MLA_Q_RANK, -1)], axis=1).astype(BF16)
    wukv = p["w_ukv"].reshape(MLA_KV_RANK, MLA_HEADS, MLA_NOPE + MLA_V)
    wukv = jnp.concatenate([wukv[:, :, :MLA_NOPE].reshape(MLA_KV_RANK, -1),
                            wukv[:, :, MLA_NOPE:].reshape(MLA_KV_RANK, -1)], axis=1).astype(BF16)
    wg_pad = jnp.zeros((TAIL_W, GLA_KEY), F32).at[TAIL_GLR:TAIL_GLR + GLA_GATE_RANK].set(p["w_gk_up"])

    inv_freq = ROPE_THETA ** (-jnp.arange(0, MLA_ROPE, 2, dtype=F32) / MLA_ROPE)
    invf = jnp.tile(inv_freq, LANES // ROPE_HALF).reshape(1, LANES)
    sgn = jnp.tile(jnp.concatenate([-jnp.ones(ROPE_HALF, F32), jnp.ones(ROPE_HALF, F32)]),
                   LANES // MLA_ROPE).reshape(1, LANES)
    n_rope = MLA_HEADS * MLA_ROPE
    lane = jnp.arange(n_rope)
    grp = (lane[:, None] // MLA_ROPE == lane[None, :] // MLA_ROPE).astype(BF16)

    row = lambda a: a.reshape(1, -1)
    x2d = _ffn(x2d, mod3, row(p["g_ffn1"]), p["w1_a"][None], p["w3_a"][None],
               p["w2_a"][None], row(p["g_final"]), base=0, final_norm=False, seq=seq)
    proj, tail = _inproj(x2d, mod3, row(p["g_mix"]), w_main, w_tail, base=3, seq=seq)
    q, k, v = _mla_prep(
        proj, tail, pos2d, (invf, sgn, grp),
        (row(p["g_q_lat"]), wuq, row(p["g_qn"]), row(jnp.tile(p["g_qr"], MLA_HEADS)),
         row(p["g_kv_lat"]), wukv, row(p["g_kn"]), row(p["g_kr"])),
        batch=batch, seq=seq)
    attn_o = _mla_attn(q, k, v).reshape(batch * seq, MLA_HEADS * MLA_V)
    gla_o = _gla(proj, tail, wg_pad, row(p["b_gk"]), row(p["g_gla"]), batch=batch, seq=seq)
    x2d = _merge(x2d, mod3, attn_o, gla_o, proj, p["w_proj_a"].astype(BF16),
                 p["w_proj_b"].astype(BF16), p["w_out"].astype(BF16), base=3, seq=seq)
    x2d = _ffn(x2d, mod3, row(p["g_ffn2"]), p["w1_b"][None], p["w3_b"][None],
               p["w2_b"][None], row(p["g_final"]), base=6, final_norm=True, seq=seq)
    return x2d


def kernel(x, c, positions, w_ada, b_ada, g_ffn1, w1_a, w3_a, w2_a, g_mix, w_in, g_q_lat, w_uq, g_qn, g_qr, g_kv_lat, w_ukv, g_kn, g_kr, w_gk_up, b_gk, g_gla, w_proj_a, w_proj_b, w_out, g_ffn2, w1_b, w3_b, w2_b, g_final):
    batch, seq, d = x.shape
    depth = w_ada.shape[0]
    assert depth == 1, "the final norm is fused into the last FFN of a single layer"
    names = ("g_ffn1", "w1_a", "w3_a", "w2_a", "g_mix", "w_in", "g_q_lat", "w_uq", "g_qn", "g_qr",
             "g_kv_lat", "w_ukv", "g_kn", "g_kr", "w_gk_up", "b_gk", "g_gla", "w_proj_a", "w_proj_b",
             "w_out", "g_ffn2", "w1_b", "w3_b", "w2_b", "g_final")
    stacked = (g_ffn1, w1_a, w3_a, w2_a, g_mix, w_in, g_q_lat, w_uq, g_qn, g_qr, g_kv_lat, w_ukv,
               g_kn, g_kr, w_gk_up, b_gk, g_gla, w_proj_a, w_proj_b, w_out, g_ffn2, w1_b, w3_b, w2_b,
               g_final)
    x2d = x.reshape(batch * seq, d)
    pos2d = positions.reshape(batch * seq, 1)
    p = {n: a[0] for n, a in zip(names, stacked)}
    mod3 = _adaln(c, w_ada[0], b_ada[0]).reshape(batch, N_MOD, d)
    x2d = _layer(x2d, mod3, pos2d, p, batch=batch, seq=seq)
    return x2d.reshape(batch, seq, d)
```

```python
import functools

import jax
import jax.numpy as jnp
from jax import lax
from jax.experimental import pallas as pl
from jax.experimental.pallas import tpu as pltpu

F32 = jnp.float32
BF16 = jnp.bfloat16

EPS = 1e-6
LOG2E = 1.4426950408889634
CHUNK = 64
MLA_HEADS = 8
MLA_Q_RANK = 512
MLA_KV_RANK = 512
MLA_NOPE = 128
MLA_ROPE = 64
MLA_V = 128
MLA_QK = MLA_NOPE + MLA_ROPE
ROPE_THETA = 10000.0
GLA_HEADS = 4
GLA_DK = 128
GLA_DV = 256
GLA_GATE_RANK = 16
GLA_GATE_NORMALIZER = 16.0
GLA_KEY = GLA_HEADS * GLA_DK
GLA_VAL = GLA_HEADS * GLA_DV
N_MOD = 9

LANES = 128
SUBLANES = 8
VMEM_LIMIT = 56 * 1024 * 1024

COL_QLAT = 0
COL_KVLAT = COL_QLAT + MLA_Q_RANK
COL_GQ = COL_KVLAT + MLA_KV_RANK
COL_GK = COL_GQ + GLA_KEY
COL_GV = COL_GK + GLA_KEY
COL_GOUT = COL_GV + GLA_VAL
COL_GATE_A = COL_GOUT + GLA_VAL
TAIL_W = LANES
TAIL_KROPE = 0
TAIL_GLR = MLA_ROPE


def _params(sem):
    return pltpu.CompilerParams(dimension_semantics=sem, vmem_limit_bytes=VMEM_LIMIT)


def _dot(a, b):
    return jnp.dot(a, b, preferred_element_type=F32)


def _dot_t0(a, b):
    return lax.dot_general(a, b, (((0,), (0,)), ((), ())), preferred_element_type=F32)


def _dot_t1(a, b):
    return lax.dot_general(a, b, (((1,), (1,)), ((), ())), preferred_element_type=F32)


def _split_dot(x, w_bf16):
    hi = x.astype(BF16)
    lo = (x - hi.astype(F32)).astype(BF16)
    return _dot(hi, w_bf16) + _dot(lo, w_bf16)


def _split_dot_t0(x, w_bf16):
    hi = x.astype(BF16)
    lo = (x - hi.astype(F32)).astype(BF16)
    return _dot_t0(hi, w_bf16) + _dot_t0(lo, w_bf16)


def _silu(x):
    return x * jax.nn.sigmoid(x)


def _rms(x, g):
    ms = jnp.mean(x * x, axis=-1, keepdims=True)
    return x * lax.rsqrt(ms + EPS) * g


ADA_TN = 1024
ADA_RC = 64


def _adaln_kernel(ct_ref, w_ref, b_ref, o_ref, s_ref):
    d, nb = ct_ref.shape
    tn = w_ref.shape[1]

    @pl.when(pl.program_id(0) == 0)
    def _():
        s = _silu(ct_ref[...])
        for b in range(nb):
            s_ref[b] = jnp.broadcast_to(s[:, b:b + 1], (d, LANES))

    def body(k, acc):
        rows = pl.ds(pl.multiple_of(k * ADA_RC, ADA_RC), ADA_RC)
        out = []
        for b in range(nb):
            sb = s_ref[b, rows, :]
            cols = []
            for col in range(tn // LANES):
                p = w_ref[rows, col * LANES:(col + 1) * LANES] * sb
                cols.append(p.reshape(ADA_RC // SUBLANES, SUBLANES, LANES).sum(axis=0))
            out.append(acc[b] + jnp.concatenate(cols, axis=-1))
        return tuple(out)

    init = tuple(jnp.zeros((SUBLANES, tn), F32) for _ in range(nb))
    acc = lax.fori_loop(0, d // ADA_RC, body, init, unroll=2)
    rows = [jnp.sum(a, axis=0, keepdims=True) for a in acc]
    o_ref[...] = jnp.concatenate(rows, axis=0) + b_ref[...]


def _adaln(c, w_ada, b_ada):
    nb, d = c.shape
    n = w_ada.shape[1]
    return pl.pallas_call(
        _adaln_kernel,
        out_shape=jax.ShapeDtypeStruct((nb, n), F32),
        grid=(n // ADA_TN,),
        in_specs=[
            pl.BlockSpec((d, nb), lambda j: (0, 0)),
            pl.BlockSpec((d, ADA_TN), lambda j: (0, j)),
            pl.BlockSpec((1, ADA_TN), lambda j: (0, j)),
        ],
        out_specs=pl.BlockSpec((nb, ADA_TN), lambda j: (0, j)),
        scratch_shapes=[pltpu.VMEM((nb, d, LANES), F32)],
        compiler_params=_params(("arbitrary",)),
        name="adaln",
    )(c.T, w_ada, b_ada.reshape(1, n))


def _mod_norm(x, g, mod_ref, base):
    shift = mod_ref[0, base:base + 1, :]
    scale = mod_ref[0, base + 1:base + 2, :]
    return _rms(x, g) * (1.0 + scale) + shift


FFN_TM = 512
FFN_TF = 512
FFN_HEAD_TILES = 2
FFN_HEAD_TF = 256
FFN_GROUP = 256
FFN_AHEAD_CHUNKS = 8
FFN_AHEAD_ROWS = FFN_TM // FFN_AHEAD_CHUNKS


def _swiglu_accumulate(h, w1t, w3t, w2t, o_ref):
    upd = None
    for s in range(w1t.shape[1] // FFN_GROUP):
        cols = slice(s * FFN_GROUP, (s + 1) * FFN_GROUP)
        a = _dot(h, w1t[:, cols])
        b = _dot(h, w3t[:, cols])
        p = (_silu(a) * b).astype(BF16)
        part = _dot(p, w2t[cols, :])
        upd = part if upd is None else upd + part
    o_ref[...] += upd


def _ffn_finish(x_ref, mod_ref, gf_ref, o_ref, *, base, final_norm):
    gate = mod_ref[0, base + 2:base + 3, :]
    y = x_ref[...] + 0.5 * gate * o_ref[...]
    if final_norm:
        y = _rms(y, gf_ref[...])
    o_ref[...] = y


def _ffn_head_kernel(x_ref, mod_ref, g_ref, w1_ref, w3_ref, w2_ref, gf_ref,
                     o_ref, w1b_ref, w3b_ref, w2b_ref, h_ref, *, base, final_norm):
    j = pl.program_id(1)

    @pl.when(j == 0)
    def _():
        h_ref[...] = _mod_norm(x_ref[...], g_ref[...], mod_ref, base).astype(BF16)
        o_ref[...] = jnp.zeros_like(o_ref)

    w1 = w1_ref[0].astype(BF16)
    w3 = w3_ref[0].astype(BF16)
    w2 = w2_ref[0].astype(BF16)
    w1b_ref[...] = w1
    w3b_ref[...] = w3
    w2b_ref[...] = w2
    _swiglu_accumulate(h_ref[...], w1, w3, w2, o_ref)

    @pl.when(j == pl.num_programs(1) - 1)
    def _():
        _ffn_finish(x_ref, mod_ref, gf_ref, o_ref, base=base, final_norm=final_norm)


def _ffn_kernel(x_ref, xn_ref, mod_ref, modn_ref, g_ref, w1_ref, w3_ref, w2_ref, gf_ref, y0_ref,
                o_ref, h_ref, *, base, final_norm):
    i = pl.program_id(0)
    j = pl.program_id(1)
    slot = i % 2

    @pl.when(i >= FFN_HEAD_TILES)
    def _():
        @pl.when(j == 0)
        def _():
            o_ref[...] = jnp.zeros_like(o_ref)

            @pl.when(i == FFN_HEAD_TILES)
            def _():
                h_ref[slot] = _mod_norm(x_ref[...], g_ref[...], mod_ref, base).astype(BF16)

        chunk = jnp.minimum(j, FFN_AHEAD_CHUNKS - 1)
        rows = pl.ds(pl.multiple_of(chunk * FFN_AHEAD_ROWS, FFN_AHEAD_ROWS), FFN_AHEAD_ROWS)
        h_next = _mod_norm(xn_ref[rows, :], g_ref[...], modn_ref, base).astype(BF16)

        _swiglu_accumulate(h_ref[slot], w1_ref[...], w3_ref[...], w2_ref[...], o_ref)
        h_ref[1 - slot, rows, :] = h_next

        @pl.when(j == pl.num_programs(1) - 1)
        def _():
            _ffn_finish(x_ref, mod_ref, gf_ref, o_ref, base=base, final_norm=final_norm)

    @pl.when((i < FFN_HEAD_TILES) & (j == pl.num_programs(1) - 1))
    def _():
        o_ref[...] = y0_ref[...]


def _ffn(x2d, mod3, g, w1, w3, w2, g_final, *, base, final_norm, seq):
    t, d = x2d.shape
    dff = w1.shape[2]
    tiles_per_batch = seq // FFN_TM
    const = lambda i, j: (0, 0)
    kw = dict(base=base, final_norm=final_norm)
    tag = "_final" if final_norm else ""
    common = [
        pl.BlockSpec((1, N_MOD, d), lambda i, j: (i // tiles_per_batch, 0, 0)),
        pl.BlockSpec((1, d), const),
    ]
    hf = FFN_HEAD_TF
    head_rows = FFN_HEAD_TILES * FFN_TM
    assert head_rows <= seq
    y0, w1b, w3b, w2b = pl.pallas_call(
        functools.partial(_ffn_head_kernel, **kw),
        out_shape=(jax.ShapeDtypeStruct((head_rows, d), F32),
                   jax.ShapeDtypeStruct((d, dff), BF16), jax.ShapeDtypeStruct((d, dff), BF16),
                   jax.ShapeDtypeStruct((dff, d), BF16)),
        grid=(1, dff // hf),
        in_specs=[pl.BlockSpec((head_rows, d), const, pipeline_mode=pl.Buffered(1))] + common + [
            pl.BlockSpec((1, d, hf), lambda i, j: (0, 0, j)),
            pl.BlockSpec((1, d, hf), lambda i, j: (0, 0, j)),
            pl.BlockSpec((1, hf, d), lambda i, j: (0, j, 0)),
            pl.BlockSpec((1, d), const),
        ],
        out_specs=(pl.BlockSpec((head_rows, d), const),
                   pl.BlockSpec((d, hf), lambda i, j: (0, j)),
                   pl.BlockSpec((d, hf), lambda i, j: (0, j)),
                   pl.BlockSpec((hf, d), lambda i, j: (j, 0))),
        scratch_shapes=[pltpu.VMEM((head_rows, d), BF16)],
        compiler_params=_params(("arbitrary", "arbitrary")),
        name="ffn_head" + tag,
    )(x2d, mod3, g, w1, w3, w2, g_final)

    wj = lambda i, j: jnp.where(i < FFN_HEAD_TILES, 0, j)
    wcol = lambda i, j: (0, wj(i, j))
    wrow = lambda i, j: (wj(i, j), 0)
    n_tiles = t // FFN_TM
    assert dff // FFN_TF >= FFN_AHEAD_CHUNKS
    nxt = lambda i: jnp.minimum(i + 1, n_tiles - 1)
    return pl.pallas_call(
        functools.partial(_ffn_kernel, **kw),
        out_shape=jax.ShapeDtypeStruct((t, d), F32),
        grid=(n_tiles, dff // FFN_TF),
        in_specs=[
            pl.BlockSpec((FFN_TM, d), lambda i, j: (i, 0)),
            pl.BlockSpec((FFN_TM, d), lambda i, j: (nxt(i), 0)),
            pl.BlockSpec((1, N_MOD, d), lambda i, j: (i // tiles_per_batch, 0, 0)),
            pl.BlockSpec((1, N_MOD, d), lambda i, j: (nxt(i) // tiles_per_batch, 0, 0)),
            pl.BlockSpec((1, d), const),
            pl.BlockSpec((d, FFN_TF), wcol),
            pl.BlockSpec((d, FFN_TF), wcol),
            pl.BlockSpec((FFN_TF, d), wrow),
            pl.BlockSpec((1, d), const),
            pl.BlockSpec((FFN_TM, d), lambda i, j: (jnp.minimum(i, FFN_HEAD_TILES - 1), 0)),
        ],
        out_specs=pl.BlockSpec((FFN_TM, d), lambda i, j: (i, 0)),
        scratch_shapes=[pltpu.VMEM((2, FFN_TM, d), BF16)],
        compiler_params=_params(("arbitrary", "arbitrary")),
        name="ffn" + tag,
    )(x2d, x2d, mod3, mod3, g, w1b, w3b, w2b, g_final, y0)


IN_SPLITS = (MLA_Q_RANK, MLA_KV_RANK, MLA_ROPE, GLA_KEY, GLA_KEY, GLA_VAL, GLA_GATE_RANK, GLA_VAL,
             2048, 2048)
MAIN_ORDER = (0, 1, 3, 4, 5, 7, 8, 9)
TAIL_ORDER = (2, 6)
RELAYOUT_ROWS = 512


def _segments(order):
    starts = [sum(IN_SPLITS[:i]) for i in range(len(IN_SPLITS))]
    segs, dst = [], 0
    for i in order:
        src, n = starts[i], IN_SPLITS[i]
        if segs and segs[-1][0] + segs[-1][2] == src:
            segs[-1] = (segs[-1][0], segs[-1][1], segs[-1][2] + n)
        else:
            segs.append((src, dst, n))
        dst += n
    return segs, dst


def _relayout_kernel(lo_ref, hi_ref, kr_ref, glr_ref, main_ref, tail_ref):
    r = pl.program_id(0)
    rows = RELAYOUT_ROWS
    for src, dst, n in _segments(MAIN_ORDER)[0]:
        shift = src - dst

        @pl.when((r >= dst // rows) & (r < (dst + n) // rows))
        def _(shift=shift):
            main_ref[:rows - shift, :] = lo_ref[shift:, :].astype(BF16)
            if shift:
                main_ref[rows - shift:, :] = hi_ref[:shift, :].astype(BF16)

    tail_ref[:MLA_ROPE, :] = kr_ref[...].astype(BF16)
    tail_ref[MLA_ROPE:MLA_ROPE + GLA_GATE_RANK, :] = glr_ref[...].astype(BF16)
    pad = TAIL_W - MLA_ROPE - GLA_GATE_RANK
    tail_ref[MLA_ROPE + GLA_GATE_RANK:, :] = jnp.zeros((pad, tail_ref.shape[1]), BF16)


def _relayout_w_in(w_in_t):
    d_in, d = w_in_t.shape
    segs, n_main = _segments(MAIN_ORDER)
    rows = RELAYOUT_ROWS
    assert all(dst % rows == 0 and n % rows == 0 and 0 <= src - dst <= LANES and (src - dst) % 16 == 0
               for src, dst, n in segs)
    (kr_src, _, _), (glr_src, _, _) = _segments(TAIL_ORDER)[0]
    assert kr_src % MLA_ROPE == 0 and glr_src % GLA_GATE_RANK == 0
    return pl.pallas_call(
        _relayout_kernel,
        out_shape=(jax.ShapeDtypeStruct((n_main, d), BF16), jax.ShapeDtypeStruct((TAIL_W, d), BF16)),
        grid=(n_main // rows,),
        in_specs=[pl.BlockSpec((rows, d), lambda r: (r, 0)),
                  pl.BlockSpec((LANES, d), lambda r: ((r + 1) * (rows // LANES), 0)),
                  pl.BlockSpec((MLA_ROPE, d), lambda r: (kr_src // MLA_ROPE, 0)),
                  pl.BlockSpec((GLA_GATE_RANK, d), lambda r: (glr_src // GLA_GATE_RANK, 0))],
        out_specs=(pl.BlockSpec((rows, d), lambda r: (r, 0)),
                   pl.BlockSpec((TAIL_W, d), lambda r: (0, 0))),
        compiler_params=_params(("arbitrary",)),
        name="w_in_layout",
    )(w_in_t, w_in_t, w_in_t, w_in_t)


INP_TM = 1024
INP_TN = 1024


def _inproj_kernel(x_ref, mod_ref, g_ref, w_ref, wt_ref, o_ref, tail_ref, h_ref, *, base):
    @pl.when(pl.program_id(1) == 0)
    def _():
        h = _mod_norm(x_ref[...], g_ref[...], mod_ref, base).astype(BF16)
        h_ref[...] = h
        tail_ref[...] = _dot_t1(h, wt_ref[...])

    o_ref[...] = _dot_t1(h_ref[...], w_ref[...]).astype(BF16)


def _inproj(x2d, mod3, g, w_main, w_tail, *, base, seq):
    t, d = x2d.shape
    n = w_main.shape[0]
    tiles_per_batch = seq // INP_TM
    return pl.pallas_call(
        functools.partial(_inproj_kernel, base=base),
        out_shape=(jax.ShapeDtypeStruct((t, n), BF16),
                   jax.ShapeDtypeStruct((t, TAIL_W), F32)),
        grid=(t // INP_TM, n // INP_TN),
        in_specs=[
            pl.BlockSpec((INP_TM, d), lambda i, j: (i, 0)),
            pl.BlockSpec((1, N_MOD, d), lambda i, j: (i // tiles_per_batch, 0, 0)),
            pl.BlockSpec((1, d), lambda i, j: (0, 0)),
            pl.BlockSpec((INP_TN, d), lambda i, j: (j, 0)),
            pl.BlockSpec((TAIL_W, d), lambda i, j: (0, 0)),
        ],
        out_specs=(pl.BlockSpec((INP_TM, INP_TN), lambda i, j: (i, j)),
                   pl.BlockSpec((INP_TM, TAIL_W), lambda i, j: (i, 0))),
        scratch_shapes=[pltpu.VMEM((INP_TM, d), BF16)],
        compiler_params=_params(("parallel", "arbitrary")),
        name="in_proj",
    )(x2d, mod3, g, w_main, w_tail)


PREP_TM = 256
ROPE_HALF = MLA_ROPE // 2


def _rope_rotate(x, cos, sin_signed):
    width = x.shape[-1]
    lane = lax.broadcasted_iota(jnp.int32, x.shape, 1)
    first_half = (lane % MLA_ROPE) < ROPE_HALF
    partner = jnp.where(first_half,
                        pltpu.roll(x, width - ROPE_HALF, 1),
                        pltpu.roll(x, ROPE_HALF, 1))
    return x * cos + partner * sin_signed


def _prep_kernel(qlat_ref, kvlat_ref, tail_ref, pos_ref, invf_ref, sgn_ref,
                 gql_ref, wuq_ref, gqn_ref, gqr_ref, gkvl_ref, wukv_ref, gkn_ref, gkr_ref,
                 grp_ref, q_ref, k_ref, v_ref):
    ang_t = invf_ref[...] * pos_ref[0].astype(F32)
    lane_reps = LANES // ROPE_HALF
    cos = jnp.tile(jnp.cos(ang_t).T, (1, lane_reps))
    sin_signed = jnp.tile(jnp.sin(ang_t).T, (1, lane_reps)) * sgn_ref[...]
    n_rope = MLA_HEADS * MLA_ROPE
    cos_q = jnp.concatenate([cos] * (n_rope // LANES), axis=-1)
    sin_q = jnp.concatenate([sin_signed] * (n_rope // LANES), axis=-1)

    cq = _rms(qlat_ref[...].astype(F32), gql_ref[...]).astype(BF16)
    ckv = _rms(kvlat_ref[...].astype(F32), gkvl_ref[...]).astype(BF16)
    n_nope = MLA_HEADS * MLA_NOPE
    pair = 2 * MLA_NOPE

    for hp in range(MLA_HEADS // 2):
        cols = slice(hp * pair, (hp + 1) * pair)
        qp = _dot(cq, wuq_ref[:, cols])
        kp = _dot(ckv, wukv_ref[:, cols])
        for hh in range(2):
            h = 2 * hp + hh
            sub = slice(hh * MLA_NOPE, (hh + 1) * MLA_NOPE)
            q_ref[0, h, :, :MLA_NOPE] = _rms(qp[:, sub], gqn_ref[...]).astype(BF16)
            k_ref[0, h, :, :MLA_NOPE] = _rms(kp[:, sub], gkn_ref[...]).astype(BF16)
        v_ref[0, :, cols] = _dot(ckv, wukv_ref[:, n_nope + hp * pair:n_nope + (hp + 1) * pair]).astype(BF16)

    qr = _dot(cq, wuq_ref[:, n_nope:])
    ssq = _split_dot(qr * qr, grp_ref[...])
    qr = qr * lax.rsqrt(ssq * (1.0 / MLA_ROPE) + EPS) * gqr_ref[...]
    qr = _rope_rotate(qr, cos_q, sin_q).astype(BF16)
    kr = tail_ref[:, TAIL_KROPE:TAIL_KROPE + MLA_ROPE]
    kr = _rms(kr, gkr_ref[...])
    kr = _rope_rotate(kr, cos[:, :MLA_ROPE], sin_signed[:, :MLA_ROPE]).astype(BF16)
    for h in range(MLA_HEADS):
        q_ref[0, h, :, MLA_NOPE:] = qr[:, h * MLA_ROPE:(h + 1) * MLA_ROPE]
        k_ref[0, h, :, MLA_NOPE:] = kr


def _mla_prep(proj, tail, pos3d, consts, weights, *, batch, seq):
    t = proj.shape[0]
    tiles_per_batch = seq // PREP_TM
    row = lambda i: (i, 0)
    const = lambda i: (0, 0)
    (invf, sgn, grp) = consts
    (gql, wuq, gqn, gqr, gkvl, wukv, gkn, gkr) = weights
    head_out = pl.BlockSpec((1, MLA_HEADS, PREP_TM, MLA_QK),
                            lambda i: (i // tiles_per_batch, 0, i % tiles_per_batch, 0))
    full = lambda a: pl.BlockSpec(a.shape, const)
    return pl.pallas_call(
        _prep_kernel,
        out_shape=(jax.ShapeDtypeStruct((batch, MLA_HEADS, seq, MLA_QK), BF16),
                   jax.ShapeDtypeStruct((batch, MLA_HEADS, seq, MLA_QK), BF16),
                   jax.ShapeDtypeStruct((batch, seq, MLA_HEADS * MLA_V), BF16)),
        grid=(t // PREP_TM,),
        in_specs=[
            pl.BlockSpec((PREP_TM, MLA_Q_RANK), lambda i: (i, COL_QLAT // MLA_Q_RANK)),
            pl.BlockSpec((PREP_TM, MLA_KV_RANK), lambda i: (i, COL_KVLAT // MLA_KV_RANK)),
            pl.BlockSpec((PREP_TM, TAIL_W), row),
            pl.BlockSpec((1, 1, PREP_TM), lambda i: (i, 0, 0)),
            full(invf), full(sgn),
            full(gql), full(wuq), full(gqn), full(gqr),
            full(gkvl), full(wukv), full(gkn), full(gkr),
            full(grp),
        ],
        out_specs=(head_out, head_out,
                   pl.BlockSpec((1, PREP_TM, MLA_HEADS * MLA_V),
                                lambda i: (i // tiles_per_batch, i % tiles_per_batch, 0))),
        compiler_params=_params(("parallel",)),
        name="mla_prep",
    )(proj, proj, tail, pos3d, invf, sgn, gql, wuq, gqn, gqr, gkvl, wukv, gkn, gkr, grp)


ATT_T = 1024
ATT_H = ATT_T // 2


def _attn_kernel(q_ref, k_ref, v_ref, o_ref):
    n_tiles = k_ref.shape[2] // ATT_T
    ones = jnp.ones((ATT_T, MLA_V), BF16)
    r = lax.broadcasted_iota(jnp.int32, (ATT_H, ATT_H), 0)
    c = lax.broadcasted_iota(jnp.int32, (ATT_H, ATT_H), 1)
    visible = (c // CHUNK) <= (r // CHUNK)
    state = {}

    def update(chain, s, v2):
        m_cur = jnp.max(s, axis=-1, keepdims=True)
        if chain not in state:
            m_next = jnp.broadcast_to(m_cur, (ATT_H, LANES))
        else:
            m_prev, l_prev, acc_prev = state[chain]
            m_next = jnp.maximum(m_prev, m_cur)
            alpha = jnp.exp2(m_prev - m_next)
        p = jnp.exp2(s - jnp.tile(m_next, (1, s.shape[1] // LANES)))
        pv = _dot(p.astype(BF16), v2)
        acc, l = pv[:, :MLA_V], pv[:, MLA_V:]
        if chain in state:
            acc, l = alpha * acc_prev + acc, alpha * l_prev + l
        state[chain] = (m_next, l, acc)

    for t in range(n_tiles):
        rows = slice(t * ATT_T, (t + 1) * ATT_T)
        k = k_ref[0, 0, rows, :]
        v2 = jnp.concatenate([v_ref[0, rows, :], ones], axis=-1)
        for qt in range(t, n_tiles):
            q0 = q_ref[0, 0, qt * ATT_T:qt * ATT_T + ATT_H, :]
            q1 = q_ref[0, 0, qt * ATT_T + ATT_H:(qt + 1) * ATT_T, :]
            if t < qt:
                update((qt, 0), _dot_t1(q0, k), v2)
                update((qt, 1), _dot_t1(q1, k), v2)
            else:
                update((qt, 0), jnp.where(visible, _dot_t1(q0, k[:ATT_H]), -jnp.inf), v2[:ATT_H])
                s1 = _dot_t1(q1, k)
                s1 = jnp.concatenate([s1[:, :ATT_H], jnp.where(visible, s1[:, ATT_H:], -jnp.inf)],
                                     axis=-1)
                update((qt, 1), s1, v2)
                for half in range(2):
                    _, l, acc = state[(qt, half)]
                    lo = qt * ATT_T + half * ATT_H
                    o_ref[0, lo:lo + ATT_H, :] = (acc / l).astype(BF16)


def _mla_attn(q, k, v):
    batch, heads, seq, _ = q.shape
    return pl.pallas_call(
        _attn_kernel,
        out_shape=jax.ShapeDtypeStruct((batch, seq, heads * MLA_V), BF16),
        grid=(batch, heads),
        in_specs=[
            pl.BlockSpec((1, 1, seq, MLA_QK), lambda b, h: (b, h, 0, 0)),
            pl.BlockSpec((1, 1, seq, MLA_QK), lambda b, h: (b, h, 0, 0)),
            pl.BlockSpec((1, seq, MLA_V), lambda b, h: (b, 0, h)),
        ],
        out_specs=pl.BlockSpec((1, seq, MLA_V), lambda b, h: (b, 0, h)),
        compiler_params=_params(("parallel", "parallel")),
        name="mla_attn",
    )(q, k, v)


GLA_L = 512
GLA_NC = GLA_L // CHUNK
GLA_HP = 4


def _gla_head(log_a, q, k, v, go, gg, state_ref, causal, tri):
    la_hi = log_a.astype(BF16)
    la_lo = (log_a - la_hi.astype(F32)).astype(BF16)
    cum = _dot(tri, jnp.concatenate([la_hi, la_lo], axis=-1))
    b_cum = cum[:, :GLA_DK] + cum[:, GLA_DK:]
    b3 = b_cum.reshape(GLA_NC, CHUNK, GLA_DK)
    b_end = b3[:, CHUNK - 1:CHUNK, :]
    b_tot = jnp.broadcast_to(b_end, b3.shape).reshape(GLA_L, GLA_DK)

    qf = q.astype(F32) * (GLA_DK ** -0.5)
    kf = k.astype(F32)
    q_dec = (qf * jnp.exp(b_cum)).astype(BF16)
    k_dec = (kf * jnp.exp(-b_cum)).astype(BF16)
    k_end = (kf * jnp.exp(b_tot - b_cum)).astype(BF16)

    attn = jnp.where(causal, _dot_t1(q_dec, k_dec), 0.0).astype(BF16)
    o_intra = _dot(attn, v)

    decay = jnp.exp(b_end)
    state = state_ref[...]
    outs = []
    for n in range(GLA_NC):
        rows = slice(n * CHUNK, (n + 1) * CHUNK)
        outs.append(o_intra[rows] + _dot_t1(q_dec[rows], state.astype(BF16)))
        state = state * decay[n] + _dot_t0(v[rows], k_end[rows])
    state_ref[...] = state

    o = jnp.concatenate(outs, axis=0)
    return (_rms(o, gg) * _silu(go.astype(F32))).astype(BF16)


def _gla_kernel(q_ref, k_ref, v_ref, go_ref, tail_ref, wg_ref, bg_ref, gg_ref,
                o_ref, state_ref):
    @pl.when(pl.program_id(2) == 0)
    def _():
        state_ref[...] = jnp.zeros_like(state_ref)

    wg = wg_ref[...]
    wg_hi = wg.astype(BF16)
    wg_lo = (wg - wg_hi.astype(F32)).astype(BF16)
    tl = tail_ref[...]
    tl_hi = tl.astype(BF16)
    tl_lo = (tl - tl_hi.astype(F32)).astype(BF16)
    z = _dot(tl_hi, wg_hi) + _dot(tl_lo, wg_hi) + _dot(tl_hi, wg_lo) + bg_ref[...]
    log_a = (jnp.minimum(z, 0.0) - jnp.log1p(jnp.exp(-jnp.abs(z)))) * (1.0 / GLA_GATE_NORMALIZER)

    r = lax.broadcasted_iota(jnp.int32, (GLA_L, GLA_L), 0)
    c = lax.broadcasted_iota(jnp.int32, (GLA_L, GLA_L), 1)
    causal = ((r // CHUNK) == (c // CHUNK)) & (c <= r)
    tri = causal.astype(BF16)

    for hh in range(GLA_HP):
        dk = slice(hh * GLA_DK, (hh + 1) * GLA_DK)
        dv = slice(hh * GLA_DV, (hh + 1) * GLA_DV)
        o_ref[:, dv] = _gla_head(log_a[:, dk], q_ref[:, dk], k_ref[:, dk], v_ref[:, dv],
                                 go_ref[:, dv], gg_ref[...], state_ref.at[hh], causal, tri)


def _gla(proj, tail, wg_pad, b_gk, g_gla, *, batch, seq):
    t = proj.shape[0]
    steps = seq // GLA_L
    tok = lambda b, h, i: b * steps + i
    wk, wv = GLA_HP * GLA_DK, GLA_HP * GLA_DV
    return pl.pallas_call(
        _gla_kernel,
        out_shape=jax.ShapeDtypeStruct((t, GLA_VAL), BF16),
        grid=(batch, GLA_HEADS // GLA_HP, steps),
        in_specs=[
            pl.BlockSpec((GLA_L, wk), lambda b, h, i: (tok(b, h, i), COL_GQ // wk + h)),
            pl.BlockSpec((GLA_L, wk), lambda b, h, i: (tok(b, h, i), COL_GK // wk + h)),
            pl.BlockSpec((GLA_L, wv), lambda b, h, i: (tok(b, h, i), COL_GV // wv + h)),
            pl.BlockSpec((GLA_L, wv), lambda b, h, i: (tok(b, h, i), COL_GOUT // wv + h)),
            pl.BlockSpec((GLA_L, TAIL_W), lambda b, h, i: (tok(b, h, i), 0)),
            pl.BlockSpec((TAIL_W, wk), lambda b, h, i: (0, h)),
            pl.BlockSpec((1, wk), lambda b, h, i: (0, h)),
            pl.BlockSpec((1, GLA_DV), lambda b, h, i: (0, 0)),
        ],
        out_specs=pl.BlockSpec((GLA_L, wv), lambda b, h, i: (tok(b, h, i), h)),
        scratch_shapes=[pltpu.VMEM((GLA_HP, GLA_DV, GLA_DK), F32)],
        compiler_params=_params(("parallel", "parallel", "arbitrary")),
        name="gla",
    )(proj, proj, proj, proj, tail, wg_pad, b_gk, g_gla)


MRG_TM = 512


def _merge_kernel(x_ref, mod_ref, a_ref, b_ref, ga_ref, gb_ref, wa_ref, wb_ref, wo_ref, o_ref,
                  *, base):
    ya = _dot(a_ref[...], wa_ref[...])
    yb = _dot(b_ref[...], wb_ref[...])
    merged = (jax.nn.sigmoid(ga_ref[...].astype(F32)) * ya
              + jax.nn.sigmoid(gb_ref[...].astype(F32)) * yb).astype(BF16)
    gate = mod_ref[0, base + 2:base + 3, :]
    o_ref[...] = x_ref[...] + gate * _dot(merged, wo_ref[...])


def _merge(x2d, mod3, attn_o, gla_o, proj, wa, wb, wo, *, base, seq):
    t, d = x2d.shape
    tiles_per_batch = seq // MRG_TM
    row = lambda i: (i, 0)
    resident = lambda a: pl.BlockSpec(a.shape, lambda i: (0, 0), pipeline_mode=pl.Buffered(1))
    return pl.pallas_call(
        functools.partial(_merge_kernel, base=base),
        out_shape=jax.ShapeDtypeStruct((t, d), F32),
        grid=(t // MRG_TM,),
        in_specs=[
            pl.BlockSpec((MRG_TM, d), row),
            pl.BlockSpec((1, N_MOD, d), lambda i: (i // tiles_per_batch, 0, 0)),
            pl.BlockSpec((MRG_TM, attn_o.shape[1]), row),
            pl.BlockSpec((MRG_TM, gla_o.shape[1]), row),
            pl.BlockSpec((MRG_TM, d), lambda i: (i, COL_GATE_A // d)),
            pl.BlockSpec((MRG_TM, d), lambda i: (i, COL_GATE_A // d + 1)),
            resident(wa), resident(wb), resident(wo),
        ],
        out_specs=pl.BlockSpec((MRG_TM, d), row),
        compiler_params=_params(("parallel",)),
        name="merge",
    )(x2d, mod3, attn_o, gla_o, proj, proj, wa, wb, wo)


def _layer(x2d, mod3, pos3d, p, *, batch, seq):
    d = x2d.shape[1]
    assert IN_SPLITS[-1] == d and sum(IN_SPLITS) == p["w_in"].shape[1]
    w_main, w_tail = _relayout_w_in(p["w_in"].T)
    wuq = p["w_uq"].reshape(MLA_Q_RANK, MLA_HEADS, MLA_QK)
    wuq = jnp.concatenate([wuq[:, :, :MLA_NOPE].reshape(MLA_Q_RANK, -1),
                           wuq[:, :, MLA_NOPE:].reshape(MLA_Q_RANK, -1)], axis=1).astype(BF16)
    wukv = p["w_ukv"].reshape(MLA_KV_RANK, MLA_HEADS, MLA_NOPE + MLA_V)
    wukv = jnp.concatenate([wukv[:, :, :MLA_NOPE].reshape(MLA_KV_RANK, -1),
                            wukv[:, :, MLA_NOPE:].reshape(MLA_KV_RANK, -1)], axis=1).astype(BF16)
    wg_pad = jnp.zeros((TAIL_W, GLA_KEY), F32).at[TAIL_GLR:TAIL_GLR + GLA_GATE_RANK].set(p["w_gk_up"])

    inv_freq = ROPE_THETA ** (-jnp.arange(0, MLA_ROPE, 2, dtype=F32) / MLA_ROPE)
    invf = inv_freq.reshape(ROPE_HALF, 1)
    sgn = jnp.tile(jnp.concatenate([-jnp.ones(ROPE_HALF, F32), jnp.ones(ROPE_HALF, F32)]),
                   LANES // MLA_ROPE).reshape(1, LANES)
    n_rope = MLA_HEADS * MLA_ROPE
    lane = jnp.arange(n_rope)
    grp = (lane[:, None] // MLA_ROPE == lane[None, :] // MLA_ROPE).astype(BF16)

    row = lambda a: a.reshape(1, -1)
    q_scale = MLA_QK ** -0.5 * LOG2E
    x2d = _ffn(x2d, mod3, row(p["g_ffn1"]), p["w1_a"][None], p["w3_a"][None],
               p["w2_a"][None], row(p["g_final"]), base=0, final_norm=False, seq=seq)
    proj, tail = _inproj(x2d, mod3, row(p["g_mix"]), w_main, w_tail, base=3, seq=seq)
    q, k, v = _mla_prep(
        proj, tail, pos3d, (invf, sgn, grp),
        (row(p["g_q_lat"]), wuq, row(p["g_qn"] * q_scale),
         row(jnp.tile(p["g_qr"], MLA_HEADS) * q_scale),
         row(p["g_kv_lat"]), wukv, row(p["g_kn"]), row(p["g_kr"])),
        batch=batch, seq=seq)
    attn_o = _mla_attn(q, k, v).reshape(batch * seq, MLA_HEADS * MLA_V)
    gla_o = _gla(proj, tail, wg_pad, row(p["b_gk"]), row(p["g_gla"]), batch=batch, seq=seq)
    x2d = _merge(x2d, mod3, attn_o, gla_o, proj, p["w_proj_a"].astype(BF16),
                 p["w_proj_b"].astype(BF16), p["w_out"].astype(BF16), base=3, seq=seq)
    x2d = _ffn(x2d, mod3, row(p["g_ffn2"]), p["w1_b"][None], p["w3_b"][None],
               p["w2_b"][None], row(p["g_final"]), base=6, final_norm=True, seq=seq)
    return x2d


def kernel(x, c, positions, w_ada, b_ada, g_ffn1, w1_a, w3_a, w2_a, g_mix, w_in, g_q_lat, w_uq, g_qn, g_qr, g_kv_lat, w_ukv, g_kn, g_kr, w_gk_up, b_gk, g_gla, w_proj_a, w_proj_b, w_out, g_ffn2, w1_b, w3_b, w2_b, g_final):
    batch, seq, d = x.shape
    depth = w_ada.shape[0]
    assert depth == 1, "the final norm is fused into the last FFN of a single layer"
    names = ("g_ffn1", "w1_a", "w3_a", "w2_a", "g_mix", "w_in", "g_q_lat", "w_uq", "g_qn", "g_qr",
             "g_kv_lat", "w_ukv", "g_kn", "g_kr", "w_gk_up", "b_gk", "g_gla", "w_proj_a", "w_proj_b",
             "w_out", "g_ffn2", "w1_b", "w3_b", "w2_b", "g_final")
    stacked = (g_ffn1, w1_a, w3_a, w2_a, g_mix, w_in, g_q_lat, w_uq, g_qn, g_qr, g_kv_lat, w_ukv,
               g_kn, g_kr, w_gk_up, b_gk, g_gla, w_proj_a, w_proj_b, w_out, g_ffn2, w1_b, w3_b, w2_b,
               g_final)
    x2d = x.reshape(batch * seq, d)
    pos3d = positions.reshape(batch * seq // PREP_TM, 1, PREP_TM)
    p = {n: a[0] for n, a in zip(names, stacked)}
    mod3 = _adaln(c, w_ada[0], b_ada[0]).reshape(batch, N_MOD, d)
    x2d = _layer(x2d, mod3, pos3d, p, batch=batch, seq=seq)
    return x2d.reshape(batch, seq, d)
```

```python
import functools

import jax
import jax.numpy as jnp
from jax import lax
from jax.experimental import pallas as pl
from jax.experimental.pallas import tpu as pltpu

F32 = jnp.float32
BF16 = jnp.bfloat16

EPS = 1e-6
LOG2E = 1.4426950408889634
CHUNK = 64
MLA_HEADS = 8
MLA_Q_RANK = 512
MLA_KV_RANK = 512
MLA_NOPE = 128
MLA_ROPE = 64
MLA_V = 128
MLA_QK = MLA_NOPE + MLA_ROPE
ROPE_THETA = 10000.0
GLA_HEADS = 4
GLA_DK = 128
GLA_DV = 256
GLA_GATE_RANK = 16
GLA_GATE_NORMALIZER = 16.0
GLA_KEY = GLA_HEADS * GLA_DK
GLA_VAL = GLA_HEADS * GLA_DV
N_MOD = 9

LANES = 128
SUBLANES = 8
VMEM_LIMIT = 56 * 1024 * 1024

COL_QLAT = 0
COL_KVLAT = COL_QLAT + MLA_Q_RANK
COL_GQ = COL_KVLAT + MLA_KV_RANK
COL_GK = COL_GQ + GLA_KEY
COL_GV = COL_GK + GLA_KEY
COL_GOUT = COL_GV + GLA_VAL
COL_GATE_A = COL_GOUT + GLA_VAL
TAIL_W = LANES
TAIL_KROPE = 0
TAIL_GLR = MLA_ROPE


def _params(sem):
    return pltpu.CompilerParams(dimension_semantics=sem, vmem_limit_bytes=VMEM_LIMIT)


def _dot(a, b):
    return jnp.dot(a, b, preferred_element_type=F32)


def _dot_t0(a, b):
    return lax.dot_general(a, b, (((0,), (0,)), ((), ())), preferred_element_type=F32)


def _dot_t1(a, b):
    return lax.dot_general(a, b, (((1,), (1,)), ((), ())), preferred_element_type=F32)


def _split_dot(x, w_bf16):
    hi = x.astype(BF16)
    lo = (x - hi.astype(F32)).astype(BF16)
    return _dot(hi, w_bf16) + _dot(lo, w_bf16)


def _split_dot_t0(x, w_bf16):
    hi = x.astype(BF16)
    lo = (x - hi.astype(F32)).astype(BF16)
    return _dot_t0(hi, w_bf16) + _dot_t0(lo, w_bf16)


def _silu(x):
    return x * jax.nn.sigmoid(x)


def _rms(x, g):
    ms = jnp.mean(x * x, axis=-1, keepdims=True)
    return x * lax.rsqrt(ms + EPS) * g


ADA_TN = 1024
ADA_RC = 64


def _adaln_kernel(ct_ref, w_ref, b_ref, o_ref, s_ref):
    d, nb = ct_ref.shape
    tn = w_ref.shape[1]

    @pl.when(pl.program_id(0) == 0)
    def _():
        s = _silu(ct_ref[...])
        for b in range(nb):
            s_ref[b] = jnp.broadcast_to(s[:, b:b + 1], (d, LANES))

    def body(k, acc):
        rows = pl.ds(pl.multiple_of(k * ADA_RC, ADA_RC), ADA_RC)
        out = []
        for b in range(nb):
            sb = s_ref[b, rows, :]
            cols = []
            for col in range(tn // LANES):
                p = w_ref[rows, col * LANES:(col + 1) * LANES] * sb
                cols.append(p.reshape(ADA_RC // SUBLANES, SUBLANES, LANES).sum(axis=0))
            out.append(acc[b] + jnp.concatenate(cols, axis=-1))
        return tuple(out)

    init = tuple(jnp.zeros((SUBLANES, tn), F32) for _ in range(nb))
    acc = lax.fori_loop(0, d // ADA_RC, body, init, unroll=2)
    rows = [jnp.sum(a, axis=0, keepdims=True) for a in acc]
    o_ref[...] = jnp.concatenate(rows, axis=0) + b_ref[...]


def _adaln(c, w_ada, b_ada):
    nb, d = c.shape
    n = w_ada.shape[1]
    return pl.pallas_call(
        _adaln_kernel,
        out_shape=jax.ShapeDtypeStruct((nb, n), F32),
        grid=(n // ADA_TN,),
        in_specs=[
            pl.BlockSpec((d, nb), lambda j: (0, 0)),
            pl.BlockSpec((d, ADA_TN), lambda j: (0, j)),
            pl.BlockSpec((1, ADA_TN), lambda j: (0, j)),
        ],
        out_specs=pl.BlockSpec((nb, ADA_TN), lambda j: (0, j)),
        scratch_shapes=[pltpu.VMEM((nb, d, LANES), F32)],
        compiler_params=_params(("arbitrary",)),
        name="adaln",
    )(c.T, w_ada, b_ada.reshape(1, n))


def _mod_norm(x, g, mod_ref, base):
    shift = mod_ref[0, base:base + 1, :]
    scale = mod_ref[0, base + 1:base + 2, :]
    return _rms(x, g) * (1.0 + scale) + shift


FFN_TM = 512
FFN_TF = 512
FFN_HEAD_TILES = 2
FFN_HEAD_TF = 256
FFN_GROUP = 256
FFN_AHEAD_CHUNKS = 8
FFN_AHEAD_ROWS = FFN_TM // FFN_AHEAD_CHUNKS


def _swiglu_accumulate(h, w1g, w3g, w2t, o_ref):
    upd = None
    for s in range(w1g.shape[0]):
        a = _dot(h, w1g[s])
        b = _dot(h, w3g[s])
        p = (_silu(a) * b).astype(BF16)
        part = _dot(p, w2t[s * FFN_GROUP:(s + 1) * FFN_GROUP, :])
        upd = part if upd is None else upd + part
    o_ref[...] += upd


def _ffn_finish(x_ref, mod_ref, gf_ref, o_ref, *, base, final_norm):
    gate = mod_ref[0, base + 2:base + 3, :]
    y = x_ref[...] + 0.5 * gate * o_ref[...]
    if final_norm:
        y = _rms(y, gf_ref[...])
    o_ref[...] = y


def _ffn_head_kernel(x_ref, mod_ref, g_ref, w1_ref, w3_ref, w2_ref, gf_ref,
                     o_ref, w1b_ref, w3b_ref, w2b_ref, h_ref, *, base, final_norm):
    j = pl.program_id(1)

    @pl.when(j == 0)
    def _():
        h_ref[...] = _mod_norm(x_ref[...], g_ref[...], mod_ref, base).astype(BF16)
        o_ref[...] = jnp.zeros_like(o_ref)

    w1 = w1_ref[0].astype(BF16)
    w3 = w3_ref[0].astype(BF16)
    w2 = w2_ref[0].astype(BF16)
    w1b_ref[0] = w1
    w3b_ref[0] = w3
    w2b_ref[...] = w2
    _swiglu_accumulate(h_ref[...], w1[None], w3[None], w2, o_ref)

    @pl.when(j == pl.num_programs(1) - 1)
    def _():
        _ffn_finish(x_ref, mod_ref, gf_ref, o_ref, base=base, final_norm=final_norm)


def _ffn_kernel(x_ref, xn_ref, mod_ref, modn_ref, g_ref, w1_ref, w3_ref, w2_ref, gf_ref, y0_ref,
                o_ref, h_ref, *, base, final_norm):
    i = pl.program_id(0)
    j = pl.program_id(1)
    slot = i % 2

    @pl.when(i >= FFN_HEAD_TILES)
    def _():
        @pl.when(j == 0)
        def _():
            o_ref[...] = jnp.zeros_like(o_ref)

            @pl.when(i == FFN_HEAD_TILES)
            def _():
                h_ref[slot] = _mod_norm(x_ref[...], g_ref[...], mod_ref, base).astype(BF16)

        chunk = jnp.minimum(j, FFN_AHEAD_CHUNKS - 1)
        rows = pl.ds(pl.multiple_of(chunk * FFN_AHEAD_ROWS, FFN_AHEAD_ROWS), FFN_AHEAD_ROWS)
        h_next = _mod_norm(xn_ref[rows, :], g_ref[...], modn_ref, base).astype(BF16)

        _swiglu_accumulate(h_ref[slot], w1_ref, w3_ref, w2_ref[...], o_ref)
        h_ref[1 - slot, rows, :] = h_next

        @pl.when(j == pl.num_programs(1) - 1)
        def _():
            _ffn_finish(x_ref, mod_ref, gf_ref, o_ref, base=base, final_norm=final_norm)

    @pl.when((i < FFN_HEAD_TILES) & (j == pl.num_programs(1) - 1))
    def _():
        o_ref[...] = y0_ref[...]


def _ffn(x2d, mod3, g, w1, w3, w2, g_final, *, base, final_norm, seq):
    t, d = x2d.shape
    dff = w1.shape[2]
    tiles_per_batch = seq // FFN_TM
    const = lambda i, j: (0, 0)
    kw = dict(base=base, final_norm=final_norm)
    tag = "_final" if final_norm else ""
    common = [
        pl.BlockSpec((1, N_MOD, d), lambda i, j: (i // tiles_per_batch, 0, 0)),
        pl.BlockSpec((1, d), const),
    ]
    hf = FFN_HEAD_TF
    head_rows = FFN_HEAD_TILES * FFN_TM
    assert head_rows <= seq
    assert hf == FFN_GROUP
    w13_shape = jax.ShapeDtypeStruct((dff // FFN_GROUP, d, FFN_GROUP), BF16)
    w13_spec = pl.BlockSpec((1, d, FFN_GROUP), lambda i, j: (j, 0, 0))
    y0, w1b, w3b, w2b = pl.pallas_call(
        functools.partial(_ffn_head_kernel, **kw),
        out_shape=(jax.ShapeDtypeStruct((head_rows, d), F32), w13_shape, w13_shape,
                   jax.ShapeDtypeStruct((dff, d), BF16)),
        grid=(1, dff // hf),
        in_specs=[pl.BlockSpec((head_rows, d), const, pipeline_mode=pl.Buffered(1))] + common + [
            pl.BlockSpec((1, d, hf), lambda i, j: (0, 0, j)),
            pl.BlockSpec((1, d, hf), lambda i, j: (0, 0, j)),
            pl.BlockSpec((1, hf, d), lambda i, j: (0, j, 0)),
            pl.BlockSpec((1, d), const),
        ],
        out_specs=(pl.BlockSpec((head_rows, d), const), w13_spec, w13_spec,
                   pl.BlockSpec((hf, d), lambda i, j: (j, 0))),
        scratch_shapes=[pltpu.VMEM((head_rows, d), BF16)],
        compiler_params=_params(("arbitrary", "arbitrary")),
        name="ffn_head" + tag,
    )(x2d, mod3, g, w1, w3, w2, g_final)

    wj = lambda i, j: jnp.where(i < FFN_HEAD_TILES, 0, j)
    wtile = lambda i, j: (wj(i, j), 0, 0)
    wrow = lambda i, j: (wj(i, j), 0)
    n_tiles = t // FFN_TM
    assert dff // FFN_TF >= FFN_AHEAD_CHUNKS
    nxt = lambda i: jnp.minimum(i + 1, n_tiles - 1)
    return pl.pallas_call(
        functools.partial(_ffn_kernel, **kw),
        out_shape=jax.ShapeDtypeStruct((t, d), F32),
        grid=(n_tiles, dff // FFN_TF),
        in_specs=[
            pl.BlockSpec((FFN_TM, d), lambda i, j: (i, 0)),
            pl.BlockSpec((FFN_TM, d), lambda i, j: (nxt(i), 0)),
            pl.BlockSpec((1, N_MOD, d), lambda i, j: (i // tiles_per_batch, 0, 0)),
            pl.BlockSpec((1, N_MOD, d), lambda i, j: (nxt(i) // tiles_per_batch, 0, 0)),
            pl.BlockSpec((1, d), const),
            pl.BlockSpec((FFN_TF // FFN_GROUP, d, FFN_GROUP), wtile),
            pl.BlockSpec((FFN_TF // FFN_GROUP, d, FFN_GROUP), wtile),
            pl.BlockSpec((FFN_TF, d), wrow),
            pl.BlockSpec((1, d), const),
            pl.BlockSpec((FFN_TM, d), lambda i, j: (jnp.minimum(i, FFN_HEAD_TILES - 1), 0)),
        ],
        out_specs=pl.BlockSpec((FFN_TM, d), lambda i, j: (i, 0)),
        scratch_shapes=[pltpu.VMEM((2, FFN_TM, d), BF16)],
        compiler_params=_params(("arbitrary", "arbitrary")),
        name="ffn" + tag,
    )(x2d, x2d, mod3, mod3, g, w1b, w3b, w2b, g_final, y0)


IN_SPLITS = (MLA_Q_RANK, MLA_KV_RANK, MLA_ROPE, GLA_KEY, GLA_KEY, GLA_VAL, GLA_GATE_RANK, GLA_VAL,
             2048, 2048)
MAIN_ORDER = (0, 1, 3, 4, 5, 7, 8, 9)
TAIL_ORDER = (2, 6)
RELAYOUT_ROWS = 512


def _segments(order):
    starts = [sum(IN_SPLITS[:i]) for i in range(len(IN_SPLITS))]
    segs, dst = [], 0
    for i in order:
        src, n = starts[i], IN_SPLITS[i]
        if segs and segs[-1][0] + segs[-1][2] == src:
            segs[-1] = (segs[-1][0], segs[-1][1], segs[-1][2] + n)
        else:
            segs.append((src, dst, n))
        dst += n
    return segs, dst


def _relayout_kernel(lo_ref, hi_ref, kr_ref, glr_ref, main_ref, tail_ref):
    r = pl.program_id(0)
    rows = RELAYOUT_ROWS
    for src, dst, n in _segments(MAIN_ORDER)[0]:
        shift = src - dst

        @pl.when((r >= dst // rows) & (r < (dst + n) // rows))
        def _(shift=shift):
            main_ref[:rows - shift, :] = lo_ref[shift:, :].astype(BF16)
            if shift:
                main_ref[rows - shift:, :] = hi_ref[:shift, :].astype(BF16)

    tail_ref[:MLA_ROPE, :] = kr_ref[...].astype(BF16)
    tail_ref[MLA_ROPE:MLA_ROPE + GLA_GATE_RANK, :] = glr_ref[...].astype(BF16)
    pad = TAIL_W - MLA_ROPE - GLA_GATE_RANK
    tail_ref[MLA_ROPE + GLA_GATE_RANK:, :] = jnp.zeros((pad, tail_ref.shape[1]), BF16)


def _relayout_w_in(w_in_t):
    d_in, d = w_in_t.shape
    segs, n_main = _segments(MAIN_ORDER)
    rows = RELAYOUT_ROWS
    assert all(dst % rows == 0 and n % rows == 0 and 0 <= src - dst <= LANES and (src - dst) % 16 == 0
               for src, dst, n in segs)
    (kr_src, _, _), (glr_src, _, _) = _segments(TAIL_ORDER)[0]
    assert kr_src % MLA_ROPE == 0 and glr_src % GLA_GATE_RANK == 0
    return pl.pallas_call(
        _relayout_kernel,
        out_shape=(jax.ShapeDtypeStruct((n_main, d), BF16), jax.ShapeDtypeStruct((TAIL_W, d), BF16)),
        grid=(n_main // rows,),
        in_specs=[pl.BlockSpec((rows, d), lambda r: (r, 0)),
                  pl.BlockSpec((LANES, d), lambda r: ((r + 1) * (rows // LANES), 0)),
                  pl.BlockSpec((MLA_ROPE, d), lambda r: (kr_src // MLA_ROPE, 0)),
                  pl.BlockSpec((GLA_GATE_RANK, d), lambda r: (glr_src // GLA_GATE_RANK, 0))],
        out_specs=(pl.BlockSpec((rows, d), lambda r: (r, 0)),
                   pl.BlockSpec((TAIL_W, d), lambda r: (0, 0))),
        compiler_params=_params(("arbitrary",)),
        name="w_in_layout",
    )(w_in_t, w_in_t, w_in_t, w_in_t)


INP_TM = 1024
INP_TN = 1024


def _inproj_kernel(x_ref, mod_ref, g_ref, w_ref, wt_ref, o_ref, tail_ref, h_ref, *, base):
    @pl.when(pl.program_id(1) == 0)
    def _():
        h = _mod_norm(x_ref[...], g_ref[...], mod_ref, base).astype(BF16)
        h_ref[...] = h
        tail_ref[...] = _dot_t1(h, wt_ref[...])

    o_ref[...] = _dot_t1(h_ref[...], w_ref[...]).astype(BF16)


def _inproj(x2d, mod3, g, w_main, w_tail, *, base, seq):
    t, d = x2d.shape
    n = w_main.shape[0]
    tiles_per_batch = seq // INP_TM
    return pl.pallas_call(
        functools.partial(_inproj_kernel, base=base),
        out_shape=(jax.ShapeDtypeStruct((t, n), BF16),
                   jax.ShapeDtypeStruct((t, TAIL_W), F32)),
        grid=(t // INP_TM, n // INP_TN),
        in_specs=[
            pl.BlockSpec((INP_TM, d), lambda i, j: (i, 0)),
            pl.BlockSpec((1, N_MOD, d), lambda i, j: (i // tiles_per_batch, 0, 0)),
            pl.BlockSpec((1, d), lambda i, j: (0, 0)),
            pl.BlockSpec((INP_TN, d), lambda i, j: (j, 0)),
            pl.BlockSpec((TAIL_W, d), lambda i, j: (0, 0)),
        ],
        out_specs=(pl.BlockSpec((INP_TM, INP_TN), lambda i, j: (i, j)),
                   pl.BlockSpec((INP_TM, TAIL_W), lambda i, j: (i, 0))),
        scratch_shapes=[pltpu.VMEM((INP_TM, d), BF16)],
        compiler_params=_params(("parallel", "arbitrary")),
        name="in_proj",
    )(x2d, mod3, g, w_main, w_tail)


PREP_TM = 256
ROPE_HALF = MLA_ROPE // 2


def _rope_rotate(x, cos, sin_signed):
    width = x.shape[-1]
    lane = lax.broadcasted_iota(jnp.int32, x.shape, 1)
    first_half = (lane % MLA_ROPE) < ROPE_HALF
    partner = jnp.where(first_half,
                        pltpu.roll(x, width - ROPE_HALF, 1),
                        pltpu.roll(x, ROPE_HALF, 1))
    return x * cos + partner * sin_signed


def _prep_kernel(qlat_ref, kvlat_ref, tail_ref, pos_ref, invf_ref, sgn_ref,
                 gql_ref, wuq_ref, gqn_ref, gqr_ref, gkvl_ref, wukv_ref, gkn_ref, gkr_ref,
                 grp_ref, q_ref, k_ref, v_ref):
    ang_t = invf_ref[...] * pos_ref[0].astype(F32)
    lane_reps = LANES // ROPE_HALF
    cos = jnp.tile(jnp.cos(ang_t).T, (1, lane_reps))
    sin_signed = jnp.tile(jnp.sin(ang_t).T, (1, lane_reps)) * sgn_ref[...]
    n_rope = MLA_HEADS * MLA_ROPE
    cos_q = jnp.concatenate([cos] * (n_rope // LANES), axis=-1)
    sin_q = jnp.concatenate([sin_signed] * (n_rope // LANES), axis=-1)

    cq = _rms(qlat_ref[...].astype(F32), gql_ref[...]).astype(BF16)
    ckv = _rms(kvlat_ref[...].astype(F32), gkvl_ref[...]).astype(BF16)
    n_nope = MLA_HEADS * MLA_NOPE
    pair = 2 * MLA_NOPE

    for hp in range(MLA_HEADS // 2):
        cols = slice(hp * pair, (hp + 1) * pair)
        qp = _dot(cq, wuq_ref[:, cols])
        kp = _dot(ckv, wukv_ref[:, cols])
        for hh in range(2):
            h = 2 * hp + hh
            sub = slice(hh * MLA_NOPE, (hh + 1) * MLA_NOPE)
            q_ref[0, h, :, :MLA_NOPE] = _rms(qp[:, sub], gqn_ref[...]).astype(BF16)
            k_ref[0, h, :, :MLA_NOPE] = _rms(kp[:, sub], gkn_ref[...]).astype(BF16)
        v_ref[0, :, cols] = _dot(ckv, wukv_ref[:, n_nope + hp * pair:n_nope + (hp + 1) * pair]).astype(BF16)

    qr = _dot(cq, wuq_ref[:, n_nope:])
    ssq = _split_dot(qr * qr, grp_ref[...])
    qr = qr * lax.rsqrt(ssq * (1.0 / MLA_ROPE) + EPS) * gqr_ref[...]
    qr = _rope_rotate(qr, cos_q, sin_q).astype(BF16)
    kr = tail_ref[:, TAIL_KROPE:TAIL_KROPE + MLA_ROPE]
    kr = _rms(kr, gkr_ref[...])
    kr = _rope_rotate(kr, cos[:, :MLA_ROPE], sin_signed[:, :MLA_ROPE]).astype(BF16)
    for h in range(MLA_HEADS):
        q_ref[0, h, :, MLA_NOPE:] = qr[:, h * MLA_ROPE:(h + 1) * MLA_ROPE]
        k_ref[0, h, :, MLA_NOPE:] = kr


def _mla_prep(proj, tail, pos3d, consts, weights, *, batch, seq):
    t = proj.shape[0]
    tiles_per_batch = seq // PREP_TM
    row = lambda i: (i, 0)
    const = lambda i: (0, 0)
    (invf, sgn, grp) = consts
    (gql, wuq, gqn, gqr, gkvl, wukv, gkn, gkr) = weights
    head_out = pl.BlockSpec((1, MLA_HEADS, PREP_TM, MLA_QK),
                            lambda i: (i // tiles_per_batch, 0, i % tiles_per_batch, 0))
    full = lambda a: pl.BlockSpec(a.shape, const)
    return pl.pallas_call(
        _prep_kernel,
        out_shape=(jax.ShapeDtypeStruct((batch, MLA_HEADS, seq, MLA_QK), BF16),
                   jax.ShapeDtypeStruct((batch, MLA_HEADS, seq, MLA_QK), BF16),
                   jax.ShapeDtypeStruct((batch, seq, MLA_HEADS * MLA_V), BF16)),
        grid=(t // PREP_TM,),
        in_specs=[
            pl.BlockSpec((PREP_TM, MLA_Q_RANK), lambda i: (i, COL_QLAT // MLA_Q_RANK)),
            pl.BlockSpec((PREP_TM, MLA_KV_RANK), lambda i: (i, COL_KVLAT // MLA_KV_RANK)),
            pl.BlockSpec((PREP_TM, TAIL_W), row),
            pl.BlockSpec((1, 1, PREP_TM), lambda i: (i, 0, 0)),
            full(invf), full(sgn),
            full(gql), full(wuq), full(gqn), full(gqr),
            full(gkvl), full(wukv), full(gkn), full(gkr),
            full(grp),
        ],
        out_specs=(head_out, head_out,
                   pl.BlockSpec((1, PREP_TM, MLA_HEADS * MLA_V),
                                lambda i: (i // tiles_per_batch, i % tiles_per_batch, 0))),
        compiler_params=_params(("parallel",)),
        name="mla_prep",
    )(proj, proj, tail, pos3d, invf, sgn, gql, wuq, gqn, gqr, gkvl, wukv, gkn, gkr, grp)


ATT_T = 1024
ATT_H = ATT_T // 2


def _attn_kernel(q_ref, k_ref, v_ref, o_ref):
    n_tiles = k_ref.shape[2] // ATT_T
    ones = jnp.ones((ATT_T, MLA_V), BF16)
    r = lax.broadcasted_iota(jnp.int32, (ATT_H, ATT_H), 0)
    c = lax.broadcasted_iota(jnp.int32, (ATT_H, ATT_H), 1)
    visible = (c // CHUNK) <= (r // CHUNK)
    state = {}

    def update(chain, s, v2):
        m_cur = jnp.max(s, axis=-1, keepdims=True)
        if chain not in state:
            m_next = jnp.broadcast_to(m_cur, (ATT_H, LANES))
        else:
            m_prev, l_prev, acc_prev = state[chain]
            m_next = jnp.maximum(m_prev, m_cur)
            alpha = jnp.exp2(m_prev - m_next)
        p = jnp.exp2(s - jnp.tile(m_next, (1, s.shape[1] // LANES)))
        pv = _dot(p.astype(BF16), v2)
        acc, l = pv[:, :MLA_V], pv[:, MLA_V:]
        if chain in state:
            acc, l = alpha * acc_prev + acc, alpha * l_prev + l
        state[chain] = (m_next, l, acc)

    for t in range(n_tiles):
        rows = slice(t * ATT_T, (t + 1) * ATT_T)
        k = k_ref[0, 0, rows, :]
        v2 = jnp.concatenate([v_ref[0, rows, :], ones], axis=-1)
        for qt in range(t, n_tiles):
            q0 = q_ref[0, 0, qt * ATT_T:qt * ATT_T + ATT_H, :]
            q1 = q_ref[0, 0, qt * ATT_T + ATT_H:(qt + 1) * ATT_T, :]
            if t < qt:
                update((qt, 0), _dot_t1(q0, k), v2)
                update((qt, 1), _dot_t1(q1, k), v2)
            else:
                update((qt, 0), jnp.where(visible, _dot_t1(q0, k[:ATT_H]), -jnp.inf), v2[:ATT_H])
                s1 = _dot_t1(q1, k)
                s1 = jnp.concatenate([s1[:, :ATT_H], jnp.where(visible, s1[:, ATT_H:], -jnp.inf)],
                                     axis=-1)
                update((qt, 1), s1, v2)
                for half in range(2):
                    _, l, acc = state[(qt, half)]
                    lo = qt * ATT_T + half * ATT_H
                    o_ref[0, lo:lo + ATT_H, :] = (acc / l).astype(BF16)


def _mla_attn(q, k, v):
    batch, heads, seq, _ = q.shape
    return pl.pallas_call(
        _attn_kernel,
        out_shape=jax.ShapeDtypeStruct((batch, seq, heads * MLA_V), BF16),
        grid=(batch, heads),
        in_specs=[
            pl.BlockSpec((1, 1, seq, MLA_QK), lambda b, h: (b, h, 0, 0)),
            pl.BlockSpec((1, 1, seq, MLA_QK), lambda b, h: (b, h, 0, 0)),
            pl.BlockSpec((1, seq, MLA_V), lambda b, h: (b, 0, h)),
        ],
        out_specs=pl.BlockSpec((1, seq, MLA_V), lambda b, h: (b, 0, h)),
        compiler_params=_params(("parallel", "parallel")),
        name="mla_attn",
    )(q, k, v)


GLA_L = 512
GLA_NC = GLA_L // CHUNK
GLA_HP = 4


def _gla_head(log_a, q, k, v, go, gg, state_ref, causal, tri):
    la_hi = log_a.astype(BF16)
    la_lo = (log_a - la_hi.astype(F32)).astype(BF16)
    cum = _dot(tri, jnp.concatenate([la_hi, la_lo], axis=-1))
    b_cum = cum[:, :GLA_DK] + cum[:, GLA_DK:]
    b3 = b_cum.reshape(GLA_NC, CHUNK, GLA_DK)
    b_end = b3[:, CHUNK - 1:CHUNK, :]
    b_tot = jnp.broadcast_to(b_end, b3.shape).reshape(GLA_L, GLA_DK)

    qf = q.astype(F32) * (GLA_DK ** -0.5)
    kf = k.astype(F32)
    q_dec = (qf * jnp.exp(b_cum)).astype(BF16)
    k_dec = (kf * jnp.exp(-b_cum)).astype(BF16)
    k_end = (kf * jnp.exp(b_tot - b_cum)).astype(BF16)

    attn = jnp.where(causal, _dot_t1(q_dec, k_dec), 0.0).astype(BF16)
    o_intra = _dot(attn, v)

    decay = jnp.exp(b_end)
    state = state_ref[...]
    outs = []
    for n in range(GLA_NC):
        rows = slice(n * CHUNK, (n + 1) * CHUNK)
        outs.append(o_intra[rows] + _dot_t1(q_dec[rows], state.astype(BF16)))
        state = state * decay[n] + _dot_t0(v[rows], k_end[rows])
    state_ref[...] = state

    o = jnp.concatenate(outs, axis=0)
    return (_rms(o, gg) * _silu(go.astype(F32))).astype(BF16)


def _gla_kernel(q_ref, k_ref, v_ref, go_ref, tail_ref, wg_ref, bg_ref, gg_ref,
                o_ref, state_ref):
    @pl.when(pl.program_id(2) == 0)
    def _():
        state_ref[...] = jnp.zeros_like(state_ref)

    wg = wg_ref[...]
    wg_hi = wg.astype(BF16)
    wg_lo = (wg - wg_hi.astype(F32)).astype(BF16)
    tl = tail_ref[...]
    tl_hi = tl.astype(BF16)
    tl_lo = (tl - tl_hi.astype(F32)).astype(BF16)
    z = _dot(tl_hi, wg_hi) + _dot(tl_lo, wg_hi) + _dot(tl_hi, wg_lo) + bg_ref[...]
    log_a = (jnp.minimum(z, 0.0) - jnp.log1p(jnp.exp(-jnp.abs(z)))) * (1.0 / GLA_GATE_NORMALIZER)

    r = lax.broadcasted_iota(jnp.int32, (GLA_L, GLA_L), 0)
    c = lax.broadcasted_iota(jnp.int32, (GLA_L, GLA_L), 1)
    causal = ((r // CHUNK) == (c // CHUNK)) & (c <= r)
    tri = causal.astype(BF16)

    for hh in range(GLA_HP):
        dk = slice(hh * GLA_DK, (hh + 1) * GLA_DK)
        dv = slice(hh * GLA_DV, (hh + 1) * GLA_DV)
        o_ref[:, dv] = _gla_head(log_a[:, dk], q_ref[:, dk], k_ref[:, dk], v_ref[:, dv],
                                 go_ref[:, dv], gg_ref[...], state_ref.at[hh], causal, tri)


def _gla(proj, tail, wg_pad, b_gk, g_gla, *, batch, seq):
    t = proj.shape[0]
    steps = seq // GLA_L
    tok = lambda b, h, i: b * steps + i
    wk, wv = GLA_HP * GLA_DK, GLA_HP * GLA_DV
    return pl.pallas_call(
        _gla_kernel,
        out_shape=jax.ShapeDtypeStruct((t, GLA_VAL), BF16),
        grid=(batch, GLA_HEADS // GLA_HP, steps),
        in_specs=[
            pl.BlockSpec((GLA_L, wk), lambda b, h, i: (tok(b, h, i), COL_GQ // wk + h)),
            pl.BlockSpec((GLA_L, wk), lambda b, h, i: (tok(b, h, i), COL_GK // wk + h)),
            pl.BlockSpec((GLA_L, wv), lambda b, h, i: (tok(b, h, i), COL_GV // wv + h)),
            pl.BlockSpec((GLA_L, wv), lambda b, h, i: (tok(b, h, i), COL_GOUT // wv + h)),
            pl.BlockSpec((GLA_L, TAIL_W), lambda b, h, i: (tok(b, h, i), 0)),
            pl.BlockSpec((TAIL_W, wk), lambda b, h, i: (0, h)),
            pl.BlockSpec((1, wk), lambda b, h, i: (0, h)),
            pl.BlockSpec((1, GLA_DV), lambda b, h, i: (0, 0)),
        ],
        out_specs=pl.BlockSpec((GLA_L, wv), lambda b, h, i: (tok(b, h, i), h)),
        scratch_shapes=[pltpu.VMEM((GLA_HP, GLA_DV, GLA_DK), F32)],
        compiler_params=_params(("parallel", "parallel", "arbitrary")),
        name="gla",
    )(proj, proj, proj, proj, tail, wg_pad, b_gk, g_gla)


MRG_TM = 512


def _merge_kernel(x_ref, mod_ref, a_ref, b_ref, ga_ref, gb_ref, wa_ref, wb_ref, wo_ref, o_ref,
                  *, base):
    ya = _dot(a_ref[...], wa_ref[...])
    yb = _dot(b_ref[...], wb_ref[...])
    merged = (jax.nn.sigmoid(ga_ref[...].astype(F32)) * ya
              + jax.nn.sigmoid(gb_ref[...].astype(F32)) * yb).astype(BF16)
    gate = mod_ref[0, base + 2:base + 3, :]
    o_ref[...] = x_ref[...] + gate * _dot(merged, wo_ref[...])


def _merge(x2d, mod3, attn_o, gla_o, proj, wa, wb, wo, *, base, seq):
    t, d = x2d.shape
    tiles_per_batch = seq // MRG_TM
    row = lambda i: (i, 0)
    resident = lambda a: pl.BlockSpec(a.shape, lambda i: (0, 0), pipeline_mode=pl.Buffered(1))
    return pl.pallas_call(
        functools.partial(_merge_kernel, base=base),
        out_shape=jax.ShapeDtypeStruct((t, d), F32),
        grid=(t // MRG_TM,),
        in_specs=[
            pl.BlockSpec((MRG_TM, d), row),
            pl.BlockSpec((1, N_MOD, d), lambda i: (i // tiles_per_batch, 0, 0)),
            pl.BlockSpec((MRG_TM, attn_o.shape[1]), row),
            pl.BlockSpec((MRG_TM, gla_o.shape[1]), row),
            pl.BlockSpec((MRG_TM, d), lambda i: (i, COL_GATE_A // d)),
            pl.BlockSpec((MRG_TM, d), lambda i: (i, COL_GATE_A // d + 1)),
            resident(wa), resident(wb), resident(wo),
        ],
        out_specs=pl.BlockSpec((MRG_TM, d), row),
        compiler_params=_params(("parallel",)),
        name="merge",
    )(x2d, mod3, attn_o, gla_o, proj, proj, wa, wb, wo)


def _layer(x2d, mod3, pos3d, p, *, batch, seq):
    d = x2d.shape[1]
    assert IN_SPLITS[-1] == d and sum(IN_SPLITS) == p["w_in"].shape[1]
    w_main, w_tail = _relayout_w_in(p["w_in"].T)
    wuq = p["w_uq"].reshape(MLA_Q_RANK, MLA_HEADS, MLA_QK)
    wuq = jnp.concatenate([wuq[:, :, :MLA_NOPE].reshape(MLA_Q_RANK, -1),
                           wuq[:, :, MLA_NOPE:].reshape(MLA_Q_RANK, -1)], axis=1).astype(BF16)
    wukv = p["w_ukv"].reshape(MLA_KV_RANK, MLA_HEADS, MLA_NOPE + MLA_V)
    wukv = jnp.concatenate([wukv[:, :, :MLA_NOPE].reshape(MLA_KV_RANK, -1),
                            wukv[:, :, MLA_NOPE:].reshape(MLA_KV_RANK, -1)], axis=1).astype(BF16)
    wg_pad = jnp.zeros((TAIL_W, GLA_KEY), F32).at[TAIL_GLR:TAIL_GLR + GLA_GATE_RANK].set(p["w_gk_up"])

    inv_freq = ROPE_THETA ** (-jnp.arange(0, MLA_ROPE, 2, dtype=F32) / MLA_ROPE)
    invf = inv_freq.reshape(ROPE_HALF, 1)
    sgn = jnp.tile(jnp.concatenate([-jnp.ones(ROPE_HALF, F32), jnp.ones(ROPE_HALF, F32)]),
                   LANES // MLA_ROPE).reshape(1, LANES)
    n_rope = MLA_HEADS * MLA_ROPE
    lane = jnp.arange(n_rope)
    grp = (lane[:, None] // MLA_ROPE == lane[None, :] // MLA_ROPE).astype(BF16)

    row = lambda a: a.reshape(1, -1)
    q_scale = MLA_QK ** -0.5 * LOG2E
    x2d = _ffn(x2d, mod3, row(p["g_ffn1"]), p["w1_a"][None], p["w3_a"][None],
               p["w2_a"][None], row(p["g_final"]), base=0, final_norm=False, seq=seq)
    proj, tail = _inproj(x2d, mod3, row(p["g_mix"]), w_main, w_tail, base=3, seq=seq)
    q, k, v = _mla_prep(
        proj, tail, pos3d, (invf, sgn, grp),
        (row(p["g_q_lat"]), wuq, row(p["g_qn"] * q_scale),
         row(jnp.tile(p["g_qr"], MLA_HEADS) * q_scale),
         row(p["g_kv_lat"]), wukv, row(p["g_kn"]), row(p["g_kr"])),
        batch=batch, seq=seq)
    attn_o = _mla_attn(q, k, v).reshape(batch * seq, MLA_HEADS * MLA_V)
    gla_o = _gla(proj, tail, wg_pad, row(p["b_gk"]), row(p["g_gla"]), batch=batch, seq=seq)
    x2d = _merge(x2d, mod3, attn_o, gla_o, proj, p["w_proj_a"].astype(BF16),
                 p["w_proj_b"].astype(BF16), p["w_out"].astype(BF16), base=3, seq=seq)
    x2d = _ffn(x2d, mod3, row(p["g_ffn2"]), p["w1_b"][None], p["w3_b"][None],
               p["w2_b"][None], row(p["g_final"]), base=6, final_norm=True, seq=seq)
    return x2d


def kernel(x, c, positions, w_ada, b_ada, g_ffn1, w1_a, w3_a, w2_a, g_mix, w_in, g_q_lat, w_uq, g_qn, g_qr, g_kv_lat, w_ukv, g_kn, g_kr, w_gk_up, b_gk, g_gla, w_proj_a, w_proj_b, w_out, g_ffn2, w1_b, w3_b, w2_b, g_final):
    batch, seq, d = x.shape
    depth = w_ada.shape[0]
    assert depth == 1, "the final norm is fused into the last FFN of a single layer"
    names = ("g_ffn1", "w1_a", "w3_a", "w2_a", "g_mix", "w_in", "g_q_lat", "w_uq", "g_qn", "g_qr",
             "g_kv_lat", "w_ukv", "g_kn", "g_kr", "w_gk_up", "b_gk", "g_gla", "w_proj_a", "w_proj_b",
             "w_out", "g_ffn2", "w1_b", "w3_b", "w2_b", "g_final")
    stacked = (g_ffn1, w1_a, w3_a, w2_a, g_mix, w_in, g_q_lat, w_uq, g_qn, g_qr, g_kv_lat, w_ukv,
               g_kn, g_kr, w_gk_up, b_gk, g_gla, w_proj_a, w_proj_b, w_out, g_ffn2, w1_b, w3_b, w2_b,
               g_final)
    x2d = x.reshape(batch * seq, d)
    pos3d = positions.reshape(batch * seq // PREP_TM, 1, PREP_TM)
    p = {n: a[0] for n, a in zip(names, stacked)}
    mod3 = _adaln(c, w_ada[0], b_ada[0]).reshape(batch, N_MOD, d)
    x2d = _layer(x2d, mod3, pos3d, p, batch=batch, seq=seq)
    return x2d.reshape(batch, seq, d)
```

```python
import functools

import jax
import jax.numpy as jnp
from jax import lax
from jax.experimental import pallas as pl
from jax.experimental.pallas import tpu as pltpu

F32 = jnp.float32
BF16 = jnp.bfloat16

EPS = 1e-6
LOG2E = 1.4426950408889634
CHUNK = 64
MLA_HEADS = 8
MLA_Q_RANK = 512
MLA_KV_RANK = 512
MLA_NOPE = 128
MLA_ROPE = 64
MLA_V = 128
MLA_QK = MLA_NOPE + MLA_ROPE
ROPE_THETA = 10000.0
GLA_HEADS = 4
GLA_DK = 128
GLA_DV = 256
GLA_GATE_RANK = 16
GLA_GATE_NORMALIZER = 16.0
GLA_KEY = GLA_HEADS * GLA_DK
GLA_VAL = GLA_HEADS * GLA_DV
N_MOD = 9

LANES = 128
SUBLANES = 8
VMEM_LIMIT = 56 * 1024 * 1024

COL_QLAT = 0
COL_KVLAT = COL_QLAT + MLA_Q_RANK
COL_GQ = COL_KVLAT + MLA_KV_RANK
COL_GK = COL_GQ + GLA_KEY
COL_GV = COL_GK + GLA_KEY
COL_GOUT = COL_GV + GLA_VAL
COL_GATE_A = COL_GOUT + GLA_VAL
TAIL_W = LANES
TAIL_KROPE = 0
TAIL_GLR = MLA_ROPE


def _params(sem):
    return pltpu.CompilerParams(dimension_semantics=sem, vmem_limit_bytes=VMEM_LIMIT)


def _dot(a, b):
    return jnp.dot(a, b, preferred_element_type=F32)


def _dot_t0(a, b):
    return lax.dot_general(a, b, (((0,), (0,)), ((), ())), preferred_element_type=F32)


def _dot_t1(a, b):
    return lax.dot_general(a, b, (((1,), (1,)), ((), ())), preferred_element_type=F32)


def _split_dot(x, w_bf16):
    hi = x.astype(BF16)
    lo = (x - hi.astype(F32)).astype(BF16)
    return _dot(hi, w_bf16) + _dot(lo, w_bf16)


def _split_dot_t0(x, w_bf16):
    hi = x.astype(BF16)
    lo = (x - hi.astype(F32)).astype(BF16)
    return _dot_t0(hi, w_bf16) + _dot_t0(lo, w_bf16)


def _silu(x):
    return x * jax.nn.sigmoid(x)


def _rms(x, g):
    ms = jnp.mean(x * x, axis=-1, keepdims=True)
    return x * lax.rsqrt(ms + EPS) * g


ADA_TN = 2048
ADA_RC = 64


def _adaln_kernel(ct_ref, w_ref, b_ref, o_ref, s_ref):
    d, nb = ct_ref.shape
    tn = w_ref.shape[1]

    @pl.when(pl.program_id(0) == 0)
    def _():
        s = _silu(ct_ref[...])
        for b in range(nb):
            s_ref[b] = jnp.broadcast_to(s[:, b:b + 1], (d, LANES))

    def body(k, acc):
        rows = pl.ds(pl.multiple_of(k * ADA_RC, ADA_RC), ADA_RC)
        out = []
        for b in range(nb):
            sb = s_ref[b, rows, :]
            cols = []
            for col in range(tn // LANES):
                p = w_ref[rows, col * LANES:(col + 1) * LANES] * sb
                cols.append(p.reshape(ADA_RC // SUBLANES, SUBLANES, LANES).sum(axis=0))
            out.append(acc[b] + jnp.concatenate(cols, axis=-1))
        return tuple(out)

    init = tuple(jnp.zeros((SUBLANES, tn), F32) for _ in range(nb))
    acc = lax.fori_loop(0, d // ADA_RC, body, init, unroll=2)
    rows = [jnp.sum(a, axis=0, keepdims=True) for a in acc]
    o_ref[...] = jnp.concatenate(rows, axis=0) + b_ref[...]


def _adaln(c, w_ada, b_ada):
    nb, d = c.shape
    n = w_ada.shape[1]
    return pl.pallas_call(
        _adaln_kernel,
        out_shape=jax.ShapeDtypeStruct((nb, n), F32),
        grid=(n // ADA_TN,),
        in_specs=[
            pl.BlockSpec((d, nb), lambda j: (0, 0)),
            pl.BlockSpec((d, ADA_TN), lambda j: (0, j)),
            pl.BlockSpec((1, ADA_TN), lambda j: (0, j)),
        ],
        out_specs=pl.BlockSpec((nb, ADA_TN), lambda j: (0, j)),
        scratch_shapes=[pltpu.VMEM((nb, d, LANES), F32)],
        compiler_params=_params(("arbitrary",)),
        name="adaln",
    )(c.T, w_ada, b_ada.reshape(1, n))


def _mod_norm(x, g, mod_ref, base):
    shift = mod_ref[0, base:base + 1, :]
    scale = mod_ref[0, base + 1:base + 2, :]
    return _rms(x, g) * (1.0 + scale) + shift


FFN_TM = 512
FFN_TF = 512
FFN_HEAD_TILES = 2
FFN_HEAD_TF = 256
FFN_GROUP = 256
FFN_AHEAD_CHUNKS = 8
FFN_AHEAD_ROWS = FFN_TM // FFN_AHEAD_CHUNKS


def _swiglu_accumulate(h, w1g, w3g, w2t, o_ref):
    upd = None
    for s in range(w1g.shape[0]):
        a = _dot(h, w1g[s])
        b = _dot(h, w3g[s])
        p = (_silu(a) * b).astype(BF16)
        part = _dot(p, w2t[s * FFN_GROUP:(s + 1) * FFN_GROUP, :])
        upd = part if upd is None else upd + part
    o_ref[...] += upd


def _ffn_finish(x_ref, mod_ref, gf_ref, o_ref, *, base, final_norm):
    gate = mod_ref[0, base + 2:base + 3, :]
    y = x_ref[...] + 0.5 * gate * o_ref[...]
    if final_norm:
        y = _rms(y, gf_ref[...])
    o_ref[...] = y


def _ffn_head_kernel(x_ref, mod_ref, g_ref, w1_ref, w3_ref, w2_ref, gf_ref,
                     o_ref, w1b_ref, w3b_ref, w2b_ref, h_ref, *, base, final_norm):
    j = pl.program_id(1)

    @pl.when(j == 0)
    def _():
        h_ref[...] = _mod_norm(x_ref[...], g_ref[...], mod_ref, base).astype(BF16)
        o_ref[...] = jnp.zeros_like(o_ref)

    w1 = w1_ref[0].astype(BF16)
    w3 = w3_ref[0].astype(BF16)
    w2 = w2_ref[0].astype(BF16)
    w1b_ref[0] = w1
    w3b_ref[0] = w3
    w2b_ref[...] = w2
    _swiglu_accumulate(h_ref[...], w1[None], w3[None], w2, o_ref)

    @pl.when(j == pl.num_programs(1) - 1)
    def _():
        _ffn_finish(x_ref, mod_ref, gf_ref, o_ref, base=base, final_norm=final_norm)


def _ffn_kernel(x_ref, xn_ref, mod_ref, modn_ref, g_ref, w1_ref, w3_ref, w2_ref, gf_ref, y0_ref,
                o_ref, h_ref, *, base, final_norm):
    i = pl.program_id(0)
    j = pl.program_id(1)
    slot = i % 2

    @pl.when(i >= FFN_HEAD_TILES)
    def _():
        @pl.when(j == 0)
        def _():
            o_ref[...] = jnp.zeros_like(o_ref)

            @pl.when(i == FFN_HEAD_TILES)
            def _():
                h_ref[slot] = _mod_norm(x_ref[...], g_ref[...], mod_ref, base).astype(BF16)

        chunk = jnp.minimum(j, FFN_AHEAD_CHUNKS - 1)
        rows = pl.ds(pl.multiple_of(chunk * FFN_AHEAD_ROWS, FFN_AHEAD_ROWS), FFN_AHEAD_ROWS)
        h_next = _mod_norm(xn_ref[rows, :], g_ref[...], modn_ref, base).astype(BF16)

        _swiglu_accumulate(h_ref[slot], w1_ref, w3_ref, w2_ref[...], o_ref)
        h_ref[1 - slot, rows, :] = h_next

        @pl.when(j == pl.num_programs(1) - 1)
        def _():
            _ffn_finish(x_ref, mod_ref, gf_ref, o_ref, base=base, final_norm=final_norm)

    @pl.when((i < FFN_HEAD_TILES) & (j == pl.num_programs(1) - 1))
    def _():
        o_ref[...] = y0_ref[...]


def _ffn(x2d, mod3, g, w1, w3, w2, g_final, *, base, final_norm, seq):
    t, d = x2d.shape
    dff = w1.shape[2]
    tiles_per_batch = seq // FFN_TM
    const = lambda i, j: (0, 0)
    kw = dict(base=base, final_norm=final_norm)
    tag = "_final" if final_norm else ""
    common = [
        pl.BlockSpec((1, N_MOD, d), lambda i, j: (i // tiles_per_batch, 0, 0)),
        pl.BlockSpec((1, d), const),
    ]
    hf = FFN_HEAD_TF
    head_rows = FFN_HEAD_TILES * FFN_TM
    assert head_rows <= seq
    assert hf == FFN_GROUP
    w13_shape = jax.ShapeDtypeStruct((dff // FFN_GROUP, d, FFN_GROUP), BF16)
    w13_spec = pl.BlockSpec((1, d, FFN_GROUP), lambda i, j: (j, 0, 0))
    y0, w1b, w3b, w2b = pl.pallas_call(
        functools.partial(_ffn_head_kernel, **kw),
        out_shape=(jax.ShapeDtypeStruct((head_rows, d), F32), w13_shape, w13_shape,
                   jax.ShapeDtypeStruct((dff, d), BF16)),
        grid=(1, dff // hf),
        in_specs=[pl.BlockSpec((head_rows, d), const, pipeline_mode=pl.Buffered(1))] + common + [
            pl.BlockSpec((1, d, hf), lambda i, j: (0, 0, j)),
            pl.BlockSpec((1, d, hf), lambda i, j: (0, 0, j)),
            pl.BlockSpec((1, hf, d), lambda i, j: (0, j, 0)),
            pl.BlockSpec((1, d), const),
        ],
        out_specs=(pl.BlockSpec((head_rows, d), const), w13_spec, w13_spec,
                   pl.BlockSpec((hf, d), lambda i, j: (j, 0))),
        scratch_shapes=[pltpu.VMEM((head_rows, d), BF16)],
        compiler_params=_params(("arbitrary", "arbitrary")),
        name="ffn_head" + tag,
    )(x2d, mod3, g, w1, w3, w2, g_final)

    wj = lambda i, j: jnp.where(i < FFN_HEAD_TILES, 0, j)
    wtile = lambda i, j: (wj(i, j), 0, 0)
    wrow = lambda i, j: (wj(i, j), 0)
    n_tiles = t // FFN_TM
    assert dff // FFN_TF >= FFN_AHEAD_CHUNKS
    nxt = lambda i: jnp.minimum(i + 1, n_tiles - 1)
    return pl.pallas_call(
        functools.partial(_ffn_kernel, **kw),
        out_shape=jax.ShapeDtypeStruct((t, d), F32),
        grid=(n_tiles, dff // FFN_TF),
        in_specs=[
            pl.BlockSpec((FFN_TM, d), lambda i, j: (i, 0)),
            pl.BlockSpec((FFN_TM, d), lambda i, j: (nxt(i), 0)),
            pl.BlockSpec((1, N_MOD, d), lambda i, j: (i // tiles_per_batch, 0, 0)),
            pl.BlockSpec((1, N_MOD, d), lambda i, j: (nxt(i) // tiles_per_batch, 0, 0)),
            pl.BlockSpec((1, d), const),
            pl.BlockSpec((FFN_TF // FFN_GROUP, d, FFN_GROUP), wtile),
            pl.BlockSpec((FFN_TF // FFN_GROUP, d, FFN_GROUP), wtile),
            pl.BlockSpec((FFN_TF, d), wrow),
            pl.BlockSpec((1, d), const),
            pl.BlockSpec((FFN_TM, d), lambda i, j: (jnp.minimum(i, FFN_HEAD_TILES - 1), 0)),
        ],
        out_specs=pl.BlockSpec((FFN_TM, d), lambda i, j: (i, 0)),
        scratch_shapes=[pltpu.VMEM((2, FFN_TM, d), BF16)],
        compiler_params=_params(("arbitrary", "arbitrary")),
        name="ffn" + tag,
    )(x2d, x2d, mod3, mod3, g, w1b, w3b, w2b, g_final, y0)


IN_SPLITS = (MLA_Q_RANK, MLA_KV_RANK, MLA_ROPE, GLA_KEY, GLA_KEY, GLA_VAL, GLA_GATE_RANK, GLA_VAL,
             2048, 2048)
MAIN_ORDER = (0, 1, 3, 4, 5, 7, 8, 9)
TAIL_ORDER = (2, 6)
RELAYOUT_ROWS = 1024


def _segments(order):
    starts = [sum(IN_SPLITS[:i]) for i in range(len(IN_SPLITS))]
    segs, dst = [], 0
    for i in order:
        src, n = starts[i], IN_SPLITS[i]
        if segs and segs[-1][0] + segs[-1][2] == src:
            segs[-1] = (segs[-1][0], segs[-1][1], segs[-1][2] + n)
        else:
            segs.append((src, dst, n))
        dst += n
    return segs, dst


def _relayout_kernel(lo_ref, hi_ref, kr_ref, glr_ref, main_ref, tail_ref):
    r = pl.program_id(0)
    rows = RELAYOUT_ROWS
    for src, dst, n in _segments(MAIN_ORDER)[0]:
        shift = src - dst

        @pl.when((r >= dst // rows) & (r < (dst + n) // rows))
        def _(shift=shift):
            main_ref[:rows - shift, :] = lo_ref[shift:, :].astype(BF16)
            if shift:
                main_ref[rows - shift:, :] = hi_ref[:shift, :].astype(BF16)

    tail_ref[:MLA_ROPE, :] = kr_ref[...].astype(BF16)
    tail_ref[MLA_ROPE:MLA_ROPE + GLA_GATE_RANK, :] = glr_ref[...].astype(BF16)
    pad = TAIL_W - MLA_ROPE - GLA_GATE_RANK
    tail_ref[MLA_ROPE + GLA_GATE_RANK:, :] = jnp.zeros((pad, tail_ref.shape[1]), BF16)


def _relayout_w_in(w_in_t):
    d_in, d = w_in_t.shape
    segs, n_main = _segments(MAIN_ORDER)
    rows = RELAYOUT_ROWS
    assert all(dst % rows == 0 and n % rows == 0 and 0 <= src - dst <= LANES and (src - dst) % 16 == 0
               for src, dst, n in segs)
    (kr_src, _, _), (glr_src, _, _) = _segments(TAIL_ORDER)[0]
    assert kr_src % MLA_ROPE == 0 and glr_src % GLA_GATE_RANK == 0
    return pl.pallas_call(
        _relayout_kernel,
        out_shape=(jax.ShapeDtypeStruct((n_main, d), BF16), jax.ShapeDtypeStruct((TAIL_W, d), BF16)),
        grid=(n_main // rows,),
        in_specs=[pl.BlockSpec((rows, d), lambda r: (r, 0)),
                  pl.BlockSpec((LANES, d), lambda r: ((r + 1) * (rows // LANES), 0)),
                  pl.BlockSpec((MLA_ROPE, d), lambda r: (kr_src // MLA_ROPE, 0)),
                  pl.BlockSpec((GLA_GATE_RANK, d), lambda r: (glr_src // GLA_GATE_RANK, 0))],
        out_specs=(pl.BlockSpec((rows, d), lambda r: (r, 0)),
                   pl.BlockSpec((TAIL_W, d), lambda r: (0, 0))),
        compiler_params=_params(("arbitrary",)),
        name="w_in_layout",
    )(w_in_t, w_in_t, w_in_t, w_in_t)


INP_TM = 1024
INP_TN = 2048


def _inproj_kernel(x_ref, mod_ref, g_ref, w_ref, wt_ref, o_ref, tail_ref, h_ref, *, base):
    @pl.when(pl.program_id(1) == 0)
    def _():
        h = _mod_norm(x_ref[...], g_ref[...], mod_ref, base).astype(BF16)
        h_ref[...] = h
        tail_ref[...] = _dot_t1(h, wt_ref[...])

    o_ref[...] = _dot_t1(h_ref[...], w_ref[...]).astype(BF16)


def _inproj(x2d, mod3, g, w_main, w_tail, *, base, seq):
    t, d = x2d.shape
    n = w_main.shape[0]
    tiles_per_batch = seq // INP_TM
    return pl.pallas_call(
        functools.partial(_inproj_kernel, base=base),
        out_shape=(jax.ShapeDtypeStruct((t, n), BF16),
                   jax.ShapeDtypeStruct((t, TAIL_W), F32)),
        grid=(t // INP_TM, n // INP_TN),
        in_specs=[
            pl.BlockSpec((INP_TM, d), lambda i, j: (i, 0)),
            pl.BlockSpec((1, N_MOD, d), lambda i, j: (i // tiles_per_batch, 0, 0)),
            pl.BlockSpec((1, d), lambda i, j: (0, 0)),
            pl.BlockSpec((INP_TN, d), lambda i, j: (j, 0)),
            pl.BlockSpec((TAIL_W, d), lambda i, j: (0, 0)),
        ],
        out_specs=(pl.BlockSpec((INP_TM, INP_TN), lambda i, j: (i, j)),
                   pl.BlockSpec((INP_TM, TAIL_W), lambda i, j: (i, 0))),
        scratch_shapes=[pltpu.VMEM((INP_TM, d), BF16)],
        compiler_params=_params(("parallel", "arbitrary")),
        name="in_proj",
    )(x2d, mod3, g, w_main, w_tail)


PREP_TM = 512
ROPE_HALF = MLA_ROPE // 2


def _rope_rotate(x, cos, sin_signed):
    width = x.shape[-1]
    lane = lax.broadcasted_iota(jnp.int32, x.shape, 1)
    first_half = (lane % MLA_ROPE) < ROPE_HALF
    partner = jnp.where(first_half,
                        pltpu.roll(x, width - ROPE_HALF, 1),
                        pltpu.roll(x, ROPE_HALF, 1))
    return x * cos + partner * sin_signed


def _prep_kernel(qlat_ref, kvlat_ref, tail_ref, pos_ref, invf_ref, sgn_ref,
                 gql_ref, wuq_ref, gqn_ref, gqr_ref, gkvl_ref, wukv_ref, gkn_ref, gkr_ref,
                 grp_ref, q_ref, k_ref, v_ref):
    ang_t = invf_ref[...] * pos_ref[0].astype(F32)
    lane_reps = LANES // ROPE_HALF
    cos = jnp.tile(jnp.cos(ang_t).T, (1, lane_reps))
    sin_signed = jnp.tile(jnp.sin(ang_t).T, (1, lane_reps)) * sgn_ref[...]
    n_rope = MLA_HEADS * MLA_ROPE
    cos_q = jnp.concatenate([cos] * (n_rope // LANES), axis=-1)
    sin_q = jnp.concatenate([sin_signed] * (n_rope // LANES), axis=-1)

    cq = _rms(qlat_ref[...].astype(F32), gql_ref[...]).astype(BF16)
    ckv = _rms(kvlat_ref[...].astype(F32), gkvl_ref[...]).astype(BF16)
    n_nope = MLA_HEADS * MLA_NOPE
    pair = 2 * MLA_NOPE

    for hp in range(MLA_HEADS // 2):
        cols = slice(hp * pair, (hp + 1) * pair)
        qp = _dot(cq, wuq_ref[:, cols])
        kp = _dot(ckv, wukv_ref[:, cols])
        for hh in range(2):
            h = 2 * hp + hh
            sub = slice(hh * MLA_NOPE, (hh + 1) * MLA_NOPE)
            q_ref[0, h, :, :MLA_NOPE] = _rms(qp[:, sub], gqn_ref[...]).astype(BF16)
            k_ref[0, h, :, :MLA_NOPE] = _rms(kp[:, sub], gkn_ref[...]).astype(BF16)
        v_ref[0, :, cols] = _dot(ckv, wukv_ref[:, n_nope + hp * pair:n_nope + (hp + 1) * pair]).astype(BF16)

    qr = _dot(cq, wuq_ref[:, n_nope:])
    ssq = _split_dot(qr * qr, grp_ref[...])
    qr = qr * lax.rsqrt(ssq * (1.0 / MLA_ROPE) + EPS) * gqr_ref[...]
    qr = _rope_rotate(qr, cos_q, sin_q).astype(BF16)
    kr = tail_ref[:, TAIL_KROPE:TAIL_KROPE + MLA_ROPE]
    kr = _rms(kr, gkr_ref[...])
    kr = _rope_rotate(kr, cos[:, :MLA_ROPE], sin_signed[:, :MLA_ROPE]).astype(BF16)
    for h in range(MLA_HEADS):
        q_ref[0, h, :, MLA_NOPE:] = qr[:, h * MLA_ROPE:(h + 1) * MLA_ROPE]
        k_ref[0, h, :, MLA_NOPE:] = kr


def _mla_prep(proj, tail, pos3d, consts, weights, *, batch, seq):
    t = proj.shape[0]
    tiles_per_batch = seq // PREP_TM
    row = lambda i: (i, 0)
    const = lambda i: (0, 0)
    (invf, sgn, grp) = consts
    (gql, wuq, gqn, gqr, gkvl, wukv, gkn, gkr) = weights
    head_out = pl.BlockSpec((1, MLA_HEADS, PREP_TM, MLA_QK),
                            lambda i: (i // tiles_per_batch, 0, i % tiles_per_batch, 0))
    full = lambda a: pl.BlockSpec(a.shape, const)
    return pl.pallas_call(
        _prep_kernel,
        out_shape=(jax.ShapeDtypeStruct((batch, MLA_HEADS, seq, MLA_QK), BF16),
                   jax.ShapeDtypeStruct((batch, MLA_HEADS, seq, MLA_QK), BF16),
                   jax.ShapeDtypeStruct((batch, seq, MLA_HEADS * MLA_V), BF16)),
        grid=(t // PREP_TM,),
        in_specs=[
            pl.BlockSpec((PREP_TM, MLA_Q_RANK), lambda i: (i, COL_QLAT // MLA_Q_RANK)),
            pl.BlockSpec((PREP_TM, MLA_KV_RANK), lambda i: (i, COL_KVLAT // MLA_KV_RANK)),
            pl.BlockSpec((PREP_TM, TAIL_W), row),
            pl.BlockSpec((1, 1, PREP_TM), lambda i: (i, 0, 0)),
            full(invf), full(sgn),
            full(gql), full(wuq), full(gqn), full(gqr),
            full(gkvl), full(wukv), full(gkn), full(gkr),
            full(grp),
        ],
        out_specs=(head_out, head_out,
                   pl.BlockSpec((1, PREP_TM, MLA_HEADS * MLA_V),
                                lambda i: (i // tiles_per_batch, i % tiles_per_batch, 0))),
        compiler_params=_params(("parallel",)),
        name="mla_prep",
    )(proj, proj, tail, pos3d, invf, sgn, gql, wuq, gqn, gqr, gkvl, wukv, gkn, gkr, grp)


ATT_T = 1024
ATT_H = ATT_T // 2


def _attn_kernel(q_ref, k_ref, v_ref, o_ref):
    n_tiles = k_ref.shape[2] // ATT_T
    ones = jnp.ones((ATT_T, MLA_V), BF16)
    r = lax.broadcasted_iota(jnp.int32, (ATT_H, ATT_H), 0)
    c = lax.broadcasted_iota(jnp.int32, (ATT_H, ATT_H), 1)
    visible = (c // CHUNK) <= (r // CHUNK)
    state = {}

    def update(chain, s, v2):
        m_cur = jnp.max(s, axis=-1, keepdims=True)
        if chain not in state:
            m_next = jnp.broadcast_to(m_cur, (ATT_H, LANES))
        else:
            m_prev, l_prev, acc_prev = state[chain]
            m_next = jnp.maximum(m_prev, m_cur)
            alpha = jnp.exp2(m_prev - m_next)
        p = jnp.exp2(s - jnp.tile(m_next, (1, s.shape[1] // LANES)))
        pv = _dot(p.astype(BF16), v2)
        acc, l = pv[:, :MLA_V], pv[:, MLA_V:]
        if chain in state:
            acc, l = alpha * acc_prev + acc, alpha * l_prev + l
        state[chain] = (m_next, l, acc)

    for t in range(n_tiles):
        rows = slice(t * ATT_T, (t + 1) * ATT_T)
        k = k_ref[0, 0, rows, :]
        v2 = jnp.concatenate([v_ref[0, rows, :], ones], axis=-1)
        for qt in range(t, n_tiles):
            q0 = q_ref[0, 0, qt * ATT_T:qt * ATT_T + ATT_H, :]
            q1 = q_ref[0, 0, qt * ATT_T + ATT_H:(qt + 1) * ATT_T, :]
            if t < qt:
                update((qt, 0), _dot_t1(q0, k), v2)
                update((qt, 1), _dot_t1(q1, k), v2)
            else:
                update((qt, 0), jnp.where(visible, _dot_t1(q0, k[:ATT_H]), -jnp.inf), v2[:ATT_H])
                s1 = _dot_t1(q1, k)
                s1 = jnp.concatenate([s1[:, :ATT_H], jnp.where(visible, s1[:, ATT_H:], -jnp.inf)],
                                     axis=-1)
                update((qt, 1), s1, v2)
                for half in range(2):
                    _, l, acc = state[(qt, half)]
                    lo = qt * ATT_T + half * ATT_H
                    o_ref[0, lo:lo + ATT_H, :] = (acc / l).astype(BF16)


def _mla_attn(q, k, v):
    batch, heads, seq, _ = q.shape
    return pl.pallas_call(
        _attn_kernel,
        out_shape=jax.ShapeDtypeStruct((batch, seq, heads * MLA_V), BF16),
        grid=(batch, heads),
        in_specs=[
            pl.BlockSpec((1, 1, seq, MLA_QK), lambda b, h: (b, h, 0, 0)),
            pl.BlockSpec((1, 1, seq, MLA_QK), lambda b, h: (b, h, 0, 0)),
            pl.BlockSpec((1, seq, MLA_V), lambda b, h: (b, 0, h)),
        ],
        out_specs=pl.BlockSpec((1, seq, MLA_V), lambda b, h: (b, 0, h)),
        compiler_params=_params(("parallel", "parallel")),
        name="mla_attn",
    )(q, k, v)


GLA_L = 512
GLA_NC = GLA_L // CHUNK
GLA_HP = 4


def _gla_head(log_a, q, k, v, go, gg, state_ref, causal, tri):
    la_hi = log_a.astype(BF16)
    la_lo = (log_a - la_hi.astype(F32)).astype(BF16)
    cum = _dot(tri, jnp.concatenate([la_hi, la_lo], axis=-1))
    b_cum = cum[:, :GLA_DK] + cum[:, GLA_DK:]
    b3 = b_cum.reshape(GLA_NC, CHUNK, GLA_DK)
    b_end = b3[:, CHUNK - 1:CHUNK, :]
    b_tot = jnp.broadcast_to(b_end, b3.shape).reshape(GLA_L, GLA_DK)

    qf = q.astype(F32) * (GLA_DK ** -0.5)
    kf = k.astype(F32)
    q_dec = (qf * jnp.exp(b_cum)).astype(BF16)
    k_dec = (kf * jnp.exp(-b_cum)).astype(BF16)
    k_end = (kf * jnp.exp(b_tot - b_cum)).astype(BF16)

    attn = jnp.where(causal, _dot_t1(q_dec, k_dec), 0.0).astype(BF16)
    o_intra = _dot(attn, v)

    decay = jnp.exp(b_end)
    state = state_ref[...]
    outs = []
    for n in range(GLA_NC):
        rows = slice(n * CHUNK, (n + 1) * CHUNK)
        outs.append(o_intra[rows] + _dot_t1(q_dec[rows], state.astype(BF16)))
        state = state * decay[n] + _dot_t0(v[rows], k_end[rows])
    state_ref[...] = state

    o = jnp.concatenate(outs, axis=0)
    return (_rms(o, gg) * _silu(go.astype(F32))).astype(BF16)


def _gla_kernel(q_ref, k_ref, v_ref, go_ref, tail_ref, wg_ref, bg_ref, gg_ref,
                o_ref, state_ref):
    @pl.when(pl.program_id(2) == 0)
    def _():
        state_ref[...] = jnp.zeros_like(state_ref)

    wg = wg_ref[...]
    wg_hi = wg.astype(BF16)
    wg_lo = (wg - wg_hi.astype(F32)).astype(BF16)
    tl = tail_ref[...]
    tl_hi = tl.astype(BF16)
    tl_lo = (tl - tl_hi.astype(F32)).astype(BF16)
    z = _dot(tl_hi, wg_hi) + _dot(tl_lo, wg_hi) + _dot(tl_hi, wg_lo) + bg_ref[...]
    log_a = (jnp.minimum(z, 0.0) - jnp.log1p(jnp.exp(-jnp.abs(z)))) * (1.0 / GLA_GATE_NORMALIZER)

    r = lax.broadcasted_iota(jnp.int32, (GLA_L, GLA_L), 0)
    c = lax.broadcasted_iota(jnp.int32, (GLA_L, GLA_L), 1)
    causal = ((r // CHUNK) == (c // CHUNK)) & (c <= r)
    tri = causal.astype(BF16)

    for hh in range(GLA_HP):
        dk = slice(hh * GLA_DK, (hh + 1) * GLA_DK)
        dv = slice(hh * GLA_DV, (hh + 1) * GLA_DV)
        o_ref[:, dv] = _gla_head(log_a[:, dk], q_ref[:, dk], k_ref[:, dk], v_ref[:, dv],
                                 go_ref[:, dv], gg_ref[...], state_ref.at[hh], causal, tri)


def _gla(proj, tail, wg_pad, b_gk, g_gla, *, batch, seq):
    t = proj.shape[0]
    steps = seq // GLA_L
    tok = lambda b, h, i: b * steps + i
    wk, wv = GLA_HP * GLA_DK, GLA_HP * GLA_DV
    return pl.pallas_call(
        _gla_kernel,
        out_shape=jax.ShapeDtypeStruct((t, GLA_VAL), BF16),
        grid=(batch, GLA_HEADS // GLA_HP, steps),
        in_specs=[
            pl.BlockSpec((GLA_L, wk), lambda b, h, i: (tok(b, h, i), COL_GQ // wk + h)),
            pl.BlockSpec((GLA_L, wk), lambda b, h, i: (tok(b, h, i), COL_GK // wk + h)),
            pl.BlockSpec((GLA_L, wv), lambda b, h, i: (tok(b, h, i), COL_GV // wv + h)),
            pl.BlockSpec((GLA_L, wv), lambda b, h, i: (tok(b, h, i), COL_GOUT // wv + h)),
            pl.BlockSpec((GLA_L, TAIL_W), lambda b, h, i: (tok(b, h, i), 0)),
            pl.BlockSpec((TAIL_W, wk), lambda b, h, i: (0, h)),
            pl.BlockSpec((1, wk), lambda b, h, i: (0, h)),
            pl.BlockSpec((1, GLA_DV), lambda b, h, i: (0, 0)),
        ],
        out_specs=pl.BlockSpec((GLA_L, wv), lambda b, h, i: (tok(b, h, i), h)),
        scratch_shapes=[pltpu.VMEM((GLA_HP, GLA_DV, GLA_DK), F32)],
        compiler_params=_params(("parallel", "parallel", "arbitrary")),
        name="gla",
    )(proj, proj, proj, proj, tail, wg_pad, b_gk, g_gla)


MRG_TM = 512


def _merge_kernel(x_ref, mod_ref, a_ref, b_ref, ga_ref, gb_ref, wa_ref, wb_ref, wo_ref, o_ref,
                  *, base):
    ya = _dot(a_ref[...], wa_ref[...])
    yb = _dot(b_ref[...], wb_ref[...])
    merged = (jax.nn.sigmoid(ga_ref[...].astype(F32)) * ya
              + jax.nn.sigmoid(gb_ref[...].astype(F32)) * yb).astype(BF16)
    gate = mod_ref[0, base + 2:base + 3, :]
    o_ref[...] = x_ref[...] + gate * _dot(merged, wo_ref[...])


def _merge(x2d, mod3, attn_o, gla_o, proj, wa, wb, wo, *, base, seq):
    t, d = x2d.shape
    tiles_per_batch = seq // MRG_TM
    row = lambda i: (i, 0)
    resident = lambda a: pl.BlockSpec(a.shape, lambda i: (0, 0), pipeline_mode=pl.Buffered(1))
    return pl.pallas_call(
        functools.partial(_merge_kernel, base=base),
        out_shape=jax.ShapeDtypeStruct((t, d), F32),
        grid=(t // MRG_TM,),
        in_specs=[
            pl.BlockSpec((MRG_TM, d), row),
            pl.BlockSpec((1, N_MOD, d), lambda i: (i // tiles_per_batch, 0, 0)),
            pl.BlockSpec((MRG_TM, attn_o.shape[1]), row),
            pl.BlockSpec((MRG_TM, gla_o.shape[1]), row),
            pl.BlockSpec((MRG_TM, d), lambda i: (i, COL_GATE_A // d)),
            pl.BlockSpec((MRG_TM, d), lambda i: (i, COL_GATE_A // d + 1)),
            resident(wa), resident(wb), resident(wo),
        ],
        out_specs=pl.BlockSpec((MRG_TM, d), row),
        compiler_params=_params(("parallel",)),
        name="merge",
    )(x2d, mod3, attn_o, gla_o, proj, proj, wa, wb, wo)


def _layer(x2d, mod3, pos3d, p, *, batch, seq):
    d = x2d.shape[1]
    assert IN_SPLITS[-1] == d and sum(IN_SPLITS) == p["w_in"].shape[1]
    w_main, w_tail = _relayout_w_in(p["w_in"].T)
    wuq = p["w_uq"].reshape(MLA_Q_RANK, MLA_HEADS, MLA_QK)
    wuq = jnp.concatenate([wuq[:, :, :MLA_NOPE].reshape(MLA_Q_RANK, -1),
                           wuq[:, :, MLA_NOPE:].reshape(MLA_Q_RANK, -1)], axis=1).astype(BF16)
    wukv = p["w_ukv"].reshape(MLA_KV_RANK, MLA_HEADS, MLA_NOPE + MLA_V)
    wukv = jnp.concatenate([wukv[:, :, :MLA_NOPE].reshape(MLA_KV_RANK, -1),
                            wukv[:, :, MLA_NOPE:].reshape(MLA_KV_RANK, -1)], axis=1).astype(BF16)
    wg_pad = jnp.zeros((TAIL_W, GLA_KEY), F32).at[TAIL_GLR:TAIL_GLR + GLA_GATE_RANK].set(p["w_gk_up"])

    inv_freq = ROPE_THETA ** (-jnp.arange(0, MLA_ROPE, 2, dtype=F32) / MLA_ROPE)
    invf = inv_freq.reshape(ROPE_HALF, 1)
    sgn = jnp.tile(jnp.concatenate([-jnp.ones(ROPE_HALF, F32), jnp.ones(ROPE_HALF, F32)]),
                   LANES // MLA_ROPE).reshape(1, LANES)
    n_rope = MLA_HEADS * MLA_ROPE
    lane = jnp.arange(n_rope)
    grp = (lane[:, None] // MLA_ROPE == lane[None, :] // MLA_ROPE).astype(BF16)

    row = lambda a: a.reshape(1, -1)
    q_scale = MLA_QK ** -0.5 * LOG2E
    x2d = _ffn(x2d, mod3, row(p["g_ffn1"]), p["w1_a"][None], p["w3_a"][None],
               p["w2_a"][None], row(p["g_final"]), base=0, final_norm=False, seq=seq)
    proj, tail = _inproj(x2d, mod3, row(p["g_mix"]), w_main, w_tail, base=3, seq=seq)
    q, k, v = _mla_prep(
        proj, tail, pos3d, (invf, sgn, grp),
        (row(p["g_q_lat"]), wuq, row(p["g_qn"] * q_scale),
         row(jnp.tile(p["g_qr"], MLA_HEADS) * q_scale),
         row(p["g_kv_lat"]), wukv, row(p["g_kn"]), row(p["g_kr"])),
        batch=batch, seq=seq)
    attn_o = _mla_attn(q, k, v).reshape(batch * seq, MLA_HEADS * MLA_V)
    gla_o = _gla(proj, tail, wg_pad, row(p["b_gk"]), row(p["g_gla"]), batch=batch, seq=seq)
    x2d = _merge(x2d, mod3, attn_o, gla_o, proj, p["w_proj_a"].astype(BF16),
                 p["w_proj_b"].astype(BF16), p["w_out"].astype(BF16), base=3, seq=seq)
    x2d = _ffn(x2d, mod3, row(p["g_ffn2"]), p["w1_b"][None], p["w3_b"][None],
               p["w2_b"][None], row(p["g_final"]), base=6, final_norm=True, seq=seq)
    return x2d


def kernel(x, c, positions, w_ada, b_ada, g_ffn1, w1_a, w3_a, w2_a, g_mix, w_in, g_q_lat, w_uq, g_qn, g_qr, g_kv_lat, w_ukv, g_kn, g_kr, w_gk_up, b_gk, g_gla, w_proj_a, w_proj_b, w_out, g_ffn2, w1_b, w3_b, w2_b, g_final):
    batch, seq, d = x.shape
    depth = w_ada.shape[0]
    assert depth == 1, "the final norm is fused into the last FFN of a single layer"
    names = ("g_ffn1", "w1_a", "w3_a", "w2_a", "g_mix", "w_in", "g_q_lat", "w_uq", "g_qn", "g_qr",
             "g_kv_lat", "w_ukv", "g_kn", "g_kr", "w_gk_up", "b_gk", "g_gla", "w_proj_a", "w_proj_b",
             "w_out", "g_ffn2", "w1_b", "w3_b", "w2_b", "g_final")
    stacked = (g_ffn1, w1_a, w3_a, w2_a, g_mix, w_in, g_q_lat, w_uq, g_qn, g_qr, g_kv_lat, w_ukv,
               g_kn, g_kr, w_gk_up, b_gk, g_gla, w_proj_a, w_proj_b, w_out, g_ffn2, w1_b, w3_b, w2_b,
               g_final)
    x2d = x.reshape(batch * seq, d)
    pos3d = positions.reshape(batch * seq // PREP_TM, 1, PREP_TM)
    p = {n: a[0] for n, a in zip(names, stacked)}
    mod3 = _adaln(c, w_ada[0], b_ada[0]).reshape(batch, N_MOD, d)
    x2d = _layer(x2d, mod3, pos3d, p, batch=batch, seq=seq)
    return x2d.reshape(batch, seq, d)
```

```python
import functools

import jax
import jax.numpy as jnp
from jax import lax
from jax.experimental import pallas as pl
from jax.experimental.pallas import tpu as pltpu

F32 = jnp.float32
BF16 = jnp.bfloat16

EPS = 1e-6
LOG2E = 1.4426950408889634
CHUNK = 64
MLA_HEADS = 8
MLA_Q_RANK = 512
MLA_KV_RANK = 512
MLA_NOPE = 128
MLA_ROPE = 64
MLA_V = 128
MLA_QK = MLA_NOPE + MLA_ROPE
ROPE_THETA = 10000.0
GLA_HEADS = 4
GLA_DK = 128
GLA_DV = 256
GLA_GATE_RANK = 16
GLA_GATE_NORMALIZER = 16.0
GLA_KEY = GLA_HEADS * GLA_DK
GLA_VAL = GLA_HEADS * GLA_DV
N_MOD = 9

LANES = 128
SUBLANES = 8
VMEM_LIMIT = 56 * 1024 * 1024

COL_QLAT = 0
COL_KVLAT = COL_QLAT + MLA_Q_RANK
COL_GQ = COL_KVLAT + MLA_KV_RANK
COL_GK = COL_GQ + GLA_KEY
COL_GV = COL_GK + GLA_KEY
COL_GOUT = COL_GV + GLA_VAL
COL_GATE_A = COL_GOUT + GLA_VAL
TAIL_W = LANES
TAIL_KROPE = 0
TAIL_GLR = MLA_ROPE


def _params(sem):
    return pltpu.CompilerParams(dimension_semantics=sem, vmem_limit_bytes=VMEM_LIMIT)


def _dot(a, b):
    return jnp.dot(a, b, preferred_element_type=F32)


def _dot_t0(a, b):
    return lax.dot_general(a, b, (((0,), (0,)), ((), ())), preferred_element_type=F32)


def _dot_t1(a, b):
    return lax.dot_general(a, b, (((1,), (1,)), ((), ())), preferred_element_type=F32)


def _split_dot(x, w_bf16):
    hi = x.astype(BF16)
    lo = (x - hi.astype(F32)).astype(BF16)
    return _dot(hi, w_bf16) + _dot(lo, w_bf16)


def _split_dot_t0(x, w_bf16):
    hi = x.astype(BF16)
    lo = (x - hi.astype(F32)).astype(BF16)
    return _dot_t0(hi, w_bf16) + _dot_t0(lo, w_bf16)


def _silu(x):
    return x * jax.nn.sigmoid(x)


def _rms(x, g):
    ms = jnp.mean(x * x, axis=-1, keepdims=True)
    return x * lax.rsqrt(ms + EPS) * g


ADA_TN = 2048
ADA_RC = 64


def _adaln_kernel(ct_ref, w_ref, b_ref, o_ref, s_ref):
    d, nb = ct_ref.shape
    tn = w_ref.shape[1]

    @pl.when(pl.program_id(0) == 0)
    def _():
        s = _silu(ct_ref[...])
        for b in range(nb):
            s_ref[b] = jnp.broadcast_to(s[:, b:b + 1], (d, LANES))

    def body(k, acc):
        rows = pl.ds(pl.multiple_of(k * ADA_RC, ADA_RC), ADA_RC)
        out = []
        for b in range(nb):
            sb = s_ref[b, rows, :]
            cols = []
            for col in range(tn // LANES):
                p = w_ref[rows, col * LANES:(col + 1) * LANES] * sb
                cols.append(p.reshape(ADA_RC // SUBLANES, SUBLANES, LANES).sum(axis=0))
            out.append(acc[b] + jnp.concatenate(cols, axis=-1))
        return tuple(out)

    init = tuple(jnp.zeros((SUBLANES, tn), F32) for _ in range(nb))
    acc = lax.fori_loop(0, d // ADA_RC, body, init, unroll=2)
    rows = [jnp.sum(a, axis=0, keepdims=True) for a in acc]
    o_ref[...] = jnp.concatenate(rows, axis=0) + b_ref[...]


def _adaln(c, w_ada, b_ada):
    nb, d = c.shape
    n = w_ada.shape[1]
    return pl.pallas_call(
        _adaln_kernel,
        out_shape=jax.ShapeDtypeStruct((nb, n), F32),
        grid=(n // ADA_TN,),
        in_specs=[
            pl.BlockSpec((d, nb), lambda j: (0, 0)),
            pl.BlockSpec((d, ADA_TN), lambda j: (0, j)),
            pl.BlockSpec((1, ADA_TN), lambda j: (0, j)),
        ],
        out_specs=pl.BlockSpec((nb, ADA_TN), lambda j: (0, j)),
        scratch_shapes=[pltpu.VMEM((nb, d, LANES), F32)],
        compiler_params=_params(("arbitrary",)),
        name="adaln",
    )(c.T, w_ada, b_ada.reshape(1, n))


def _mod_norm(x, g, mod_ref, base):
    shift = mod_ref[0, base:base + 1, :]
    scale = mod_ref[0, base + 1:base + 2, :]
    return _rms(x, g) * (1.0 + scale) + shift


FFN_TM = 512
FFN_TF = 512
FFN_HEAD_TILES = 2
FFN_HEAD_TF = 256
FFN_GROUP = 256
FFN_AHEAD_CHUNKS = 8
FFN_AHEAD_ROWS = FFN_TM // FFN_AHEAD_CHUNKS


def _swiglu_accumulate(h, w1g, w3g, w2t, o_ref, first=False):
    upd = None
    for s in range(w1g.shape[0]):
        a = _dot(h, w1g[s])
        b = _dot(h, w3g[s])
        p = (_silu(a) * b).astype(BF16)
        part = _dot(p, w2t[s * FFN_GROUP:(s + 1) * FFN_GROUP, :])
        upd = part if upd is None else upd + part
    if first:
        o_ref[...] = upd
    else:
        o_ref[...] += upd


def _ffn_finish(x_ref, mod_ref, gf_ref, o_ref, *, base, final_norm):
    gate = mod_ref[0, base + 2:base + 3, :]
    y = x_ref[...] + 0.5 * gate * o_ref[...]
    if final_norm:
        y = _rms(y, gf_ref[...])
    o_ref[...] = y


def _ffn_head_kernel(x_ref, mod_ref, g_ref, w1_ref, w3_ref, w2_ref, gf_ref,
                     o_ref, w1b_ref, w3b_ref, w2b_ref, h_ref, *, base, final_norm):
    j = pl.program_id(1)

    @pl.when(j == 0)
    def _():
        h_ref[...] = _mod_norm(x_ref[...], g_ref[...], mod_ref, base).astype(BF16)
        o_ref[...] = jnp.zeros_like(o_ref)

    w1 = w1_ref[0].astype(BF16)
    w3 = w3_ref[0].astype(BF16)
    w2 = w2_ref[0].astype(BF16)
    w1b_ref[0] = w1
    w3b_ref[0] = w3
    w2b_ref[...] = w2
    _swiglu_accumulate(h_ref[...], w1[None], w3[None], w2, o_ref)

    @pl.when(j == pl.num_programs(1) - 1)
    def _():
        _ffn_finish(x_ref, mod_ref, gf_ref, o_ref, base=base, final_norm=final_norm)


def _ffn_kernel(x_ref, xn_ref, mod_ref, modn_ref, g_ref, w1_hbm, w3_hbm, w2_hbm, gf_ref, y0_ref,
                o_ref, h_ref, w1_buf, w3_buf, w2_buf, sem, *, base, final_norm, n_steps):
    i = pl.program_id(0)
    last_tile = pl.num_programs(0) - 1
    h_slot = i % 2
    parity = ((i - FFN_HEAD_TILES) * n_steps) % 2
    groups = FFN_TF // FFN_GROUP

    def copies(j, slot):
        return (pltpu.make_async_copy(w1_hbm.at[pl.ds(j * groups, groups)], w1_buf.at[slot], sem.at[slot, 0]),
                pltpu.make_async_copy(w3_hbm.at[pl.ds(j * groups, groups)], w3_buf.at[slot], sem.at[slot, 1]),
                pltpu.make_async_copy(w2_hbm.at[pl.ds(j * FFN_TF, FFN_TF)], w2_buf.at[slot], sem.at[slot, 2]))

    @pl.when(i >= FFN_HEAD_TILES)
    def _():
        @pl.when(i == FFN_HEAD_TILES)
        def _():
            for c in copies(0, parity):
                c.start()
            h_ref[h_slot] = _mod_norm(x_ref[...], g_ref[...], mod_ref, base).astype(BF16)

        for j in range(n_steps):
            slot = (parity + j) % 2
            for c in copies((j + 1) % n_steps, 1 - slot):
                c.start()
            for c in copies(j, slot):
                c.wait()
            if j < FFN_AHEAD_CHUNKS:
                rows = slice(j * FFN_AHEAD_ROWS, (j + 1) * FFN_AHEAD_ROWS)
                h_next = _mod_norm(xn_ref[rows, :], g_ref[...], modn_ref, base).astype(BF16)
            _swiglu_accumulate(h_ref[h_slot], w1_buf.at[slot], w3_buf.at[slot], w2_buf[slot], o_ref,
                               first=j == 0)
            if j < FFN_AHEAD_CHUNKS:
                h_ref[1 - h_slot, rows, :] = h_next
        _ffn_finish(x_ref, mod_ref, gf_ref, o_ref, base=base, final_norm=final_norm)

        @pl.when(i == last_tile)
        def _():
            for c in copies(0, 1 - (parity + n_steps - 1) % 2):
                c.wait()

    @pl.when(i < FFN_HEAD_TILES)
    def _():
        o_ref[...] = y0_ref[...]


def _ffn(x2d, mod3, g, w1, w3, w2, g_final, *, base, final_norm, seq):
    t, d = x2d.shape
    dff = w1.shape[2]
    tiles_per_batch = seq // FFN_TM
    const = lambda i, j: (0, 0)
    kw = dict(base=base, final_norm=final_norm)
    tag = "_final" if final_norm else ""
    common = [
        pl.BlockSpec((1, N_MOD, d), lambda i, j: (i // tiles_per_batch, 0, 0)),
        pl.BlockSpec((1, d), const),
    ]
    hf = FFN_HEAD_TF
    head_rows = FFN_HEAD_TILES * FFN_TM
    assert head_rows <= seq
    assert hf == FFN_GROUP
    w13_shape = jax.ShapeDtypeStruct((dff // FFN_GROUP, d, FFN_GROUP), BF16)
    w13_spec = pl.BlockSpec((1, d, FFN_GROUP), lambda i, j: (j, 0, 0))
    y0, w1b, w3b, w2b = pl.pallas_call(
        functools.partial(_ffn_head_kernel, **kw),
        out_shape=(jax.ShapeDtypeStruct((head_rows, d), F32), w13_shape, w13_shape,
                   jax.ShapeDtypeStruct((dff, d), BF16)),
        grid=(1, dff // hf),
        in_specs=[pl.BlockSpec((head_rows, d), const, pipeline_mode=pl.Buffered(1))] + common + [
            pl.BlockSpec((1, d, hf), lambda i, j: (0, 0, j)),
            pl.BlockSpec((1, d, hf), lambda i, j: (0, 0, j)),
            pl.BlockSpec((1, hf, d), lambda i, j: (0, j, 0)),
            pl.BlockSpec((1, d), const),
        ],
        out_specs=(pl.BlockSpec((head_rows, d), const), w13_spec, w13_spec,
                   pl.BlockSpec((hf, d), lambda i, j: (j, 0))),
        scratch_shapes=[pltpu.VMEM((head_rows, d), BF16)],
        compiler_params=_params(("arbitrary", "arbitrary")),
        name="ffn_head" + tag,
    )(x2d, mod3, g, w1, w3, w2, g_final)

    n_tiles = t // FFN_TM
    n_steps = dff // FFN_TF
    assert n_steps >= FFN_AHEAD_CHUNKS and n_tiles > FFN_HEAD_TILES
    nxt = lambda i: jnp.minimum(i + 1, n_tiles - 1)
    groups = FFN_TF // FFN_GROUP
    hbm = pl.BlockSpec(memory_space=pl.ANY)
    return pl.pallas_call(
        functools.partial(_ffn_kernel, n_steps=n_steps, **kw),
        out_shape=jax.ShapeDtypeStruct((t, d), F32),
        grid=(n_tiles,),
        in_specs=[
            pl.BlockSpec((FFN_TM, d), lambda i: (i, 0)),
            pl.BlockSpec((FFN_TM, d), lambda i: (nxt(i), 0)),
            pl.BlockSpec((1, N_MOD, d), lambda i: (i // tiles_per_batch, 0, 0)),
            pl.BlockSpec((1, N_MOD, d), lambda i: (nxt(i) // tiles_per_batch, 0, 0)),
            pl.BlockSpec((1, d), lambda i: (0, 0)),
            hbm, hbm, hbm,
            pl.BlockSpec((1, d), lambda i: (0, 0)),
            pl.BlockSpec((FFN_TM, d), lambda i: (jnp.minimum(i, FFN_HEAD_TILES - 1), 0)),
        ],
        out_specs=pl.BlockSpec((FFN_TM, d), lambda i: (i, 0)),
        scratch_shapes=[pltpu.VMEM((2, FFN_TM, d), BF16),
                        pltpu.VMEM((2, groups, d, FFN_GROUP), BF16),
                        pltpu.VMEM((2, groups, d, FFN_GROUP), BF16),
                        pltpu.VMEM((2, FFN_TF, d), BF16),
                        pltpu.SemaphoreType.DMA((2, 3))],
        compiler_params=_params(("arbitrary",)),
        name="ffn" + tag,
    )(x2d, x2d, mod3, mod3, g, w1b, w3b, w2b, g_final, y0)


IN_SPLITS = (MLA_Q_RANK, MLA_KV_RANK, MLA_ROPE, GLA_KEY, GLA_KEY, GLA_VAL, GLA_GATE_RANK, GLA_VAL,
             2048, 2048)
MAIN_ORDER = (0, 1, 3, 4, 5, 7, 8, 9)
TAIL_ORDER = (2, 6)
RELAYOUT_ROWS = 1024


def _segments(order):
    starts = [sum(IN_SPLITS[:i]) for i in range(len(IN_SPLITS))]
    segs, dst = [], 0
    for i in order:
        src, n = starts[i], IN_SPLITS[i]
        if segs and segs[-1][0] + segs[-1][2] == src:
            segs[-1] = (segs[-1][0], segs[-1][1], segs[-1][2] + n)
        else:
            segs.append((src, dst, n))
        dst += n
    return segs, dst


def _relayout_kernel(lo_ref, hi_ref, kr_ref, glr_ref, main_ref, tail_ref):
    r = pl.program_id(0)
    rows = RELAYOUT_ROWS
    for src, dst, n in _segments(MAIN_ORDER)[0]:
        shift = src - dst

        @pl.when((r >= dst // rows) & (r < (dst + n) // rows))
        def _(shift=shift):
            main_ref[:rows - shift, :] = lo_ref[shift:, :].astype(BF16)
            if shift:
                main_ref[rows - shift:, :] = hi_ref[:shift, :].astype(BF16)

    tail_ref[:MLA_ROPE, :] = kr_ref[...].astype(BF16)
    tail_ref[MLA_ROPE:MLA_ROPE + GLA_GATE_RANK, :] = glr_ref[...].astype(BF16)
    pad = TAIL_W - MLA_ROPE - GLA_GATE_RANK
    tail_ref[MLA_ROPE + GLA_GATE_RANK:, :] = jnp.zeros((pad, tail_ref.shape[1]), BF16)


def _relayout_w_in(w_in_t):
    d_in, d = w_in_t.shape
    segs, n_main = _segments(MAIN_ORDER)
    rows = RELAYOUT_ROWS
    assert all(dst % rows == 0 and n % rows == 0 and 0 <= src - dst <= LANES and (src - dst) % 16 == 0
               for src, dst, n in segs)
    (kr_src, _, _), (glr_src, _, _) = _segments(TAIL_ORDER)[0]
    assert kr_src % MLA_ROPE == 0 and glr_src % GLA_GATE_RANK == 0
    return pl.pallas_call(
        _relayout_kernel,
        out_shape=(jax.ShapeDtypeStruct((n_main, d), BF16), jax.ShapeDtypeStruct((TAIL_W, d), BF16)),
        grid=(n_main // rows,),
        in_specs=[pl.BlockSpec((rows, d), lambda r: (r, 0)),
                  pl.BlockSpec((LANES, d), lambda r: ((r + 1) * (rows // LANES), 0)),
                  pl.BlockSpec((MLA_ROPE, d), lambda r: (kr_src // MLA_ROPE, 0)),
                  pl.BlockSpec((GLA_GATE_RANK, d), lambda r: (glr_src // GLA_GATE_RANK, 0))],
        out_specs=(pl.BlockSpec((rows, d), lambda r: (r, 0)),
                   pl.BlockSpec((TAIL_W, d), lambda r: (0, 0))),
        compiler_params=_params(("arbitrary",)),
        name="w_in_layout",
    )(w_in_t, w_in_t, w_in_t, w_in_t)


INP_TM = 1024
INP_TN = 2048


def _inproj_kernel(x_ref, mod_ref, g_ref, w_ref, wt_ref, o_ref, tail_ref, h_ref, *, base):
    @pl.when(pl.program_id(1) == 0)
    def _():
        h = _mod_norm(x_ref[...], g_ref[...], mod_ref, base).astype(BF16)
        h_ref[...] = h
        tail_ref[...] = _dot_t1(h, wt_ref[...])

    o_ref[...] = _dot_t1(h_ref[...], w_ref[...]).astype(BF16)


def _inproj(x2d, mod3, g, w_main, w_tail, *, base, seq):
    t, d = x2d.shape
    n = w_main.shape[0]
    tiles_per_batch = seq // INP_TM
    return pl.pallas_call(
        functools.partial(_inproj_kernel, base=base),
        out_shape=(jax.ShapeDtypeStruct((t, n), BF16),
                   jax.ShapeDtypeStruct((t, TAIL_W), F32)),
        grid=(t // INP_TM, n // INP_TN),
        in_specs=[
            pl.BlockSpec((INP_TM, d), lambda i, j: (i, 0)),
            pl.BlockSpec((1, N_MOD, d), lambda i, j: (i // tiles_per_batch, 0, 0)),
            pl.BlockSpec((1, d), lambda i, j: (0, 0)),
            pl.BlockSpec((INP_TN, d), lambda i, j: (j, 0)),
            pl.BlockSpec((TAIL_W, d), lambda i, j: (0, 0)),
        ],
        out_specs=(pl.BlockSpec((INP_TM, INP_TN), lambda i, j: (i, j)),
                   pl.BlockSpec((INP_TM, TAIL_W), lambda i, j: (i, 0))),
        scratch_shapes=[pltpu.VMEM((INP_TM, d), BF16)],
        compiler_params=_params(("parallel", "arbitrary")),
        name="in_proj",
    )(x2d, mod3, g, w_main, w_tail)


PREP_TM = 512
ROPE_HALF = MLA_ROPE // 2


def _rope_rotate(x, cos, sin_signed):
    width = x.shape[-1]
    lane = lax.broadcasted_iota(jnp.int32, x.shape, 1)
    first_half = (lane % MLA_ROPE) < ROPE_HALF
    partner = jnp.where(first_half,
                        pltpu.roll(x, width - ROPE_HALF, 1),
                        pltpu.roll(x, ROPE_HALF, 1))
    return x * cos + partner * sin_signed


def _prep_kernel(qlat_ref, kvlat_ref, tail_ref, pos_ref, invf_ref, sgn_ref,
                 gql_ref, wuq_ref, gqn_ref, gqr_ref, gkvl_ref, wukv_ref, gkn_ref, gkr_ref,
                 grp_ref, q_ref, k_ref, v_ref):
    ang_t = invf_ref[...] * pos_ref[0].astype(F32)
    lane_reps = LANES // ROPE_HALF
    cos = jnp.tile(jnp.cos(ang_t).T, (1, lane_reps))
    sin_signed = jnp.tile(jnp.sin(ang_t).T, (1, lane_reps)) * sgn_ref[...]
    n_rope = MLA_HEADS * MLA_ROPE
    cos_q = jnp.concatenate([cos] * (n_rope // LANES), axis=-1)
    sin_q = jnp.concatenate([sin_signed] * (n_rope // LANES), axis=-1)

    cq = _rms(qlat_ref[...].astype(F32), gql_ref[...]).astype(BF16)
    ckv = _rms(kvlat_ref[...].astype(F32), gkvl_ref[...]).astype(BF16)
    n_nope = MLA_HEADS * MLA_NOPE
    pair = 2 * MLA_NOPE

    for hp in range(MLA_HEADS // 2):
        cols = slice(hp * pair, (hp + 1) * pair)
        qp = _dot(cq, wuq_ref[:, cols])
        kp = _dot(ckv, wukv_ref[:, cols])
        for hh in range(2):
            h = 2 * hp + hh
            sub = slice(hh * MLA_NOPE, (hh + 1) * MLA_NOPE)
            q_ref[0, h, :, :MLA_NOPE] = _rms(qp[:, sub], gqn_ref[...]).astype(BF16)
            k_ref[0, h, :, :MLA_NOPE] = _rms(kp[:, sub], gkn_ref[...]).astype(BF16)
        v_ref[0, :, cols] = _dot(ckv, wukv_ref[:, n_nope + hp * pair:n_nope + (hp + 1) * pair]).astype(BF16)

    qr = _dot(cq, wuq_ref[:, n_nope:])
    ssq = _split_dot(qr * qr, grp_ref[...])
    qr = qr * lax.rsqrt(ssq * (1.0 / MLA_ROPE) + EPS) * gqr_ref[...]
    qr = _rope_rotate(qr, cos_q, sin_q).astype(BF16)
    kr = tail_ref[:, TAIL_KROPE:TAIL_KROPE + MLA_ROPE]
    kr = _rms(kr, gkr_ref[...])
    kr = _rope_rotate(kr, cos[:, :MLA_ROPE], sin_signed[:, :MLA_ROPE]).astype(BF16)
    for h in range(MLA_HEADS):
        q_ref[0, h, :, MLA_NOPE:] = qr[:, h * MLA_ROPE:(h + 1) * MLA_ROPE]
        k_ref[0, h, :, MLA_NOPE:] = kr


def _mla_prep(proj, tail, pos3d, consts, weights, *, batch, seq):
    t = proj.shape[0]
    tiles_per_batch = seq // PREP_TM
    row = lambda i: (i, 0)
    const = lambda i: (0, 0)
    (invf, sgn, grp) = consts
    (gql, wuq, gqn, gqr, gkvl, wukv, gkn, gkr) = weights
    head_out = pl.BlockSpec((1, MLA_HEADS, PREP_TM, MLA_QK),
                            lambda i: (i // tiles_per_batch, 0, i % tiles_per_batch, 0))
    full = lambda a: pl.BlockSpec(a.shape, const)
    return pl.pallas_call(
        _prep_kernel,
        out_shape=(jax.ShapeDtypeStruct((batch, MLA_HEADS, seq, MLA_QK), BF16),
                   jax.ShapeDtypeStruct((batch, MLA_HEADS, seq, MLA_QK), BF16),
                   jax.ShapeDtypeStruct((batch, seq, MLA_HEADS * MLA_V), BF16)),
        grid=(t // PREP_TM,),
        in_specs=[
            pl.BlockSpec((PREP_TM, MLA_Q_RANK), lambda i: (i, COL_QLAT // MLA_Q_RANK)),
            pl.BlockSpec((PREP_TM, MLA_KV_RANK), lambda i: (i, COL_KVLAT // MLA_KV_RANK)),
            pl.BlockSpec((PREP_TM, TAIL_W), row),
            pl.BlockSpec((1, 1, PREP_TM), lambda i: (i, 0, 0)),
            full(invf), full(sgn),
            full(gql), full(wuq), full(gqn), full(gqr),
            full(gkvl), full(wukv), full(gkn), full(gkr),
            full(grp),
        ],
        out_specs=(head_out, head_out,
                   pl.BlockSpec((1, PREP_TM, MLA_HEADS * MLA_V),
                                lambda i: (i // tiles_per_batch, i % tiles_per_batch, 0))),
        compiler_params=_params(("parallel",)),
        name="mla_prep",
    )(proj, proj, tail, pos3d, invf, sgn, gql, wuq, gqn, gqr, gkvl, wukv, gkn, gkr, grp)


ATT_T = 1024
ATT_H = ATT_T // 2


def _attn_kernel(q_ref, k_ref, v_ref, o_ref):
    n_tiles = k_ref.shape[2] // ATT_T
    ones = jnp.ones((ATT_T, MLA_V), BF16)
    r = lax.broadcasted_iota(jnp.int32, (ATT_H, ATT_H), 0)
    c = lax.broadcasted_iota(jnp.int32, (ATT_H, ATT_H), 1)
    visible = (c // CHUNK) <= (r // CHUNK)
    state = {}

    def update(chain, s, v2):
        m_cur = jnp.max(s, axis=-1, keepdims=True)
        if chain not in state:
            m_next = jnp.broadcast_to(m_cur, (ATT_H, LANES))
        else:
            m_prev, l_prev, acc_prev = state[chain]
            m_next = jnp.maximum(m_prev, m_cur)
            alpha = jnp.exp2(m_prev - m_next)
        p = jnp.exp2(s - jnp.tile(m_next, (1, s.shape[1] // LANES)))
        pv = _dot(p.astype(BF16), v2)
        acc, l = pv[:, :MLA_V], pv[:, MLA_V:]
        if chain in state:
            acc, l = alpha * acc_prev + acc, alpha * l_prev + l
        state[chain] = (m_next, l, acc)

    for t in range(n_tiles):
        rows = slice(t * ATT_T, (t + 1) * ATT_T)
        k = k_ref[0, 0, rows, :]
        v2 = jnp.concatenate([v_ref[0, rows, :], ones], axis=-1)
        for qt in range(t, n_tiles):
            q0 = q_ref[0, 0, qt * ATT_T:qt * ATT_T + ATT_H, :]
            q1 = q_ref[0, 0, qt * ATT_T + ATT_H:(qt + 1) * ATT_T, :]
            if t < qt:
                update((qt, 0), _dot_t1(q0, k), v2)
                update((qt, 1), _dot_t1(q1, k), v2)
            else:
                update((qt, 0), jnp.where(visible, _dot_t1(q0, k[:ATT_H]), -jnp.inf), v2[:ATT_H])
                s1 = _dot_t1(q1, k)
                s1 = jnp.concatenate([s1[:, :ATT_H], jnp.where(visible, s1[:, ATT_H:], -jnp.inf)],
                                     axis=-1)
                update((qt, 1), s1, v2)
                for half in range(2):
                    _, l, acc = state[(qt, half)]
                    lo = qt * ATT_T + half * ATT_H
                    o_ref[0, lo:lo + ATT_H, :] = (acc / l).astype(BF16)


def _mla_attn(q, k, v):
    batch, heads, seq, _ = q.shape
    return pl.pallas_call(
        _attn_kernel,
        out_shape=jax.ShapeDtypeStruct((batch, seq, heads * MLA_V), BF16),
        grid=(batch, heads),
        in_specs=[
            pl.BlockSpec((1, 1, seq, MLA_QK), lambda b, h: (b, h, 0, 0)),
            pl.BlockSpec((1, 1, seq, MLA_QK), lambda b, h: (b, h, 0, 0)),
            pl.BlockSpec((1, seq, MLA_V), lambda b, h: (b, 0, h)),
        ],
        out_specs=pl.BlockSpec((1, seq, MLA_V), lambda b, h: (b, 0, h)),
        compiler_params=_params(("parallel", "parallel")),
        name="mla_attn",
    )(q, k, v)


GLA_L = 512
GLA_NC = GLA_L // CHUNK
GLA_HP = 4


def _gla_head(log_a, q, k, v, go, gg, state_ref, causal, tri):
    la_hi = log_a.astype(BF16)
    la_lo = (log_a - la_hi.astype(F32)).astype(BF16)
    cum = _dot(tri, jnp.concatenate([la_hi, la_lo], axis=-1))
    b_cum = cum[:, :GLA_DK] + cum[:, GLA_DK:]
    b3 = b_cum.reshape(GLA_NC, CHUNK, GLA_DK)
    b_end = b3[:, CHUNK - 1:CHUNK, :]
    b_tot = jnp.broadcast_to(b_end, b3.shape).reshape(GLA_L, GLA_DK)

    qf = q.astype(F32) * (GLA_DK ** -0.5)
    kf = k.astype(F32)
    q_dec = (qf * jnp.exp(b_cum)).astype(BF16)
    k_dec = (kf * jnp.exp(-b_cum)).astype(BF16)
    k_end = (kf * jnp.exp(b_tot - b_cum)).astype(BF16)

    attn = jnp.where(causal, _dot_t1(q_dec, k_dec), 0.0).astype(BF16)
    o_intra = _dot(attn, v)

    decay = jnp.exp(b_end)
    state = state_ref[...]
    outs = []
    for n in range(GLA_NC):
        rows = slice(n * CHUNK, (n + 1) * CHUNK)
        outs.append(o_intra[rows] + _dot_t1(q_dec[rows], state.astype(BF16)))
        state = state * decay[n] + _dot_t0(v[rows], k_end[rows])
    state_ref[...] = state

    o = jnp.concatenate(outs, axis=0)
    return (_rms(o, gg) * _silu(go.astype(F32))).astype(BF16)


def _gla_kernel(q_ref, k_ref, v_ref, go_ref, tail_ref, wg_ref, bg_ref, gg_ref,
                o_ref, state_ref):
    @pl.when(pl.program_id(2) == 0)
    def _():
        state_ref[...] = jnp.zeros_like(state_ref)

    wg = wg_ref[...]
    wg_hi = wg.astype(BF16)
    wg_lo = (wg - wg_hi.astype(F32)).astype(BF16)
    tl = tail_ref[...]
    tl_hi = tl.astype(BF16)
    tl_lo = (tl - tl_hi.astype(F32)).astype(BF16)
    z = _dot(tl_hi, wg_hi) + _dot(tl_lo, wg_hi) + _dot(tl_hi, wg_lo) + bg_ref[...]
    log_a = (jnp.minimum(z, 0.0) - jnp.log1p(jnp.exp(-jnp.abs(z)))) * (1.0 / GLA_GATE_NORMALIZER)

    r = lax.broadcasted_iota(jnp.int32, (GLA_L, GLA_L), 0)
    c = lax.broadcasted_iota(jnp.int32, (GLA_L, GLA_L), 1)
    causal = ((r // CHUNK) == (c // CHUNK)) & (c <= r)
    tri = causal.astype(BF16)

    for hh in range(GLA_HP):
        dk = slice(hh * GLA_DK, (hh + 1) * GLA_DK)
        dv = slice(hh * GLA_DV, (hh + 1) * GLA_DV)
        o_ref[:, dv] = _gla_head(log_a[:, dk], q_ref[:, dk], k_ref[:, dk], v_ref[:, dv],
                                 go_ref[:, dv], gg_ref[...], state_ref.at[hh], causal, tri)


def _gla(proj, tail, wg_pad, b_gk, g_gla, *, batch, seq):
    t = proj.shape[0]
    steps = seq // GLA_L
    tok = lambda b, h, i: b * steps + i
    wk, wv = GLA_HP * GLA_DK, GLA_HP * GLA_DV
    return pl.pallas_call(
        _gla_kernel,
        out_shape=jax.ShapeDtypeStruct((t, GLA_VAL), BF16),
        grid=(batch, GLA_HEADS // GLA_HP, steps),
        in_specs=[
            pl.BlockSpec((GLA_L, wk), lambda b, h, i: (tok(b, h, i), COL_GQ // wk + h)),
            pl.BlockSpec((GLA_L, wk), lambda b, h, i: (tok(b, h, i), COL_GK // wk + h)),
            pl.BlockSpec((GLA_L, wv), lambda b, h, i: (tok(b, h, i), COL_GV // wv + h)),
            pl.BlockSpec((GLA_L, wv), lambda b, h, i: (tok(b, h, i), COL_GOUT // wv + h)),
            pl.BlockSpec((GLA_L, TAIL_W), lambda b, h, i: (tok(b, h, i), 0)),
            pl.BlockSpec((TAIL_W, wk), lambda b, h, i: (0, h)),
            pl.BlockSpec((1, wk), lambda b, h, i: (0, h)),
            pl.BlockSpec((1, GLA_DV), lambda b, h, i: (0, 0)),
        ],
        out_specs=pl.BlockSpec((GLA_L, wv), lambda b, h, i: (tok(b, h, i), h)),
        scratch_shapes=[pltpu.VMEM((GLA_HP, GLA_DV, GLA_DK), F32)],
        compiler_params=_params(("parallel", "parallel", "arbitrary")),
        name="gla",
    )(proj, proj, proj, proj, tail, wg_pad, b_gk, g_gla)


MRG_TM = 512


def _merge_kernel(x_ref, mod_ref, a_ref, b_ref, ga_ref, gb_ref, wa_ref, wb_ref, wo_ref, o_ref,
                  *, base):
    ya = _dot(a_ref[...], wa_ref[...])
    yb = _dot(b_ref[...], wb_ref[...])
    merged = (jax.nn.sigmoid(ga_ref[...].astype(F32)) * ya
              + jax.nn.sigmoid(gb_ref[...].astype(F32)) * yb).astype(BF16)
    gate = mod_ref[0, base + 2:base + 3, :]
    o_ref[...] = x_ref[...] + gate * _dot(merged, wo_ref[...])


def _merge(x2d, mod3, attn_o, gla_o, proj, wa, wb, wo, *, base, seq):
    t, d = x2d.shape
    tiles_per_batch = seq // MRG_TM
    row = lambda i: (i, 0)
    resident = lambda a: pl.BlockSpec(a.shape, lambda i: (0, 0), pipeline_mode=pl.Buffered(1))
    return pl.pallas_call(
        functools.partial(_merge_kernel, base=base),
        out_shape=jax.ShapeDtypeStruct((t, d), F32),
        grid=(t // MRG_TM,),
        in_specs=[
            pl.BlockSpec((MRG_TM, d), row),
            pl.BlockSpec((1, N_MOD, d), lambda i: (i // tiles_per_batch, 0, 0)),
            pl.BlockSpec((MRG_TM, attn_o.shape[1]), row),
            pl.BlockSpec((MRG_TM, gla_o.shape[1]), row),
            pl.BlockSpec((MRG_TM, d), lambda i: (i, COL_GATE_A // d)),
            pl.BlockSpec((MRG_TM, d), lambda i: (i, COL_GATE_A // d + 1)),
            resident(wa), resident(wb), resident(wo),
        ],
        out_specs=pl.BlockSpec((MRG_TM, d), row),
        compiler_params=_params(("parallel",)),
        name="merge",
    )(x2d, mod3, attn_o, gla_o, proj, proj, wa, wb, wo)


def _layer(x2d, mod3, pos3d, p, *, batch, seq):
    d = x2d.shape[1]
    assert IN_SPLITS[-1] == d and sum(IN_SPLITS) == p["w_in"].shape[1]
    w_main, w_tail = _relayout_w_in(p["w_in"].T)
    wuq = p["w_uq"].reshape(MLA_Q_RANK, MLA_HEADS, MLA_QK)
    wuq = jnp.concatenate([wuq[:, :, :MLA_NOPE].reshape(MLA_Q_RANK, -1),
                           wuq[:, :, MLA_NOPE:].reshape(MLA_Q_RANK, -1)], axis=1).astype(BF16)
    wukv = p["w_ukv"].reshape(MLA_KV_RANK, MLA_HEADS, MLA_NOPE + MLA_V)
    wukv = jnp.concatenate([wukv[:, :, :MLA_NOPE].reshape(MLA_KV_RANK, -1),
                            wukv[:, :, MLA_NOPE:].reshape(MLA_KV_RANK, -1)], axis=1).astype(BF16)
    wg_pad = jnp.zeros((TAIL_W, GLA_KEY), F32).at[TAIL_GLR:TAIL_GLR + GLA_GATE_RANK].set(p["w_gk_up"])

    inv_freq = ROPE_THETA ** (-jnp.arange(0, MLA_ROPE, 2, dtype=F32) / MLA_ROPE)
    invf = inv_freq.reshape(ROPE_HALF, 1)
    sgn = jnp.tile(jnp.concatenate([-jnp.ones(ROPE_HALF, F32), jnp.ones(ROPE_HALF, F32)]),
                   LANES // MLA_ROPE).reshape(1, LANES)
    n_rope = MLA_HEADS * MLA_ROPE
    lane = jnp.arange(n_rope)
    grp = (lane[:, None] // MLA_ROPE == lane[None, :] // MLA_ROPE).astype(BF16)

    row = lambda a: a.reshape(1, -1)
    q_scale = MLA_QK ** -0.5 * LOG2E
    x2d = _ffn(x2d, mod3, row(p["g_ffn1"]), p["w1_a"][None], p["w3_a"][None],
               p["w2_a"][None], row(p["g_final"]), base=0, final_norm=False, seq=seq)
    proj, tail = _inproj(x2d, mod3, row(p["g_mix"]), w_main, w_tail, base=3, seq=seq)
    q, k, v = _mla_prep(
        proj, tail, pos3d, (invf, sgn, grp),
        (row(p["g_q_lat"]), wuq, row(p["g_qn"] * q_scale),
         row(jnp.tile(p["g_qr"], MLA_HEADS) * q_scale),
         row(p["g_kv_lat"]), wukv, row(p["g_kn"]), row(p["g_kr"])),
        batch=batch, seq=seq)
    attn_o = _mla_attn(q, k, v).reshape(batch * seq, MLA_HEADS * MLA_V)
    gla_o = _gla(proj, tail, wg_pad, row(p["b_gk"]), row(p["g_gla"]), batch=batch, seq=seq)
    x2d = _merge(x2d, mod3, attn_o, gla_o, proj, p["w_proj_a"].astype(BF16),
                 p["w_proj_b"].astype(BF16), p["w_out"].astype(BF16), base=3, seq=seq)
    x2d = _ffn(x2d, mod3, row(p["g_ffn2"]), p["w1_b"][None], p["w3_b"][None],
               p["w2_b"][None], row(p["g_final"]), base=6, final_norm=True, seq=seq)
    return x2d


def kernel(x, c, positions, w_ada, b_ada, g_ffn1, w1_a, w3_a, w2_a, g_mix, w_in, g_q_lat, w_uq, g_qn, g_qr, g_kv_lat, w_ukv, g_kn, g_kr, w_gk_up, b_gk, g_gla, w_proj_a, w_proj_b, w_out, g_ffn2, w1_b, w3_b, w2_b, g_final):
    batch, seq, d = x.shape
    depth = w_ada.shape[0]
    assert depth == 1, "the final norm is fused into the last FFN of a single layer"
    names = ("g_ffn1", "w1_a", "w3_a", "w2_a", "g_mix", "w_in", "g_q_lat", "w_uq", "g_qn", "g_qr",
             "g_kv_lat", "w_ukv", "g_kn", "g_kr", "w_gk_up", "b_gk", "g_gla", "w_proj_a", "w_proj_b",
             "w_out", "g_ffn2", "w1_b", "w3_b", "w2_b", "g_final")
    stacked = (g_ffn1, w1_a, w3_a, w2_a, g_mix, w_in, g_q_lat, w_uq, g_qn, g_qr, g_kv_lat, w_ukv,
               g_kn, g_kr, w_gk_up, b_gk, g_gla, w_proj_a, w_proj_b, w_out, g_ffn2, w1_b, w3_b, w2_b,
               g_final)
    x2d = x.reshape(batch * seq, d)
    pos3d = positions.reshape(batch * seq // PREP_TM, 1, PREP_TM)
    p = {n: a[0] for n, a in zip(names, stacked)}
    mod3 = _adaln(c, w_ada[0], b_ada[0]).reshape(batch, N_MOD, d)
    x2d = _layer(x2d, mod3, pos3d, p, batch=batch, seq=seq)
    return x2d.reshape(batch, seq, d)
```

```python
import functools

import jax
import jax.numpy as jnp
from jax import lax
from jax.experimental import pallas as pl
from jax.experimental.pallas import tpu as pltpu

F32 = jnp.float32
BF16 = jnp.bfloat16

EPS = 1e-6
LOG2E = 1.4426950408889634
CHUNK = 64
MLA_HEADS = 8
MLA_Q_RANK = 512
MLA_KV_RANK = 512
MLA_NOPE = 128
MLA_ROPE = 64
MLA_V = 128
MLA_QK = MLA_NOPE + MLA_ROPE
ROPE_THETA = 10000.0
GLA_HEADS = 4
GLA_DK = 128
GLA_DV = 256
GLA_GATE_RANK = 16
GLA_GATE_NORMALIZER = 16.0
GLA_KEY = GLA_HEADS * GLA_DK
GLA_VAL = GLA_HEADS * GLA_DV
N_MOD = 9

LANES = 128
SUBLANES = 8
VMEM_LIMIT = 56 * 1024 * 1024

COL_QLAT = 0
COL_KVLAT = COL_QLAT + MLA_Q_RANK
COL_GQ = COL_KVLAT + MLA_KV_RANK
COL_GK = COL_GQ + GLA_KEY
COL_GV = COL_GK + GLA_KEY
COL_GOUT = COL_GV + GLA_VAL
COL_GATE_A = COL_GOUT + GLA_VAL
TAIL_W = LANES
TAIL_KROPE = 0
TAIL_GLR = MLA_ROPE


def _params(sem):
    return pltpu.CompilerParams(dimension_semantics=sem, vmem_limit_bytes=VMEM_LIMIT)


def _dot(a, b):
    return jnp.dot(a, b, preferred_element_type=F32)


def _dot_t0(a, b):
    return lax.dot_general(a, b, (((0,), (0,)), ((), ())), preferred_element_type=F32)


def _dot_t1(a, b):
    return lax.dot_general(a, b, (((1,), (1,)), ((), ())), preferred_element_type=F32)


def _split_dot(x, w_bf16):
    hi = x.astype(BF16)
    lo = (x - hi.astype(F32)).astype(BF16)
    return _dot(hi, w_bf16) + _dot(lo, w_bf16)


def _split_dot_t0(x, w_bf16):
    hi = x.astype(BF16)
    lo = (x - hi.astype(F32)).astype(BF16)
    return _dot_t0(hi, w_bf16) + _dot_t0(lo, w_bf16)


def _silu(x):
    return x * jax.nn.sigmoid(x)


def _rms(x, g):
    ms = jnp.mean(x * x, axis=-1, keepdims=True)
    return x * lax.rsqrt(ms + EPS) * g


ADA_TN = 2048
ADA_RC = 64


def _adaln_kernel(ct_ref, w_ref, b_ref, o_ref, s_ref):
    d, nb = ct_ref.shape
    tn = w_ref.shape[1]

    @pl.when(pl.program_id(0) == 0)
    def _():
        s = _silu(ct_ref[...])
        for b in range(nb):
            s_ref[b] = jnp.broadcast_to(s[:, b:b + 1], (d, LANES))

    def body(k, acc):
        rows = pl.ds(pl.multiple_of(k * ADA_RC, ADA_RC), ADA_RC)
        out = []
        for b in range(nb):
            sb = s_ref[b, rows, :]
            cols = []
            for col in range(tn // LANES):
                p = w_ref[rows, col * LANES:(col + 1) * LANES] * sb
                cols.append(p.reshape(ADA_RC // SUBLANES, SUBLANES, LANES).sum(axis=0))
            out.append(acc[b] + jnp.concatenate(cols, axis=-1))
        return tuple(out)

    init = tuple(jnp.zeros((SUBLANES, tn), F32) for _ in range(nb))
    acc = lax.fori_loop(0, d // ADA_RC, body, init, unroll=2)
    rows = [jnp.sum(a, axis=0, keepdims=True) for a in acc]
    o_ref[...] = jnp.concatenate(rows, axis=0) + b_ref[...]


def _adaln(c, w_ada, b_ada):
    nb, d = c.shape
    n = w_ada.shape[1]
    return pl.pallas_call(
        _adaln_kernel,
        out_shape=jax.ShapeDtypeStruct((nb, n), F32),
        grid=(n // ADA_TN,),
        in_specs=[
            pl.BlockSpec((d, nb), lambda j: (0, 0)),
            pl.BlockSpec((d, ADA_TN), lambda j: (0, j)),
            pl.BlockSpec((1, ADA_TN), lambda j: (0, j)),
        ],
        out_specs=pl.BlockSpec((nb, ADA_TN), lambda j: (0, j)),
        scratch_shapes=[pltpu.VMEM((nb, d, LANES), F32)],
        compiler_params=_params(("arbitrary",)),
        name="adaln",
    )(c.T, w_ada, b_ada.reshape(1, n))


def _mod_norm(x, g, mod_ref, base):
    shift = mod_ref[0, base:base + 1, :]
    scale = mod_ref[0, base + 1:base + 2, :]
    return _rms(x, g) * (1.0 + scale) + shift


FFN_TM = 512
FFN_TF = 512
FFN_HEAD_TILES = 2
FFN_HEAD_TF = 256
FFN_GROUP = 256
FFN_AHEAD_CHUNKS = 8
FFN_AHEAD_ROWS = FFN_TM // FFN_AHEAD_CHUNKS


def _swiglu_accumulate(h, w1g, w3g, w2t, o_ref):
    upd = None
    for s in range(w1g.shape[0]):
        a = _dot(h, w1g[s])
        b = _dot(h, w3g[s])
        p = (_silu(a) * b).astype(BF16)
        part = _dot(p, w2t[s * FFN_GROUP:(s + 1) * FFN_GROUP, :])
        upd = part if upd is None else upd + part
    o_ref[...] += upd


def _ffn_finish(x_ref, mod_ref, gf_ref, o_ref, *, base, final_norm):
    gate = mod_ref[0, base + 2:base + 3, :]
    y = x_ref[...] + 0.5 * gate * o_ref[...]
    if final_norm:
        y = _rms(y, gf_ref[...])
    o_ref[...] = y


def _ffn_head_kernel(x_ref, mod_ref, g_ref, w1_ref, w3_ref, w2_ref, gf_ref,
                     o_ref, w1b_ref, w3b_ref, w2b_ref, h_ref, *, base, final_norm):
    j = pl.program_id(1)

    @pl.when(j == 0)
    def _():
        h_ref[...] = _mod_norm(x_ref[...], g_ref[...], mod_ref, base).astype(BF16)
        o_ref[...] = jnp.zeros_like(o_ref)

    w1 = w1_ref[0].astype(BF16)
    w3 = w3_ref[0].astype(BF16)
    w2 = w2_ref[0].astype(BF16)
    w1b_ref[0] = w1
    w3b_ref[0] = w3
    w2b_ref[...] = w2
    _swiglu_accumulate(h_ref[...], w1[None], w3[None], w2, o_ref)

    @pl.when(j == pl.num_programs(1) - 1)
    def _():
        _ffn_finish(x_ref, mod_ref, gf_ref, o_ref, base=base, final_norm=final_norm)


def _ffn_kernel(x_ref, xn_ref, mod_ref, modn_ref, g_ref, w1_ref, w3_ref, w2_ref, gf_ref, head_ref,
                o_ref, h_ref, *, base, final_norm):
    del head_ref
    i = pl.program_id(0)
    j = pl.program_id(1)
    slot = i % 2

    @pl.when(j == 0)
    def _():
        o_ref[...] = jnp.zeros_like(o_ref)

        @pl.when(i == 0)
        def _():
            h_ref[slot] = _mod_norm(x_ref[...], g_ref[...], mod_ref, base).astype(BF16)

    chunk = jnp.minimum(j, FFN_AHEAD_CHUNKS - 1)
    rows = pl.ds(pl.multiple_of(chunk * FFN_AHEAD_ROWS, FFN_AHEAD_ROWS), FFN_AHEAD_ROWS)
    h_next = _mod_norm(xn_ref[rows, :], g_ref[...], modn_ref, base).astype(BF16)

    _swiglu_accumulate(h_ref[slot], w1_ref, w3_ref, w2_ref[...], o_ref)
    h_ref[1 - slot, rows, :] = h_next

    @pl.when(j == pl.num_programs(1) - 1)
    def _():
        _ffn_finish(x_ref, mod_ref, gf_ref, o_ref, base=base, final_norm=final_norm)


def _ffn(x2d, mod3, g, w1, w3, w2, g_final, *, base, final_norm, seq):
    t, d = x2d.shape
    dff = w1.shape[2]
    tiles_per_batch = seq // FFN_TM
    const = lambda i, j: (0, 0)
    kw = dict(base=base, final_norm=final_norm)
    tag = "_final" if final_norm else ""
    common = [
        pl.BlockSpec((1, N_MOD, d), lambda i, j: (i // tiles_per_batch, 0, 0)),
        pl.BlockSpec((1, d), const),
    ]
    hf = FFN_HEAD_TF
    head_rows = FFN_HEAD_TILES * FFN_TM
    assert head_rows <= seq
    assert hf == FFN_GROUP
    w13_shape = jax.ShapeDtypeStruct((dff // FFN_GROUP, d, FFN_GROUP), BF16)
    w13_spec = pl.BlockSpec((1, d, FFN_GROUP), lambda i, j: (j, 0, 0))
    y_head, w1b, w3b, w2b = pl.pallas_call(
        functools.partial(_ffn_head_kernel, **kw),
        out_shape=(jax.ShapeDtypeStruct((t, d), F32), w13_shape, w13_shape,
                   jax.ShapeDtypeStruct((dff, d), BF16)),
        grid=(1, dff // hf),
        in_specs=[pl.BlockSpec((head_rows, d), const, pipeline_mode=pl.Buffered(1))] + common + [
            pl.BlockSpec((1, d, hf), lambda i, j: (0, 0, j)),
            pl.BlockSpec((1, d, hf), lambda i, j: (0, 0, j)),
            pl.BlockSpec((1, hf, d), lambda i, j: (0, j, 0)),
            pl.BlockSpec((1, d), const),
        ],
        out_specs=(pl.BlockSpec((head_rows, d), const), w13_spec, w13_spec,
                   pl.BlockSpec((hf, d), lambda i, j: (j, 0))),
        scratch_shapes=[pltpu.VMEM((head_rows, d), BF16)],
        compiler_params=_params(("arbitrary", "arbitrary")),
        name="ffn_head" + tag,
    )(x2d, mod3, g, w1, w3, w2, g_final)

    n_tiles = t // FFN_TM
    assert dff // FFN_TF >= FFN_AHEAD_CHUNKS and n_tiles > FFN_HEAD_TILES
    tile = lambda i: i + FFN_HEAD_TILES
    nxt = lambda i: jnp.minimum(tile(i) + 1, n_tiles - 1)
    head_operand = 9
    return pl.pallas_call(
        functools.partial(_ffn_kernel, **kw),
        out_shape=jax.ShapeDtypeStruct((t, d), F32),
        grid=(n_tiles - FFN_HEAD_TILES, dff // FFN_TF),
        in_specs=[
            pl.BlockSpec((FFN_TM, d), lambda i, j: (tile(i), 0)),
            pl.BlockSpec((FFN_TM, d), lambda i, j: (nxt(i), 0)),
            pl.BlockSpec((1, N_MOD, d), lambda i, j: (tile(i) // tiles_per_batch, 0, 0)),
            pl.BlockSpec((1, N_MOD, d), lambda i, j: (nxt(i) // tiles_per_batch, 0, 0)),
            pl.BlockSpec((1, d), const),
            pl.BlockSpec((FFN_TF // FFN_GROUP, d, FFN_GROUP), lambda i, j: (j, 0, 0)),
            pl.BlockSpec((FFN_TF // FFN_GROUP, d, FFN_GROUP), lambda i, j: (j, 0, 0)),
            pl.BlockSpec((FFN_TF, d), lambda i, j: (j, 0)),
            pl.BlockSpec((1, d), const),
            pl.BlockSpec(memory_space=pl.ANY),
        ],
        out_specs=pl.BlockSpec((FFN_TM, d), lambda i, j: (tile(i), 0)),
        scratch_shapes=[pltpu.VMEM((2, FFN_TM, d), BF16)],
        input_output_aliases={head_operand: 0},
        compiler_params=_params(("arbitrary", "arbitrary")),
        name="ffn" + tag,
    )(x2d, x2d, mod3, mod3, g, w1b, w3b, w2b, g_final, y_head)


IN_SPLITS = (MLA_Q_RANK, MLA_KV_RANK, MLA_ROPE, GLA_KEY, GLA_KEY, GLA_VAL, GLA_GATE_RANK, GLA_VAL,
             2048, 2048)
MAIN_ORDER = (0, 1, 3, 4, 5, 7, 8, 9)
TAIL_ORDER = (2, 6)
RELAYOUT_ROWS = 1024


def _segments(order):
    starts = [sum(IN_SPLITS[:i]) for i in range(len(IN_SPLITS))]
    segs, dst = [], 0
    for i in order:
        src, n = starts[i], IN_SPLITS[i]
        if segs and segs[-1][0] + segs[-1][2] == src:
            segs[-1] = (segs[-1][0], segs[-1][1], segs[-1][2] + n)
        else:
            segs.append((src, dst, n))
        dst += n
    return segs, dst


def _relayout_kernel(lo_ref, hi_ref, kr_ref, glr_ref, main_ref, tail_ref):
    r = pl.program_id(0)
    rows = RELAYOUT_ROWS
    for src, dst, n in _segments(MAIN_ORDER)[0]:
        shift = src - dst

        @pl.when((r >= dst // rows) & (r < (dst + n) // rows))
        def _(shift=shift):
            main_ref[:rows - shift, :] = lo_ref[shift:, :].astype(BF16)
            if shift:
                main_ref[rows - shift:, :] = hi_ref[:shift, :].astype(BF16)

    tail_ref[:MLA_ROPE, :] = kr_ref[...].astype(BF16)
    tail_ref[MLA_ROPE:MLA_ROPE + GLA_GATE_RANK, :] = glr_ref[...].astype(BF16)
    pad = TAIL_W - MLA_ROPE - GLA_GATE_RANK
    tail_ref[MLA_ROPE + GLA_GATE_RANK:, :] = jnp.zeros((pad, tail_ref.shape[1]), BF16)


def _relayout_w_in(w_in_t):
    d_in, d = w_in_t.shape
    segs, n_main = _segments(MAIN_ORDER)
    rows = RELAYOUT_ROWS
    assert all(dst % rows == 0 and n % rows == 0 and 0 <= src - dst <= LANES and (src - dst) % 16 == 0
               for src, dst, n in segs)
    (kr_src, _, _), (glr_src, _, _) = _segments(TAIL_ORDER)[0]
    assert kr_src % MLA_ROPE == 0 and glr_src % GLA_GATE_RANK == 0
    return pl.pallas_call(
        _relayout_kernel,
        out_shape=(jax.ShapeDtypeStruct((n_main, d), BF16), jax.ShapeDtypeStruct((TAIL_W, d), BF16)),
        grid=(n_main // rows,),
        in_specs=[pl.BlockSpec((rows, d), lambda r: (r, 0)),
                  pl.BlockSpec((LANES, d), lambda r: ((r + 1) * (rows // LANES), 0)),
                  pl.BlockSpec((MLA_ROPE, d), lambda r: (kr_src // MLA_ROPE, 0)),
                  pl.BlockSpec((GLA_GATE_RANK, d), lambda r: (glr_src // GLA_GATE_RANK, 0))],
        out_specs=(pl.BlockSpec((rows, d), lambda r: (r, 0)),
                   pl.BlockSpec((TAIL_W, d), lambda r: (0, 0))),
        compiler_params=_params(("arbitrary",)),
        name="w_in_layout",
    )(w_in_t, w_in_t, w_in_t, w_in_t)


INP_TM = 1024
INP_TN = 2048


def _inproj_kernel(x_ref, mod_ref, g_ref, w_ref, wt_ref, o_ref, tail_ref, h_ref, *, base):
    @pl.when(pl.program_id(1) == 0)
    def _():
        h = _mod_norm(x_ref[...], g_ref[...], mod_ref, base).astype(BF16)
        h_ref[...] = h
        tail_ref[...] = _dot_t1(h, wt_ref[...])

    o_ref[...] = _dot_t1(h_ref[...], w_ref[...]).astype(BF16)


def _inproj(x2d, mod3, g, w_main, w_tail, *, base, seq):
    t, d = x2d.shape
    n = w_main.shape[0]
    tiles_per_batch = seq // INP_TM
    return pl.pallas_call(
        functools.partial(_inproj_kernel, base=base),
        out_shape=(jax.ShapeDtypeStruct((t, n), BF16),
                   jax.ShapeDtypeStruct((t, TAIL_W), F32)),
        grid=(t // INP_TM, n // INP_TN),
        in_specs=[
            pl.BlockSpec((INP_TM, d), lambda i, j: (i, 0)),
            pl.BlockSpec((1, N_MOD, d), lambda i, j: (i // tiles_per_batch, 0, 0)),
            pl.BlockSpec((1, d), lambda i, j: (0, 0)),
            pl.BlockSpec((INP_TN, d), lambda i, j: (j, 0)),
            pl.BlockSpec((TAIL_W, d), lambda i, j: (0, 0)),
        ],
        out_specs=(pl.BlockSpec((INP_TM, INP_TN), lambda i, j: (i, j)),
                   pl.BlockSpec((INP_TM, TAIL_W), lambda i, j: (i, 0))),
        scratch_shapes=[pltpu.VMEM((INP_TM, d), BF16)],
        compiler_params=_params(("parallel", "arbitrary")),
        name="in_proj",
    )(x2d, mod3, g, w_main, w_tail)


PREP_TM = 512
ROPE_HALF = MLA_ROPE // 2


def _rope_rotate(x, cos, sin_signed):
    width = x.shape[-1]
    lane = lax.broadcasted_iota(jnp.int32, x.shape, 1)
    first_half = (lane % MLA_ROPE) < ROPE_HALF
    partner = jnp.where(first_half,
                        pltpu.roll(x, width - ROPE_HALF, 1),
                        pltpu.roll(x, ROPE_HALF, 1))
    return x * cos + partner * sin_signed


def _prep_kernel(qlat_ref, kvlat_ref, tail_ref, pos_ref, invf_ref, sgn_ref,
                 gql_ref, wuq_ref, gqn_ref, gqr_ref, gkvl_ref, wukv_ref, gkn_ref, gkr_ref,
                 grp_ref, q_ref, k_ref, v_ref):
    ang_t = invf_ref[...] * pos_ref[0].astype(F32)
    lane_reps = LANES // ROPE_HALF
    cos = jnp.tile(jnp.cos(ang_t).T, (1, lane_reps))
    sin_signed = jnp.tile(jnp.sin(ang_t).T, (1, lane_reps)) * sgn_ref[...]
    n_rope = MLA_HEADS * MLA_ROPE
    cos_q = jnp.concatenate([cos] * (n_rope // LANES), axis=-1)
    sin_q = jnp.concatenate([sin_signed] * (n_rope // LANES), axis=-1)

    cq = _rms(qlat_ref[...].astype(F32), gql_ref[...]).astype(BF16)
    ckv = _rms(kvlat_ref[...].astype(F32), gkvl_ref[...]).astype(BF16)
    n_nope = MLA_HEADS * MLA_NOPE
    pair = 2 * MLA_NOPE

    for hp in range(MLA_HEADS // 2):
        cols = slice(hp * pair, (hp + 1) * pair)
        qp = _dot(cq, wuq_ref[:, cols])
        kp = _dot(ckv, wukv_ref[:, cols])
        for hh in range(2):
            h = 2 * hp + hh
            sub = slice(hh * MLA_NOPE, (hh + 1) * MLA_NOPE)
            q_ref[0, h, :, :MLA_NOPE] = _rms(qp[:, sub], gqn_ref[...]).astype(BF16)
            k_ref[0, h, :, :MLA_NOPE] = _rms(kp[:, sub], gkn_ref[...]).astype(BF16)
        v_ref[0, :, cols] = _dot(ckv, wukv_ref[:, n_nope + hp * pair:n_nope + (hp + 1) * pair]).astype(BF16)

    qr = _dot(cq, wuq_ref[:, n_nope:])
    ssq = _split_dot(qr * qr, grp_ref[...])
    qr = qr * lax.rsqrt(ssq * (1.0 / MLA_ROPE) + EPS) * gqr_ref[...]
    qr = _rope_rotate(qr, cos_q, sin_q).astype(BF16)
    kr = tail_ref[:, TAIL_KROPE:TAIL_KROPE + MLA_ROPE]
    kr = _rms(kr, gkr_ref[...])
    kr = _rope_rotate(kr, cos[:, :MLA_ROPE], sin_signed[:, :MLA_ROPE]).astype(BF16)
    for h in range(MLA_HEADS):
        q_ref[0, h, :, MLA_NOPE:] = qr[:, h * MLA_ROPE:(h + 1) * MLA_ROPE]
        k_ref[0, h, :, MLA_NOPE:] = kr


def _mla_prep(proj, tail, pos3d, consts, weights, *, batch, seq):
    t = proj.shape[0]
    tiles_per_batch = seq // PREP_TM
    row = lambda i: (i, 0)
    const = lambda i: (0, 0)
    (invf, sgn, grp) = consts
    (gql, wuq, gqn, gqr, gkvl, wukv, gkn, gkr) = weights
    head_out = pl.BlockSpec((1, MLA_HEADS, PREP_TM, MLA_QK),
                            lambda i: (i // tiles_per_batch, 0, i % tiles_per_batch, 0))
    full = lambda a: pl.BlockSpec(a.shape, const)
    return pl.pallas_call(
        _prep_kernel,
        out_shape=(jax.ShapeDtypeStruct((batch, MLA_HEADS, seq, MLA_QK), BF16),
                   jax.ShapeDtypeStruct((batch, MLA_HEADS, seq, MLA_QK), BF16),
                   jax.ShapeDtypeStruct((batch, seq, MLA_HEADS * MLA_V), BF16)),
        grid=(t // PREP_TM,),
        in_specs=[
            pl.BlockSpec((PREP_TM, MLA_Q_RANK), lambda i: (i, COL_QLAT // MLA_Q_RANK)),
            pl.BlockSpec((PREP_TM, MLA_KV_RANK), lambda i: (i, COL_KVLAT // MLA_KV_RANK)),
            pl.BlockSpec((PREP_TM, TAIL_W), row),
            pl.BlockSpec((1, 1, PREP_TM), lambda i: (i, 0, 0)),
            full(invf), full(sgn),
            full(gql), full(wuq), full(gqn), full(gqr),
            full(gkvl), full(wukv), full(gkn), full(gkr),
            full(grp),
        ],
        out_specs=(head_out, head_out,
                   pl.BlockSpec((1, PREP_TM, MLA_HEADS * MLA_V),
                                lambda i: (i // tiles_per_batch, i % tiles_per_batch, 0))),
        compiler_params=_params(("parallel",)),
        name="mla_prep",
    )(proj, proj, tail, pos3d, invf, sgn, gql, wuq, gqn, gqr, gkvl, wukv, gkn, gkr, grp)


ATT_T = 1024
ATT_H = ATT_T // 2


def _attn_kernel(q_ref, k_ref, v_ref, o_ref):
    n_tiles = k_ref.shape[2] // ATT_T
    ones = jnp.ones((ATT_T, MLA_V), BF16)
    r = lax.broadcasted_iota(jnp.int32, (ATT_H, ATT_H), 0)
    c = lax.broadcasted_iota(jnp.int32, (ATT_H, ATT_H), 1)
    visible = (c // CHUNK) <= (r // CHUNK)
    state = {}

    def update(chain, s, v2):
        m_cur = jnp.max(s, axis=-1, keepdims=True)
        if chain not in state:
            m_next = jnp.broadcast_to(m_cur, (ATT_H, LANES))
        else:
            m_prev, l_prev, acc_prev = state[chain]
            m_next = jnp.maximum(m_prev, m_cur)
            alpha = jnp.exp2(m_prev - m_next)
        p = jnp.exp2(s - jnp.tile(m_next, (1, s.shape[1] // LANES)))
        pv = _dot(p.astype(BF16), v2)
        acc, l = pv[:, :MLA_V], pv[:, MLA_V:]
        if chain in state:
            acc, l = alpha * acc_prev + acc, alpha * l_prev + l
        state[chain] = (m_next, l, acc)

    for t in range(n_tiles):
        rows = slice(t * ATT_T, (t + 1) * ATT_T)
        k = k_ref[0, 0, rows, :]
        v2 = jnp.concatenate([v_ref[0, rows, :], ones], axis=-1)
        for qt in range(t, n_tiles):
            q0 = q_ref[0, 0, qt * ATT_T:qt * ATT_T + ATT_H, :]
            q1 = q_ref[0, 0, qt * ATT_T + ATT_H:(qt + 1) * ATT_T, :]
            if t < qt:
                update((qt, 0), _dot_t1(q0, k), v2)
                update((qt, 1), _dot_t1(q1, k), v2)
            else:
                update((qt, 0), jnp.where(visible, _dot_t1(q0, k[:ATT_H]), -jnp.inf), v2[:ATT_H])
                s1 = _dot_t1(q1, k)
                s1 = jnp.concatenate([s1[:, :ATT_H], jnp.where(visible, s1[:, ATT_H:], -jnp.inf)],
                                     axis=-1)
                update((qt, 1), s1, v2)
                for half in range(2):
                    _, l, acc = state[(qt, half)]
                    lo = qt * ATT_T + half * ATT_H
                    o_ref[0, lo:lo + ATT_H, :] = (acc / l).astype(BF16)


def _mla_attn(q, k, v):
    batch, heads, seq, _ = q.shape
    return pl.pallas_call(
        _attn_kernel,
        out_shape=jax.ShapeDtypeStruct((batch, seq, heads * MLA_V), BF16),
        grid=(batch, heads),
        in_specs=[
            pl.BlockSpec((1, 1, seq, MLA_QK), lambda b, h: (b, h, 0, 0)),
            pl.BlockSpec((1, 1, seq, MLA_QK), lambda b, h: (b, h, 0, 0)),
            pl.BlockSpec((1, seq, MLA_V), lambda b, h: (b, 0, h)),
        ],
        out_specs=pl.BlockSpec((1, seq, MLA_V), lambda b, h: (b, 0, h)),
        compiler_params=_params(("parallel", "parallel")),
        name="mla_attn",
    )(q, k, v)


GLA_L = 512
GLA_NC = GLA_L // CHUNK
GLA_SUB = 512


def _gla_head(log_a, q, k, v, go, gg, state_ref, causal, tri):
    sub_nc = GLA_SUB // CHUNK
    q_dec, k_end, o_intra, b_ends = [], [], [], []
    for sb in range(GLA_L // GLA_SUB):
        rows = slice(sb * GLA_SUB, (sb + 1) * GLA_SUB)
        la = log_a[rows]
        la_hi = la.astype(BF16)
        la_lo = (la - la_hi.astype(F32)).astype(BF16)
        cum = _dot(tri, jnp.concatenate([la_hi, la_lo], axis=-1))
        b_cum = cum[:, :GLA_DK] + cum[:, GLA_DK:]
        b3 = b_cum.reshape(sub_nc, CHUNK, GLA_DK)
        b_end = b3[:, CHUNK - 1:CHUNK, :]
        b_tot = jnp.broadcast_to(b_end, b3.shape).reshape(GLA_SUB, GLA_DK)
        b_ends.append(jnp.exp(b_end))

        qf = q[rows].astype(F32) * (GLA_DK ** -0.5)
        kf = k[rows].astype(F32)
        qd = (qf * jnp.exp(b_cum)).astype(BF16)
        kd = (kf * jnp.exp(-b_cum)).astype(BF16)
        q_dec.append(qd)
        k_end.append((kf * jnp.exp(b_tot - b_cum)).astype(BF16))
        attn = jnp.where(causal, _dot_t1(qd, kd), 0.0).astype(BF16)
        o_intra.append(_dot(attn, v[rows]))

    state = state_ref[...]
    outs = []
    for n in range(GLA_NC):
        sb, rows = n // sub_nc, slice((n % sub_nc) * CHUNK, (n % sub_nc + 1) * CHUNK)
        outs.append(o_intra[sb][rows] + _dot_t1(q_dec[sb][rows], state.astype(BF16)))
        v_n = v[n * CHUNK:(n + 1) * CHUNK]
        state = state * b_ends[sb][n % sub_nc] + _dot_t0(v_n, k_end[sb][rows])
    state_ref[...] = state

    o = jnp.concatenate(outs, axis=0)
    return (_rms(o, gg) * _silu(go.astype(F32))).astype(BF16)


def _gla_kernel(q_ref, k_ref, v_ref, go_ref, tail_ref, wg_ref, bg_ref, gg_ref,
                o_ref, state_ref):
    @pl.when(pl.program_id(1) == 0)
    def _():
        state_ref[...] = jnp.zeros_like(state_ref)

    wg = wg_ref[...]
    wg_hi = wg.astype(BF16)
    wg_lo = (wg - wg_hi.astype(F32)).astype(BF16)
    tl = tail_ref[...]
    tl_hi = tl.astype(BF16)
    tl_lo = (tl - tl_hi.astype(F32)).astype(BF16)
    z = _dot(tl_hi, wg_hi) + _dot(tl_lo, wg_hi) + _dot(tl_hi, wg_lo) + bg_ref[...]
    log_a = (jnp.minimum(z, 0.0) - jnp.log1p(jnp.exp(-jnp.abs(z)))) * (1.0 / GLA_GATE_NORMALIZER)

    r = lax.broadcasted_iota(jnp.int32, (GLA_SUB, GLA_SUB), 0)
    c = lax.broadcasted_iota(jnp.int32, (GLA_SUB, GLA_SUB), 1)
    causal = ((r // CHUNK) == (c // CHUNK)) & (c <= r)
    tri = causal.astype(BF16)

    for hh in range(GLA_HEADS):
        dk = slice(hh * GLA_DK, (hh + 1) * GLA_DK)
        dv = slice(hh * GLA_DV, (hh + 1) * GLA_DV)
        o_ref[:, dv] = _gla_head(log_a[:, dk], q_ref[:, dk], k_ref[:, dk], v_ref[:, dv],
                                 go_ref[:, dv], gg_ref[...], state_ref.at[hh], causal, tri)


def _gla(proj, tail, wg_pad, b_gk, g_gla, *, batch, seq):
    t = proj.shape[0]
    steps = seq // GLA_L
    tok = lambda b, i: b * steps + i
    return pl.pallas_call(
        _gla_kernel,
        out_shape=jax.ShapeDtypeStruct((t, GLA_VAL), BF16),
        grid=(batch, steps),
        in_specs=[
            pl.BlockSpec((GLA_L, GLA_KEY), lambda b, i: (tok(b, i), COL_GQ // GLA_KEY)),
            pl.BlockSpec((GLA_L, GLA_KEY), lambda b, i: (tok(b, i), COL_GK // GLA_KEY)),
            pl.BlockSpec((GLA_L, GLA_VAL), lambda b, i: (tok(b, i), COL_GV // GLA_VAL)),
            pl.BlockSpec((GLA_L, GLA_VAL), lambda b, i: (tok(b, i), COL_GOUT // GLA_VAL)),
            pl.BlockSpec((GLA_L, TAIL_W), lambda b, i: (tok(b, i), 0)),
            pl.BlockSpec((TAIL_W, GLA_KEY), lambda b, i: (0, 0)),
            pl.BlockSpec((1, GLA_KEY), lambda b, i: (0, 0)),
            pl.BlockSpec((1, GLA_DV), lambda b, i: (0, 0)),
        ],
        out_specs=pl.BlockSpec((GLA_L, GLA_VAL), lambda b, i: (tok(b, i), 0)),
        scratch_shapes=[pltpu.VMEM((GLA_HEADS, GLA_DV, GLA_DK), F32)],
        compiler_params=_params(("parallel", "arbitrary")),
        name="gla",
    )(proj, proj, proj, proj, tail, wg_pad, b_gk, g_gla)


MRG_TM = 512


def _merge_kernel(x_ref, mod_ref, a_ref, b_ref, ga_ref, gb_ref, wa_ref, wb_ref, wo_ref, o_ref,
                  *, base):
    ya = _dot(a_ref[...], wa_ref[...])
    yb = _dot(b_ref[...], wb_ref[...])
    merged = (jax.nn.sigmoid(ga_ref[...].astype(F32)) * ya
              + jax.nn.sigmoid(gb_ref[...].astype(F32)) * yb).astype(BF16)
    gate = mod_ref[0, base + 2:base + 3, :]
    o_ref[...] = x_ref[...] + gate * _dot(merged, wo_ref[...])


def _merge(x2d, mod3, attn_o, gla_o, proj, wa, wb, wo, *, base, seq):
    t, d = x2d.shape
    tiles_per_batch = seq // MRG_TM
    row = lambda i: (i, 0)
    resident = lambda a: pl.BlockSpec(a.shape, lambda i: (0, 0), pipeline_mode=pl.Buffered(1))
    return pl.pallas_call(
        functools.partial(_merge_kernel, base=base),
        out_shape=jax.ShapeDtypeStruct((t, d), F32),
        grid=(t // MRG_TM,),
        in_specs=[
            pl.BlockSpec((MRG_TM, d), row),
            pl.BlockSpec((1, N_MOD, d), lambda i: (i // tiles_per_batch, 0, 0)),
            pl.BlockSpec((MRG_TM, attn_o.shape[1]), row),
            pl.BlockSpec((MRG_TM, gla_o.shape[1]), row),
            pl.BlockSpec((MRG_TM, d), lambda i: (i, COL_GATE_A // d)),
            pl.BlockSpec((MRG_TM, d), lambda i: (i, COL_GATE_A // d + 1)),
            resident(wa), resident(wb), resident(wo),
        ],
        out_specs=pl.BlockSpec((MRG_TM, d), row),
        compiler_params=_params(("parallel",)),
        name="merge",
    )(x2d, mod3, attn_o, gla_o, proj, proj, wa, wb, wo)


def _layer(x2d, mod3, pos3d, p, *, batch, seq):
    d = x2d.shape[1]
    assert IN_SPLITS[-1] == d and sum(IN_SPLITS) == p["w_in"].shape[1]
    w_main, w_tail = _relayout_w_in(p["w_in"].T)
    wuq = p["w_uq"].reshape(MLA_Q_RANK, MLA_HEADS, MLA_QK)
    wuq = jnp.concatenate([wuq[:, :, :MLA_NOPE].reshape(MLA_Q_RANK, -1),
                           wuq[:, :, MLA_NOPE:].reshape(MLA_Q_RANK, -1)], axis=1).astype(BF16)
    wukv = p["w_ukv"].reshape(MLA_KV_RANK, MLA_HEADS, MLA_NOPE + MLA_V)
    wukv = jnp.concatenate([wukv[:, :, :MLA_NOPE].reshape(MLA_KV_RANK, -1),
                            wukv[:, :, MLA_NOPE:].reshape(MLA_KV_RANK, -1)], axis=1).astype(BF16)
    wg_pad = jnp.zeros((TAIL_W, GLA_KEY), F32).at[TAIL_GLR:TAIL_GLR + GLA_GATE_RANK].set(p["w_gk_up"])

    inv_freq = ROPE_THETA ** (-jnp.arange(0, MLA_ROPE, 2, dtype=F32) / MLA_ROPE)
    invf = inv_freq.reshape(ROPE_HALF, 1)
    sgn = jnp.tile(jnp.concatenate([-jnp.ones(ROPE_HALF, F32), jnp.ones(ROPE_HALF, F32)]),
                   LANES // MLA_ROPE).reshape(1, LANES)
    n_rope = MLA_HEADS * MLA_ROPE
    lane = jnp.arange(n_rope)
    grp = (lane[:, None] // MLA_ROPE == lane[None, :] // MLA_ROPE).astype(BF16)

    row = lambda a: a.reshape(1, -1)
    q_scale = MLA_QK ** -0.5 * LOG2E
    x2d = _ffn(x2d, mod3, row(p["g_ffn1"]), p["w1_a"][None], p["w3_a"][None],
               p["w2_a"][None], row(p["g_final"]), base=0, final_norm=False, seq=seq)
    proj, tail = _inproj(x2d, mod3, row(p["g_mix"]), w_main, w_tail, base=3, seq=seq)
    q, k, v = _mla_prep(
        proj, tail, pos3d, (invf, sgn, grp),
        (row(p["g_q_lat"]), wuq, row(p["g_qn"] * q_scale),
         row(jnp.tile(p["g_qr"], MLA_HEADS) * q_scale),
         row(p["g_kv_lat"]), wukv, row(p["g_kn"]), row(p["g_kr"])),
        batch=batch, seq=seq)
    attn_o = _mla_attn(q, k, v).reshape(batch * seq, MLA_HEADS * MLA_V)
    gla_o = _gla(proj, tail, wg_pad, row(p["b_gk"]), row(p["g_gla"]), batch=batch, seq=seq)
    x2d = _merge(x2d, mod3, attn_o, gla_o, proj, p["w_proj_a"].astype(BF16),
                 p["w_proj_b"].astype(BF16), p["w_out"].astype(BF16), base=3, seq=seq)
    x2d = _ffn(x2d, mod3, row(p["g_ffn2"]), p["w1_b"][None], p["w3_b"][None],
               p["w2_b"][None], row(p["g_final"]), base=6, final_norm=True, seq=seq)
    return x2d


def kernel(x, c, positions, w_ada, b_ada, g_ffn1, w1_a, w3_a, w2_a, g_mix, w_in, g_q_lat, w_uq, g_qn, g_qr, g_kv_lat, w_ukv, g_kn, g_kr, w_gk_up, b_gk, g_gla, w_proj_a, w_proj_b, w_out, g_ffn2, w1_b, w3_b, w2_b, g_final):
    batch, seq, d = x.shape
    depth = w_ada.shape[0]
    assert depth == 1, "the final norm is fused into the last FFN of a single layer"
    names = ("g_ffn1", "w1_a", "w3_a", "w2_a", "g_mix", "w_in", "g_q_lat", "w_uq", "g_qn", "g_qr",
             "g_kv_lat", "w_ukv", "g_kn", "g_kr", "w_gk_up", "b_gk", "g_gla", "w_proj_a", "w_proj_b",
             "w_out", "g_ffn2", "w1_b", "w3_b", "w2_b", "g_final")
    stacked = (g_ffn1, w1_a, w3_a, w2_a, g_mix, w_in, g_q_lat, w_uq, g_qn, g_qr, g_kv_lat, w_ukv,
               g_kn, g_kr, w_gk_up, b_gk, g_gla, w_proj_a, w_proj_b, w_out, g_ffn2, w1_b, w3_b, w2_b,
               g_final)
    x2d = x.reshape(batch * seq, d)
    pos3d = positions.reshape(batch * seq // PREP_TM, 1, PREP_TM)
    p = {n: a[0] for n, a in zip(names, stacked)}
    mod3 = _adaln(c, w_ada[0], b_ada[0]).reshape(batch, N_MOD, d)
    x2d = _layer(x2d, mod3, pos3d, p, batch=batch, seq=seq)
    return x2d.reshape(batch, seq, d)
```

```python
import functools

import jax
import jax.numpy as jnp
from jax import lax
from jax.experimental import pallas as pl
from jax.experimental.pallas import tpu as pltpu

F32 = jnp.float32
BF16 = jnp.bfloat16

EPS = 1e-6
LOG2E = 1.4426950408889634
CHUNK = 64
MLA_HEADS = 8
MLA_Q_RANK = 512
MLA_KV_RANK = 512
MLA_NOPE = 128
MLA_ROPE = 64
MLA_V = 128
MLA_QK = MLA_NOPE + MLA_ROPE
ROPE_THETA = 10000.0
GLA_HEADS = 4
GLA_DK = 128
GLA_DV = 256
GLA_GATE_RANK = 16
GLA_GATE_NORMALIZER = 16.0
GLA_KEY = GLA_HEADS * GLA_DK
GLA_VAL = GLA_HEADS * GLA_DV
N_MOD = 9

LANES = 128
SUBLANES = 8
VMEM_LIMIT = 56 * 1024 * 1024

COL_QLAT = 0
COL_KVLAT = COL_QLAT + MLA_Q_RANK
COL_GQ = COL_KVLAT + MLA_KV_RANK
COL_GK = COL_GQ + GLA_KEY
COL_GV = COL_GK + GLA_KEY
COL_GOUT = COL_GV + GLA_VAL
COL_GATE_A = COL_GOUT + GLA_VAL
TAIL_W = LANES
TAIL_KROPE = 0
TAIL_GLR = MLA_ROPE


def _params(sem):
    return pltpu.CompilerParams(dimension_semantics=sem, vmem_limit_bytes=VMEM_LIMIT)


def _dot(a, b):
    return jnp.dot(a, b, preferred_element_type=F32)


def _dot_t0(a, b):
    return lax.dot_general(a, b, (((0,), (0,)), ((), ())), preferred_element_type=F32)


def _dot_t1(a, b):
    return lax.dot_general(a, b, (((1,), (1,)), ((), ())), preferred_element_type=F32)


def _split_dot(x, w_bf16):
    hi = x.astype(BF16)
    lo = (x - hi.astype(F32)).astype(BF16)
    return _dot(hi, w_bf16) + _dot(lo, w_bf16)


def _split_dot_t0(x, w_bf16):
    hi = x.astype(BF16)
    lo = (x - hi.astype(F32)).astype(BF16)
    return _dot_t0(hi, w_bf16) + _dot_t0(lo, w_bf16)


def _silu(x):
    return x * jax.nn.sigmoid(x)


def _rms(x, g):
    ms = jnp.mean(x * x, axis=-1, keepdims=True)
    return x * lax.rsqrt(ms + EPS) * g


ADA_TN = 2048
ADA_RC = 64


def _adaln_kernel(ct_ref, w_ref, b_ref, o_ref, s_ref):
    d, nb = ct_ref.shape
    tn = w_ref.shape[1]

    @pl.when(pl.program_id(0) == 0)
    def _():
        s = _silu(ct_ref[...])
        for b in range(nb):
            s_ref[b] = jnp.broadcast_to(s[:, b:b + 1], (d, LANES))

    def body(k, acc):
        rows = pl.ds(pl.multiple_of(k * ADA_RC, ADA_RC), ADA_RC)
        out = []
        for b in range(nb):
            sb = s_ref[b, rows, :]
            cols = []
            for col in range(tn // LANES):
                p = w_ref[rows, col * LANES:(col + 1) * LANES] * sb
                cols.append(p.reshape(ADA_RC // SUBLANES, SUBLANES, LANES).sum(axis=0))
            out.append(acc[b] + jnp.concatenate(cols, axis=-1))
        return tuple(out)

    init = tuple(jnp.zeros((SUBLANES, tn), F32) for _ in range(nb))
    acc = lax.fori_loop(0, d // ADA_RC, body, init, unroll=2)
    rows = [jnp.sum(a, axis=0, keepdims=True) for a in acc]
    o_ref[...] = jnp.concatenate(rows, axis=0) + b_ref[...]


def _adaln(c, w_ada, b_ada):
    nb, d = c.shape
    n = w_ada.shape[1]
    return pl.pallas_call(
        _adaln_kernel,
        out_shape=jax.ShapeDtypeStruct((nb, n), F32),
        grid=(n // ADA_TN,),
        in_specs=[
            pl.BlockSpec((d, nb), lambda j: (0, 0)),
            pl.BlockSpec((d, ADA_TN), lambda j: (0, j)),
            pl.BlockSpec((1, ADA_TN), lambda j: (0, j)),
        ],
        out_specs=pl.BlockSpec((nb, ADA_TN), lambda j: (0, j)),
        scratch_shapes=[pltpu.VMEM((nb, d, LANES), F32)],
        compiler_params=_params(("arbitrary",)),
        name="adaln",
    )(c.T, w_ada, b_ada.reshape(1, n))


def _mod_norm(x, g, mod_ref, base):
    shift = mod_ref[0, base:base + 1, :]
    scale = mod_ref[0, base + 1:base + 2, :]
    return _rms(x, g) * (1.0 + scale) + shift


FFN_TM = 512
FFN_TF = 512
FFN_HEAD_TILES = 2
FFN_HEAD_TF = 256
FFN_GROUP = 256
FFN_WSLOTS = 3
FFN_AHEAD_CHUNKS = 8
FFN_AHEAD_ROWS = FFN_TM // FFN_AHEAD_CHUNKS


def _swiglu_tile(h, w1g, w3g, w2t):
    upd = None
    for s in range(w1g.shape[0]):
        a = _dot(h, w1g[s])
        b = _dot(h, w3g[s])
        p = (_silu(a) * b).astype(BF16)
        part = _dot(p, w2t[s * FFN_GROUP:(s + 1) * FFN_GROUP, :])
        upd = part if upd is None else upd + part
    return upd


def _ffn_finish(x_ref, mod_ref, gf_ref, o_ref, *, base, final_norm):
    gate = mod_ref[0, base + 2:base + 3, :]
    y = x_ref[...] + 0.5 * gate * o_ref[...]
    if final_norm:
        y = _rms(y, gf_ref[...])
    o_ref[...] = y


def _ffn_head_kernel(x_ref, mod_ref, g_ref, w1_ref, w3_ref, w2_ref, gf_ref,
                     o_ref, w1b_ref, w3b_ref, w2b_ref, h_ref, *, base, final_norm):
    j = pl.program_id(1)

    @pl.when(j == 0)
    def _():
        h_ref[...] = _mod_norm(x_ref[...], g_ref[...], mod_ref, base).astype(BF16)
        o_ref[...] = jnp.zeros_like(o_ref)

    w1 = w1_ref[0].astype(BF16)
    w3 = w3_ref[0].astype(BF16)
    w2 = w2_ref[0].astype(BF16)
    w1b_ref[0] = w1
    w3b_ref[0] = w3
    w2b_ref[...] = w2
    o_ref[...] += _swiglu_tile(h_ref[...], w1[None], w3[None], w2)

    @pl.when(j == pl.num_programs(1) - 1)
    def _():
        _ffn_finish(x_ref, mod_ref, gf_ref, o_ref, base=base, final_norm=final_norm)


def _ffn_kernel(x_ref, xn_ref, mod_ref, modn_ref, g_ref, w1_hbm, w3_hbm, w2_hbm, gf_ref, head_ref,
                o_ref, h_ref, w1_buf, w3_buf, w2_buf, sem, *, base, final_norm, n_steps):
    del head_ref
    i = pl.program_id(0)
    h_slot = i % 2
    first = i * n_steps
    groups = FFN_TF // FFN_GROUP
    ahead = FFN_WSLOTS - 1

    def copies(j, slot):
        jj = j % n_steps
        return (pltpu.make_async_copy(w1_hbm.at[pl.ds(jj * groups, groups)], w1_buf.at[slot], sem.at[slot, 0]),
                pltpu.make_async_copy(w3_hbm.at[pl.ds(jj * groups, groups)], w3_buf.at[slot], sem.at[slot, 1]),
                pltpu.make_async_copy(w2_hbm.at[pl.ds(jj * FFN_TF, FFN_TF)], w2_buf.at[slot], sem.at[slot, 2]))

    @pl.when(i == 0)
    def _():
        for j in range(ahead):
            for c in copies(j, j):
                c.start()
        h_ref[h_slot] = _mod_norm(x_ref[...], g_ref[...], mod_ref, base).astype(BF16)

    for j in range(n_steps):
        slot = (first + j) % FFN_WSLOTS
        for c in copies(j, slot):
            c.wait()
        if j < FFN_AHEAD_CHUNKS:
            rows = slice(j * FFN_AHEAD_ROWS, (j + 1) * FFN_AHEAD_ROWS)
            h_next = _mod_norm(xn_ref[rows, :], g_ref[...], modn_ref, base).astype(BF16)
        upd = _swiglu_tile(h_ref[h_slot], w1_buf.at[slot], w3_buf.at[slot], w2_buf[slot])
        for c in copies(j + ahead, (first + j + ahead) % FFN_WSLOTS):
            c.start()
        if j == 0:
            o_ref[...] = upd
        else:
            o_ref[...] += upd
        if j < FFN_AHEAD_CHUNKS:
            h_ref[1 - h_slot, rows, :] = h_next
    _ffn_finish(x_ref, mod_ref, gf_ref, o_ref, base=base, final_norm=final_norm)

    @pl.when(i == pl.num_programs(0) - 1)
    def _():
        for j in range(n_steps, n_steps + ahead):
            for c in copies(j, (first + j) % FFN_WSLOTS):
                c.wait()


def _ffn(x2d, mod3, g, w1, w3, w2, g_final, *, base, final_norm, seq):
    t, d = x2d.shape
    dff = w1.shape[2]
    tiles_per_batch = seq // FFN_TM
    const = lambda i, j: (0, 0)
    kw = dict(base=base, final_norm=final_norm)
    tag = "_final" if final_norm else ""
    common = [
        pl.BlockSpec((1, N_MOD, d), lambda i, j: (i // tiles_per_batch, 0, 0)),
        pl.BlockSpec((1, d), const),
    ]
    hf = FFN_HEAD_TF
    head_rows = FFN_HEAD_TILES * FFN_TM
    assert head_rows <= seq
    assert hf == FFN_GROUP
    w13_shape = jax.ShapeDtypeStruct((dff // FFN_GROUP, d, FFN_GROUP), BF16)
    w13_spec = pl.BlockSpec((1, d, FFN_GROUP), lambda i, j: (j, 0, 0))
    y_head, w1b, w3b, w2b = pl.pallas_call(
        functools.partial(_ffn_head_kernel, **kw),
        out_shape=(jax.ShapeDtypeStruct((t, d), F32), w13_shape, w13_shape,
                   jax.ShapeDtypeStruct((dff, d), BF16)),
        grid=(1, dff // hf),
        in_specs=[pl.BlockSpec((head_rows, d), const, pipeline_mode=pl.Buffered(1))] + common + [
            pl.BlockSpec((1, d, hf), lambda i, j: (0, 0, j)),
            pl.BlockSpec((1, d, hf), lambda i, j: (0, 0, j)),
            pl.BlockSpec((1, hf, d), lambda i, j: (0, j, 0)),
            pl.BlockSpec((1, d), const),
        ],
        out_specs=(pl.BlockSpec((head_rows, d), const), w13_spec, w13_spec,
                   pl.BlockSpec((hf, d), lambda i, j: (j, 0))),
        scratch_shapes=[pltpu.VMEM((head_rows, d), BF16)],
        compiler_params=_params(("arbitrary", "arbitrary")),
        name="ffn_head" + tag,
    )(x2d, mod3, g, w1, w3, w2, g_final)

    n_tiles = t // FFN_TM
    assert dff // FFN_TF >= FFN_AHEAD_CHUNKS and n_tiles > FFN_HEAD_TILES
    tile = lambda i: i + FFN_HEAD_TILES
    nxt = lambda i: jnp.minimum(tile(i) + 1, n_tiles - 1)
    head_operand = 9
    groups = FFN_TF // FFN_GROUP
    hbm = pl.BlockSpec(memory_space=pl.ANY)
    return pl.pallas_call(
        functools.partial(_ffn_kernel, n_steps=dff // FFN_TF, **kw),
        out_shape=jax.ShapeDtypeStruct((t, d), F32),
        grid=(n_tiles - FFN_HEAD_TILES,),
        in_specs=[
            pl.BlockSpec((FFN_TM, d), lambda i: (tile(i), 0)),
            pl.BlockSpec((FFN_TM, d), lambda i: (nxt(i), 0)),
            pl.BlockSpec((1, N_MOD, d), lambda i: (tile(i) // tiles_per_batch, 0, 0)),
            pl.BlockSpec((1, N_MOD, d), lambda i: (nxt(i) // tiles_per_batch, 0, 0)),
            pl.BlockSpec((1, d), lambda i: (0, 0)),
            hbm, hbm, hbm,
            pl.BlockSpec((1, d), lambda i: (0, 0)),
            hbm,
        ],
        out_specs=pl.BlockSpec((FFN_TM, d), lambda i: (tile(i), 0)),
        scratch_shapes=[pltpu.VMEM((2, FFN_TM, d), BF16),
                        pltpu.VMEM((FFN_WSLOTS, groups, d, FFN_GROUP), BF16),
                        pltpu.VMEM((FFN_WSLOTS, groups, d, FFN_GROUP), BF16),
                        pltpu.VMEM((FFN_WSLOTS, FFN_TF, d), BF16),
                        pltpu.SemaphoreType.DMA((FFN_WSLOTS, 3))],
        input_output_aliases={head_operand: 0},
        compiler_params=_params(("arbitrary",)),
        name="ffn" + tag,
    )(x2d, x2d, mod3, mod3, g, w1b, w3b, w2b, g_final, y_head)


IN_SPLITS = (MLA_Q_RANK, MLA_KV_RANK, MLA_ROPE, GLA_KEY, GLA_KEY, GLA_VAL, GLA_GATE_RANK, GLA_VAL,
             2048, 2048)
MAIN_ORDER = (0, 1, 3, 4, 5, 7, 8, 9)
TAIL_ORDER = (2, 6)
RELAYOUT_ROWS = 1024


def _segments(order):
    starts = [sum(IN_SPLITS[:i]) for i in range(len(IN_SPLITS))]
    segs, dst = [], 0
    for i in order:
        src, n = starts[i], IN_SPLITS[i]
        if segs and segs[-1][0] + segs[-1][2] == src:
            segs[-1] = (segs[-1][0], segs[-1][1], segs[-1][2] + n)
        else:
            segs.append((src, dst, n))
        dst += n
    return segs, dst


def _relayout_kernel(lo_ref, hi_ref, kr_ref, glr_ref, main_ref, tail_ref):
    r = pl.program_id(0)
    rows = RELAYOUT_ROWS
    for src, dst, n in _segments(MAIN_ORDER)[0]:
        shift = src - dst

        @pl.when((r >= dst // rows) & (r < (dst + n) // rows))
        def _(shift=shift):
            main_ref[:rows - shift, :] = lo_ref[shift:, :].astype(BF16)
            if shift:
                main_ref[rows - shift:, :] = hi_ref[:shift, :].astype(BF16)

    tail_ref[:MLA_ROPE, :] = kr_ref[...].astype(BF16)
    tail_ref[MLA_ROPE:MLA_ROPE + GLA_GATE_RANK, :] = glr_ref[...].astype(BF16)
    pad = TAIL_W - MLA_ROPE - GLA_GATE_RANK
    tail_ref[MLA_ROPE + GLA_GATE_RANK:, :] = jnp.zeros((pad, tail_ref.shape[1]), BF16)


def _relayout_w_in(w_in_t):
    d_in, d = w_in_t.shape
    segs, n_main = _segments(MAIN_ORDER)
    rows = RELAYOUT_ROWS
    assert all(dst % rows == 0 and n % rows == 0 and 0 <= src - dst <= LANES and (src - dst) % 16 == 0
               for src, dst, n in segs)
    (kr_src, _, _), (glr_src, _, _) = _segments(TAIL_ORDER)[0]
    assert kr_src % MLA_ROPE == 0 and glr_src % GLA_GATE_RANK == 0
    return pl.pallas_call(
        _relayout_kernel,
        out_shape=(jax.ShapeDtypeStruct((n_main, d), BF16), jax.ShapeDtypeStruct((TAIL_W, d), BF16)),
        grid=(n_main // rows,),
        in_specs=[pl.BlockSpec((rows, d), lambda r: (r, 0)),
                  pl.BlockSpec((LANES, d), lambda r: ((r + 1) * (rows // LANES), 0)),
                  pl.BlockSpec((MLA_ROPE, d), lambda r: (kr_src // MLA_ROPE, 0)),
                  pl.BlockSpec((GLA_GATE_RANK, d), lambda r: (glr_src // GLA_GATE_RANK, 0))],
        out_specs=(pl.BlockSpec((rows, d), lambda r: (r, 0)),
                   pl.BlockSpec((TAIL_W, d), lambda r: (0, 0))),
        compiler_params=_params(("arbitrary",)),
        name="w_in_layout",
    )(w_in_t, w_in_t, w_in_t, w_in_t)


INP_TM = 1024
INP_TN = 2048


def _inproj_kernel(x_ref, mod_ref, g_ref, w_ref, wt_ref, o_ref, tail_ref, h_ref, *, base):
    @pl.when(pl.program_id(1) == 0)
    def _():
        h = _mod_norm(x_ref[...], g_ref[...], mod_ref, base).astype(BF16)
        h_ref[...] = h
        tail_ref[...] = _dot_t1(h, wt_ref[...])

    o_ref[...] = _dot_t1(h_ref[...], w_ref[...]).astype(BF16)


def _inproj(x2d, mod3, g, w_main, w_tail, *, base, seq):
    t, d = x2d.shape
    n = w_main.shape[0]
    tiles_per_batch = seq // INP_TM
    return pl.pallas_call(
        functools.partial(_inproj_kernel, base=base),
        out_shape=(jax.ShapeDtypeStruct((t, n), BF16),
                   jax.ShapeDtypeStruct((t, TAIL_W), F32)),
        grid=(t // INP_TM, n // INP_TN),
        in_specs=[
            pl.BlockSpec((INP_TM, d), lambda i, j: (i, 0)),
            pl.BlockSpec((1, N_MOD, d), lambda i, j: (i // tiles_per_batch, 0, 0)),
            pl.BlockSpec((1, d), lambda i, j: (0, 0)),
            pl.BlockSpec((INP_TN, d), lambda i, j: (j, 0)),
            pl.BlockSpec((TAIL_W, d), lambda i, j: (0, 0)),
        ],
        out_specs=(pl.BlockSpec((INP_TM, INP_TN), lambda i, j: (i, j)),
                   pl.BlockSpec((INP_TM, TAIL_W), lambda i, j: (i, 0))),
        scratch_shapes=[pltpu.VMEM((INP_TM, d), BF16)],
        compiler_params=_params(("parallel", "arbitrary")),
        name="in_proj",
    )(x2d, mod3, g, w_main, w_tail)


PREP_TM = 512
ROPE_HALF = MLA_ROPE // 2


def _rope_rotate(x, cos, sin_signed):
    width = x.shape[-1]
    lane = lax.broadcasted_iota(jnp.int32, x.shape, 1)
    first_half = (lane % MLA_ROPE) < ROPE_HALF
    partner = jnp.where(first_half,
                        pltpu.roll(x, width - ROPE_HALF, 1),
                        pltpu.roll(x, ROPE_HALF, 1))
    return x * cos + partner * sin_signed


def _prep_kernel(qlat_ref, kvlat_ref, tail_ref, pos_ref, invf_ref, sgn_ref,
                 gql_ref, wuq_ref, gqn_ref, gqr_ref, gkvl_ref, wukv_ref, gkn_ref, gkr_ref,
                 grp_ref, q_ref, k_ref, v_ref):
    ang_t = invf_ref[...] * pos_ref[0].astype(F32)
    lane_reps = LANES // ROPE_HALF
    cos = jnp.tile(jnp.cos(ang_t).T, (1, lane_reps))
    sin_signed = jnp.tile(jnp.sin(ang_t).T, (1, lane_reps)) * sgn_ref[...]
    n_rope = MLA_HEADS * MLA_ROPE
    cos_q = jnp.concatenate([cos] * (n_rope // LANES), axis=-1)
    sin_q = jnp.concatenate([sin_signed] * (n_rope // LANES), axis=-1)

    cq = _rms(qlat_ref[...].astype(F32), gql_ref[...]).astype(BF16)
    ckv = _rms(kvlat_ref[...].astype(F32), gkvl_ref[...]).astype(BF16)
    n_nope = MLA_HEADS * MLA_NOPE
    pair = 2 * MLA_NOPE

    for hp in range(MLA_HEADS // 2):
        cols = slice(hp * pair, (hp + 1) * pair)
        qp = _dot(cq, wuq_ref[:, cols])
        kp = _dot(ckv, wukv_ref[:, cols])
        for hh in range(2):
            h = 2 * hp + hh
            sub = slice(hh * MLA_NOPE, (hh + 1) * MLA_NOPE)
            q_ref[0, h, :, :MLA_NOPE] = _rms(qp[:, sub], gqn_ref[...]).astype(BF16)
            k_ref[0, h, :, :MLA_NOPE] = _rms(kp[:, sub], gkn_ref[...]).astype(BF16)
        v_ref[0, :, cols] = _dot(ckv, wukv_ref[:, n_nope + hp * pair:n_nope + (hp + 1) * pair]).astype(BF16)

    qr = _dot(cq, wuq_ref[:, n_nope:])
    ssq = _split_dot(qr * qr, grp_ref[...])
    qr = qr * lax.rsqrt(ssq * (1.0 / MLA_ROPE) + EPS) * gqr_ref[...]
    qr = _rope_rotate(qr, cos_q, sin_q).astype(BF16)
    kr = tail_ref[:, TAIL_KROPE:TAIL_KROPE + MLA_ROPE]
    kr = _rms(kr, gkr_ref[...])
    kr = _rope_rotate(kr, cos[:, :MLA_ROPE], sin_signed[:, :MLA_ROPE]).astype(BF16)
    for h in range(MLA_HEADS):
        q_ref[0, h, :, MLA_NOPE:] = qr[:, h * MLA_ROPE:(h + 1) * MLA_ROPE]
        k_ref[0, h, :, MLA_NOPE:] = kr


def _mla_prep(proj, tail, pos3d, consts, weights, *, batch, seq):
    t = proj.shape[0]
    tiles_per_batch = seq // PREP_TM
    row = lambda i: (i, 0)
    const = lambda i: (0, 0)
    (invf, sgn, grp) = consts
    (gql, wuq, gqn, gqr, gkvl, wukv, gkn, gkr) = weights
    head_out = pl.BlockSpec((1, MLA_HEADS, PREP_TM, MLA_QK),
                            lambda i: (i // tiles_per_batch, 0, i % tiles_per_batch, 0))
    full = lambda a: pl.BlockSpec(a.shape, const)
    return pl.pallas_call(
        _prep_kernel,
        out_shape=(jax.ShapeDtypeStruct((batch, MLA_HEADS, seq, MLA_QK), BF16),
                   jax.ShapeDtypeStruct((batch, MLA_HEADS, seq, MLA_QK), BF16),
                   jax.ShapeDtypeStruct((batch, seq, MLA_HEADS * MLA_V), BF16)),
        grid=(t // PREP_TM,),
        in_specs=[
            pl.BlockSpec((PREP_TM, MLA_Q_RANK), lambda i: (i, COL_QLAT // MLA_Q_RANK)),
            pl.BlockSpec((PREP_TM, MLA_KV_RANK), lambda i: (i, COL_KVLAT // MLA_KV_RANK)),
            pl.BlockSpec((PREP_TM, TAIL_W), row),
            pl.BlockSpec((1, 1, PREP_TM), lambda i: (i, 0, 0)),
            full(invf), full(sgn),
            full(gql), full(wuq), full(gqn), full(gqr),
            full(gkvl), full(wukv), full(gkn), full(gkr),
            full(grp),
        ],
        out_specs=(head_out, head_out,
                   pl.BlockSpec((1, PREP_TM, MLA_HEADS * MLA_V),
                                lambda i: (i // tiles_per_batch, i % tiles_per_batch, 0))),
        compiler_params=_params(("parallel",)),
        name="mla_prep",
    )(proj, proj, tail, pos3d, invf, sgn, gql, wuq, gqn, gqr, gkvl, wukv, gkn, gkr, grp)


ATT_T = 1024
ATT_H = ATT_T // 2


def _attn_kernel(q_ref, k_ref, v_ref, o_ref):
    n_tiles = k_ref.shape[2] // ATT_T
    ones = jnp.ones((ATT_T, MLA_V), BF16)
    r = lax.broadcasted_iota(jnp.int32, (ATT_H, ATT_H), 0)
    c = lax.broadcasted_iota(jnp.int32, (ATT_H, ATT_H), 1)
    visible = (c // CHUNK) <= (r // CHUNK)
    state = {}

    def update(chain, s, v2):
        m_cur = jnp.max(s, axis=-1, keepdims=True)
        if chain not in state:
            m_next = jnp.broadcast_to(m_cur, (ATT_H, LANES))
        else:
            m_prev, l_prev, acc_prev = state[chain]
            m_next = jnp.maximum(m_prev, m_cur)
            alpha = jnp.exp2(m_prev - m_next)
        p = jnp.exp2(s - jnp.tile(m_next, (1, s.shape[1] // LANES)))
        pv = _dot(p.astype(BF16), v2)
        acc, l = pv[:, :MLA_V], pv[:, MLA_V:]
        if chain in state:
            acc, l = alpha * acc_prev + acc, alpha * l_prev + l
        state[chain] = (m_next, l, acc)

    for t in range(n_tiles):
        rows = slice(t * ATT_T, (t + 1) * ATT_T)
        k = k_ref[0, 0, rows, :]
        v2 = jnp.concatenate([v_ref[0, rows, :], ones], axis=-1)
        for qt in range(t, n_tiles):
            q0 = q_ref[0, 0, qt * ATT_T:qt * ATT_T + ATT_H, :]
            q1 = q_ref[0, 0, qt * ATT_T + ATT_H:(qt + 1) * ATT_T, :]
            if t < qt:
                update((qt, 0), _dot_t1(q0, k), v2)
                update((qt, 1), _dot_t1(q1, k), v2)
            else:
                update((qt, 0), jnp.where(visible, _dot_t1(q0, k[:ATT_H]), -jnp.inf), v2[:ATT_H])
                s1 = _dot_t1(q1, k)
                s1 = jnp.concatenate([s1[:, :ATT_H], jnp.where(visible, s1[:, ATT_H:], -jnp.inf)],
                                     axis=-1)
                update((qt, 1), s1, v2)
                for half in range(2):
                    _, l, acc = state[(qt, half)]
                    lo = qt * ATT_T + half * ATT_H
                    o_ref[0, lo:lo + ATT_H, :] = (acc / l).astype(BF16)


def _mla_attn(q, k, v):
    batch, heads, seq, _ = q.shape
    return pl.pallas_call(
        _attn_kernel,
        out_shape=jax.ShapeDtypeStruct((batch, seq, heads * MLA_V), BF16),
        grid=(batch, heads),
        in_specs=[
            pl.BlockSpec((1, 1, seq, MLA_QK), lambda b, h: (b, h, 0, 0)),
            pl.BlockSpec((1, 1, seq, MLA_QK), lambda b, h: (b, h, 0, 0)),
            pl.BlockSpec((1, seq, MLA_V), lambda b, h: (b, 0, h)),
        ],
        out_specs=pl.BlockSpec((1, seq, MLA_V), lambda b, h: (b, 0, h)),
        compiler_params=_params(("parallel", "parallel")),
        name="mla_attn",
    )(q, k, v)


GLA_L = 512
GLA_NC = GLA_L // CHUNK
GLA_SUB = 512


def _gla_head(log_a, q, k, v, go, gg, state_ref, causal, tri):
    sub_nc = GLA_SUB // CHUNK
    q_dec, k_end, o_intra, b_ends = [], [], [], []
    for sb in range(GLA_L // GLA_SUB):
        rows = slice(sb * GLA_SUB, (sb + 1) * GLA_SUB)
        la = log_a[rows]
        la_hi = la.astype(BF16)
        la_lo = (la - la_hi.astype(F32)).astype(BF16)
        cum = _dot(tri, jnp.concatenate([la_hi, la_lo], axis=-1))
        b_cum = cum[:, :GLA_DK] + cum[:, GLA_DK:]
        b3 = b_cum.reshape(sub_nc, CHUNK, GLA_DK)
        b_end = b3[:, CHUNK - 1:CHUNK, :]
        b_tot = jnp.broadcast_to(b_end, b3.shape).reshape(GLA_SUB, GLA_DK)
        b_ends.append(jnp.exp(b_end))

        qf = q[rows].astype(F32) * (GLA_DK ** -0.5)
        kf = k[rows].astype(F32)
        qd = (qf * jnp.exp(b_cum)).astype(BF16)
        kd = (kf * jnp.exp(-b_cum)).astype(BF16)
        q_dec.append(qd)
        k_end.append((kf * jnp.exp(b_tot - b_cum)).astype(BF16))
        attn = jnp.where(causal, _dot_t1(qd, kd), 0.0).astype(BF16)
        o_intra.append(_dot(attn, v[rows]))

    state = state_ref[...]
    outs = []
    for n in range(GLA_NC):
        sb, rows = n // sub_nc, slice((n % sub_nc) * CHUNK, (n % sub_nc + 1) * CHUNK)
        outs.append(o_intra[sb][rows] + _dot_t1(q_dec[sb][rows], state.astype(BF16)))
        v_n = v[n * CHUNK:(n + 1) * CHUNK]
        state = state * b_ends[sb][n % sub_nc] + _dot_t0(v_n, k_end[sb][rows])
    state_ref[...] = state

    o = jnp.concatenate(outs, axis=0)
    return (_rms(o, gg) * _silu(go.astype(F32))).astype(BF16)


def _gla_kernel(q_ref, k_ref, v_ref, go_ref, tail_ref, wg_ref, bg_ref, gg_ref,
                o_ref, state_ref):
    @pl.when(pl.program_id(1) == 0)
    def _():
        state_ref[...] = jnp.zeros_like(state_ref)

    wg = wg_ref[...]
    wg_hi = wg.astype(BF16)
    wg_lo = (wg - wg_hi.astype(F32)).astype(BF16)
    tl = tail_ref[...]
    tl_hi = tl.astype(BF16)
    tl_lo = (tl - tl_hi.astype(F32)).astype(BF16)
    z = _dot(tl_hi, wg_hi) + _dot(tl_lo, wg_hi) + _dot(tl_hi, wg_lo) + bg_ref[...]
    log_a = (jnp.minimum(z, 0.0) - jnp.log1p(jnp.exp(-jnp.abs(z)))) * (1.0 / GLA_GATE_NORMALIZER)

    r = lax.broadcasted_iota(jnp.int32, (GLA_SUB, GLA_SUB), 0)
    c = lax.broadcasted_iota(jnp.int32, (GLA_SUB, GLA_SUB), 1)
    causal = ((r // CHUNK) == (c // CHUNK)) & (c <= r)
    tri = causal.astype(BF16)

    for hh in range(GLA_HEADS):
        dk = slice(hh * GLA_DK, (hh + 1) * GLA_DK)
        dv = slice(hh * GLA_DV, (hh + 1) * GLA_DV)
        o_ref[:, dv] = _gla_head(log_a[:, dk], q_ref[:, dk], k_ref[:, dk], v_ref[:, dv],
                                 go_ref[:, dv], gg_ref[...], state_ref.at[hh], causal, tri)


def _gla(proj, tail, wg_pad, b_gk, g_gla, *, batch, seq):
    t = proj.shape[0]
    steps = seq // GLA_L
    tok = lambda b, i: b * steps + i
    return pl.pallas_call(
        _gla_kernel,
        out_shape=jax.ShapeDtypeStruct((t, GLA_VAL), BF16),
        grid=(batch, steps),
        in_specs=[
            pl.BlockSpec((GLA_L, GLA_KEY), lambda b, i: (tok(b, i), COL_GQ // GLA_KEY)),
            pl.BlockSpec((GLA_L, GLA_KEY), lambda b, i: (tok(b, i), COL_GK // GLA_KEY)),
            pl.BlockSpec((GLA_L, GLA_VAL), lambda b, i: (tok(b, i), COL_GV // GLA_VAL)),
            pl.BlockSpec((GLA_L, GLA_VAL), lambda b, i: (tok(b, i), COL_GOUT // GLA_VAL)),
            pl.BlockSpec((GLA_L, TAIL_W), lambda b, i: (tok(b, i), 0)),
            pl.BlockSpec((TAIL_W, GLA_KEY), lambda b, i: (0, 0)),
            pl.BlockSpec((1, GLA_KEY), lambda b, i: (0, 0)),
            pl.BlockSpec((1, GLA_DV), lambda b, i: (0, 0)),
        ],
        out_specs=pl.BlockSpec((GLA_L, GLA_VAL), lambda b, i: (tok(b, i), 0)),
        scratch_shapes=[pltpu.VMEM((GLA_HEADS, GLA_DV, GLA_DK), F32)],
        compiler_params=_params(("parallel", "arbitrary")),
        name="gla",
    )(proj, proj, proj, proj, tail, wg_pad, b_gk, g_gla)


MRG_TM = 512


def _merge_kernel(x_ref, mod_ref, a_ref, b_ref, ga_ref, gb_ref, wa_ref, wb_ref, wo_ref, o_ref,
                  *, base):
    ya = _dot(a_ref[...], wa_ref[...])
    yb = _dot(b_ref[...], wb_ref[...])
    merged = (jax.nn.sigmoid(ga_ref[...].astype(F32)) * ya
              + jax.nn.sigmoid(gb_ref[...].astype(F32)) * yb).astype(BF16)
    gate = mod_ref[0, base + 2:base + 3, :]
    o_ref[...] = x_ref[...] + gate * _dot(merged, wo_ref[...])


def _merge(x2d, mod3, attn_o, gla_o, proj, wa, wb, wo, *, base, seq):
    t, d = x2d.shape
    tiles_per_batch = seq // MRG_TM
    row = lambda i: (i, 0)
    resident = lambda a: pl.BlockSpec(a.shape, lambda i: (0, 0), pipeline_mode=pl.Buffered(1))
    return pl.pallas_call(
        functools.partial(_merge_kernel, base=base),
        out_shape=jax.ShapeDtypeStruct((t, d), F32),
        grid=(t // MRG_TM,),
        in_specs=[
            pl.BlockSpec((MRG_TM, d), row),
            pl.BlockSpec((1, N_MOD, d), lambda i: (i // tiles_per_batch, 0, 0)),
            pl.BlockSpec((MRG_TM, attn_o.shape[1]), row),
            pl.BlockSpec((MRG_TM, gla_o.shape[1]), row),
            pl.BlockSpec((MRG_TM, d), lambda i: (i, COL_GATE_A // d)),
            pl.BlockSpec((MRG_TM, d), lambda i: (i, COL_GATE_A // d + 1)),
            resident(wa), resident(wb), resident(wo),
        ],
        out_specs=pl.BlockSpec((MRG_TM, d), row),
        compiler_params=_params(("parallel",)),
        name="merge",
    )(x2d, mod3, attn_o, gla_o, proj, proj, wa, wb, wo)


def _layer(x2d, mod3, pos3d, p, *, batch, seq):
    d = x2d.shape[1]
    assert IN_SPLITS[-1] == d and sum(IN_SPLITS) == p["w_in"].shape[1]
    w_main, w_tail = _relayout_w_in(p["w_in"].T)
    wuq = p["w_uq"].reshape(MLA_Q_RANK, MLA_HEADS, MLA_QK)
    wuq = jnp.concatenate([wuq[:, :, :MLA_NOPE].reshape(MLA_Q_RANK, -1),
                           wuq[:, :, MLA_NOPE:].reshape(MLA_Q_RANK, -1)], axis=1).astype(BF16)
    wukv = p["w_ukv"].reshape(MLA_KV_RANK, MLA_HEADS, MLA_NOPE + MLA_V)
    wukv = jnp.concatenate([wukv[:, :, :MLA_NOPE].reshape(MLA_KV_RANK, -1),
                            wukv[:, :, MLA_NOPE:].reshape(MLA_KV_RANK, -1)], axis=1).astype(BF16)
    wg_pad = jnp.zeros((TAIL_W, GLA_KEY), F32).at[TAIL_GLR:TAIL_GLR + GLA_GATE_RANK].set(p["w_gk_up"])

    inv_freq = ROPE_THETA ** (-jnp.arange(0, MLA_ROPE, 2, dtype=F32) / MLA_ROPE)
    invf = inv_freq.reshape(ROPE_HALF, 1)
    sgn = jnp.tile(jnp.concatenate([-jnp.ones(ROPE_HALF, F32), jnp.ones(ROPE_HALF, F32)]),
                   LANES // MLA_ROPE).reshape(1, LANES)
    n_rope = MLA_HEADS * MLA_ROPE
    lane = jnp.arange(n_rope)
    grp = (lane[:, None] // MLA_ROPE == lane[None, :] // MLA_ROPE).astype(BF16)

    row = lambda a: a.reshape(1, -1)
    q_scale = MLA_QK ** -0.5 * LOG2E
    x2d = _ffn(x2d, mod3, row(p["g_ffn1"]), p["w1_a"][None], p["w3_a"][None],
               p["w2_a"][None], row(p["g_final"]), base=0, final_norm=False, seq=seq)
    proj, tail = _inproj(x2d, mod3, row(p["g_mix"]), w_main, w_tail, base=3, seq=seq)
    q, k, v = _mla_prep(
        proj, tail, pos3d, (invf, sgn, grp),
        (row(p["g_q_lat"]), wuq, row(p["g_qn"] * q_scale),
         row(jnp.tile(p["g_qr"], MLA_HEADS) * q_scale),
         row(p["g_kv_lat"]), wukv, row(p["g_kn"]), row(p["g_kr"])),
        batch=batch, seq=seq)
    attn_o = _mla_attn(q, k, v).reshape(batch * seq, MLA_HEADS * MLA_V)
    gla_o = _gla(proj, tail, wg_pad, row(p["b_gk"]), row(p["g_gla"]), batch=batch, seq=seq)
    x2d = _merge(x2d, mod3, attn_o, gla_o, proj, p["w_proj_a"].astype(BF16),
                 p["w_proj_b"].astype(BF16), p["w_out"].astype(BF16), base=3, seq=seq)
    x2d = _ffn(x2d, mod3, row(p["g_ffn2"]), p["w1_b"][None], p["w3_b"][None],
               p["w2_b"][None], row(p["g_final"]), base=6, final_norm=True, seq=seq)
    return x2d


def kernel(x, c, positions, w_ada, b_ada, g_ffn1, w1_a, w3_a, w2_a, g_mix, w_in, g_q_lat, w_uq, g_qn, g_qr, g_kv_lat, w_ukv, g_kn, g_kr, w_gk_up, b_gk, g_gla, w_proj_a, w_proj_b, w_out, g_ffn2, w1_b, w3_b, w2_b, g_final):
    batch, seq, d = x.shape
    depth = w_ada.shape[0]
    assert depth == 1, "the final norm is fused into the last FFN of a single layer"
    names = ("g_ffn1", "w1_a", "w3_a", "w2_a", "g_mix", "w_in", "g_q_lat", "w_uq", "g_qn", "g_qr",
             "g_kv_lat", "w_ukv", "g_kn", "g_kr", "w_gk_up", "b_gk", "g_gla", "w_proj_a", "w_proj_b",
             "w_out", "g_ffn2", "w1_b", "w3_b", "w2_b", "g_final")
    stacked = (g_ffn1, w1_a, w3_a, w2_a, g_mix, w_in, g_q_lat, w_uq, g_qn, g_qr, g_kv_lat, w_ukv,
               g_kn, g_kr, w_gk_up, b_gk, g_gla, w_proj_a, w_proj_b, w_out, g_ffn2, w1_b, w3_b, w2_b,
               g_final)
    x2d = x.reshape(batch * seq, d)
    pos3d = positions.reshape(batch * seq // PREP_TM, 1, PREP_TM)
    p = {n: a[0] for n, a in zip(names, stacked)}
    mod3 = _adaln(c, w_ada[0], b_ada[0]).reshape(batch, N_MOD, d)
    x2d = _layer(x2d, mod3, pos3d, p, batch=batch, seq=seq)
    return x2d.reshape(batch, seq, d)
```

```python
import functools

import jax
import jax.numpy as jnp
from jax import lax
from jax.experimental import pallas as pl
from jax.experimental.pallas import tpu as pltpu

F32 = jnp.float32
BF16 = jnp.bfloat16

EPS = 1e-6
LOG2E = 1.4426950408889634
CHUNK = 64
MLA_HEADS = 8
MLA_Q_RANK = 512
MLA_KV_RANK = 512
MLA_NOPE = 128
MLA_ROPE = 64
MLA_V = 128
MLA_QK = MLA_NOPE + MLA_ROPE
ROPE_THETA = 10000.0
GLA_HEADS = 4
GLA_DK = 128
GLA_DV = 256
GLA_GATE_RANK = 16
GLA_GATE_NORMALIZER = 16.0
GLA_KEY = GLA_HEADS * GLA_DK
GLA_VAL = GLA_HEADS * GLA_DV
N_MOD = 9

LANES = 128
SUBLANES = 8
VMEM_LIMIT = 56 * 1024 * 1024

COL_QLAT = 0
COL_KVLAT = COL_QLAT + MLA_Q_RANK
COL_GQ = COL_KVLAT + MLA_KV_RANK
COL_GK = COL_GQ + GLA_KEY
COL_GV = COL_GK + GLA_KEY
COL_GOUT = COL_GV + GLA_VAL
COL_GATE_A = COL_GOUT + GLA_VAL
TAIL_W = LANES
TAIL_KROPE = 0
TAIL_GLR = MLA_ROPE


def _params(sem, vmem_limit=VMEM_LIMIT):
    return pltpu.CompilerParams(dimension_semantics=sem, vmem_limit_bytes=vmem_limit)


def _dot(a, b):
    return jnp.dot(a, b, preferred_element_type=F32)


def _dot_t0(a, b):
    return lax.dot_general(a, b, (((0,), (0,)), ((), ())), preferred_element_type=F32)


def _dot_t1(a, b):
    return lax.dot_general(a, b, (((1,), (1,)), ((), ())), preferred_element_type=F32)


def _split_dot(x, w_bf16):
    hi = x.astype(BF16)
    lo = (x - hi.astype(F32)).astype(BF16)
    return _dot(hi, w_bf16) + _dot(lo, w_bf16)


def _split_dot_t0(x, w_bf16):
    hi = x.astype(BF16)
    lo = (x - hi.astype(F32)).astype(BF16)
    return _dot_t0(hi, w_bf16) + _dot_t0(lo, w_bf16)


def _silu(x):
    return x * jax.nn.sigmoid(x)


def _rms(x, g):
    ms = jnp.mean(x * x, axis=-1, keepdims=True)
    return x * lax.rsqrt(ms + EPS) * g


ADA_TN = 2048
ADA_RC = 64


def _adaln_kernel(ct_ref, w_ref, b_ref, o_ref, s_ref):
    d, nb = ct_ref.shape
    tn = w_ref.shape[1]

    @pl.when(pl.program_id(0) == 0)
    def _():
        s = _silu(ct_ref[...])
        for b in range(nb):
            s_ref[b] = jnp.broadcast_to(s[:, b:b + 1], (d, LANES))

    def body(k, acc):
        rows = pl.ds(pl.multiple_of(k * ADA_RC, ADA_RC), ADA_RC)
        out = []
        for b in range(nb):
            sb = s_ref[b, rows, :]
            cols = []
            for col in range(tn // LANES):
                p = w_ref[rows, col * LANES:(col + 1) * LANES] * sb
                cols.append(p.reshape(ADA_RC // SUBLANES, SUBLANES, LANES).sum(axis=0))
            out.append(acc[b] + jnp.concatenate(cols, axis=-1))
        return tuple(out)

    init = tuple(jnp.zeros((SUBLANES, tn), F32) for _ in range(nb))
    acc = lax.fori_loop(0, d // ADA_RC, body, init, unroll=2)
    rows = [jnp.sum(a, axis=0, keepdims=True) for a in acc]
    o_ref[...] = jnp.concatenate(rows, axis=0) + b_ref[...]


def _adaln(c, w_ada, b_ada):
    nb, d = c.shape
    n = w_ada.shape[1]
    return pl.pallas_call(
        _adaln_kernel,
        out_shape=jax.ShapeDtypeStruct((nb, n), F32),
        grid=(n // ADA_TN,),
        in_specs=[
            pl.BlockSpec((d, nb), lambda j: (0, 0)),
            pl.BlockSpec((d, ADA_TN), lambda j: (0, j)),
            pl.BlockSpec((1, ADA_TN), lambda j: (0, j)),
        ],
        out_specs=pl.BlockSpec((nb, ADA_TN), lambda j: (0, j)),
        scratch_shapes=[pltpu.VMEM((nb, d, LANES), F32)],
        compiler_params=_params(("arbitrary",)),
        name="adaln",
    )(c.T, w_ada, b_ada.reshape(1, n))


def _mod_norm(x, g, mod_ref, base):
    shift = mod_ref[0, base:base + 1, :]
    scale = mod_ref[0, base + 1:base + 2, :]
    return _rms(x, g) * (1.0 + scale) + shift


FFN_TM = 512
FFN_TF = 512
FFN_HEAD_TILES = 2
FFN_HEAD_TF = 256
FFN_GROUP = 256
FFN_WSLOTS = 3
FFN_VMEM_LIMIT = 60 * 1024 * 1024
FFN_AHEAD_CHUNKS = 8
FFN_AHEAD_ROWS = FFN_TM // FFN_AHEAD_CHUNKS


def _swiglu_tile(h, w1g, w3g, w2t):
    upd = None
    for s in range(w1g.shape[0]):
        a = _dot(h, w1g[s])
        b = _dot(h, w3g[s])
        p = (_silu(a) * b).astype(BF16)
        part = _dot(p, w2t[s * FFN_GROUP:(s + 1) * FFN_GROUP, :])
        upd = part if upd is None else upd + part
    return upd


def _ffn_finish(x_ref, mod_ref, gf_ref, o_ref, *, base, final_norm):
    gate = mod_ref[0, base + 2:base + 3, :]
    y = x_ref[...] + 0.5 * gate * o_ref[...]
    if final_norm:
        y = _rms(y, gf_ref[...])
    o_ref[...] = y


def _ffn_head_kernel(x_ref, mod_ref, g_ref, w1_ref, w3_ref, w2_ref, gf_ref,
                     o_ref, w1b_ref, w3b_ref, w2b_ref, h_ref, *, base, final_norm):
    j = pl.program_id(1)

    @pl.when(j == 0)
    def _():
        h_ref[...] = _mod_norm(x_ref[...], g_ref[...], mod_ref, base).astype(BF16)
        o_ref[...] = jnp.zeros_like(o_ref)

    w1 = w1_ref[0].astype(BF16)
    w3 = w3_ref[0].astype(BF16)
    w2 = w2_ref[0].astype(BF16)
    w1b_ref[0] = w1
    w3b_ref[0] = w3
    w2b_ref[...] = w2
    o_ref[...] += _swiglu_tile(h_ref[...], w1[None], w3[None], w2)

    @pl.when(j == pl.num_programs(1) - 1)
    def _():
        _ffn_finish(x_ref, mod_ref, gf_ref, o_ref, base=base, final_norm=final_norm)


def _ffn_kernel(x_ref, xn_ref, mod_ref, modn_ref, g_ref, w1_hbm, w3_hbm, w2_hbm, gf_ref, y0_ref,
                o_ref, h_ref, w1_buf, w3_buf, w2_buf, sem, *, base, final_norm, n_steps):
    i = pl.program_id(0)
    h_slot = i % 2
    first = (i - FFN_HEAD_TILES) * n_steps
    groups = FFN_TF // FFN_GROUP
    ahead = FFN_WSLOTS - 1

    def copies(j, slot):
        jj = j % n_steps
        return (pltpu.make_async_copy(w1_hbm.at[pl.ds(jj * groups, groups)], w1_buf.at[slot], sem.at[slot, 0]),
                pltpu.make_async_copy(w3_hbm.at[pl.ds(jj * groups, groups)], w3_buf.at[slot], sem.at[slot, 1]),
                pltpu.make_async_copy(w2_hbm.at[pl.ds(jj * FFN_TF, FFN_TF)], w2_buf.at[slot], sem.at[slot, 2]))

    @pl.when(i < FFN_HEAD_TILES)
    def _():
        o_ref[...] = y0_ref[...]

    @pl.when(i >= FFN_HEAD_TILES)
    def _():
        @pl.when(i == FFN_HEAD_TILES)
        def _():
            for j in range(ahead):
                for c in copies(j, j):
                    c.start()
            h_ref[h_slot] = _mod_norm(x_ref[...], g_ref[...], mod_ref, base).astype(BF16)

        for j in range(n_steps):
            slot = (first + j) % FFN_WSLOTS
            for c in copies(j, slot):
                c.wait()
            if j < FFN_AHEAD_CHUNKS:
                rows = slice(j * FFN_AHEAD_ROWS, (j + 1) * FFN_AHEAD_ROWS)
                h_next = _mod_norm(xn_ref[rows, :], g_ref[...], modn_ref, base).astype(BF16)
            upd = _swiglu_tile(h_ref[h_slot], w1_buf.at[slot], w3_buf.at[slot], w2_buf[slot])
            for c in copies(j + ahead, (first + j + ahead) % FFN_WSLOTS):
                c.start()
            if j == 0:
                o_ref[...] = upd
            else:
                o_ref[...] += upd
            if j < FFN_AHEAD_CHUNKS:
                h_ref[1 - h_slot, rows, :] = h_next
        _ffn_finish(x_ref, mod_ref, gf_ref, o_ref, base=base, final_norm=final_norm)

        @pl.when(i == pl.num_programs(0) - 1)
        def _():
            for j in range(n_steps, n_steps + ahead):
                for c in copies(j, (first + j) % FFN_WSLOTS):
                    c.wait()


def _ffn(x2d, mod3, g, w1, w3, w2, g_final, *, base, final_norm, seq):
    t, d = x2d.shape
    dff = w1.shape[2]
    tiles_per_batch = seq // FFN_TM
    const = lambda i, j: (0, 0)
    kw = dict(base=base, final_norm=final_norm)
    tag = "_final" if final_norm else ""
    common = [
        pl.BlockSpec((1, N_MOD, d), lambda i, j: (i // tiles_per_batch, 0, 0)),
        pl.BlockSpec((1, d), const),
    ]
    hf = FFN_HEAD_TF
    head_rows = FFN_HEAD_TILES * FFN_TM
    assert head_rows <= seq
    assert hf == FFN_GROUP
    w13_shape = jax.ShapeDtypeStruct((dff // FFN_GROUP, d, FFN_GROUP), BF16)
    w13_spec = pl.BlockSpec((1, d, FFN_GROUP), lambda i, j: (j, 0, 0))
    y0, w1b, w3b, w2b = pl.pallas_call(
        functools.partial(_ffn_head_kernel, **kw),
        out_shape=(jax.ShapeDtypeStruct((head_rows, d), F32), w13_shape, w13_shape,
                   jax.ShapeDtypeStruct((dff, d), BF16)),
        grid=(1, dff // hf),
        in_specs=[pl.BlockSpec((head_rows, d), const, pipeline_mode=pl.Buffered(1))] + common + [
            pl.BlockSpec((1, d, hf), lambda i, j: (0, 0, j)),
            pl.BlockSpec((1, d, hf), lambda i, j: (0, 0, j)),
            pl.BlockSpec((1, hf, d), lambda i, j: (0, j, 0)),
            pl.BlockSpec((1, d), const),
        ],
        out_specs=(pl.BlockSpec((head_rows, d), const), w13_spec, w13_spec,
                   pl.BlockSpec((hf, d), lambda i, j: (j, 0))),
        scratch_shapes=[pltpu.VMEM((head_rows, d), BF16)],
        compiler_params=_params(("arbitrary", "arbitrary")),
        name="ffn_head" + tag,
    )(x2d, mod3, g, w1, w3, w2, g_final)

    n_tiles = t // FFN_TM
    assert dff // FFN_TF >= FFN_AHEAD_CHUNKS and n_tiles > FFN_HEAD_TILES
    nxt = lambda i: jnp.minimum(i + 1, n_tiles - 1)
    groups = FFN_TF // FFN_GROUP
    hbm = pl.BlockSpec(memory_space=pl.ANY)
    return pl.pallas_call(
        functools.partial(_ffn_kernel, n_steps=dff // FFN_TF, **kw),
        out_shape=jax.ShapeDtypeStruct((t, d), F32),
        grid=(n_tiles,),
        in_specs=[
            pl.BlockSpec((FFN_TM, d), lambda i: (i, 0)),
            pl.BlockSpec((FFN_TM, d), lambda i: (nxt(i), 0)),
            pl.BlockSpec((1, N_MOD, d), lambda i: (i // tiles_per_batch, 0, 0)),
            pl.BlockSpec((1, N_MOD, d), lambda i: (nxt(i) // tiles_per_batch, 0, 0)),
            pl.BlockSpec((1, d), lambda i: (0, 0)),
            hbm, hbm, hbm,
            pl.BlockSpec((1, d), lambda i: (0, 0)),
            pl.BlockSpec((FFN_TM, d), lambda i: (jnp.minimum(i, FFN_HEAD_TILES - 1), 0),
                         pipeline_mode=pl.Buffered(1)),
        ],
        out_specs=pl.BlockSpec((FFN_TM, d), lambda i: (i, 0)),
        scratch_shapes=[pltpu.VMEM((2, FFN_TM, d), BF16),
                        pltpu.VMEM((FFN_WSLOTS, groups, d, FFN_GROUP), BF16),
                        pltpu.VMEM((FFN_WSLOTS, groups, d, FFN_GROUP), BF16),
                        pltpu.VMEM((FFN_WSLOTS, FFN_TF, d), BF16),
                        pltpu.SemaphoreType.DMA((FFN_WSLOTS, 3))],
        compiler_params=_params(("arbitrary",), FFN_VMEM_LIMIT),
        name="ffn" + tag,
    )(x2d, x2d, mod3, mod3, g, w1b, w3b, w2b, g_final, y0)


IN_SPLITS = (MLA_Q_RANK, MLA_KV_RANK, MLA_ROPE, GLA_KEY, GLA_KEY, GLA_VAL, GLA_GATE_RANK, GLA_VAL,
             2048, 2048)
MAIN_ORDER = (0, 1, 3, 4, 5, 7, 8, 9)
TAIL_ORDER = (2, 6)
RELAYOUT_ROWS = 1024


def _segments(order):
    starts = [sum(IN_SPLITS[:i]) for i in range(len(IN_SPLITS))]
    segs, dst = [], 0
    for i in order:
        src, n = starts[i], IN_SPLITS[i]
        if segs and segs[-1][0] + segs[-1][2] == src:
            segs[-1] = (segs[-1][0], segs[-1][1], segs[-1][2] + n)
        else:
            segs.append((src, dst, n))
        dst += n
    return segs, dst


def _relayout_kernel(lo_ref, hi_ref, kr_ref, glr_ref, main_ref, tail_ref):
    r = pl.program_id(0)
    rows = RELAYOUT_ROWS
    for src, dst, n in _segments(MAIN_ORDER)[0]:
        shift = src - dst

        @pl.when((r >= dst // rows) & (r < (dst + n) // rows))
        def _(shift=shift):
            main_ref[:rows - shift, :] = lo_ref[shift:, :].astype(BF16)
            if shift:
                main_ref[rows - shift:, :] = hi_ref[:shift, :].astype(BF16)

    tail_ref[:MLA_ROPE, :] = kr_ref[...].astype(BF16)
    tail_ref[MLA_ROPE:MLA_ROPE + GLA_GATE_RANK, :] = glr_ref[...].astype(BF16)
    pad = TAIL_W - MLA_ROPE - GLA_GATE_RANK
    tail_ref[MLA_ROPE + GLA_GATE_RANK:, :] = jnp.zeros((pad, tail_ref.shape[1]), BF16)


def _relayout_w_in(w_in_t):
    d_in, d = w_in_t.shape
    segs, n_main = _segments(MAIN_ORDER)
    rows = RELAYOUT_ROWS
    assert all(dst % rows == 0 and n % rows == 0 and 0 <= src - dst <= LANES and (src - dst) % 16 == 0
               for src, dst, n in segs)
    (kr_src, _, _), (glr_src, _, _) = _segments(TAIL_ORDER)[0]
    assert kr_src % MLA_ROPE == 0 and glr_src % GLA_GATE_RANK == 0
    return pl.pallas_call(
        _relayout_kernel,
        out_shape=(jax.ShapeDtypeStruct((n_main, d), BF16), jax.ShapeDtypeStruct((TAIL_W, d), BF16)),
        grid=(n_main // rows,),
        in_specs=[pl.BlockSpec((rows, d), lambda r: (r, 0)),
                  pl.BlockSpec((LANES, d), lambda r: ((r + 1) * (rows // LANES), 0)),
                  pl.BlockSpec((MLA_ROPE, d), lambda r: (kr_src // MLA_ROPE, 0)),
                  pl.BlockSpec((GLA_GATE_RANK, d), lambda r: (glr_src // GLA_GATE_RANK, 0))],
        out_specs=(pl.BlockSpec((rows, d), lambda r: (r, 0)),
                   pl.BlockSpec((TAIL_W, d), lambda r: (0, 0))),
        compiler_params=_params(("arbitrary",)),
        name="w_in_layout",
    )(w_in_t, w_in_t, w_in_t, w_in_t)


INP_TM = 1024
INP_TN = 2048


def _inproj_kernel(x_ref, mod_ref, g_ref, w_ref, wt_ref, o_ref, tail_ref, h_ref, *, base):
    @pl.when(pl.program_id(1) == 0)
    def _():
        h = _mod_norm(x_ref[...], g_ref[...], mod_ref, base).astype(BF16)
        h_ref[...] = h
        tail_ref[...] = _dot_t1(h, wt_ref[...])

    o_ref[...] = _dot_t1(h_ref[...], w_ref[...]).astype(BF16)


def _inproj(x2d, mod3, g, w_main, w_tail, *, base, seq):
    t, d = x2d.shape
    n = w_main.shape[0]
    tiles_per_batch = seq // INP_TM
    return pl.pallas_call(
        functools.partial(_inproj_kernel, base=base),
        out_shape=(jax.ShapeDtypeStruct((t, n), BF16),
                   jax.ShapeDtypeStruct((t, TAIL_W), F32)),
        grid=(t // INP_TM, n // INP_TN),
        in_specs=[
            pl.BlockSpec((INP_TM, d), lambda i, j: (i, 0)),
            pl.BlockSpec((1, N_MOD, d), lambda i, j: (i // tiles_per_batch, 0, 0)),
            pl.BlockSpec((1, d), lambda i, j: (0, 0)),
            pl.BlockSpec((INP_TN, d), lambda i, j: (j, 0)),
            pl.BlockSpec((TAIL_W, d), lambda i, j: (0, 0)),
        ],
        out_specs=(pl.BlockSpec((INP_TM, INP_TN), lambda i, j: (i, j)),
                   pl.BlockSpec((INP_TM, TAIL_W), lambda i, j: (i, 0))),
        scratch_shapes=[pltpu.VMEM((INP_TM, d), BF16)],
        compiler_params=_params(("parallel", "arbitrary")),
        name="in_proj",
    )(x2d, mod3, g, w_main, w_tail)


PREP_TM = 512
ROPE_HALF = MLA_ROPE // 2


def _rope_rotate(x, cos, sin_signed):
    width = x.shape[-1]
    lane = lax.broadcasted_iota(jnp.int32, x.shape, 1)
    first_half = (lane % MLA_ROPE) < ROPE_HALF
    partner = jnp.where(first_half,
                        pltpu.roll(x, width - ROPE_HALF, 1),
                        pltpu.roll(x, ROPE_HALF, 1))
    return x * cos + partner * sin_signed


def _prep_kernel(qlat_ref, kvlat_ref, tail_ref, pos_ref, invf_ref, sgn_ref,
                 gql_ref, wuq_ref, gqn_ref, gqr_ref, gkvl_ref, wukv_ref, gkn_ref, gkr_ref,
                 grp_ref, q_ref, k_ref, v_ref):
    ang_t = invf_ref[...] * pos_ref[0].astype(F32)
    lane_reps = LANES // ROPE_HALF
    cos = jnp.tile(jnp.cos(ang_t).T, (1, lane_reps))
    sin_signed = jnp.tile(jnp.sin(ang_t).T, (1, lane_reps)) * sgn_ref[...]
    n_rope = MLA_HEADS * MLA_ROPE
    cos_q = jnp.concatenate([cos] * (n_rope // LANES), axis=-1)
    sin_q = jnp.concatenate([sin_signed] * (n_rope // LANES), axis=-1)

    cq = _rms(qlat_ref[...].astype(F32), gql_ref[...]).astype(BF16)
    ckv = _rms(kvlat_ref[...].astype(F32), gkvl_ref[...]).astype(BF16)
    n_nope = MLA_HEADS * MLA_NOPE
    pair = 2 * MLA_NOPE

    for hp in range(MLA_HEADS // 2):
        cols = slice(hp * pair, (hp + 1) * pair)
        qp = _dot(cq, wuq_ref[:, cols])
        kp = _dot(ckv, wukv_ref[:, cols])
        for hh in range(2):
            h = 2 * hp + hh
            sub = slice(hh * MLA_NOPE, (hh + 1) * MLA_NOPE)
            q_ref[0, h, :, :MLA_NOPE] = _rms(qp[:, sub], gqn_ref[...]).astype(BF16)
            k_ref[0, h, :, :MLA_NOPE] = _rms(kp[:, sub], gkn_ref[...]).astype(BF16)
        v_ref[0, :, cols] = _dot(ckv, wukv_ref[:, n_nope + hp * pair:n_nope + (hp + 1) * pair]).astype(BF16)

    qr = _dot(cq, wuq_ref[:, n_nope:])
    ssq = _split_dot(qr * qr, grp_ref[...])
    qr = qr * lax.rsqrt(ssq * (1.0 / MLA_ROPE) + EPS) * gqr_ref[...]
    qr = _rope_rotate(qr, cos_q, sin_q).astype(BF16)
    kr = tail_ref[:, TAIL_KROPE:TAIL_KROPE + MLA_ROPE]
    kr = _rms(kr, gkr_ref[...])
    kr = _rope_rotate(kr, cos[:, :MLA_ROPE], sin_signed[:, :MLA_ROPE]).astype(BF16)
    for h in range(MLA_HEADS):
        q_ref[0, h, :, MLA_NOPE:] = qr[:, h * MLA_ROPE:(h + 1) * MLA_ROPE]
        k_ref[0, h, :, MLA_NOPE:] = kr


def _mla_prep(proj, tail, pos3d, consts, weights, *, batch, seq):
    t = proj.shape[0]
    tiles_per_batch = seq // PREP_TM
    row = lambda i: (i, 0)
    const = lambda i: (0, 0)
    (invf, sgn, grp) = consts
    (gql, wuq, gqn, gqr, gkvl, wukv, gkn, gkr) = weights
    head_out = pl.BlockSpec((1, MLA_HEADS, PREP_TM, MLA_QK),
                            lambda i: (i // tiles_per_batch, 0, i % tiles_per_batch, 0))
    full = lambda a: pl.BlockSpec(a.shape, const)
    return pl.pallas_call(
        _prep_kernel,
        out_shape=(jax.ShapeDtypeStruct((batch, MLA_HEADS, seq, MLA_QK), BF16),
                   jax.ShapeDtypeStruct((batch, MLA_HEADS, seq, MLA_QK), BF16),
                   jax.ShapeDtypeStruct((batch, seq, MLA_HEADS * MLA_V), BF16)),
        grid=(t // PREP_TM,),
        in_specs=[
            pl.BlockSpec((PREP_TM, MLA_Q_RANK), lambda i: (i, COL_QLAT // MLA_Q_RANK)),
            pl.BlockSpec((PREP_TM, MLA_KV_RANK), lambda i: (i, COL_KVLAT // MLA_KV_RANK)),
            pl.BlockSpec((PREP_TM, TAIL_W), row),
            pl.BlockSpec((1, 1, PREP_TM), lambda i: (i, 0, 0)),
            full(invf), full(sgn),
            full(gql), full(wuq), full(gqn), full(gqr),
            full(gkvl), full(wukv), full(gkn), full(gkr),
            full(grp),
        ],
        out_specs=(head_out, head_out,
                   pl.BlockSpec((1, PREP_TM, MLA_HEADS * MLA_V),
                                lambda i: (i // tiles_per_batch, i % tiles_per_batch, 0))),
        compiler_params=_params(("parallel",)),
        name="mla_prep",
    )(proj, proj, tail, pos3d, invf, sgn, gql, wuq, gqn, gqr, gkvl, wukv, gkn, gkr, grp)


ATT_T = 1024
ATT_H = ATT_T // 2


def _attn_kernel(q_ref, k_ref, v_ref, o_ref):
    n_tiles = k_ref.shape[2] // ATT_T
    ones = jnp.ones((ATT_T, MLA_V), BF16)
    r = lax.broadcasted_iota(jnp.int32, (ATT_H, ATT_H), 0)
    c = lax.broadcasted_iota(jnp.int32, (ATT_H, ATT_H), 1)
    visible = (c // CHUNK) <= (r // CHUNK)
    state = {}

    def update(chain, s, v2):
        m_cur = jnp.max(s, axis=-1, keepdims=True)
        if chain not in state:
            m_next = jnp.broadcast_to(m_cur, (ATT_H, LANES))
        else:
            m_prev, l_prev, acc_prev = state[chain]
            m_next = jnp.maximum(m_prev, m_cur)
            alpha = jnp.exp2(m_prev - m_next)
        p = jnp.exp2(s - jnp.tile(m_next, (1, s.shape[1] // LANES)))
        pv = _dot(p.astype(BF16), v2)
        acc, l = pv[:, :MLA_V], pv[:, MLA_V:]
        if chain in state:
            acc, l = alpha * acc_prev + acc, alpha * l_prev + l
        state[chain] = (m_next, l, acc)

    for t in range(n_tiles):
        rows = slice(t * ATT_T, (t + 1) * ATT_T)
        k = k_ref[0, 0, rows, :]
        v2 = jnp.concatenate([v_ref[0, rows, :], ones], axis=-1)
        for qt in range(t, n_tiles):
            q0 = q_ref[0, 0, qt * ATT_T:qt * ATT_T + ATT_H, :]
            q1 = q_ref[0, 0, qt * ATT_T + ATT_H:(qt + 1) * ATT_T, :]
            if t < qt:
                update((qt, 0), _dot_t1(q0, k), v2)
                update((qt, 1), _dot_t1(q1, k), v2)
            else:
                update((qt, 0), jnp.where(visible, _dot_t1(q0, k[:ATT_H]), -jnp.inf), v2[:ATT_H])
                s1 = _dot_t1(q1, k)
                s1 = jnp.concatenate([s1[:, :ATT_H], jnp.where(visible, s1[:, ATT_H:], -jnp.inf)],
                                     axis=-1)
                update((qt, 1), s1, v2)
                for half in range(2):
                    _, l, acc = state[(qt, half)]
                    lo = qt * ATT_T + half * ATT_H
                    o_ref[0, lo:lo + ATT_H, :] = (acc / l).astype(BF16)


def _mla_attn(q, k, v):
    batch, heads, seq, _ = q.shape
    return pl.pallas_call(
        _attn_kernel,
        out_shape=jax.ShapeDtypeStruct((batch, seq, heads * MLA_V), BF16),
        grid=(batch, heads),
        in_specs=[
            pl.BlockSpec((1, 1, seq, MLA_QK), lambda b, h: (b, h, 0, 0)),
            pl.BlockSpec((1, 1, seq, MLA_QK), lambda b, h: (b, h, 0, 0)),
            pl.BlockSpec((1, seq, MLA_V), lambda b, h: (b, 0, h)),
        ],
        out_specs=pl.BlockSpec((1, seq, MLA_V), lambda b, h: (b, 0, h)),
        compiler_params=_params(("parallel", "parallel")),
        name="mla_attn",
    )(q, k, v)


GLA_L = 512
GLA_NC = GLA_L // CHUNK
GLA_SUB = 512


def _gla_head(log_a, q, k, v, go, gg, state_ref, causal, tri):
    sub_nc = GLA_SUB // CHUNK
    q_dec, k_end, o_intra, b_ends = [], [], [], []
    for sb in range(GLA_L // GLA_SUB):
        rows = slice(sb * GLA_SUB, (sb + 1) * GLA_SUB)
        la = log_a[rows]
        la_hi = la.astype(BF16)
        la_lo = (la - la_hi.astype(F32)).astype(BF16)
        cum = _dot(tri, jnp.concatenate([la_hi, la_lo], axis=-1))
        b_cum = cum[:, :GLA_DK] + cum[:, GLA_DK:]
        b3 = b_cum.reshape(sub_nc, CHUNK, GLA_DK)
        b_end = b3[:, CHUNK - 1:CHUNK, :]
        b_tot = jnp.broadcast_to(b_end, b3.shape).reshape(GLA_SUB, GLA_DK)
        b_ends.append(jnp.exp(b_end))

        qf = q[rows].astype(F32) * (GLA_DK ** -0.5)
        kf = k[rows].astype(F32)
        qd = (qf * jnp.exp(b_cum)).astype(BF16)
        kd = (kf * jnp.exp(-b_cum)).astype(BF16)
        q_dec.append(qd)
        k_end.append((kf * jnp.exp(b_tot - b_cum)).astype(BF16))
        attn = jnp.where(causal, _dot_t1(qd, kd), 0.0).astype(BF16)
        o_intra.append(_dot(attn, v[rows]))

    state = state_ref[...]
    outs = []
    for n in range(GLA_NC):
        sb, rows = n // sub_nc, slice((n % sub_nc) * CHUNK, (n % sub_nc + 1) * CHUNK)
        outs.append(o_intra[sb][rows] + _dot_t1(q_dec[sb][rows], state.astype(BF16)))
        v_n = v[n * CHUNK:(n + 1) * CHUNK]
        state = state * b_ends[sb][n % sub_nc] + _dot_t0(v_n, k_end[sb][rows])
    state_ref[...] = state

    o = jnp.concatenate(outs, axis=0)
    return (_rms(o, gg) * _silu(go.astype(F32))).astype(BF16)


def _gla_kernel(q_ref, k_ref, v_ref, go_ref, tail_ref, wg_ref, bg_ref, gg_ref,
                o_ref, state_ref):
    @pl.when(pl.program_id(1) == 0)
    def _():
        state_ref[...] = jnp.zeros_like(state_ref)

    wg = wg_ref[...]
    wg_hi = wg.astype(BF16)
    wg_lo = (wg - wg_hi.astype(F32)).astype(BF16)
    tl = tail_ref[...]
    tl_hi = tl.astype(BF16)
    tl_lo = (tl - tl_hi.astype(F32)).astype(BF16)
    z = _dot(tl_hi, wg_hi) + _dot(tl_lo, wg_hi) + _dot(tl_hi, wg_lo) + bg_ref[...]
    log_a = (jnp.minimum(z, 0.0) - jnp.log1p(jnp.exp(-jnp.abs(z)))) * (1.0 / GLA_GATE_NORMALIZER)

    r = lax.broadcasted_iota(jnp.int32, (GLA_SUB, GLA_SUB), 0)
    c = lax.broadcasted_iota(jnp.int32, (GLA_SUB, GLA_SUB), 1)
    causal = ((r // CHUNK) == (c // CHUNK)) & (c <= r)
    tri = causal.astype(BF16)

    for hh in range(GLA_HEADS):
        dk = slice(hh * GLA_DK, (hh + 1) * GLA_DK)
        dv = slice(hh * GLA_DV, (hh + 1) * GLA_DV)
        o_ref[:, dv] = _gla_head(log_a[:, dk], q_ref[:, dk], k_ref[:, dk], v_ref[:, dv],
                                 go_ref[:, dv], gg_ref[...], state_ref.at[hh], causal, tri)


def _gla(proj, tail, wg_pad, b_gk, g_gla, *, batch, seq):
    t = proj.shape[0]
    steps = seq // GLA_L
    tok = lambda b, i: b * steps + i
    return pl.pallas_call(
        _gla_kernel,
        out_shape=jax.ShapeDtypeStruct((t, GLA_VAL), BF16),
        grid=(batch, steps),
        in_specs=[
            pl.BlockSpec((GLA_L, GLA_KEY), lambda b, i: (tok(b, i), COL_GQ // GLA_KEY)),
            pl.BlockSpec((GLA_L, GLA_KEY), lambda b, i: (tok(b, i), COL_GK // GLA_KEY)),
            pl.BlockSpec((GLA_L, GLA_VAL), lambda b, i: (tok(b, i), COL_GV // GLA_VAL)),
            pl.BlockSpec((GLA_L, GLA_VAL), lambda b, i: (tok(b, i), COL_GOUT // GLA_VAL)),
            pl.BlockSpec((GLA_L, TAIL_W), lambda b, i: (tok(b, i), 0)),
            pl.BlockSpec((TAIL_W, GLA_KEY), lambda b, i: (0, 0)),
            pl.BlockSpec((1, GLA_KEY), lambda b, i: (0, 0)),
            pl.BlockSpec((1, GLA_DV), lambda b, i: (0, 0)),
        ],
        out_specs=pl.BlockSpec((GLA_L, GLA_VAL), lambda b, i: (tok(b, i), 0)),
        scratch_shapes=[pltpu.VMEM((GLA_HEADS, GLA_DV, GLA_DK), F32)],
        compiler_params=_params(("parallel", "arbitrary")),
        name="gla",
    )(proj, proj, proj, proj, tail, wg_pad, b_gk, g_gla)


MRG_TM = 512


def _merge_kernel(x_ref, mod_ref, a_ref, b_ref, ga_ref, gb_ref, wa_ref, wb_ref, wo_ref, o_ref,
                  *, base):
    ya = _dot(a_ref[...], wa_ref[...])
    yb = _dot(b_ref[...], wb_ref[...])
    merged = (jax.nn.sigmoid(ga_ref[...].astype(F32)) * ya
              + jax.nn.sigmoid(gb_ref[...].astype(F32)) * yb).astype(BF16)
    gate = mod_ref[0, base + 2:base + 3, :]
    o_ref[...] = x_ref[...] + gate * _dot(merged, wo_ref[...])


def _merge(x2d, mod3, attn_o, gla_o, proj, wa, wb, wo, *, base, seq):
    t, d = x2d.shape
    tiles_per_batch = seq // MRG_TM
    row = lambda i: (i, 0)
    resident = lambda a: pl.BlockSpec(a.shape, lambda i: (0, 0), pipeline_mode=pl.Buffered(1))
    return pl.pallas_call(
        functools.partial(_merge_kernel, base=base),
        out_shape=jax.ShapeDtypeStruct((t, d), F32),
        grid=(t // MRG_TM,),
        in_specs=[
            pl.BlockSpec((MRG_TM, d), row),
            pl.BlockSpec((1, N_MOD, d), lambda i: (i // tiles_per_batch, 0, 0)),
            pl.BlockSpec((MRG_TM, attn_o.shape[1]), row),
            pl.BlockSpec((MRG_TM, gla_o.shape[1]), row),
            pl.BlockSpec((MRG_TM, d), lambda i: (i, COL_GATE_A // d)),
            pl.BlockSpec((MRG_TM, d), lambda i: (i, COL_GATE_A // d + 1)),
            resident(wa), resident(wb), resident(wo),
        ],
        out_specs=pl.BlockSpec((MRG_TM, d), row),
        compiler_params=_params(("parallel",)),
        name="merge",
    )(x2d, mod3, attn_o, gla_o, proj, proj, wa, wb, wo)


def _layer(x2d, mod3, pos3d, p, *, batch, seq):
    d = x2d.shape[1]
    assert IN_SPLITS[-1] == d and sum(IN_SPLITS) == p["w_in"].shape[1]
    w_main, w_tail = _relayout_w_in(p["w_in"].T)
    wuq = p["w_uq"].reshape(MLA_Q_RANK, MLA_HEADS, MLA_QK)
    wuq = jnp.concatenate([wuq[:, :, :MLA_NOPE].reshape(MLA_Q_RANK, -1),
                           wuq[:, :, MLA_NOPE:].reshape(MLA_Q_RANK, -1)], axis=1).astype(BF16)
    wukv = p["w_ukv"].reshape(MLA_KV_RANK, MLA_HEADS, MLA_NOPE + MLA_V)
    wukv = jnp.concatenate([wukv[:, :, :MLA_NOPE].reshape(MLA_KV_RANK, -1),
                            wukv[:, :, MLA_NOPE:].reshape(MLA_KV_RANK, -1)], axis=1).astype(BF16)
    wg_pad = jnp.zeros((TAIL_W, GLA_KEY), F32).at[TAIL_GLR:TAIL_GLR + GLA_GATE_RANK].set(p["w_gk_up"])

    inv_freq = ROPE_THETA ** (-jnp.arange(0, MLA_ROPE, 2, dtype=F32) / MLA_ROPE)
    invf = inv_freq.reshape(ROPE_HALF, 1)
    sgn = jnp.tile(jnp.concatenate([-jnp.ones(ROPE_HALF, F32), jnp.ones(ROPE_HALF, F32)]),
                   LANES // MLA_ROPE).reshape(1, LANES)
    n_rope = MLA_HEADS * MLA_ROPE
    lane = jnp.arange(n_rope)
    grp = (lane[:, None] // MLA_ROPE == lane[None, :] // MLA_ROPE).astype(BF16)

    row = lambda a: a.reshape(1, -1)
    q_scale = MLA_QK ** -0.5 * LOG2E
    x2d = _ffn(x2d, mod3, row(p["g_ffn1"]), p["w1_a"][None], p["w3_a"][None],
               p["w2_a"][None], row(p["g_final"]), base=0, final_norm=False, seq=seq)
    proj, tail = _inproj(x2d, mod3, row(p["g_mix"]), w_main, w_tail, base=3, seq=seq)
    q, k, v = _mla_prep(
        proj, tail, pos3d, (invf, sgn, grp),
        (row(p["g_q_lat"]), wuq, row(p["g_qn"] * q_scale),
         row(jnp.tile(p["g_qr"], MLA_HEADS) * q_scale),
         row(p["g_kv_lat"]), wukv, row(p["g_kn"]), row(p["g_kr"])),
        batch=batch, seq=seq)
    attn_o = _mla_attn(q, k, v).reshape(batch * seq, MLA_HEADS * MLA_V)
    gla_o = _gla(proj, tail, wg_pad, row(p["b_gk"]), row(p["g_gla"]), batch=batch, seq=seq)
    x2d = _merge(x2d, mod3, attn_o, gla_o, proj, p["w_proj_a"].astype(BF16),
                 p["w_proj_b"].astype(BF16), p["w_out"].astype(BF16), base=3, seq=seq)
    x2d = _ffn(x2d, mod3, row(p["g_ffn2"]), p["w1_b"][None], p["w3_b"][None],
               p["w2_b"][None], row(p["g_final"]), base=6, final_norm=True, seq=seq)
    return x2d


def kernel(x, c, positions, w_ada, b_ada, g_ffn1, w1_a, w3_a, w2_a, g_mix, w_in, g_q_lat, w_uq, g_qn, g_qr, g_kv_lat, w_ukv, g_kn, g_kr, w_gk_up, b_gk, g_gla, w_proj_a, w_proj_b, w_out, g_ffn2, w1_b, w3_b, w2_b, g_final):
    batch, seq, d = x.shape
    depth = w_ada.shape[0]
    assert depth == 1, "the final norm is fused into the last FFN of a single layer"
    names = ("g_ffn1", "w1_a", "w3_a", "w2_a", "g_mix", "w_in", "g_q_lat", "w_uq", "g_qn", "g_qr",
             "g_kv_lat", "w_ukv", "g_kn", "g_kr", "w_gk_up", "b_gk", "g_gla", "w_proj_a", "w_proj_b",
             "w_out", "g_ffn2", "w1_b", "w3_b", "w2_b", "g_final")
    stacked = (g_ffn1, w1_a, w3_a, w2_a, g_mix, w_in, g_q_lat, w_uq, g_qn, g_qr, g_kv_lat, w_ukv,
               g_kn, g_kr, w_gk_up, b_gk, g_gla, w_proj_a, w_proj_b, w_out, g_ffn2, w1_b, w3_b, w2_b,
               g_final)
    x2d = x.reshape(batch * seq, d)
    pos3d = positions.reshape(batch * seq // PREP_TM, 1, PREP_TM)
    p = {n: a[0] for n, a in zip(names, stacked)}
    mod3 = _adaln(c, w_ada[0], b_ada[0]).reshape(batch, N_MOD, d)
    x2d = _layer(x2d, mod3, pos3d, p, batch=batch, seq=seq)
    return x2d.reshape(batch, seq, d)
```

```python
import functools

import jax
import jax.numpy as jnp
from jax import lax
from jax.experimental import pallas as pl
from jax.experimental.pallas import tpu as pltpu

F32 = jnp.float32
BF16 = jnp.bfloat16

EPS = 1e-6
LOG2E = 1.4426950408889634
CHUNK = 64
MLA_HEADS = 8
MLA_Q_RANK = 512
MLA_KV_RANK = 512
MLA_NOPE = 128
MLA_ROPE = 64
MLA_V = 128
MLA_QK = MLA_NOPE + MLA_ROPE
ROPE_THETA = 10000.0
GLA_HEADS = 4
GLA_DK = 128
GLA_DV = 256
GLA_GATE_RANK = 16
GLA_GATE_NORMALIZER = 16.0
GLA_KEY = GLA_HEADS * GLA_DK
GLA_VAL = GLA_HEADS * GLA_DV
N_MOD = 9

LANES = 128
SUBLANES = 8
VMEM_LIMIT = 56 * 1024 * 1024

COL_QLAT = 0
COL_KVLAT = COL_QLAT + MLA_Q_RANK
COL_GQ = COL_KVLAT + MLA_KV_RANK
COL_GK = COL_GQ + GLA_KEY
COL_GV = COL_GK + GLA_KEY
COL_GOUT = COL_GV + GLA_VAL
COL_GATE_A = COL_GOUT + GLA_VAL
TAIL_W = LANES
TAIL_KROPE = 0
TAIL_GLR = MLA_ROPE


def _params(sem, vmem_limit=VMEM_LIMIT):
    return pltpu.CompilerParams(dimension_semantics=sem, vmem_limit_bytes=vmem_limit)


def _dot(a, b):
    return jnp.dot(a, b, preferred_element_type=F32)


def _dot_t0(a, b):
    return lax.dot_general(a, b, (((0,), (0,)), ((), ())), preferred_element_type=F32)


def _dot_t1(a, b):
    return lax.dot_general(a, b, (((1,), (1,)), ((), ())), preferred_element_type=F32)


def _split_dot(x, w_bf16):
    hi = x.astype(BF16)
    lo = (x - hi.astype(F32)).astype(BF16)
    return _dot(hi, w_bf16) + _dot(lo, w_bf16)


def _silu(x):
    return x * jax.nn.sigmoid(x)


def _rms(x, g):
    ms = jnp.mean(x * x, axis=-1, keepdims=True)
    return x * lax.rsqrt(ms + EPS) * g


ADA_TN = 2048
ADA_RC = 64


def _adaln_kernel(ct_ref, w_ref, b_ref, o_ref, s_ref):
    d, nb = ct_ref.shape
    tn = w_ref.shape[1]

    @pl.when(pl.program_id(0) == 0)
    def _():
        s = _silu(ct_ref[...])
        for b in range(nb):
            s_ref[b] = jnp.broadcast_to(s[:, b:b + 1], (d, LANES))

    def body(k, acc):
        rows = pl.ds(pl.multiple_of(k * ADA_RC, ADA_RC), ADA_RC)
        out = []
        for b in range(nb):
            sb = s_ref[b, rows, :]
            cols = []
            for col in range(tn // LANES):
                p = w_ref[rows, col * LANES:(col + 1) * LANES] * sb
                cols.append(p.reshape(ADA_RC // SUBLANES, SUBLANES, LANES).sum(axis=0))
            out.append(acc[b] + jnp.concatenate(cols, axis=-1))
        return tuple(out)

    init = tuple(jnp.zeros((SUBLANES, tn), F32) for _ in range(nb))
    acc = lax.fori_loop(0, d // ADA_RC, body, init, unroll=2)
    rows = [jnp.sum(a, axis=0, keepdims=True) for a in acc]
    o_ref[...] = jnp.concatenate(rows, axis=0) + b_ref[...]


def _adaln(c, w_ada, b_ada):
    nb, d = c.shape
    n = w_ada.shape[1]
    return pl.pallas_call(
        _adaln_kernel,
        out_shape=jax.ShapeDtypeStruct((nb, n), F32),
        grid=(n // ADA_TN,),
        in_specs=[
            pl.BlockSpec((d, nb), lambda j: (0, 0)),
            pl.BlockSpec((d, ADA_TN), lambda j: (0, j)),
            pl.BlockSpec((1, ADA_TN), lambda j: (0, j)),
        ],
        out_specs=pl.BlockSpec((nb, ADA_TN), lambda j: (0, j)),
        scratch_shapes=[pltpu.VMEM((nb, d, LANES), F32)],
        compiler_params=_params(("arbitrary",)),
        name="adaln",
    )(c.T, w_ada, b_ada.reshape(1, n))


def _mod_norm(x, g, mod_ref, base):
    shift = mod_ref[0, base:base + 1, :]
    scale = mod_ref[0, base + 1:base + 2, :]
    return _rms(x, g) * (1.0 + scale) + shift


FFN_TM = 512
FFN_TF = 512
FFN_HEAD_TILES = 2
FFN_HEAD_TF = 256
FFN_GROUP = 256
FFN_WSLOTS = 3
FFN_VMEM_LIMIT = 60 * 1024 * 1024
FFN_AHEAD_CHUNKS = 8
FFN_AHEAD_ROWS = FFN_TM // FFN_AHEAD_CHUNKS


def _swiglu_tile(h, w1g, w3g, w2t):
    upd = None
    for s in range(w1g.shape[0]):
        a = _dot(h, w1g[s])
        b = _dot(h, w3g[s])
        p = (_silu(a) * b).astype(BF16)
        part = _dot(p, w2t[s * FFN_GROUP:(s + 1) * FFN_GROUP, :])
        upd = part if upd is None else upd + part
    return upd


def _ffn_finish(x_ref, mod_ref, gf_ref, o_ref, *, base, final_norm):
    gate = mod_ref[0, base + 2:base + 3, :]
    y = x_ref[...] + 0.5 * gate * o_ref[...]
    if final_norm:
        y = _rms(y, gf_ref[...])
    o_ref[...] = y


def _ffn_head_kernel(x_ref, mod_ref, g_ref, w1_ref, w3_ref, w2_ref, gf_ref,
                     o_ref, w1b_ref, w3b_ref, w2b_ref, h_ref, *, base, final_norm):
    j = pl.program_id(1)

    @pl.when(j == 0)
    def _():
        h_ref[...] = _mod_norm(x_ref[...], g_ref[...], mod_ref, base).astype(BF16)
        o_ref[...] = jnp.zeros_like(o_ref)

    w1 = w1_ref[0].astype(BF16)
    w3 = w3_ref[0].astype(BF16)
    w2 = w2_ref[0].astype(BF16)
    w1b_ref[0] = w1
    w3b_ref[0] = w3
    w2b_ref[...] = w2
    o_ref[...] += _swiglu_tile(h_ref[...], w1[None], w3[None], w2)

    @pl.when(j == pl.num_programs(1) - 1)
    def _():
        _ffn_finish(x_ref, mod_ref, gf_ref, o_ref, base=base, final_norm=final_norm)


def _ffn_kernel(x_ref, xn_ref, mod_ref, modn_ref, g_ref, w1_hbm, w3_hbm, w2_hbm, gf_ref, y0_ref,
                o_ref, h_ref, w1_buf, w3_buf, w2_buf, sem, *, base, final_norm, n_steps):
    i = pl.program_id(0)
    h_slot = i % 2
    first = (i - FFN_HEAD_TILES) * n_steps
    groups = FFN_TF // FFN_GROUP
    ahead = FFN_WSLOTS - 1

    def copies(j, slot):
        jj = j % n_steps
        return (pltpu.make_async_copy(w1_hbm.at[pl.ds(jj * groups, groups)], w1_buf.at[slot], sem.at[slot, 0]),
                pltpu.make_async_copy(w3_hbm.at[pl.ds(jj * groups, groups)], w3_buf.at[slot], sem.at[slot, 1]),
                pltpu.make_async_copy(w2_hbm.at[pl.ds(jj * FFN_TF, FFN_TF)], w2_buf.at[slot], sem.at[slot, 2]))

    @pl.when(i < FFN_HEAD_TILES)
    def _():
        o_ref[...] = y0_ref[...]

    @pl.when(i >= FFN_HEAD_TILES)
    def _():
        @pl.when(i == FFN_HEAD_TILES)
        def _():
            for j in range(ahead):
                for c in copies(j, j):
                    c.start()
            h_ref[h_slot] = _mod_norm(x_ref[...], g_ref[...], mod_ref, base).astype(BF16)

        for j in range(n_steps):
            slot = (first + j) % FFN_WSLOTS
            for c in copies(j, slot):
                c.wait()
            if j < FFN_AHEAD_CHUNKS:
                rows = slice(j * FFN_AHEAD_ROWS, (j + 1) * FFN_AHEAD_ROWS)
                h_next = _mod_norm(xn_ref[rows, :], g_ref[...], modn_ref, base).astype(BF16)
            upd = _swiglu_tile(h_ref[h_slot], w1_buf.at[slot], w3_buf.at[slot], w2_buf[slot])
            for c in copies(j + ahead, (first + j + ahead) % FFN_WSLOTS):
                c.start()
            if j == 0:
                o_ref[...] = upd
            else:
                o_ref[...] += upd
            if j < FFN_AHEAD_CHUNKS:
                h_ref[1 - h_slot, rows, :] = h_next
        _ffn_finish(x_ref, mod_ref, gf_ref, o_ref, base=base, final_norm=final_norm)

        @pl.when(i == pl.num_programs(0) - 1)
        def _():
            for j in range(n_steps, n_steps + ahead):
                for c in copies(j, (first + j) % FFN_WSLOTS):
                    c.wait()


def _ffn(x2d, mod3, g, w1, w3, w2, g_final, *, base, final_norm, seq):
    t, d = x2d.shape
    dff = w1.shape[2]
    tiles_per_batch = seq // FFN_TM
    const = lambda i, j: (0, 0)
    kw = dict(base=base, final_norm=final_norm)
    tag = "_final" if final_norm else ""
    common = [
        pl.BlockSpec((1, N_MOD, d), lambda i, j: (i // tiles_per_batch, 0, 0)),
        pl.BlockSpec((1, d), const),
    ]
    hf = FFN_HEAD_TF
    head_rows = FFN_HEAD_TILES * FFN_TM
    assert head_rows <= seq
    assert hf == FFN_GROUP
    w13_shape = jax.ShapeDtypeStruct((dff // FFN_GROUP, d, FFN_GROUP), BF16)
    w13_spec = pl.BlockSpec((1, d, FFN_GROUP), lambda i, j: (j, 0, 0))
    y0, w1b, w3b, w2b = pl.pallas_call(
        functools.partial(_ffn_head_kernel, **kw),
        out_shape=(jax.ShapeDtypeStruct((head_rows, d), F32), w13_shape, w13_shape,
                   jax.ShapeDtypeStruct((dff, d), BF16)),
        grid=(1, dff // hf),
        in_specs=[pl.BlockSpec((head_rows, d), const, pipeline_mode=pl.Buffered(1))] + common + [
            pl.BlockSpec((1, d, hf), lambda i, j: (0, 0, j)),
            pl.BlockSpec((1, d, hf), lambda i, j: (0, 0, j)),
            pl.BlockSpec((1, hf, d), lambda i, j: (0, j, 0)),
            pl.BlockSpec((1, d), const),
        ],
        out_specs=(pl.BlockSpec((head_rows, d), const), w13_spec, w13_spec,
                   pl.BlockSpec((hf, d), lambda i, j: (j, 0))),
        scratch_shapes=[pltpu.VMEM((head_rows, d), BF16)],
        compiler_params=_params(("arbitrary", "arbitrary")),
        name="ffn_head" + tag,
    )(x2d, mod3, g, w1, w3, w2, g_final)

    n_tiles = t // FFN_TM
    assert dff // FFN_TF >= FFN_AHEAD_CHUNKS and n_tiles > FFN_HEAD_TILES
    nxt = lambda i: jnp.minimum(i + 1, n_tiles - 1)
    groups = FFN_TF // FFN_GROUP
    hbm = pl.BlockSpec(memory_space=pl.ANY)
    return pl.pallas_call(
        functools.partial(_ffn_kernel, n_steps=dff // FFN_TF, **kw),
        out_shape=jax.ShapeDtypeStruct((t, d), F32),
        grid=(n_tiles,),
        in_specs=[
            pl.BlockSpec((FFN_TM, d), lambda i: (i, 0)),
            pl.BlockSpec((FFN_TM, d), lambda i: (nxt(i), 0)),
            pl.BlockSpec((1, N_MOD, d), lambda i: (i // tiles_per_batch, 0, 0)),
            pl.BlockSpec((1, N_MOD, d), lambda i: (nxt(i) // tiles_per_batch, 0, 0)),
            pl.BlockSpec((1, d), lambda i: (0, 0)),
            hbm, hbm, hbm,
            pl.BlockSpec((1, d), lambda i: (0, 0)),
            pl.BlockSpec((FFN_TM, d), lambda i: (jnp.minimum(i, FFN_HEAD_TILES - 1), 0),
                         pipeline_mode=pl.Buffered(1)),
        ],
        out_specs=pl.BlockSpec((FFN_TM, d), lambda i: (i, 0)),
        scratch_shapes=[pltpu.VMEM((2, FFN_TM, d), BF16),
                        pltpu.VMEM((FFN_WSLOTS, groups, d, FFN_GROUP), BF16),
                        pltpu.VMEM((FFN_WSLOTS, groups, d, FFN_GROUP), BF16),
                        pltpu.VMEM((FFN_WSLOTS, FFN_TF, d), BF16),
                        pltpu.SemaphoreType.DMA((FFN_WSLOTS, 3))],
        compiler_params=_params(("arbitrary",), FFN_VMEM_LIMIT),
        name="ffn" + tag,
    )(x2d, x2d, mod3, mod3, g, w1b, w3b, w2b, g_final, y0)


IN_SPLITS = (MLA_Q_RANK, MLA_KV_RANK, MLA_ROPE, GLA_KEY, GLA_KEY, GLA_VAL, GLA_GATE_RANK, GLA_VAL,
             2048, 2048)
MAIN_ORDER = (0, 1, 3, 4, 5, 7, 8, 9)
TAIL_ORDER = (2, 6)
RELAYOUT_ROWS = 1024


def _segments(order):
    starts = [sum(IN_SPLITS[:i]) for i in range(len(IN_SPLITS))]
    segs, dst = [], 0
    for i in order:
        src, n = starts[i], IN_SPLITS[i]
        if segs and segs[-1][0] + segs[-1][2] == src:
            segs[-1] = (segs[-1][0], segs[-1][1], segs[-1][2] + n)
        else:
            segs.append((src, dst, n))
        dst += n
    return segs, dst


def _relayout_kernel(lo_ref, hi_ref, kr_ref, glr_ref, main_ref, tail_ref):
    r = pl.program_id(0)
    rows = RELAYOUT_ROWS
    for src, dst, n in _segments(MAIN_ORDER)[0]:
        shift = src - dst

        @pl.when((r >= dst // rows) & (r < (dst + n) // rows))
        def _(shift=shift):
            main_ref[:rows - shift, :] = lo_ref[shift:, :].astype(BF16)
            if shift:
                main_ref[rows - shift:, :] = hi_ref[:shift, :].astype(BF16)

    tail_ref[:MLA_ROPE, :] = kr_ref[...].astype(BF16)
    tail_ref[MLA_ROPE:MLA_ROPE + GLA_GATE_RANK, :] = glr_ref[...].astype(BF16)
    pad = TAIL_W - MLA_ROPE - GLA_GATE_RANK
    tail_ref[MLA_ROPE + GLA_GATE_RANK:, :] = jnp.zeros((pad, tail_ref.shape[1]), BF16)


def _relayout_w_in(w_in_t):
    d_in, d = w_in_t.shape
    segs, n_main = _segments(MAIN_ORDER)
    rows = RELAYOUT_ROWS
    assert all(dst % rows == 0 and n % rows == 0 and 0 <= src - dst <= LANES and (src - dst) % 16 == 0
               for src, dst, n in segs)
    (kr_src, _, _), (glr_src, _, _) = _segments(TAIL_ORDER)[0]
    assert kr_src % MLA_ROPE == 0 and glr_src % GLA_GATE_RANK == 0
    return pl.pallas_call(
        _relayout_kernel,
        out_shape=(jax.ShapeDtypeStruct((n_main, d), BF16), jax.ShapeDtypeStruct((TAIL_W, d), BF16)),
        grid=(n_main // rows,),
        in_specs=[pl.BlockSpec((rows, d), lambda r: (r, 0)),
                  pl.BlockSpec((LANES, d), lambda r: ((r + 1) * (rows // LANES), 0)),
                  pl.BlockSpec((MLA_ROPE, d), lambda r: (kr_src // MLA_ROPE, 0)),
                  pl.BlockSpec((GLA_GATE_RANK, d), lambda r: (glr_src // GLA_GATE_RANK, 0))],
        out_specs=(pl.BlockSpec((rows, d), lambda r: (r, 0)),
                   pl.BlockSpec((TAIL_W, d), lambda r: (0, 0))),
        compiler_params=_params(("arbitrary",)),
        name="w_in_layout",
    )(w_in_t, w_in_t, w_in_t, w_in_t)


INP_TM = 1024
INP_TN = 2048


def _inproj_kernel(x_ref, mod_ref, g_ref, w_ref, wt_ref, o_ref, tail_ref, h_ref, *, base):
    @pl.when(pl.program_id(1) == 0)
    def _():
        h = _mod_norm(x_ref[...], g_ref[...], mod_ref, base).astype(BF16)
        h_ref[...] = h
        tail_ref[...] = _dot_t1(h, wt_ref[...])

    o_ref[...] = _dot_t1(h_ref[...], w_ref[...]).astype(BF16)


def _inproj(x2d, mod3, g, w_main, w_tail, *, base, seq):
    t, d = x2d.shape
    n = w_main.shape[0]
    tiles_per_batch = seq // INP_TM
    return pl.pallas_call(
        functools.partial(_inproj_kernel, base=base),
        out_shape=(jax.ShapeDtypeStruct((t, n), BF16),
                   jax.ShapeDtypeStruct((t, TAIL_W), F32)),
        grid=(t // INP_TM, n // INP_TN),
        in_specs=[
            pl.BlockSpec((INP_TM, d), lambda i, j: (i, 0)),
            pl.BlockSpec((1, N_MOD, d), lambda i, j: (i // tiles_per_batch, 0, 0)),
            pl.BlockSpec((1, d), lambda i, j: (0, 0)),
            pl.BlockSpec((INP_TN, d), lambda i, j: (j, 0)),
            pl.BlockSpec((TAIL_W, d), lambda i, j: (0, 0)),
        ],
        out_specs=(pl.BlockSpec((INP_TM, INP_TN), lambda i, j: (i, j)),
                   pl.BlockSpec((INP_TM, TAIL_W), lambda i, j: (i, 0))),
        scratch_shapes=[pltpu.VMEM((INP_TM, d), BF16)],
        compiler_params=_params(("parallel", "arbitrary")),
        name="in_proj",
    )(x2d, mod3, g, w_main, w_tail)


PREP_TM = 1024
ROPE_HALF = MLA_ROPE // 2


def _rope_rotate(x, cos, sin_signed):
    width = x.shape[-1]
    lane = lax.broadcasted_iota(jnp.int32, x.shape, 1)
    first_half = (lane % MLA_ROPE) < ROPE_HALF
    partner = jnp.where(first_half,
                        pltpu.roll(x, width - ROPE_HALF, 1),
                        pltpu.roll(x, ROPE_HALF, 1))
    return x * cos + partner * sin_signed


def _prep_kernel(qlat_ref, kvlat_ref, tail_ref, pos_ref, invf_ref, sgn_ref,
                 gql_ref, wuq_ref, gqn_ref, gqr_ref, gkvl_ref, wukv_ref, gkn_ref, gkr_ref,
                 grp_ref, q_ref, k_ref, v_ref):
    ang_t = invf_ref[...] * pos_ref[0].astype(F32)
    lane_reps = LANES // ROPE_HALF
    cos = jnp.tile(jnp.cos(ang_t).T, (1, lane_reps))
    sin_signed = jnp.tile(jnp.sin(ang_t).T, (1, lane_reps)) * sgn_ref[...]
    n_rope = MLA_HEADS * MLA_ROPE
    cos_q = jnp.concatenate([cos] * (n_rope // LANES), axis=-1)
    sin_q = jnp.concatenate([sin_signed] * (n_rope // LANES), axis=-1)

    cq = _rms(qlat_ref[...].astype(F32), gql_ref[...]).astype(BF16)
    ckv = _rms(kvlat_ref[...].astype(F32), gkvl_ref[...]).astype(BF16)
    n_nope = MLA_HEADS * MLA_NOPE
    pair = 2 * MLA_NOPE

    for hp in range(MLA_HEADS // 2):
        cols = slice(hp * pair, (hp + 1) * pair)
        qp = _dot(cq, wuq_ref[:, cols])
        kp = _dot(ckv, wukv_ref[:, cols])
        for hh in range(2):
            h = 2 * hp + hh
            sub = slice(hh * MLA_NOPE, (hh + 1) * MLA_NOPE)
            q_ref[0, h, :, :MLA_NOPE] = _rms(qp[:, sub], gqn_ref[...]).astype(BF16)
            k_ref[0, h, :, :MLA_NOPE] = _rms(kp[:, sub], gkn_ref[...]).astype(BF16)
        v_ref[0, :, cols] = _dot(ckv, wukv_ref[:, n_nope + hp * pair:n_nope + (hp + 1) * pair]).astype(BF16)

    qr = _dot(cq, wuq_ref[:, n_nope:])
    ssq = _split_dot(qr * qr, grp_ref[...])
    qr = qr * lax.rsqrt(ssq * (1.0 / MLA_ROPE) + EPS) * gqr_ref[...]
    qr = _rope_rotate(qr, cos_q, sin_q).astype(BF16)
    kr = tail_ref[:, TAIL_KROPE:TAIL_KROPE + MLA_ROPE]
    kr = _rms(kr, gkr_ref[...])
    kr = _rope_rotate(kr, cos[:, :MLA_ROPE], sin_signed[:, :MLA_ROPE]).astype(BF16)
    for h in range(MLA_HEADS):
        q_ref[0, h, :, MLA_NOPE:] = qr[:, h * MLA_ROPE:(h + 1) * MLA_ROPE]
        k_ref[0, h, :, MLA_NOPE:] = kr


def _mla_prep(proj, tail, pos3d, consts, weights, *, batch, seq):
    t = proj.shape[0]
    tiles_per_batch = seq // PREP_TM
    row = lambda i: (i, 0)
    const = lambda i: (0, 0)
    (invf, sgn, grp) = consts
    (gql, wuq, gqn, gqr, gkvl, wukv, gkn, gkr) = weights
    head_out = pl.BlockSpec((1, MLA_HEADS, PREP_TM, MLA_QK),
                            lambda i: (i // tiles_per_batch, 0, i % tiles_per_batch, 0))
    full = lambda a: pl.BlockSpec(a.shape, const)
    return pl.pallas_call(
        _prep_kernel,
        out_shape=(jax.ShapeDtypeStruct((batch, MLA_HEADS, seq, MLA_QK), BF16),
                   jax.ShapeDtypeStruct((batch, MLA_HEADS, seq, MLA_QK), BF16),
                   jax.ShapeDtypeStruct((batch, seq, MLA_HEADS * MLA_V), BF16)),
        grid=(t // PREP_TM,),
        in_specs=[
            pl.BlockSpec((PREP_TM, MLA_Q_RANK), lambda i: (i, COL_QLAT // MLA_Q_RANK)),
            pl.BlockSpec((PREP_TM, MLA_KV_RANK), lambda i: (i, COL_KVLAT // MLA_KV_RANK)),
            pl.BlockSpec((PREP_TM, TAIL_W), row),
            pl.BlockSpec((1, 1, PREP_TM), lambda i: (i, 0, 0)),
            full(invf), full(sgn),
            full(gql), full(wuq), full(gqn), full(gqr),
            full(gkvl), full(wukv), full(gkn), full(gkr),
            full(grp),
        ],
        out_specs=(head_out, head_out,
                   pl.BlockSpec((1, PREP_TM, MLA_HEADS * MLA_V),
                                lambda i: (i // tiles_per_batch, i % tiles_per_batch, 0))),
        compiler_params=_params(("parallel",)),
        name="mla_prep",
    )(proj, proj, tail, pos3d, invf, sgn, gql, wuq, gqn, gqr, gkvl, wukv, gkn, gkr, grp)


ATT_T = 1024
ATT_H = ATT_T // 2


def _attn_kernel(q_ref, k_ref, v_ref, o_ref):
    n_tiles = k_ref.shape[2] // ATT_T
    ones = jnp.ones((ATT_T, MLA_V), BF16)
    r = lax.broadcasted_iota(jnp.int32, (ATT_H, ATT_H), 0)
    c = lax.broadcasted_iota(jnp.int32, (ATT_H, ATT_H), 1)
    visible = (c // CHUNK) <= (r // CHUNK)
    state = {}

    def update(chain, s, v2):
        m_cur = jnp.max(s, axis=-1, keepdims=True)
        if chain not in state:
            m_next = jnp.broadcast_to(m_cur, (ATT_H, LANES))
        else:
            m_prev, l_prev, acc_prev = state[chain]
            m_next = jnp.maximum(m_prev, m_cur)
            alpha = jnp.exp2(m_prev - m_next)
        p = jnp.exp2(s - jnp.tile(m_next, (1, s.shape[1] // LANES)))
        pv = _dot(p.astype(BF16), v2)
        acc, l = pv[:, :MLA_V], pv[:, MLA_V:]
        if chain in state:
            acc, l = alpha * acc_prev + acc, alpha * l_prev + l
        state[chain] = (m_next, l, acc)

    for t in range(n_tiles):
        rows = slice(t * ATT_T, (t + 1) * ATT_T)
        k = k_ref[0, 0, rows, :]
        v2 = jnp.concatenate([v_ref[0, rows, :], ones], axis=-1)
        for qt in range(t, n_tiles):
            q0 = q_ref[0, 0, qt * ATT_T:qt * ATT_T + ATT_H, :]
            q1 = q_ref[0, 0, qt * ATT_T + ATT_H:(qt + 1) * ATT_T, :]
            if t < qt:
                update((qt, 0), _dot_t1(q0, k), v2)
                update((qt, 1), _dot_t1(q1, k), v2)
            else:
                update((qt, 0), jnp.where(visible, _dot_t1(q0, k[:ATT_H]), -jnp.inf), v2[:ATT_H])
                s1 = _dot_t1(q1, k)
                s1 = jnp.concatenate([s1[:, :ATT_H], jnp.where(visible, s1[:, ATT_H:], -jnp.inf)],
                                     axis=-1)
                update((qt, 1), s1, v2)
                for half in range(2):
                    _, l, acc = state[(qt, half)]
                    lo = qt * ATT_T + half * ATT_H
                    o_ref[0, lo:lo + ATT_H, :] = (acc / l).astype(BF16)


def _mla_attn(q, k, v):
    batch, heads, seq, _ = q.shape
    return pl.pallas_call(
        _attn_kernel,
        out_shape=jax.ShapeDtypeStruct((batch, seq, heads * MLA_V), BF16),
        grid=(batch, heads),
        in_specs=[
            pl.BlockSpec((1, 1, seq, MLA_QK), lambda b, h: (b, h, 0, 0)),
            pl.BlockSpec((1, 1, seq, MLA_QK), lambda b, h: (b, h, 0, 0)),
            pl.BlockSpec((1, seq, MLA_V), lambda b, h: (b, 0, h)),
        ],
        out_specs=pl.BlockSpec((1, seq, MLA_V), lambda b, h: (b, 0, h)),
        compiler_params=_params(("parallel", "parallel")),
        name="mla_attn",
    )(q, k, v)


GLA_L = 1024
GLA_NC = GLA_L // CHUNK
GLA_SUB = 512


def _gla_head(log_a, q, k, v, go, gg, state_ref, causal, tri):
    sub_nc = GLA_SUB // CHUNK
    q_dec, k_end, o_intra, b_ends = [], [], [], []
    for sb in range(GLA_L // GLA_SUB):
        rows = slice(sb * GLA_SUB, (sb + 1) * GLA_SUB)
        la = log_a[rows]
        la_hi = la.astype(BF16)
        la_lo = (la - la_hi.astype(F32)).astype(BF16)
        cum = _dot(tri, jnp.concatenate([la_hi, la_lo], axis=-1))
        b_cum = cum[:, :GLA_DK] + cum[:, GLA_DK:]
        b3 = b_cum.reshape(sub_nc, CHUNK, GLA_DK)
        b_end = b3[:, CHUNK - 1:CHUNK, :]
        b_tot = jnp.broadcast_to(b_end, b3.shape).reshape(GLA_SUB, GLA_DK)
        b_ends.append(jnp.exp(b_end))

        qf = q[rows].astype(F32) * (GLA_DK ** -0.5)
        kf = k[rows].astype(F32)
        qd = (qf * jnp.exp(b_cum)).astype(BF16)
        kd = (kf * jnp.exp(-b_cum)).astype(BF16)
        q_dec.append(qd)
        k_end.append((kf * jnp.exp(b_tot - b_cum)).astype(BF16))
        attn = jnp.where(causal, _dot_t1(qd, kd), 0.0).astype(BF16)
        o_intra.append(_dot(attn, v[rows]))

    state = state_ref[...]
    outs = []
    for n in range(GLA_NC):
        sb, rows = n // sub_nc, slice((n % sub_nc) * CHUNK, (n % sub_nc + 1) * CHUNK)
        outs.append(o_intra[sb][rows] + _dot_t1(q_dec[sb][rows], state.astype(BF16)))
        v_n = v[n * CHUNK:(n + 1) * CHUNK]
        state = state * b_ends[sb][n % sub_nc] + _dot_t0(v_n, k_end[sb][rows])
    state_ref[...] = state

    o = jnp.concatenate(outs, axis=0)
    return (_rms(o, gg) * _silu(go.astype(F32))).astype(BF16)


def _gla_kernel(q_ref, k_ref, v_ref, go_ref, tail_ref, wg_ref, bg_ref, gg_ref,
                o_ref, state_ref):
    @pl.when(pl.program_id(1) == 0)
    def _():
        state_ref[...] = jnp.zeros_like(state_ref)

    wg = wg_ref[...]
    wg_hi = wg.astype(BF16)
    wg_lo = (wg - wg_hi.astype(F32)).astype(BF16)
    tl = tail_ref[...]
    tl_hi = tl.astype(BF16)
    tl_lo = (tl - tl_hi.astype(F32)).astype(BF16)
    z = _dot(tl_hi, wg_hi) + _dot(tl_lo, wg_hi) + _dot(tl_hi, wg_lo) + bg_ref[...]
    log_a = (jnp.minimum(z, 0.0) - jnp.log1p(jnp.exp(-jnp.abs(z)))) * (1.0 / GLA_GATE_NORMALIZER)

    r = lax.broadcasted_iota(jnp.int32, (GLA_SUB, GLA_SUB), 0)
    c = lax.broadcasted_iota(jnp.int32, (GLA_SUB, GLA_SUB), 1)
    causal = ((r // CHUNK) == (c // CHUNK)) & (c <= r)
    tri = causal.astype(BF16)

    for hh in range(GLA_HEADS):
        dk = slice(hh * GLA_DK, (hh + 1) * GLA_DK)
        dv = slice(hh * GLA_DV, (hh + 1) * GLA_DV)
        o_ref[:, dv] = _gla_head(log_a[:, dk], q_ref[:, dk], k_ref[:, dk], v_ref[:, dv],
                                 go_ref[:, dv], gg_ref[...], state_ref.at[hh], causal, tri)


def _gla(proj, tail, wg_pad, b_gk, g_gla, *, batch, seq):
    t = proj.shape[0]
    steps = seq // GLA_L
    tok = lambda b, i: b * steps + i
    return pl.pallas_call(
        _gla_kernel,
        out_shape=jax.ShapeDtypeStruct((t, GLA_VAL), BF16),
        grid=(batch, steps),
        in_specs=[
            pl.BlockSpec((GLA_L, GLA_KEY), lambda b, i: (tok(b, i), COL_GQ // GLA_KEY)),
            pl.BlockSpec((GLA_L, GLA_KEY), lambda b, i: (tok(b, i), COL_GK // GLA_KEY)),
            pl.BlockSpec((GLA_L, GLA_VAL), lambda b, i: (tok(b, i), COL_GV // GLA_VAL)),
            pl.BlockSpec((GLA_L, GLA_VAL), lambda b, i: (tok(b, i), COL_GOUT // GLA_VAL)),
            pl.BlockSpec((GLA_L, TAIL_W), lambda b, i: (tok(b, i), 0)),
            pl.BlockSpec((TAIL_W, GLA_KEY), lambda b, i: (0, 0)),
            pl.BlockSpec((1, GLA_KEY), lambda b, i: (0, 0)),
            pl.BlockSpec((1, GLA_DV), lambda b, i: (0, 0)),
        ],
        out_specs=pl.BlockSpec((GLA_L, GLA_VAL), lambda b, i: (tok(b, i), 0)),
        scratch_shapes=[pltpu.VMEM((GLA_HEADS, GLA_DV, GLA_DK), F32)],
        compiler_params=_params(("parallel", "arbitrary")),
        name="gla",
    )(proj, proj, proj, proj, tail, wg_pad, b_gk, g_gla)


MRG_TM = 512


def _merge_kernel(x_ref, mod_ref, a_ref, b_ref, ga_ref, gb_ref, wa_ref, wb_ref, wo_ref, o_ref,
                  *, base):
    ya = _dot(a_ref[...], wa_ref[...])
    yb = _dot(b_ref[...], wb_ref[...])
    merged = (jax.nn.sigmoid(ga_ref[...].astype(F32)) * ya
              + jax.nn.sigmoid(gb_ref[...].astype(F32)) * yb).astype(BF16)
    gate = mod_ref[0, base + 2:base + 3, :]
    o_ref[...] = x_ref[...] + gate * _dot(merged, wo_ref[...])


def _merge(x2d, mod3, attn_o, gla_o, proj, wa, wb, wo, *, base, seq):
    t, d = x2d.shape
    tiles_per_batch = seq // MRG_TM
    row = lambda i: (i, 0)
    resident = lambda a: pl.BlockSpec(a.shape, lambda i: (0, 0), pipeline_mode=pl.Buffered(1))
    return pl.pallas_call(
        functools.partial(_merge_kernel, base=base),
        out_shape=jax.ShapeDtypeStruct((t, d), F32),
        grid=(t // MRG_TM,),
        in_specs=[
            pl.BlockSpec((MRG_TM, d), row),
            pl.BlockSpec((1, N_MOD, d), lambda i: (i // tiles_per_batch, 0, 0)),
            pl.BlockSpec((MRG_TM, attn_o.shape[1]), row),
            pl.BlockSpec((MRG_TM, gla_o.shape[1]), row),
            pl.BlockSpec((MRG_TM, d), lambda i: (i, COL_GATE_A // d)),
            pl.BlockSpec((MRG_TM, d), lambda i: (i, COL_GATE_A // d + 1)),
            resident(wa), resident(wb), resident(wo),
        ],
        out_specs=pl.BlockSpec((MRG_TM, d), row),
        compiler_params=_params(("parallel",)),
        name="merge",
    )(x2d, mod3, attn_o, gla_o, proj, proj, wa, wb, wo)


def _layer(x2d, mod3, pos3d, p, *, batch, seq):
    d = x2d.shape[1]
    assert IN_SPLITS[-1] == d and sum(IN_SPLITS) == p["w_in"].shape[1]
    w_main, w_tail = _relayout_w_in(p["w_in"].T)
    wuq = p["w_uq"].reshape(MLA_Q_RANK, MLA_HEADS, MLA_QK)
    wuq = jnp.concatenate([wuq[:, :, :MLA_NOPE].reshape(MLA_Q_RANK, -1),
                           wuq[:, :, MLA_NOPE:].reshape(MLA_Q_RANK, -1)], axis=1).astype(BF16)
    wukv = p["w_ukv"].reshape(MLA_KV_RANK, MLA_HEADS, MLA_NOPE + MLA_V)
    wukv = jnp.concatenate([wukv[:, :, :MLA_NOPE].reshape(MLA_KV_RANK, -1),
                            wukv[:, :, MLA_NOPE:].reshape(MLA_KV_RANK, -1)], axis=1).astype(BF16)
    wg_pad = jnp.zeros((TAIL_W, GLA_KEY), F32).at[TAIL_GLR:TAIL_GLR + GLA_GATE_RANK].set(p["w_gk_up"])

    inv_freq = ROPE_THETA ** (-jnp.arange(0, MLA_ROPE, 2, dtype=F32) / MLA_ROPE)
    invf = inv_freq.reshape(ROPE_HALF, 1)
    sgn = jnp.tile(jnp.concatenate([-jnp.ones(ROPE_HALF, F32), jnp.ones(ROPE_HALF, F32)]),
                   LANES // MLA_ROPE).reshape(1, LANES)
    n_rope = MLA_HEADS * MLA_ROPE
    lane = jnp.arange(n_rope)
    grp = (lane[:, None] // MLA_ROPE == lane[None, :] // MLA_ROPE).astype(BF16)

    row = lambda a: a.reshape(1, -1)
    q_scale = MLA_QK ** -0.5 * LOG2E
    x2d = _ffn(x2d, mod3, row(p["g_ffn1"]), p["w1_a"][None], p["w3_a"][None],
               p["w2_a"][None], row(p["g_final"]), base=0, final_norm=False, seq=seq)
    proj, tail = _inproj(x2d, mod3, row(p["g_mix"]), w_main, w_tail, base=3, seq=seq)
    q, k, v = _mla_prep(
        proj, tail, pos3d, (invf, sgn, grp),
        (row(p["g_q_lat"]), wuq, row(p["g_qn"] * q_scale),
         row(jnp.tile(p["g_qr"], MLA_HEADS) * q_scale),
         row(p["g_kv_lat"]), wukv, row(p["g_kn"]), row(p["g_kr"])),
        batch=batch, seq=seq)
    attn_o = _mla_attn(q, k, v).reshape(batch * seq, MLA_HEADS * MLA_V)
    gla_o = _gla(proj, tail, wg_pad, row(p["b_gk"]), row(p["g_gla"]), batch=batch, seq=seq)
    x2d = _merge(x2d, mod3, attn_o, gla_o, proj, p["w_proj_a"].astype(BF16),
                 p["w_proj_b"].astype(BF16), p["w_out"].astype(BF16), base=3, seq=seq)
    x2d = _ffn(x2d, mod3, row(p["g_ffn2"]), p["w1_b"][None], p["w3_b"][None],
               p["w2_b"][None], row(p["g_final"]), base=6, final_norm=True, seq=seq)
    return x2d


def kernel(x, c, positions, w_ada, b_ada, g_ffn1, w1_a, w3_a, w2_a, g_mix, w_in, g_q_lat, w_uq, g_qn, g_qr, g_kv_lat, w_ukv, g_kn, g_kr, w_gk_up, b_gk, g_gla, w_proj_a, w_proj_b, w_out, g_ffn2, w1_b, w3_b, w2_b, g_final):
    batch, seq, d = x.shape
    depth = w_ada.shape[0]
    assert depth == 1, "the final norm is fused into the last FFN of a single layer"
    names = ("g_ffn1", "w1_a", "w3_a", "w2_a", "g_mix", "w_in", "g_q_lat", "w_uq", "g_qn", "g_qr",
             "g_kv_lat", "w_ukv", "g_kn", "g_kr", "w_gk_up", "b_gk", "g_gla", "w_proj_a", "w_proj_b",
             "w_out", "g_ffn2", "w1_b", "w3_b", "w2_b", "g_final")
    stacked = (g_ffn1, w1_a, w3_a, w2_a, g_mix, w_in, g_q_lat, w_uq, g_qn, g_qr, g_kv_lat, w_ukv,
               g_kn, g_kr, w_gk_up, b_gk, g_gla, w_proj_a, w_proj_b, w_out, g_ffn2, w1_b, w3_b, w2_b,
               g_final)
    x2d = x.reshape(batch * seq, d)
    pos3d = positions.reshape(batch * seq // PREP_TM, 1, PREP_TM)
    p = {n: a[0] for n, a in zip(names, stacked)}
    mod3 = _adaln(c, w_ada[0], b_ada[0]).reshape(batch, N_MOD, d)
    x2d = _layer(x2d, mod3, pos3d, p, batch=batch, seq=seq)
    return x2d.reshape(batch, seq, d)
```

```python
import functools

import jax
import jax.numpy as jnp
from jax import lax
from jax.experimental import pallas as pl
from jax.experimental.pallas import tpu as pltpu

F32 = jnp.float32
BF16 = jnp.bfloat16

EPS = 1e-6
LOG2E = 1.4426950408889634
CHUNK = 64
MLA_HEADS = 8
MLA_Q_RANK = 512
MLA_KV_RANK = 512
MLA_NOPE = 128
MLA_ROPE = 64
MLA_V = 128
MLA_QK = MLA_NOPE + MLA_ROPE
ROPE_THETA = 10000.0
GLA_HEADS = 4
GLA_DK = 128
GLA_DV = 256
GLA_GATE_RANK = 16
GLA_GATE_NORMALIZER = 16.0
GLA_KEY = GLA_HEADS * GLA_DK
GLA_VAL = GLA_HEADS * GLA_DV
N_MOD = 9

LANES = 128
SUBLANES = 8
VMEM_LIMIT = 56 * 1024 * 1024

COL_QLAT = 0
COL_KVLAT = COL_QLAT + MLA_Q_RANK
COL_GQ = COL_KVLAT + MLA_KV_RANK
COL_GK = COL_GQ + GLA_KEY
COL_GV = COL_GK + GLA_KEY
COL_GOUT = COL_GV + GLA_VAL
COL_GATE_A = COL_GOUT + GLA_VAL
TAIL_W = LANES
TAIL_KROPE = 0
TAIL_GLR = MLA_ROPE


def _params(sem, vmem_limit=VMEM_LIMIT):
    return pltpu.CompilerParams(dimension_semantics=sem, vmem_limit_bytes=vmem_limit)


def _dot(a, b):
    return jnp.dot(a, b, preferred_element_type=F32)


def _dot_t0(a, b):
    return lax.dot_general(a, b, (((0,), (0,)), ((), ())), preferred_element_type=F32)


def _dot_t1(a, b):
    return lax.dot_general(a, b, (((1,), (1,)), ((), ())), preferred_element_type=F32)


def _split_dot(x, w_bf16):
    hi = x.astype(BF16)
    lo = (x - hi.astype(F32)).astype(BF16)
    return _dot(hi, w_bf16) + _dot(lo, w_bf16)


def _silu(x):
    return x * jax.nn.sigmoid(x)


def _rms(x, g):
    ms = jnp.mean(x * x, axis=-1, keepdims=True)
    return x * lax.rsqrt(ms + EPS) * g


ADA_TN = 2048
ADA_RC = 64


def _adaln_kernel(ct_ref, w_ref, b_ref, o_ref, s_ref):
    d, nb = ct_ref.shape
    tn = w_ref.shape[1]

    @pl.when(pl.program_id(0) == 0)
    def _():
        s = _silu(ct_ref[...])
        for b in range(nb):
            s_ref[b] = jnp.broadcast_to(s[:, b:b + 1], (d, LANES))

    def body(k, acc):
        rows = pl.ds(pl.multiple_of(k * ADA_RC, ADA_RC), ADA_RC)
        out = []
        for b in range(nb):
            sb = s_ref[b, rows, :]
            cols = []
            for col in range(tn // LANES):
                p = w_ref[rows, col * LANES:(col + 1) * LANES] * sb
                cols.append(p.reshape(ADA_RC // SUBLANES, SUBLANES, LANES).sum(axis=0))
            out.append(acc[b] + jnp.concatenate(cols, axis=-1))
        return tuple(out)

    init = tuple(jnp.zeros((SUBLANES, tn), F32) for _ in range(nb))
    acc = lax.fori_loop(0, d // ADA_RC, body, init, unroll=2)
    rows = [jnp.sum(a, axis=0, keepdims=True) for a in acc]
    o_ref[...] = jnp.concatenate(rows, axis=0) + b_ref[...]


def _adaln(c, w_ada, b_ada):
    nb, d = c.shape
    n = w_ada.shape[1]
    return pl.pallas_call(
        _adaln_kernel,
        out_shape=jax.ShapeDtypeStruct((nb, n), F32),
        grid=(n // ADA_TN,),
        in_specs=[
            pl.BlockSpec((d, nb), lambda j: (0, 0)),
            pl.BlockSpec((d, ADA_TN), lambda j: (0, j)),
            pl.BlockSpec((1, ADA_TN), lambda j: (0, j)),
        ],
        out_specs=pl.BlockSpec((nb, ADA_TN), lambda j: (0, j)),
        scratch_shapes=[pltpu.VMEM((nb, d, LANES), F32)],
        compiler_params=_params(("arbitrary",)),
        name="adaln",
    )(c.T, w_ada, b_ada.reshape(1, n))


def _mod_norm(x, g, mod_ref, base):
    shift = mod_ref[0, base:base + 1, :]
    scale = mod_ref[0, base + 1:base + 2, :]
    return _rms(x, g) * (1.0 + scale) + shift


FFN_TM = 512
FFN_TF = 512
FFN_HEAD_TILES = 2
FFN_HEAD_TF = 256
FFN_GROUP = 256
FFN_WSLOTS = 3
FFN_VMEM_LIMIT = 60 * 1024 * 1024
FFN_AHEAD_CHUNKS = 8
FFN_AHEAD_ROWS = FFN_TM // FFN_AHEAD_CHUNKS


def _swiglu_tile(h, w1g, w3g, w2t):
    upd = None
    for s in range(w1g.shape[0]):
        a = _dot(h, w1g[s])
        b = _dot(h, w3g[s])
        p = (_silu(a) * b).astype(BF16)
        part = _dot(p, w2t[s * FFN_GROUP:(s + 1) * FFN_GROUP, :])
        upd = part if upd is None else upd + part
    return upd


def _ffn_finish(x_ref, mod_ref, gf_ref, o_ref, *, base, final_norm):
    gate = mod_ref[0, base + 2:base + 3, :]
    y = x_ref[...] + 0.5 * gate * o_ref[...]
    if final_norm:
        y = _rms(y, gf_ref[...])
    o_ref[...] = y


def _ffn_head_kernel(x_ref, mod_ref, g_ref, w1_ref, w3_ref, w2_ref, gf_ref,
                     o_ref, w1b_ref, w3b_ref, w2b_ref, h_ref, *, base, final_norm):
    j = pl.program_id(1)

    @pl.when(j == 0)
    def _():
        h_ref[...] = _mod_norm(x_ref[...], g_ref[...], mod_ref, base).astype(BF16)
        o_ref[...] = jnp.zeros_like(o_ref)

    w1 = w1_ref[0].astype(BF16)
    w3 = w3_ref[0].astype(BF16)
    w2 = w2_ref[0].astype(BF16)
    w1b_ref[0] = w1
    w3b_ref[0] = w3
    w2b_ref[...] = w2
    o_ref[...] += _swiglu_tile(h_ref[...], w1[None], w3[None], w2)

    @pl.when(j == pl.num_programs(1) - 1)
    def _():
        _ffn_finish(x_ref, mod_ref, gf_ref, o_ref, base=base, final_norm=final_norm)


def _ffn_kernel(x_ref, xn_ref, mod_ref, modn_ref, g_ref, w1_hbm, w3_hbm, w2_hbm, gf_ref, y0_ref,
                o_ref, h_ref, w1_buf, w3_buf, w2_buf, sem, *, base, final_norm, n_steps):
    i = pl.program_id(0)
    h_slot = i % 2
    first = (i - FFN_HEAD_TILES) * n_steps
    groups = FFN_TF // FFN_GROUP
    ahead = FFN_WSLOTS - 1

    def copies(j, slot):
        jj = j % n_steps
        return (pltpu.make_async_copy(w1_hbm.at[pl.ds(jj * groups, groups)], w1_buf.at[slot], sem.at[slot, 0]),
                pltpu.make_async_copy(w3_hbm.at[pl.ds(jj * groups, groups)], w3_buf.at[slot], sem.at[slot, 1]),
                pltpu.make_async_copy(w2_hbm.at[pl.ds(jj * FFN_TF, FFN_TF)], w2_buf.at[slot], sem.at[slot, 2]))

    @pl.when(i < FFN_HEAD_TILES)
    def _():
        o_ref[...] = y0_ref[...]

    @pl.when(i >= FFN_HEAD_TILES)
    def _():
        @pl.when(i == FFN_HEAD_TILES)
        def _():
            for j in range(ahead):
                for c in copies(j, j):
                    c.start()
            h_ref[h_slot] = _mod_norm(x_ref[...], g_ref[...], mod_ref, base).astype(BF16)

        for j in range(n_steps):
            slot = (first + j) % FFN_WSLOTS
            for c in copies(j, slot):
                c.wait()
            if j < FFN_AHEAD_CHUNKS:
                rows = slice(j * FFN_AHEAD_ROWS, (j + 1) * FFN_AHEAD_ROWS)
                h_next = _mod_norm(xn_ref[rows, :], g_ref[...], modn_ref, base).astype(BF16)
            upd = _swiglu_tile(h_ref[h_slot], w1_buf.at[slot], w3_buf.at[slot], w2_buf[slot])
            for c in copies(j + ahead, (first + j + ahead) % FFN_WSLOTS):
                c.start()
            if j == 0:
                o_ref[...] = upd
            else:
                o_ref[...] += upd
            if j < FFN_AHEAD_CHUNKS:
                h_ref[1 - h_slot, rows, :] = h_next
        _ffn_finish(x_ref, mod_ref, gf_ref, o_ref, base=base, final_norm=final_norm)

        @pl.when(i == pl.num_programs(0) - 1)
        def _():
            for j in range(n_steps, n_steps + ahead):
                for c in copies(j, (first + j) % FFN_WSLOTS):
                    c.wait()


def _ffn(x2d, mod3, g, w1, w3, w2, g_final, *, base, final_norm, seq):
    t, d = x2d.shape
    dff = w1.shape[2]
    tiles_per_batch = seq // FFN_TM
    const = lambda i, j: (0, 0)
    kw = dict(base=base, final_norm=final_norm)
    tag = "_final" if final_norm else ""
    common = [
        pl.BlockSpec((1, N_MOD, d), lambda i, j: (i // tiles_per_batch, 0, 0)),
        pl.BlockSpec((1, d), const),
    ]
    hf = FFN_HEAD_TF
    head_rows = FFN_HEAD_TILES * FFN_TM
    assert head_rows <= seq
    assert hf == FFN_GROUP
    w13_shape = jax.ShapeDtypeStruct((dff // FFN_GROUP, d, FFN_GROUP), BF16)
    w13_spec = pl.BlockSpec((1, d, FFN_GROUP), lambda i, j: (j, 0, 0))
    y0, w1b, w3b, w2b = pl.pallas_call(
        functools.partial(_ffn_head_kernel, **kw),
        out_shape=(jax.ShapeDtypeStruct((head_rows, d), F32), w13_shape, w13_shape,
                   jax.ShapeDtypeStruct((dff, d), BF16)),
        grid=(1, dff // hf),
        in_specs=[pl.BlockSpec((head_rows, d), const, pipeline_mode=pl.Buffered(1))] + common + [
            pl.BlockSpec((1, d, hf), lambda i, j: (0, 0, j)),
            pl.BlockSpec((1, d, hf), lambda i, j: (0, 0, j)),
            pl.BlockSpec((1, hf, d), lambda i, j: (0, j, 0)),
            pl.BlockSpec((1, d), const),
        ],
        out_specs=(pl.BlockSpec((head_rows, d), const), w13_spec, w13_spec,
                   pl.BlockSpec((hf, d), lambda i, j: (j, 0))),
        scratch_shapes=[pltpu.VMEM((head_rows, d), BF16)],
        compiler_params=_params(("arbitrary", "arbitrary")),
        name="ffn_head" + tag,
    )(x2d, mod3, g, w1, w3, w2, g_final)

    n_tiles = t // FFN_TM
    assert dff // FFN_TF >= FFN_AHEAD_CHUNKS and n_tiles > FFN_HEAD_TILES
    nxt = lambda i: jnp.minimum(i + 1, n_tiles - 1)
    groups = FFN_TF // FFN_GROUP
    hbm = pl.BlockSpec(memory_space=pl.ANY)
    return pl.pallas_call(
        functools.partial(_ffn_kernel, n_steps=dff // FFN_TF, **kw),
        out_shape=jax.ShapeDtypeStruct((t, d), F32),
        grid=(n_tiles,),
        in_specs=[
            pl.BlockSpec((FFN_TM, d), lambda i: (i, 0)),
            pl.BlockSpec((FFN_TM, d), lambda i: (nxt(i), 0)),
            pl.BlockSpec((1, N_MOD, d), lambda i: (i // tiles_per_batch, 0, 0)),
            pl.BlockSpec((1, N_MOD, d), lambda i: (nxt(i) // tiles_per_batch, 0, 0)),
            pl.BlockSpec((1, d), lambda i: (0, 0)),
            hbm, hbm, hbm,
            pl.BlockSpec((1, d), lambda i: (0, 0)),
            pl.BlockSpec((FFN_TM, d), lambda i: (jnp.minimum(i, FFN_HEAD_TILES - 1), 0),
                         pipeline_mode=pl.Buffered(1)),
        ],
        out_specs=pl.BlockSpec((FFN_TM, d), lambda i: (i, 0)),
        scratch_shapes=[pltpu.VMEM((2, FFN_TM, d), BF16),
                        pltpu.VMEM((FFN_WSLOTS, groups, d, FFN_GROUP), BF16),
                        pltpu.VMEM((FFN_WSLOTS, groups, d, FFN_GROUP), BF16),
                        pltpu.VMEM((FFN_WSLOTS, FFN_TF, d), BF16),
                        pltpu.SemaphoreType.DMA((FFN_WSLOTS, 3))],
        compiler_params=_params(("arbitrary",), FFN_VMEM_LIMIT),
        name="ffn" + tag,
    )(x2d, x2d, mod3, mod3, g, w1b, w3b, w2b, g_final, y0)


IN_SPLITS = (MLA_Q_RANK, MLA_KV_RANK, MLA_ROPE, GLA_KEY, GLA_KEY, GLA_VAL, GLA_GATE_RANK, GLA_VAL,
             2048, 2048)
MAIN_ORDER = (0, 1, 3, 4, 5, 7, 8, 9)
TAIL_ORDER = (2, 6)
RELAYOUT_ROWS = 1024


def _segments(order):
    starts = [sum(IN_SPLITS[:i]) for i in range(len(IN_SPLITS))]
    segs, dst = [], 0
    for i in order:
        src, n = starts[i], IN_SPLITS[i]
        if segs and segs[-1][0] + segs[-1][2] == src:
            segs[-1] = (segs[-1][0], segs[-1][1], segs[-1][2] + n)
        else:
            segs.append((src, dst, n))
        dst += n
    return segs, dst


def _relayout_kernel(lo_ref, hi_ref, kr_ref, glr_ref, main_ref, tail_ref):
    r = pl.program_id(0)
    rows = RELAYOUT_ROWS
    for src, dst, n in _segments(MAIN_ORDER)[0]:
        shift = src - dst

        @pl.when((r >= dst // rows) & (r < (dst + n) // rows))
        def _(shift=shift):
            main_ref[:rows - shift, :] = lo_ref[shift:, :].astype(BF16)
            if shift:
                main_ref[rows - shift:, :] = hi_ref[:shift, :].astype(BF16)

    tail_ref[:MLA_ROPE, :] = kr_ref[...].astype(BF16)
    tail_ref[MLA_ROPE:MLA_ROPE + GLA_GATE_RANK, :] = glr_ref[...].astype(BF16)
    pad = TAIL_W - MLA_ROPE - GLA_GATE_RANK
    tail_ref[MLA_ROPE + GLA_GATE_RANK:, :] = jnp.zeros((pad, tail_ref.shape[1]), BF16)


def _relayout_w_in(w_in_t):
    d_in, d = w_in_t.shape
    segs, n_main = _segments(MAIN_ORDER)
    rows = RELAYOUT_ROWS
    assert all(dst % rows == 0 and n % rows == 0 and 0 <= src - dst <= LANES and (src - dst) % 16 == 0
               for src, dst, n in segs)
    (kr_src, _, _), (glr_src, _, _) = _segments(TAIL_ORDER)[0]
    assert kr_src % MLA_ROPE == 0 and glr_src % GLA_GATE_RANK == 0
    return pl.pallas_call(
        _relayout_kernel,
        out_shape=(jax.ShapeDtypeStruct((n_main, d), BF16), jax.ShapeDtypeStruct((TAIL_W, d), BF16)),
        grid=(n_main // rows,),
        in_specs=[pl.BlockSpec((rows, d), lambda r: (r, 0)),
                  pl.BlockSpec((LANES, d), lambda r: ((r + 1) * (rows // LANES), 0)),
                  pl.BlockSpec((MLA_ROPE, d), lambda r: (kr_src // MLA_ROPE, 0)),
                  pl.BlockSpec((GLA_GATE_RANK, d), lambda r: (glr_src // GLA_GATE_RANK, 0))],
        out_specs=(pl.BlockSpec((rows, d), lambda r: (r, 0)),
                   pl.BlockSpec((TAIL_W, d), lambda r: (0, 0))),
        compiler_params=_params(("arbitrary",)),
        name="w_in_layout",
    )(w_in_t, w_in_t, w_in_t, w_in_t)


INP_TM = 1024
INP_TN = 2048


def _inproj_kernel(x_ref, mod_ref, g_ref, w_ref, wt_ref, o_ref, tail_ref, h_ref, *, base):
    @pl.when(pl.program_id(1) == 0)
    def _():
        h = _mod_norm(x_ref[...], g_ref[...], mod_ref, base).astype(BF16)
        h_ref[...] = h
        tail_ref[...] = _dot_t1(h, wt_ref[...])

    o_ref[...] = _dot_t1(h_ref[...], w_ref[...]).astype(BF16)


def _inproj(x2d, mod3, g, w_main, w_tail, *, base, seq):
    t, d = x2d.shape
    n = w_main.shape[0]
    tiles_per_batch = seq // INP_TM
    return pl.pallas_call(
        functools.partial(_inproj_kernel, base=base),
        out_shape=(jax.ShapeDtypeStruct((t, n), BF16),
                   jax.ShapeDtypeStruct((t, TAIL_W), F32)),
        grid=(t // INP_TM, n // INP_TN),
        in_specs=[
            pl.BlockSpec((INP_TM, d), lambda i, j: (i, 0)),
            pl.BlockSpec((1, N_MOD, d), lambda i, j: (i // tiles_per_batch, 0, 0)),
            pl.BlockSpec((1, d), lambda i, j: (0, 0)),
            pl.BlockSpec((INP_TN, d), lambda i, j: (j, 0)),
            pl.BlockSpec((TAIL_W, d), lambda i, j: (0, 0)),
        ],
        out_specs=(pl.BlockSpec((INP_TM, INP_TN), lambda i, j: (i, j)),
                   pl.BlockSpec((INP_TM, TAIL_W), lambda i, j: (i, 0))),
        scratch_shapes=[pltpu.VMEM((INP_TM, d), BF16)],
        compiler_params=_params(("parallel", "arbitrary")),
        name="in_proj",
    )(x2d, mod3, g, w_main, w_tail)


PREP_TM = 1024
ROPE_HALF = MLA_ROPE // 2


def _rope_rotate(x, cos, sin_signed):
    width = x.shape[-1]
    lane = lax.broadcasted_iota(jnp.int32, x.shape, 1)
    first_half = (lane % MLA_ROPE) < ROPE_HALF
    partner = jnp.where(first_half,
                        pltpu.roll(x, width - ROPE_HALF, 1),
                        pltpu.roll(x, ROPE_HALF, 1))
    return x * cos + partner * sin_signed


def _prep_kernel(qlat_ref, kvlat_ref, tail_ref, pos_ref, invf_ref, sgn_ref,
                 gql_ref, wuq_ref, gqn_ref, gqr_ref, gkvl_ref, wukv_ref, gkn_ref, gkr_ref,
                 grp_ref, q_ref, k_ref, v_ref):
    ang_t = invf_ref[...] * pos_ref[0].astype(F32)
    lane_reps = LANES // ROPE_HALF
    cos = jnp.tile(jnp.cos(ang_t).T, (1, lane_reps))
    sin_signed = jnp.tile(jnp.sin(ang_t).T, (1, lane_reps)) * sgn_ref[...]
    n_rope = MLA_HEADS * MLA_ROPE
    cos_q = jnp.concatenate([cos] * (n_rope // LANES), axis=-1)
    sin_q = jnp.concatenate([sin_signed] * (n_rope // LANES), axis=-1)

    cq = _rms(qlat_ref[...].astype(F32), gql_ref[...]).astype(BF16)
    ckv = _rms(kvlat_ref[...].astype(F32), gkvl_ref[...]).astype(BF16)
    n_nope = MLA_HEADS * MLA_NOPE
    pair = 2 * MLA_NOPE

    for hp in range(MLA_HEADS // 2):
        cols = slice(hp * pair, (hp + 1) * pair)
        qp = _dot(cq, wuq_ref[:, cols])
        kp = _dot(ckv, wukv_ref[:, cols])
        for hh in range(2):
            h = 2 * hp + hh
            sub = slice(hh * MLA_NOPE, (hh + 1) * MLA_NOPE)
            q_ref[0, h, :, :MLA_NOPE] = _rms(qp[:, sub], gqn_ref[...]).astype(BF16)
            k_ref[0, h, :, :MLA_NOPE] = _rms(kp[:, sub], gkn_ref[...]).astype(BF16)
        v_ref[0, :, cols] = _dot(ckv, wukv_ref[:, n_nope + hp * pair:n_nope + (hp + 1) * pair]).astype(BF16)

    qr = _dot(cq, wuq_ref[:, n_nope:])
    ssq = _split_dot(qr * qr, grp_ref[...])
    qr = qr * lax.rsqrt(ssq * (1.0 / MLA_ROPE) + EPS) * gqr_ref[...]
    qr = _rope_rotate(qr, cos_q, sin_q).astype(BF16)
    kr = tail_ref[:, TAIL_KROPE:TAIL_KROPE + MLA_ROPE]
    kr = _rms(kr, gkr_ref[...])
    kr = _rope_rotate(kr, cos[:, :MLA_ROPE], sin_signed[:, :MLA_ROPE]).astype(BF16)
    for h in range(MLA_HEADS):
        q_ref[0, h, :, MLA_NOPE:] = qr[:, h * MLA_ROPE:(h + 1) * MLA_ROPE]
        k_ref[0, h, :, MLA_NOPE:] = kr


def _mla_prep(proj, tail, pos3d, consts, weights, *, batch, seq):
    t = proj.shape[0]
    tiles_per_batch = seq // PREP_TM
    row = lambda i: (i, 0)
    const = lambda i: (0, 0)
    (invf, sgn, grp) = consts
    (gql, wuq, gqn, gqr, gkvl, wukv, gkn, gkr) = weights
    head_out = pl.BlockSpec((1, MLA_HEADS, PREP_TM, MLA_QK),
                            lambda i: (i // tiles_per_batch, 0, i % tiles_per_batch, 0))
    full = lambda a: pl.BlockSpec(a.shape, const)
    return pl.pallas_call(
        _prep_kernel,
        out_shape=(jax.ShapeDtypeStruct((batch, MLA_HEADS, seq, MLA_QK), BF16),
                   jax.ShapeDtypeStruct((batch, MLA_HEADS, seq, MLA_QK), BF16),
                   jax.ShapeDtypeStruct((batch, seq, MLA_HEADS * MLA_V), BF16)),
        grid=(t // PREP_TM,),
        in_specs=[
            pl.BlockSpec((PREP_TM, MLA_Q_RANK), lambda i: (i, COL_QLAT // MLA_Q_RANK)),
            pl.BlockSpec((PREP_TM, MLA_KV_RANK), lambda i: (i, COL_KVLAT // MLA_KV_RANK)),
            pl.BlockSpec((PREP_TM, TAIL_W), row),
            pl.BlockSpec((1, 1, PREP_TM), lambda i: (i, 0, 0)),
            full(invf), full(sgn),
            full(gql), full(wuq), full(gqn), full(gqr),
            full(gkvl), full(wukv), full(gkn), full(gkr),
            full(grp),
        ],
        out_specs=(head_out, head_out,
                   pl.BlockSpec((1, PREP_TM, MLA_HEADS * MLA_V),
                                lambda i: (i // tiles_per_batch, i % tiles_per_batch, 0))),
        compiler_params=_params(("parallel",)),
        name="mla_prep",
    )(proj, proj, tail, pos3d, invf, sgn, gql, wuq, gqn, gqr, gkvl, wukv, gkn, gkr, grp)


ATT_T = 1024
ATT_H = ATT_T // 2


def _attn_kernel(q_ref, k_ref, v_ref, *rest):
    n_cast = (len(rest) - 1) // 2
    o_ref = rest[n_cast]
    for src, dst in zip(rest[:n_cast], rest[n_cast + 1:]):
        dst[...] = src[0].astype(BF16)

    n_tiles = k_ref.shape[2] // ATT_T
    ones = jnp.ones((ATT_T, MLA_V), BF16)
    r = lax.broadcasted_iota(jnp.int32, (ATT_H, ATT_H), 0)
    c = lax.broadcasted_iota(jnp.int32, (ATT_H, ATT_H), 1)
    visible = (c // CHUNK) <= (r // CHUNK)
    state = {}

    def update(chain, s, v2):
        m_cur = jnp.max(s, axis=-1, keepdims=True)
        if chain not in state:
            m_next = jnp.broadcast_to(m_cur, (ATT_H, LANES))
        else:
            m_prev, l_prev, acc_prev = state[chain]
            m_next = jnp.maximum(m_prev, m_cur)
            alpha = jnp.exp2(m_prev - m_next)
        p = jnp.exp2(s - jnp.tile(m_next, (1, s.shape[1] // LANES)))
        pv = _dot(p.astype(BF16), v2)
        acc, l = pv[:, :MLA_V], pv[:, MLA_V:]
        if chain in state:
            acc, l = alpha * acc_prev + acc, alpha * l_prev + l
        state[chain] = (m_next, l, acc)

    for t in range(n_tiles):
        rows = slice(t * ATT_T, (t + 1) * ATT_T)
        k = k_ref[0, 0, rows, :]
        v2 = jnp.concatenate([v_ref[0, rows, :], ones], axis=-1)
        for qt in range(t, n_tiles):
            q0 = q_ref[0, 0, qt * ATT_T:qt * ATT_T + ATT_H, :]
            q1 = q_ref[0, 0, qt * ATT_T + ATT_H:(qt + 1) * ATT_T, :]
            if t < qt:
                update((qt, 0), _dot_t1(q0, k), v2)
                update((qt, 1), _dot_t1(q1, k), v2)
            else:
                update((qt, 0), jnp.where(visible, _dot_t1(q0, k[:ATT_H]), -jnp.inf), v2[:ATT_H])
                s1 = _dot_t1(q1, k)
                s1 = jnp.concatenate([s1[:, :ATT_H], jnp.where(visible, s1[:, ATT_H:], -jnp.inf)],
                                     axis=-1)
                update((qt, 1), s1, v2)
                for half in range(2):
                    _, l, acc = state[(qt, half)]
                    lo = qt * ATT_T + half * ATT_H
                    o_ref[0, lo:lo + ATT_H, :] = (acc / l).astype(BF16)


def _mla_attn(q, k, v, cast_weights):
    batch, heads, seq, _ = q.shape
    steps = batch * heads
    step = lambda b, h: b * heads + h
    cast_in, cast_out, cast_shapes = [], [], []
    for w in cast_weights:
        _, rows, cols = w.shape
        blk = rows // steps
        assert blk * steps == rows and blk % 16 == 0
        cast_in.append(pl.BlockSpec((1, blk, cols), lambda b, h: (0, step(b, h), 0)))
        cast_out.append(pl.BlockSpec((blk, cols), lambda b, h: (step(b, h), 0)))
        cast_shapes.append(jax.ShapeDtypeStruct((rows, cols), BF16))
    return pl.pallas_call(
        _attn_kernel,
        out_shape=(jax.ShapeDtypeStruct((batch, seq, heads * MLA_V), BF16), *cast_shapes),
        grid=(batch, heads),
        in_specs=[
            pl.BlockSpec((1, 1, seq, MLA_QK), lambda b, h: (b, h, 0, 0)),
            pl.BlockSpec((1, 1, seq, MLA_QK), lambda b, h: (b, h, 0, 0)),
            pl.BlockSpec((1, seq, MLA_V), lambda b, h: (b, 0, h)),
            *cast_in,
        ],
        out_specs=(pl.BlockSpec((1, seq, MLA_V), lambda b, h: (b, 0, h)), *cast_out),
        compiler_params=_params(("parallel", "parallel")),
        name="mla_attn",
    )(q, k, v, *cast_weights)


GLA_L = 1024
GLA_NC = GLA_L // CHUNK
GLA_SUB = 512


def _gla_head(log_a, q, k, v, go, gg, state_ref, causal, tri):
    sub_nc = GLA_SUB // CHUNK
    q_dec, k_end, o_intra, b_ends = [], [], [], []
    for sb in range(GLA_L // GLA_SUB):
        rows = slice(sb * GLA_SUB, (sb + 1) * GLA_SUB)
        la = log_a[rows]
        la_hi = la.astype(BF16)
        la_lo = (la - la_hi.astype(F32)).astype(BF16)
        cum = _dot(tri, jnp.concatenate([la_hi, la_lo], axis=-1))
        b_cum = cum[:, :GLA_DK] + cum[:, GLA_DK:]
        b3 = b_cum.reshape(sub_nc, CHUNK, GLA_DK)
        b_end = b3[:, CHUNK - 1:CHUNK, :]
        b_tot = jnp.broadcast_to(b_end, b3.shape).reshape(GLA_SUB, GLA_DK)
        b_ends.append(jnp.exp(b_end))

        qf = q[rows].astype(F32) * (GLA_DK ** -0.5)
        kf = k[rows].astype(F32)
        qd = (qf * jnp.exp(b_cum)).astype(BF16)
        kd = (kf * jnp.exp(-b_cum)).astype(BF16)
        q_dec.append(qd)
        k_end.append((kf * jnp.exp(b_tot - b_cum)).astype(BF16))
        attn = jnp.where(causal, _dot_t1(qd, kd), 0.0).astype(BF16)
        o_intra.append(_dot(attn, v[rows]))

    state = state_ref[...]
    outs = []
    for n in range(GLA_NC):
        sb, rows = n // sub_nc, slice((n % sub_nc) * CHUNK, (n % sub_nc + 1) * CHUNK)
        outs.append(o_intra[sb][rows] + _dot_t1(q_dec[sb][rows], state.astype(BF16)))
        v_n = v[n * CHUNK:(n + 1) * CHUNK]
        state = state * b_ends[sb][n % sub_nc] + _dot_t0(v_n, k_end[sb][rows])
    state_ref[...] = state

    o = jnp.concatenate(outs, axis=0)
    return (_rms(o, gg) * _silu(go.astype(F32))).astype(BF16)


def _gla_kernel(q_ref, k_ref, v_ref, go_ref, tail_ref, wg_ref, bg_ref, gg_ref,
                o_ref, state_ref):
    @pl.when(pl.program_id(1) == 0)
    def _():
        state_ref[...] = jnp.zeros_like(state_ref)

    wg = wg_ref[...]
    wg_hi = wg.astype(BF16)
    wg_lo = (wg - wg_hi.astype(F32)).astype(BF16)
    tl = tail_ref[...]
    tl_hi = tl.astype(BF16)
    tl_lo = (tl - tl_hi.astype(F32)).astype(BF16)
    z = _dot(tl_hi, wg_hi) + _dot(tl_lo, wg_hi) + _dot(tl_hi, wg_lo) + bg_ref[...]
    log_a = (jnp.minimum(z, 0.0) - jnp.log1p(jnp.exp(-jnp.abs(z)))) * (1.0 / GLA_GATE_NORMALIZER)

    r = lax.broadcasted_iota(jnp.int32, (GLA_SUB, GLA_SUB), 0)
    c = lax.broadcasted_iota(jnp.int32, (GLA_SUB, GLA_SUB), 1)
    causal = ((r // CHUNK) == (c // CHUNK)) & (c <= r)
    tri = causal.astype(BF16)

    for hh in range(GLA_HEADS):
        dk = slice(hh * GLA_DK, (hh + 1) * GLA_DK)
        dv = slice(hh * GLA_DV, (hh + 1) * GLA_DV)
        o_ref[:, dv] = _gla_head(log_a[:, dk], q_ref[:, dk], k_ref[:, dk], v_ref[:, dv],
                                 go_ref[:, dv], gg_ref[...], state_ref.at[hh], causal, tri)


def _gla(proj, tail, wg_pad, b_gk, g_gla, *, batch, seq):
    t = proj.shape[0]
    steps = seq // GLA_L
    tok = lambda b, i: b * steps + i
    return pl.pallas_call(
        _gla_kernel,
        out_shape=jax.ShapeDtypeStruct((t, GLA_VAL), BF16),
        grid=(batch, steps),
        in_specs=[
            pl.BlockSpec((GLA_L, GLA_KEY), lambda b, i: (tok(b, i), COL_GQ // GLA_KEY)),
            pl.BlockSpec((GLA_L, GLA_KEY), lambda b, i: (tok(b, i), COL_GK // GLA_KEY)),
            pl.BlockSpec((GLA_L, GLA_VAL), lambda b, i: (tok(b, i), COL_GV // GLA_VAL)),
            pl.BlockSpec((GLA_L, GLA_VAL), lambda b, i: (tok(b, i), COL_GOUT // GLA_VAL)),
            pl.BlockSpec((GLA_L, TAIL_W), lambda b, i: (tok(b, i), 0)),
            pl.BlockSpec((TAIL_W, GLA_KEY), lambda b, i: (0, 0)),
            pl.BlockSpec((1, GLA_KEY), lambda b, i: (0, 0)),
            pl.BlockSpec((1, GLA_DV), lambda b, i: (0, 0)),
        ],
        out_specs=pl.BlockSpec((GLA_L, GLA_VAL), lambda b, i: (tok(b, i), 0)),
        scratch_shapes=[pltpu.VMEM((GLA_HEADS, GLA_DV, GLA_DK), F32)],
        compiler_params=_params(("parallel", "arbitrary")),
        name="gla",
    )(proj, proj, proj, proj, tail, wg_pad, b_gk, g_gla)


MRG_TM = 512


def _merge_kernel(x_ref, mod_ref, a_ref, b_ref, ga_ref, gb_ref, wa_ref, wb_ref, wo_ref, o_ref,
                  *, base):
    ya = _dot(a_ref[...], wa_ref[...])
    yb = _dot(b_ref[...], wb_ref[...])
    merged = (jax.nn.sigmoid(ga_ref[...].astype(F32)) * ya
              + jax.nn.sigmoid(gb_ref[...].astype(F32)) * yb).astype(BF16)
    gate = mod_ref[0, base + 2:base + 3, :]
    o_ref[...] = x_ref[...] + gate * _dot(merged, wo_ref[...])


def _merge(x2d, mod3, attn_o, gla_o, proj, wa, wb, wo, *, base, seq):
    t, d = x2d.shape
    tiles_per_batch = seq // MRG_TM
    row = lambda i: (i, 0)
    resident = lambda a: pl.BlockSpec(a.shape, lambda i: (0, 0), pipeline_mode=pl.Buffered(1))
    return pl.pallas_call(
        functools.partial(_merge_kernel, base=base),
        out_shape=jax.ShapeDtypeStruct((t, d), F32),
        grid=(t // MRG_TM,),
        in_specs=[
            pl.BlockSpec((MRG_TM, d), row),
            pl.BlockSpec((1, N_MOD, d), lambda i: (i // tiles_per_batch, 0, 0)),
            pl.BlockSpec((MRG_TM, attn_o.shape[1]), row),
            pl.BlockSpec((MRG_TM, gla_o.shape[1]), row),
            pl.BlockSpec((MRG_TM, d), lambda i: (i, COL_GATE_A // d)),
            pl.BlockSpec((MRG_TM, d), lambda i: (i, COL_GATE_A // d + 1)),
            resident(wa), resident(wb), resident(wo),
        ],
        out_specs=pl.BlockSpec((MRG_TM, d), row),
        compiler_params=_params(("parallel",)),
        name="merge",
    )(x2d, mod3, attn_o, gla_o, proj, proj, wa, wb, wo)


def _layer(x2d, mod3, pos3d, p, *, batch, seq):
    d = x2d.shape[1]
    assert IN_SPLITS[-1] == d and sum(IN_SPLITS) == p["w_in"].shape[1]
    w_main, w_tail = _relayout_w_in(p["w_in"].T)
    wuq = p["w_uq"].reshape(MLA_Q_RANK, MLA_HEADS, MLA_QK)
    wuq = jnp.concatenate([wuq[:, :, :MLA_NOPE].reshape(MLA_Q_RANK, -1),
                           wuq[:, :, MLA_NOPE:].reshape(MLA_Q_RANK, -1)], axis=1).astype(BF16)
    wukv = p["w_ukv"].reshape(MLA_KV_RANK, MLA_HEADS, MLA_NOPE + MLA_V)
    wukv = jnp.concatenate([wukv[:, :, :MLA_NOPE].reshape(MLA_KV_RANK, -1),
                            wukv[:, :, MLA_NOPE:].reshape(MLA_KV_RANK, -1)], axis=1).astype(BF16)
    wg_pad = jnp.zeros((TAIL_W, GLA_KEY), F32).at[TAIL_GLR:TAIL_GLR + GLA_GATE_RANK].set(p["w_gk_up"])

    inv_freq = ROPE_THETA ** (-jnp.arange(0, MLA_ROPE, 2, dtype=F32) / MLA_ROPE)
    invf = inv_freq.reshape(ROPE_HALF, 1)
    sgn = jnp.tile(jnp.concatenate([-jnp.ones(ROPE_HALF, F32), jnp.ones(ROPE_HALF, F32)]),
                   LANES // MLA_ROPE).reshape(1, LANES)
    n_rope = MLA_HEADS * MLA_ROPE
    lane = jnp.arange(n_rope)
    grp = (lane[:, None] // MLA_ROPE == lane[None, :] // MLA_ROPE).astype(BF16)

    row = lambda a: a.reshape(1, -1)
    q_scale = MLA_QK ** -0.5 * LOG2E
    x2d = _ffn(x2d, mod3, row(p["g_ffn1"]), p["w1_a"][None], p["w3_a"][None],
               p["w2_a"][None], row(p["g_final"]), base=0, final_norm=False, seq=seq)
    proj, tail = _inproj(x2d, mod3, row(p["g_mix"]), w_main, w_tail, base=3, seq=seq)
    q, k, v = _mla_prep(
        proj, tail, pos3d, (invf, sgn, grp),
        (row(p["g_q_lat"]), wuq, row(p["g_qn"] * q_scale),
         row(jnp.tile(p["g_qr"], MLA_HEADS) * q_scale),
         row(p["g_kv_lat"]), wukv, row(p["g_kn"]), row(p["g_kr"])),
        batch=batch, seq=seq)
    attn_o, wa, wb, wo = _mla_attn(q, k, v, (p["w_proj_a"][None], p["w_proj_b"][None], p["w_out"][None]))
    attn_o = attn_o.reshape(batch * seq, MLA_HEADS * MLA_V)
    gla_o = _gla(proj, tail, wg_pad, row(p["b_gk"]), row(p["g_gla"]), batch=batch, seq=seq)
    x2d = _merge(x2d, mod3, attn_o, gla_o, proj, wa, wb, wo, base=3, seq=seq)
    x2d = _ffn(x2d, mod3, row(p["g_ffn2"]), p["w1_b"][None], p["w3_b"][None],
               p["w2_b"][None], row(p["g_final"]), base=6, final_norm=True, seq=seq)
    return x2d


def kernel(x, c, positions, w_ada, b_ada, g_ffn1, w1_a, w3_a, w2_a, g_mix, w_in, g_q_lat, w_uq, g_qn, g_qr, g_kv_lat, w_ukv, g_kn, g_kr, w_gk_up, b_gk, g_gla, w_proj_a, w_proj_b, w_out, g_ffn2, w1_b, w3_b, w2_b, g_final):
    batch, seq, d = x.shape
    depth = w_ada.shape[0]
    assert depth == 1, "the final norm is fused into the last FFN of a single layer"
    names = ("g_ffn1", "w1_a", "w3_a", "w2_a", "g_mix", "w_in", "g_q_lat", "w_uq", "g_qn", "g_qr",
             "g_kv_lat", "w_ukv", "g_kn", "g_kr", "w_gk_up", "b_gk", "g_gla", "w_proj_a", "w_proj_b",
             "w_out", "g_ffn2", "w1_b", "w3_b", "w2_b", "g_final")
    stacked = (g_ffn1, w1_a, w3_a, w2_a, g_mix, w_in, g_q_lat, w_uq, g_qn, g_qr, g_kv_lat, w_ukv,
               g_kn, g_kr, w_gk_up, b_gk, g_gla, w_proj_a, w_proj_b, w_out, g_ffn2, w1_b, w3_b, w2_b,
               g_final)
    x2d = x.reshape(batch * seq, d)
    pos3d = positions.reshape(batch * seq // PREP_TM, 1, PREP_TM)
    p = {n: a[0] for n, a in zip(names, stacked)}
    mod3 = _adaln(c, w_ada[0], b_ada[0]).reshape(batch, N_MOD, d)
    x2d = _layer(x2d, mod3, pos3d, p, batch=batch, seq=seq)
    return x2d.reshape(batch, seq, d)
```

```python
import functools

import jax
import jax.numpy as jnp
from jax import lax
from jax.experimental import pallas as pl
from jax.experimental.pallas import tpu as pltpu

F32 = jnp.float32
BF16 = jnp.bfloat16

EPS = 1e-6
LOG2E = 1.4426950408889634
CHUNK = 64
MLA_HEADS = 8
MLA_Q_RANK = 512
MLA_KV_RANK = 512
MLA_NOPE = 128
MLA_ROPE = 64
MLA_V = 128
MLA_QK = MLA_NOPE + MLA_ROPE
ROPE_THETA = 10000.0
GLA_HEADS = 4
GLA_DK = 128
GLA_DV = 256
GLA_GATE_RANK = 16
GLA_GATE_NORMALIZER = 16.0
GLA_KEY = GLA_HEADS * GLA_DK
GLA_VAL = GLA_HEADS * GLA_DV
N_MOD = 9

LANES = 128
SUBLANES = 8
VMEM_LIMIT = 56 * 1024 * 1024

COL_QLAT = 0
COL_KVLAT = COL_QLAT + MLA_Q_RANK
COL_GQ = COL_KVLAT + MLA_KV_RANK
COL_GK = COL_GQ + GLA_KEY
COL_GV = COL_GK + GLA_KEY
COL_GOUT = COL_GV + GLA_VAL
COL_GATE_A = COL_GOUT + GLA_VAL
TAIL_W = LANES
TAIL_KROPE = 0
TAIL_GLR = MLA_ROPE


def _params(sem, vmem_limit=VMEM_LIMIT):
    return pltpu.CompilerParams(dimension_semantics=sem, vmem_limit_bytes=vmem_limit)


def _dot(a, b):
    return jnp.dot(a, b, preferred_element_type=F32)


def _dot_t0(a, b):
    return lax.dot_general(a, b, (((0,), (0,)), ((), ())), preferred_element_type=F32)


def _dot_t1(a, b):
    return lax.dot_general(a, b, (((1,), (1,)), ((), ())), preferred_element_type=F32)


def _split_dot(x, w_bf16):
    hi = x.astype(BF16)
    lo = (x - hi.astype(F32)).astype(BF16)
    return _dot(hi, w_bf16) + _dot(lo, w_bf16)


def _silu(x):
    return x * jax.nn.sigmoid(x)


def _rms(x, g):
    ms = jnp.mean(x * x, axis=-1, keepdims=True)
    return x * lax.rsqrt(ms + EPS) * g


ADA_TN = 2048
ADA_RC = 64


def _adaln_kernel(ct_ref, w_ref, b_ref, o_ref, s_ref):
    d, nb = ct_ref.shape
    tn = w_ref.shape[1]

    @pl.when(pl.program_id(0) == 0)
    def _():
        s = _silu(ct_ref[...])
        for b in range(nb):
            s_ref[b] = jnp.broadcast_to(s[:, b:b + 1], (d, LANES))

    def body(k, acc):
        rows = pl.ds(pl.multiple_of(k * ADA_RC, ADA_RC), ADA_RC)
        out = []
        for b in range(nb):
            sb = s_ref[b, rows, :]
            cols = []
            for col in range(tn // LANES):
                p = w_ref[rows, col * LANES:(col + 1) * LANES] * sb
                cols.append(p.reshape(ADA_RC // SUBLANES, SUBLANES, LANES).sum(axis=0))
            out.append(acc[b] + jnp.concatenate(cols, axis=-1))
        return tuple(out)

    init = tuple(jnp.zeros((SUBLANES, tn), F32) for _ in range(nb))
    acc = lax.fori_loop(0, d // ADA_RC, body, init, unroll=2)
    rows = [jnp.sum(a, axis=0, keepdims=True) for a in acc]
    o_ref[...] = jnp.concatenate(rows, axis=0) + b_ref[...]


def _adaln(c, w_ada, b_ada):
    nb, d = c.shape
    n = w_ada.shape[1]
    return pl.pallas_call(
        _adaln_kernel,
        out_shape=jax.ShapeDtypeStruct((nb, n), F32),
        grid=(n // ADA_TN,),
        in_specs=[
            pl.BlockSpec((d, nb), lambda j: (0, 0)),
            pl.BlockSpec((d, ADA_TN), lambda j: (0, j)),
            pl.BlockSpec((1, ADA_TN), lambda j: (0, j)),
        ],
        out_specs=pl.BlockSpec((nb, ADA_TN), lambda j: (0, j)),
        scratch_shapes=[pltpu.VMEM((nb, d, LANES), F32)],
        compiler_params=_params(("arbitrary",)),
        name="adaln",
    )(c.T, w_ada, b_ada.reshape(1, n))


def _mod_norm(x, g, mod_ref, base):
    shift = mod_ref[0, base:base + 1, :]
    scale = mod_ref[0, base + 1:base + 2, :]
    return _rms(x, g) * (1.0 + scale) + shift


FFN_TM = 512
FFN_TF = 512
FFN_HEAD_TILES = 2
FFN_HEAD_TF = 256
FFN_GROUP = 256
FFN_WSLOTS = 3
FFN_VMEM_LIMIT = 60 * 1024 * 1024
FFN_AHEAD_CHUNKS = 8
FFN_AHEAD_ROWS = FFN_TM // FFN_AHEAD_CHUNKS


def _swiglu_tile(h, w1g, w3g, w2t):
    upd = None
    for s in range(w1g.shape[0]):
        a = _dot(h, w1g[s])
        b = _dot(h, w3g[s])
        p = (_silu(a) * b).astype(BF16)
        part = _dot(p, w2t[s * FFN_GROUP:(s + 1) * FFN_GROUP, :])
        upd = part if upd is None else upd + part
    return upd


def _ffn_finish(x_ref, mod_ref, gf_ref, o_ref, *, base, final_norm):
    gate = mod_ref[0, base + 2:base + 3, :]
    y = x_ref[...] + 0.5 * gate * o_ref[...]
    if final_norm:
        y = _rms(y, gf_ref[...])
    o_ref[...] = y


def _ffn_head_kernel(x_ref, mod_ref, g_ref, w1_ref, w3_ref, w2_ref, gf_ref,
                     o_ref, w1b_ref, w3b_ref, w2b_ref, h_ref, *, base, final_norm):
    j = pl.program_id(1)

    @pl.when(j == 0)
    def _():
        h_ref[...] = _mod_norm(x_ref[...], g_ref[...], mod_ref, base).astype(BF16)
        o_ref[...] = jnp.zeros_like(o_ref)

    w1 = w1_ref[0].astype(BF16)
    w3 = w3_ref[0].astype(BF16)
    w2 = w2_ref[0].astype(BF16)
    w1b_ref[0] = w1
    w3b_ref[0] = w3
    w2b_ref[...] = w2
    o_ref[...] += _swiglu_tile(h_ref[...], w1[None], w3[None], w2)

    @pl.when(j == pl.num_programs(1) - 1)
    def _():
        _ffn_finish(x_ref, mod_ref, gf_ref, o_ref, base=base, final_norm=final_norm)


def _ffn_kernel(x_ref, xn_ref, mod_ref, modn_ref, g_ref, w1_hbm, w3_hbm, w2_hbm, gf_ref, y0_ref,
                o_ref, h_ref, w1_buf, w3_buf, w2_buf, sem, *, base, final_norm, n_steps):
    i = pl.program_id(0)
    h_slot = i % 2
    first = (i - FFN_HEAD_TILES) * n_steps
    groups = FFN_TF // FFN_GROUP
    ahead = FFN_WSLOTS - 1

    def copies(j, slot):
        jj = j % n_steps
        return (pltpu.make_async_copy(w1_hbm.at[pl.ds(jj * groups, groups)], w1_buf.at[slot], sem.at[slot, 0]),
                pltpu.make_async_copy(w3_hbm.at[pl.ds(jj * groups, groups)], w3_buf.at[slot], sem.at[slot, 1]),
                pltpu.make_async_copy(w2_hbm.at[pl.ds(jj * FFN_TF, FFN_TF)], w2_buf.at[slot], sem.at[slot, 2]))

    @pl.when(i < FFN_HEAD_TILES)
    def _():
        o_ref[...] = y0_ref[...]

    @pl.when(i >= FFN_HEAD_TILES)
    def _():
        @pl.when(i == FFN_HEAD_TILES)
        def _():
            for j in range(ahead):
                for c in copies(j, j):
                    c.start()
            h_ref[h_slot] = _mod_norm(x_ref[...], g_ref[...], mod_ref, base).astype(BF16)

        for j in range(n_steps):
            slot = (first + j) % FFN_WSLOTS
            for c in copies(j, slot):
                c.wait()
            if j < FFN_AHEAD_CHUNKS:
                rows = slice(j * FFN_AHEAD_ROWS, (j + 1) * FFN_AHEAD_ROWS)
                h_next = _mod_norm(xn_ref[rows, :], g_ref[...], modn_ref, base).astype(BF16)
            upd = _swiglu_tile(h_ref[h_slot], w1_buf.at[slot], w3_buf.at[slot], w2_buf[slot])
            for c in copies(j + ahead, (first + j + ahead) % FFN_WSLOTS):
                c.start()
            if j == 0:
                o_ref[...] = upd
            else:
                o_ref[...] += upd
            if j < FFN_AHEAD_CHUNKS:
                h_ref[1 - h_slot, rows, :] = h_next
        _ffn_finish(x_ref, mod_ref, gf_ref, o_ref, base=base, final_norm=final_norm)

        @pl.when(i == pl.num_programs(0) - 1)
        def _():
            for j in range(n_steps, n_steps + ahead):
                for c in copies(j, (first + j) % FFN_WSLOTS):
                    c.wait()


def _ffn(x2d, mod3, g, w1, w3, w2, g_final, *, base, final_norm, seq):
    t, d = x2d.shape
    dff = w1.shape[2]
    tiles_per_batch = seq // FFN_TM
    const = lambda i, j: (0, 0)
    kw = dict(base=base, final_norm=final_norm)
    tag = "_final" if final_norm else ""
    common = [
        pl.BlockSpec((1, N_MOD, d), lambda i, j: (i // tiles_per_batch, 0, 0)),
        pl.BlockSpec((1, d), const),
    ]
    hf = FFN_HEAD_TF
    head_rows = FFN_HEAD_TILES * FFN_TM
    assert head_rows <= seq
    assert hf == FFN_GROUP
    w13_shape = jax.ShapeDtypeStruct((dff // FFN_GROUP, d, FFN_GROUP), BF16)
    w13_spec = pl.BlockSpec((1, d, FFN_GROUP), lambda i, j: (j, 0, 0))
    y0, w1b, w3b, w2b = pl.pallas_call(
        functools.partial(_ffn_head_kernel, **kw),
        out_shape=(jax.ShapeDtypeStruct((head_rows, d), F32), w13_shape, w13_shape,
                   jax.ShapeDtypeStruct((dff, d), BF16)),
        grid=(1, dff // hf),
        in_specs=[pl.BlockSpec((head_rows, d), const, pipeline_mode=pl.Buffered(1))] + common + [
            pl.BlockSpec((1, d, hf), lambda i, j: (0, 0, j)),
            pl.BlockSpec((1, d, hf), lambda i, j: (0, 0, j)),
            pl.BlockSpec((1, hf, d), lambda i, j: (0, j, 0)),
            pl.BlockSpec((1, d), const),
        ],
        out_specs=(pl.BlockSpec((head_rows, d), const), w13_spec, w13_spec,
                   pl.BlockSpec((hf, d), lambda i, j: (j, 0))),
        scratch_shapes=[pltpu.VMEM((head_rows, d), BF16)],
        compiler_params=_params(("arbitrary", "arbitrary")),
        name="ffn_head" + tag,
    )(x2d, mod3, g, w1, w3, w2, g_final)

    n_tiles = t // FFN_TM
    assert dff // FFN_TF >= FFN_AHEAD_CHUNKS and n_tiles > FFN_HEAD_TILES
    nxt = lambda i: jnp.minimum(i + 1, n_tiles - 1)
    groups = FFN_TF // FFN_GROUP
    hbm = pl.BlockSpec(memory_space=pl.ANY)
    return pl.pallas_call(
        functools.partial(_ffn_kernel, n_steps=dff // FFN_TF, **kw),
        out_shape=jax.ShapeDtypeStruct((t, d), F32),
        grid=(n_tiles,),
        in_specs=[
            pl.BlockSpec((FFN_TM, d), lambda i: (i, 0)),
            pl.BlockSpec((FFN_TM, d), lambda i: (nxt(i), 0)),
            pl.BlockSpec((1, N_MOD, d), lambda i: (i // tiles_per_batch, 0, 0)),
            pl.BlockSpec((1, N_MOD, d), lambda i: (nxt(i) // tiles_per_batch, 0, 0)),
            pl.BlockSpec((1, d), lambda i: (0, 0)),
            hbm, hbm, hbm,
            pl.BlockSpec((1, d), lambda i: (0, 0)),
            pl.BlockSpec((FFN_TM, d), lambda i: (jnp.minimum(i, FFN_HEAD_TILES - 1), 0),
                         pipeline_mode=pl.Buffered(1)),
        ],
        out_specs=pl.BlockSpec((FFN_TM, d), lambda i: (i, 0)),
        scratch_shapes=[pltpu.VMEM((2, FFN_TM, d), BF16),
                        pltpu.VMEM((FFN_WSLOTS, groups, d, FFN_GROUP), BF16),
                        pltpu.VMEM((FFN_WSLOTS, groups, d, FFN_GROUP), BF16),
                        pltpu.VMEM((FFN_WSLOTS, FFN_TF, d), BF16),
                        pltpu.SemaphoreType.DMA((FFN_WSLOTS, 3))],
        compiler_params=_params(("arbitrary",), FFN_VMEM_LIMIT),
        name="ffn" + tag,
    )(x2d, x2d, mod3, mod3, g, w1b, w3b, w2b, g_final, y0)


IN_SPLITS = (MLA_Q_RANK, MLA_KV_RANK, MLA_ROPE, GLA_KEY, GLA_KEY, GLA_VAL, GLA_GATE_RANK, GLA_VAL,
             2048, 2048)
MAIN_ORDER = (0, 1, 3, 4, 5, 7, 8, 9)
TAIL_ORDER = (2, 6)
RELAYOUT_ROWS = 1024


def _segments(order):
    starts = [sum(IN_SPLITS[:i]) for i in range(len(IN_SPLITS))]
    segs, dst = [], 0
    for i in order:
        src, n = starts[i], IN_SPLITS[i]
        if segs and segs[-1][0] + segs[-1][2] == src:
            segs[-1] = (segs[-1][0], segs[-1][1], segs[-1][2] + n)
        else:
            segs.append((src, dst, n))
        dst += n
    return segs, dst


def _relayout_kernel(lo_ref, hi_ref, kr_ref, glr_ref, pos_ref, invf_ref, sgn_ref,
                     main_ref, tail_ref, cos_ref, sin_ref):
    ang_t = invf_ref[...] * pos_ref[0].astype(F32)
    lane_reps = LANES // ROPE_HALF
    cos_ref[...] = jnp.tile(jnp.cos(ang_t).T, (1, lane_reps))
    sin_ref[...] = jnp.tile(jnp.sin(ang_t).T, (1, lane_reps)) * sgn_ref[...]

    r = pl.program_id(0)
    rows = RELAYOUT_ROWS
    for src, dst, n in _segments(MAIN_ORDER)[0]:
        shift = src - dst

        @pl.when((r >= dst // rows) & (r < (dst + n) // rows))
        def _(shift=shift):
            main_ref[:rows - shift, :] = lo_ref[shift:, :].astype(BF16)
            if shift:
                main_ref[rows - shift:, :] = hi_ref[:shift, :].astype(BF16)

    tail_ref[:MLA_ROPE, :] = kr_ref[...].astype(BF16)
    tail_ref[MLA_ROPE:MLA_ROPE + GLA_GATE_RANK, :] = glr_ref[...].astype(BF16)
    pad = TAIL_W - MLA_ROPE - GLA_GATE_RANK
    tail_ref[MLA_ROPE + GLA_GATE_RANK:, :] = jnp.zeros((pad, tail_ref.shape[1]), BF16)


def _relayout_w_in(w_in_t, positions, invf, sgn):
    d_in, d = w_in_t.shape
    segs, n_main = _segments(MAIN_ORDER)
    rows = RELAYOUT_ROWS
    steps = n_main // rows
    assert all(dst % rows == 0 and n % rows == 0 and 0 <= src - dst <= LANES and (src - dst) % 16 == 0
               for src, dst, n in segs)
    (kr_src, _, _), (glr_src, _, _) = _segments(TAIL_ORDER)[0]
    assert kr_src % MLA_ROPE == 0 and glr_src % GLA_GATE_RANK == 0
    t = positions.size
    tok = t // steps
    assert tok * steps == t and tok % LANES == 0
    const = lambda a: pl.BlockSpec(a.shape, lambda r: (0, 0))
    return pl.pallas_call(
        _relayout_kernel,
        out_shape=(jax.ShapeDtypeStruct((n_main, d), BF16), jax.ShapeDtypeStruct((TAIL_W, d), BF16),
                   jax.ShapeDtypeStruct((t, LANES), F32), jax.ShapeDtypeStruct((t, LANES), F32)),
        grid=(steps,),
        in_specs=[pl.BlockSpec((rows, d), lambda r: (r, 0)),
                  pl.BlockSpec((LANES, d), lambda r: ((r + 1) * (rows // LANES), 0)),
                  pl.BlockSpec((MLA_ROPE, d), lambda r: (kr_src // MLA_ROPE, 0)),
                  pl.BlockSpec((GLA_GATE_RANK, d), lambda r: (glr_src // GLA_GATE_RANK, 0)),
                  pl.BlockSpec((1, 1, tok), lambda r: (r, 0, 0)),
                  const(invf), const(sgn)],
        out_specs=(pl.BlockSpec((rows, d), lambda r: (r, 0)),
                   pl.BlockSpec((TAIL_W, d), lambda r: (0, 0)),
                   pl.BlockSpec((tok, LANES), lambda r: (r, 0)),
                   pl.BlockSpec((tok, LANES), lambda r: (r, 0))),
        compiler_params=_params(("arbitrary",)),
        name="w_in_layout",
    )(w_in_t, w_in_t, w_in_t, w_in_t, positions.reshape(steps, 1, tok), invf, sgn)


INP_TM = 1024
INP_TN = 2048


def _inproj_kernel(x_ref, mod_ref, g_ref, w_ref, wt_ref, o_ref, tail_ref, h_ref, *, base):
    @pl.when(pl.program_id(1) == 0)
    def _():
        h = _mod_norm(x_ref[...], g_ref[...], mod_ref, base).astype(BF16)
        h_ref[...] = h
        tail_ref[...] = _dot_t1(h, wt_ref[...])

    o_ref[...] = _dot_t1(h_ref[...], w_ref[...]).astype(BF16)


def _inproj(x2d, mod3, g, w_main, w_tail, *, base, seq):
    t, d = x2d.shape
    n = w_main.shape[0]
    tiles_per_batch = seq // INP_TM
    return pl.pallas_call(
        functools.partial(_inproj_kernel, base=base),
        out_shape=(jax.ShapeDtypeStruct((t, n), BF16),
                   jax.ShapeDtypeStruct((t, TAIL_W), F32)),
        grid=(t // INP_TM, n // INP_TN),
        in_specs=[
            pl.BlockSpec((INP_TM, d), lambda i, j: (i, 0)),
            pl.BlockSpec((1, N_MOD, d), lambda i, j: (i // tiles_per_batch, 0, 0)),
            pl.BlockSpec((1, d), lambda i, j: (0, 0)),
            pl.BlockSpec((INP_TN, d), lambda i, j: (j, 0)),
            pl.BlockSpec((TAIL_W, d), lambda i, j: (0, 0)),
        ],
        out_specs=(pl.BlockSpec((INP_TM, INP_TN), lambda i, j: (i, j)),
                   pl.BlockSpec((INP_TM, TAIL_W), lambda i, j: (i, 0))),
        scratch_shapes=[pltpu.VMEM((INP_TM, d), BF16)],
        compiler_params=_params(("parallel", "arbitrary")),
        name="in_proj",
    )(x2d, mod3, g, w_main, w_tail)


PREP_TM = 1024
ROPE_HALF = MLA_ROPE // 2


def _rope_rotate(x, cos, sin_signed):
    width = x.shape[-1]
    lane = lax.broadcasted_iota(jnp.int32, x.shape, 1)
    first_half = (lane % MLA_ROPE) < ROPE_HALF
    partner = jnp.where(first_half,
                        pltpu.roll(x, width - ROPE_HALF, 1),
                        pltpu.roll(x, ROPE_HALF, 1))
    return x * cos + partner * sin_signed


def _prep_kernel(qlat_ref, kvlat_ref, tail_ref, cos_ref, sin_ref,
                 gql_ref, wuq_ref, gqn_ref, gqr_ref, gkvl_ref, wukv_ref, gkn_ref, gkr_ref,
                 grp_ref, q_ref, k_ref, v_ref):
    cos = cos_ref[...]
    sin_signed = sin_ref[...]
    n_rope = MLA_HEADS * MLA_ROPE
    cos_q = jnp.concatenate([cos] * (n_rope // LANES), axis=-1)
    sin_q = jnp.concatenate([sin_signed] * (n_rope // LANES), axis=-1)

    cq = _rms(qlat_ref[...].astype(F32), gql_ref[...]).astype(BF16)
    ckv = _rms(kvlat_ref[...].astype(F32), gkvl_ref[...]).astype(BF16)
    n_nope = MLA_HEADS * MLA_NOPE
    pair = 2 * MLA_NOPE

    for hp in range(MLA_HEADS // 2):
        cols = slice(hp * pair, (hp + 1) * pair)
        qp = _dot(cq, wuq_ref[:, cols])
        kp = _dot(ckv, wukv_ref[:, cols])
        for hh in range(2):
            h = 2 * hp + hh
            sub = slice(hh * MLA_NOPE, (hh + 1) * MLA_NOPE)
            q_ref[0, h, :, :MLA_NOPE] = _rms(qp[:, sub], gqn_ref[...]).astype(BF16)
            k_ref[0, h, :, :MLA_NOPE] = _rms(kp[:, sub], gkn_ref[...]).astype(BF16)
        v_ref[0, :, cols] = _dot(ckv, wukv_ref[:, n_nope + hp * pair:n_nope + (hp + 1) * pair]).astype(BF16)

    qr = _dot(cq, wuq_ref[:, n_nope:])
    ssq = _split_dot(qr * qr, grp_ref[...])
    qr = qr * lax.rsqrt(ssq * (1.0 / MLA_ROPE) + EPS) * gqr_ref[...]
    qr = _rope_rotate(qr, cos_q, sin_q).astype(BF16)
    kr = tail_ref[:, TAIL_KROPE:TAIL_KROPE + MLA_ROPE]
    kr = _rms(kr, gkr_ref[...])
    kr = _rope_rotate(kr, cos[:, :MLA_ROPE], sin_signed[:, :MLA_ROPE]).astype(BF16)
    for h in range(MLA_HEADS):
        q_ref[0, h, :, MLA_NOPE:] = qr[:, h * MLA_ROPE:(h + 1) * MLA_ROPE]
        k_ref[0, h, :, MLA_NOPE:] = kr


def _mla_prep(proj, tail, cos, sin, grp, weights, *, batch, seq):
    t = proj.shape[0]
    tiles_per_batch = seq // PREP_TM
    row = lambda i: (i, 0)
    const = lambda i: (0, 0)
    (gql, wuq, gqn, gqr, gkvl, wukv, gkn, gkr) = weights
    head_out = pl.BlockSpec((1, MLA_HEADS, PREP_TM, MLA_QK),
                            lambda i: (i // tiles_per_batch, 0, i % tiles_per_batch, 0))
    full = lambda a: pl.BlockSpec(a.shape, const)
    return pl.pallas_call(
        _prep_kernel,
        out_shape=(jax.ShapeDtypeStruct((batch, MLA_HEADS, seq, MLA_QK), BF16),
                   jax.ShapeDtypeStruct((batch, MLA_HEADS, seq, MLA_QK), BF16),
                   jax.ShapeDtypeStruct((batch, seq, MLA_HEADS * MLA_V), BF16)),
        grid=(t // PREP_TM,),
        in_specs=[
            pl.BlockSpec((PREP_TM, MLA_Q_RANK), lambda i: (i, COL_QLAT // MLA_Q_RANK)),
            pl.BlockSpec((PREP_TM, MLA_KV_RANK), lambda i: (i, COL_KVLAT // MLA_KV_RANK)),
            pl.BlockSpec((PREP_TM, TAIL_W), row),
            pl.BlockSpec((PREP_TM, LANES), row),
            pl.BlockSpec((PREP_TM, LANES), row),
            full(gql), full(wuq), full(gqn), full(gqr),
            full(gkvl), full(wukv), full(gkn), full(gkr),
            full(grp),
        ],
        out_specs=(head_out, head_out,
                   pl.BlockSpec((1, PREP_TM, MLA_HEADS * MLA_V),
                                lambda i: (i // tiles_per_batch, i % tiles_per_batch, 0))),
        compiler_params=_params(("parallel",)),
        name="mla_prep",
    )(proj, proj, tail, cos, sin, gql, wuq, gqn, gqr, gkvl, wukv, gkn, gkr, grp)


ATT_T = 1024
ATT_H = ATT_T // 2


def _attn_kernel(q_ref, k_ref, v_ref, *rest):
    n_cast = (len(rest) - 1) // 2
    o_ref = rest[n_cast]
    for src, dst in zip(rest[:n_cast], rest[n_cast + 1:]):
        dst[...] = src[0].astype(BF16)

    n_tiles = k_ref.shape[2] // ATT_T
    ones = jnp.ones((ATT_T, MLA_V), BF16)
    r = lax.broadcasted_iota(jnp.int32, (ATT_H, ATT_H), 0)
    c = lax.broadcasted_iota(jnp.int32, (ATT_H, ATT_H), 1)
    visible = (c // CHUNK) <= (r // CHUNK)
    state = {}

    def update(chain, s, v2):
        m_cur = jnp.max(s, axis=-1, keepdims=True)
        if chain not in state:
            m_next = jnp.broadcast_to(m_cur, (ATT_H, LANES))
        else:
            m_prev, l_prev, acc_prev = state[chain]
            m_next = jnp.maximum(m_prev, m_cur)
            alpha = jnp.exp2(m_prev - m_next)
        p = jnp.exp2(s - jnp.tile(m_next, (1, s.shape[1] // LANES)))
        pv = _dot(p.astype(BF16), v2)
        acc, l = pv[:, :MLA_V], pv[:, MLA_V:]
        if chain in state:
            acc, l = alpha * acc_prev + acc, alpha * l_prev + l
        state[chain] = (m_next, l, acc)

    for t in range(n_tiles):
        rows = slice(t * ATT_T, (t + 1) * ATT_T)
        k = k_ref[0, 0, rows, :]
        v2 = jnp.concatenate([v_ref[0, rows, :], ones], axis=-1)
        for qt in range(t, n_tiles):
            q0 = q_ref[0, 0, qt * ATT_T:qt * ATT_T + ATT_H, :]
            q1 = q_ref[0, 0, qt * ATT_T + ATT_H:(qt + 1) * ATT_T, :]
            if t < qt:
                update((qt, 0), _dot_t1(q0, k), v2)
                update((qt, 1), _dot_t1(q1, k), v2)
            else:
                update((qt, 0), jnp.where(visible, _dot_t1(q0, k[:ATT_H]), -jnp.inf), v2[:ATT_H])
                s1 = _dot_t1(q1, k)
                s1 = jnp.concatenate([s1[:, :ATT_H], jnp.where(visible, s1[:, ATT_H:], -jnp.inf)],
                                     axis=-1)
                update((qt, 1), s1, v2)
                for half in range(2):
                    _, l, acc = state[(qt, half)]
                    lo = qt * ATT_T + half * ATT_H
                    o_ref[0, lo:lo + ATT_H, :] = (acc / l).astype(BF16)


def _mla_attn(q, k, v, cast_weights):
    batch, heads, seq, _ = q.shape
    steps = batch * heads
    step = lambda b, h: b * heads + h
    cast_in, cast_out, cast_shapes = [], [], []
    for w in cast_weights:
        _, rows, cols = w.shape
        blk = rows // steps
        assert blk * steps == rows and blk % 16 == 0
        cast_in.append(pl.BlockSpec((1, blk, cols), lambda b, h: (0, step(b, h), 0)))
        cast_out.append(pl.BlockSpec((blk, cols), lambda b, h: (step(b, h), 0)))
        cast_shapes.append(jax.ShapeDtypeStruct((rows, cols), BF16))
    return pl.pallas_call(
        _attn_kernel,
        out_shape=(jax.ShapeDtypeStruct((batch, seq, heads * MLA_V), BF16), *cast_shapes),
        grid=(batch, heads),
        in_specs=[
            pl.BlockSpec((1, 1, seq, MLA_QK), lambda b, h: (b, h, 0, 0)),
            pl.BlockSpec((1, 1, seq, MLA_QK), lambda b, h: (b, h, 0, 0)),
            pl.BlockSpec((1, seq, MLA_V), lambda b, h: (b, 0, h)),
            *cast_in,
        ],
        out_specs=(pl.BlockSpec((1, seq, MLA_V), lambda b, h: (b, 0, h)), *cast_out),
        compiler_params=_params(("parallel", "parallel")),
        name="mla_attn",
    )(q, k, v, *cast_weights)


GLA_L = 1024
GLA_NC = GLA_L // CHUNK
GLA_SUB = 512


def _gla_head(log_a, q, k, v, go, gg, state_ref, causal, tri):
    sub_nc = GLA_SUB // CHUNK
    q_dec, k_end, o_intra, b_ends = [], [], [], []
    for sb in range(GLA_L // GLA_SUB):
        rows = slice(sb * GLA_SUB, (sb + 1) * GLA_SUB)
        la = log_a[rows]
        la_hi = la.astype(BF16)
        la_lo = (la - la_hi.astype(F32)).astype(BF16)
        cum = _dot(tri, jnp.concatenate([la_hi, la_lo], axis=-1))
        b_cum = cum[:, :GLA_DK] + cum[:, GLA_DK:]
        b3 = b_cum.reshape(sub_nc, CHUNK, GLA_DK)
        b_end = b3[:, CHUNK - 1:CHUNK, :]
        b_tot = jnp.broadcast_to(b_end, b3.shape).reshape(GLA_SUB, GLA_DK)
        b_ends.append(jnp.exp(b_end))

        qf = q[rows].astype(F32) * (GLA_DK ** -0.5)
        kf = k[rows].astype(F32)
        qd = (qf * jnp.exp(b_cum)).astype(BF16)
        kd = (kf * jnp.exp(-b_cum)).astype(BF16)
        q_dec.append(qd)
        k_end.append((kf * jnp.exp(b_tot - b_cum)).astype(BF16))
        attn = jnp.where(causal, _dot_t1(qd, kd), 0.0).astype(BF16)
        o_intra.append(_dot(attn, v[rows]))

    state = state_ref[...]
    outs = []
    for n in range(GLA_NC):
        sb, rows = n // sub_nc, slice((n % sub_nc) * CHUNK, (n % sub_nc + 1) * CHUNK)
        outs.append(o_intra[sb][rows] + _dot_t1(q_dec[sb][rows], state.astype(BF16)))
        v_n = v[n * CHUNK:(n + 1) * CHUNK]
        state = state * b_ends[sb][n % sub_nc] + _dot_t0(v_n, k_end[sb][rows])
    state_ref[...] = state

    o = jnp.concatenate(outs, axis=0)
    return (_rms(o, gg) * _silu(go.astype(F32))).astype(BF16)


def _gla_kernel(q_ref, k_ref, v_ref, go_ref, tail_ref, wg_ref, bg_ref, gg_ref,
                o_ref, state_ref):
    @pl.when(pl.program_id(1) == 0)
    def _():
        state_ref[...] = jnp.zeros_like(state_ref)

    wg = wg_ref[...]
    wg_hi = wg.astype(BF16)
    wg_lo = (wg - wg_hi.astype(F32)).astype(BF16)
    tl = tail_ref[...]
    tl_hi = tl.astype(BF16)
    tl_lo = (tl - tl_hi.astype(F32)).astype(BF16)
    z = _dot(tl_hi, wg_hi) + _dot(tl_lo, wg_hi) + _dot(tl_hi, wg_lo) + bg_ref[...]
    log_a = (jnp.minimum(z, 0.0) - jnp.log1p(jnp.exp(-jnp.abs(z)))) * (1.0 / GLA_GATE_NORMALIZER)

    r = lax.broadcasted_iota(jnp.int32, (GLA_SUB, GLA_SUB), 0)
    c = lax.broadcasted_iota(jnp.int32, (GLA_SUB, GLA_SUB), 1)
    causal = ((r // CHUNK) == (c // CHUNK)) & (c <= r)
    tri = causal.astype(BF16)

    for hh in range(GLA_HEADS):
        dk = slice(hh * GLA_DK, (hh + 1) * GLA_DK)
        dv = slice(hh * GLA_DV, (hh + 1) * GLA_DV)
        o_ref[:, dv] = _gla_head(log_a[:, dk], q_ref[:, dk], k_ref[:, dk], v_ref[:, dv],
                                 go_ref[:, dv], gg_ref[...], state_ref.at[hh], causal, tri)


def _gla(proj, tail, wg_pad, b_gk, g_gla, *, batch, seq):
    t = proj.shape[0]
    steps = seq // GLA_L
    tok = lambda b, i: b * steps + i
    return pl.pallas_call(
        _gla_kernel,
        out_shape=jax.ShapeDtypeStruct((t, GLA_VAL), BF16),
        grid=(batch, steps),
        in_specs=[
            pl.BlockSpec((GLA_L, GLA_KEY), lambda b, i: (tok(b, i), COL_GQ // GLA_KEY)),
            pl.BlockSpec((GLA_L, GLA_KEY), lambda b, i: (tok(b, i), COL_GK // GLA_KEY)),
            pl.BlockSpec((GLA_L, GLA_VAL), lambda b, i: (tok(b, i), COL_GV // GLA_VAL)),
            pl.BlockSpec((GLA_L, GLA_VAL), lambda b, i: (tok(b, i), COL_GOUT // GLA_VAL)),
            pl.BlockSpec((GLA_L, TAIL_W), lambda b, i: (tok(b, i), 0)),
            pl.BlockSpec((TAIL_W, GLA_KEY), lambda b, i: (0, 0)),
            pl.BlockSpec((1, GLA_KEY), lambda b, i: (0, 0)),
            pl.BlockSpec((1, GLA_DV), lambda b, i: (0, 0)),
        ],
        out_specs=pl.BlockSpec((GLA_L, GLA_VAL), lambda b, i: (tok(b, i), 0)),
        scratch_shapes=[pltpu.VMEM((GLA_HEADS, GLA_DV, GLA_DK), F32)],
        compiler_params=_params(("parallel", "arbitrary")),
        name="gla",
    )(proj, proj, proj, proj, tail, wg_pad, b_gk, g_gla)


MRG_TM = 512


def _merge_kernel(x_ref, mod_ref, a_ref, b_ref, ga_ref, gb_ref, wa_ref, wb_ref, wo_ref, o_ref,
                  *, base):
    ya = _dot(a_ref[...], wa_ref[...])
    yb = _dot(b_ref[...], wb_ref[...])
    merged = (jax.nn.sigmoid(ga_ref[...].astype(F32)) * ya
              + jax.nn.sigmoid(gb_ref[...].astype(F32)) * yb).astype(BF16)
    gate = mod_ref[0, base + 2:base + 3, :]
    o_ref[...] = x_ref[...] + gate * _dot(merged, wo_ref[...])


def _merge(x2d, mod3, attn_o, gla_o, proj, wa, wb, wo, *, base, seq):
    t, d = x2d.shape
    tiles_per_batch = seq // MRG_TM
    row = lambda i: (i, 0)
    resident = lambda a: pl.BlockSpec(a.shape, lambda i: (0, 0), pipeline_mode=pl.Buffered(1))
    return pl.pallas_call(
        functools.partial(_merge_kernel, base=base),
        out_shape=jax.ShapeDtypeStruct((t, d), F32),
        grid=(t // MRG_TM,),
        in_specs=[
            pl.BlockSpec((MRG_TM, d), row),
            pl.BlockSpec((1, N_MOD, d), lambda i: (i // tiles_per_batch, 0, 0)),
            pl.BlockSpec((MRG_TM, attn_o.shape[1]), row),
            pl.BlockSpec((MRG_TM, gla_o.shape[1]), row),
            pl.BlockSpec((MRG_TM, d), lambda i: (i, COL_GATE_A // d)),
            pl.BlockSpec((MRG_TM, d), lambda i: (i, COL_GATE_A // d + 1)),
            resident(wa), resident(wb), resident(wo),
        ],
        out_specs=pl.BlockSpec((MRG_TM, d), row),
        compiler_params=_params(("parallel",)),
        name="merge",
    )(x2d, mod3, attn_o, gla_o, proj, proj, wa, wb, wo)


def _layer(x2d, mod3, positions, p, *, batch, seq):
    d = x2d.shape[1]
    inv_freq = ROPE_THETA ** (-jnp.arange(0, MLA_ROPE, 2, dtype=F32) / MLA_ROPE)
    invf = inv_freq.reshape(ROPE_HALF, 1)
    sgn = jnp.tile(jnp.concatenate([-jnp.ones(ROPE_HALF, F32), jnp.ones(ROPE_HALF, F32)]),
                   LANES // MLA_ROPE).reshape(1, LANES)
    n_rope = MLA_HEADS * MLA_ROPE
    lane = jnp.arange(n_rope)
    grp = (lane[:, None] // MLA_ROPE == lane[None, :] // MLA_ROPE).astype(BF16)

    assert IN_SPLITS[-1] == d and sum(IN_SPLITS) == p["w_in"].shape[1]
    w_main, w_tail, cos, sin = _relayout_w_in(p["w_in"].T, positions, invf, sgn)
    wuq = p["w_uq"].reshape(MLA_Q_RANK, MLA_HEADS, MLA_QK)
    wuq = jnp.concatenate([wuq[:, :, :MLA_NOPE].reshape(MLA_Q_RANK, -1),
                           wuq[:, :, MLA_NOPE:].reshape(MLA_Q_RANK, -1)], axis=1).astype(BF16)
    wukv = p["w_ukv"].reshape(MLA_KV_RANK, MLA_HEADS, MLA_NOPE + MLA_V)
    wukv = jnp.concatenate([wukv[:, :, :MLA_NOPE].reshape(MLA_KV_RANK, -1),
                            wukv[:, :, MLA_NOPE:].reshape(MLA_KV_RANK, -1)], axis=1).astype(BF16)
    wg_pad = jnp.zeros((TAIL_W, GLA_KEY), F32).at[TAIL_GLR:TAIL_GLR + GLA_GATE_RANK].set(p["w_gk_up"])

    row = lambda a: a.reshape(1, -1)
    q_scale = MLA_QK ** -0.5 * LOG2E
    x2d = _ffn(x2d, mod3, row(p["g_ffn1"]), p["w1_a"][None], p["w3_a"][None],
               p["w2_a"][None], row(p["g_final"]), base=0, final_norm=False, seq=seq)
    proj, tail = _inproj(x2d, mod3, row(p["g_mix"]), w_main, w_tail, base=3, seq=seq)
    q, k, v = _mla_prep(
        proj, tail, cos, sin, grp,
        (row(p["g_q_lat"]), wuq, row(p["g_qn"] * q_scale),
         row(jnp.tile(p["g_qr"], MLA_HEADS) * q_scale),
         row(p["g_kv_lat"]), wukv, row(p["g_kn"]), row(p["g_kr"])),
        batch=batch, seq=seq)
    attn_o, wa, wb, wo = _mla_attn(q, k, v, (p["w_proj_a"][None], p["w_proj_b"][None], p["w_out"][None]))
    attn_o = attn_o.reshape(batch * seq, MLA_HEADS * MLA_V)
    gla_o = _gla(proj, tail, wg_pad, row(p["b_gk"]), row(p["g_gla"]), batch=batch, seq=seq)
    x2d = _merge(x2d, mod3, attn_o, gla_o, proj, wa, wb, wo, base=3, seq=seq)
    x2d = _ffn(x2d, mod3, row(p["g_ffn2"]), p["w1_b"][None], p["w3_b"][None],
               p["w2_b"][None], row(p["g_final"]), base=6, final_norm=True, seq=seq)
    return x2d


def kernel(x, c, positions, w_ada, b_ada, g_ffn1, w1_a, w3_a, w2_a, g_mix, w_in, g_q_lat, w_uq, g_qn, g_qr, g_kv_lat, w_ukv, g_kn, g_kr, w_gk_up, b_gk, g_gla, w_proj_a, w_proj_b, w_out, g_ffn2, w1_b, w3_b, w2_b, g_final):
    batch, seq, d = x.shape
    depth = w_ada.shape[0]
    assert depth == 1, "the final norm is fused into the last FFN of a single layer"
    names = ("g_ffn1", "w1_a", "w3_a", "w2_a", "g_mix", "w_in", "g_q_lat", "w_uq", "g_qn", "g_qr",
             "g_kv_lat", "w_ukv", "g_kn", "g_kr", "w_gk_up", "b_gk", "g_gla", "w_proj_a", "w_proj_b",
             "w_out", "g_ffn2", "w1_b", "w3_b", "w2_b", "g_final")
    stacked = (g_ffn1, w1_a, w3_a, w2_a, g_mix, w_in, g_q_lat, w_uq, g_qn, g_qr, g_kv_lat, w_ukv,
               g_kn, g_kr, w_gk_up, b_gk, g_gla, w_proj_a, w_proj_b, w_out, g_ffn2, w1_b, w3_b, w2_b,
               g_final)
    x2d = x.reshape(batch * seq, d)
    p = {n: a[0] for n, a in zip(names, stacked)}
    mod3 = _adaln(c, w_ada[0], b_ada[0]).reshape(batch, N_MOD, d)
    x2d = _layer(x2d, mod3, positions, p, batch=batch, seq=seq)
    return x2d.reshape(batch, seq, d)
```

```python
import functools

import jax
import jax.numpy as jnp
from jax import lax
from jax.experimental import pallas as pl
from jax.experimental.pallas import tpu as pltpu

F32 = jnp.float32
BF16 = jnp.bfloat16

EPS = 1e-6
LOG2E = 1.4426950408889634
CHUNK = 64
MLA_HEADS = 8
MLA_Q_RANK = 512
MLA_KV_RANK = 512
MLA_NOPE = 128
MLA_ROPE = 64
MLA_V = 128
MLA_QK = MLA_NOPE + MLA_ROPE
ROPE_THETA = 10000.0
GLA_HEADS = 4
GLA_DK = 128
GLA_DV = 256
GLA_GATE_RANK = 16
GLA_GATE_NORMALIZER = 16.0
GLA_KEY = GLA_HEADS * GLA_DK
GLA_VAL = GLA_HEADS * GLA_DV
N_MOD = 9

LANES = 128
SUBLANES = 8
VMEM_LIMIT = 56 * 1024 * 1024

COL_QLAT = 0
COL_KVLAT = COL_QLAT + MLA_Q_RANK
COL_GQ = COL_KVLAT + MLA_KV_RANK
COL_GK = COL_GQ + GLA_KEY
COL_GV = COL_GK + GLA_KEY
COL_GOUT = COL_GV + GLA_VAL
COL_GATE_A = COL_GOUT + GLA_VAL
TAIL_W = LANES
TAIL_KROPE = 0
TAIL_GLR = MLA_ROPE


def _params(sem, vmem_limit=VMEM_LIMIT):
    return pltpu.CompilerParams(dimension_semantics=sem, vmem_limit_bytes=vmem_limit)


def _dot(a, b):
    return jnp.dot(a, b, preferred_element_type=F32)


def _dot_t0(a, b):
    return lax.dot_general(a, b, (((0,), (0,)), ((), ())), preferred_element_type=F32)


def _dot_t1(a, b):
    return lax.dot_general(a, b, (((1,), (1,)), ((), ())), preferred_element_type=F32)


def _split_dot(x, w_bf16):
    hi = x.astype(BF16)
    lo = (x - hi.astype(F32)).astype(BF16)
    return _dot(hi, w_bf16) + _dot(lo, w_bf16)


def _silu(x):
    return x * jax.nn.sigmoid(x)


def _rms(x, g):
    ms = jnp.mean(x * x, axis=-1, keepdims=True)
    return x * lax.rsqrt(ms + EPS) * g


ADA_TN = 2048
ADA_RC = 64


def _adaln_kernel(ct_ref, w_ref, b_ref, o_ref, s_ref):
    d, nb = ct_ref.shape
    tn = w_ref.shape[1]

    @pl.when(pl.program_id(0) == 0)
    def _():
        s = _silu(ct_ref[...])
        for b in range(nb):
            s_ref[b] = jnp.broadcast_to(s[:, b:b + 1], (d, LANES))

    def body(k, acc):
        rows = pl.ds(pl.multiple_of(k * ADA_RC, ADA_RC), ADA_RC)
        out = []
        for b in range(nb):
            sb = s_ref[b, rows, :]
            cols = []
            for col in range(tn // LANES):
                p = w_ref[rows, col * LANES:(col + 1) * LANES] * sb
                cols.append(p.reshape(ADA_RC // SUBLANES, SUBLANES, LANES).sum(axis=0))
            out.append(acc[b] + jnp.concatenate(cols, axis=-1))
        return tuple(out)

    init = tuple(jnp.zeros((SUBLANES, tn), F32) for _ in range(nb))
    acc = lax.fori_loop(0, d // ADA_RC, body, init, unroll=2)
    rows = [jnp.sum(a, axis=0, keepdims=True) for a in acc]
    o_ref[...] = jnp.concatenate(rows, axis=0) + b_ref[...]


def _adaln(c, w_ada, b_ada):
    nb, d = c.shape
    n = w_ada.shape[1]
    return pl.pallas_call(
        _adaln_kernel,
        out_shape=jax.ShapeDtypeStruct((nb, n), F32),
        grid=(n // ADA_TN,),
        in_specs=[
            pl.BlockSpec((d, nb), lambda j: (0, 0)),
            pl.BlockSpec((d, ADA_TN), lambda j: (0, j)),
            pl.BlockSpec((1, ADA_TN), lambda j: (0, j)),
        ],
        out_specs=pl.BlockSpec((nb, ADA_TN), lambda j: (0, j)),
        scratch_shapes=[pltpu.VMEM((nb, d, LANES), F32)],
        compiler_params=_params(("arbitrary",)),
        name="adaln",
    )(c.T, w_ada, b_ada.reshape(1, n))


def _mod_norm(x, g, mod_ref, base):
    shift = mod_ref[0, base:base + 1, :]
    scale = mod_ref[0, base + 1:base + 2, :]
    return _rms(x, g) * (1.0 + scale) + shift


FFN_TM = 512
FFN_TF = 512
FFN_HEAD_TILES = 2
FFN_HEAD_TF = 256
FFN_GROUP = 256
FFN_WSLOTS = 3
FFN_VMEM_LIMIT = 60 * 1024 * 1024
FFN_AHEAD_CHUNKS = 8
FFN_AHEAD_ROWS = FFN_TM // FFN_AHEAD_CHUNKS


def _swiglu_tile(h, w1g, w3g, w2t):
    upd = None
    for s in range(w1g.shape[0]):
        a = _dot(h, w1g[s])
        b = _dot(h, w3g[s])
        p = (_silu(a) * b).astype(BF16)
        part = _dot(p, w2t[s * FFN_GROUP:(s + 1) * FFN_GROUP, :])
        upd = part if upd is None else upd + part
    return upd


def _ffn_finish(x_ref, mod_ref, gf_ref, o_ref, *, base, final_norm):
    gate = mod_ref[0, base + 2:base + 3, :]
    y = x_ref[...] + 0.5 * gate * o_ref[...]
    if final_norm:
        y = _rms(y, gf_ref[...])
    o_ref[...] = y


def _ffn_head_kernel(x_ref, mod_ref, g_ref, w1_ref, w3_ref, w2_ref, gf_ref,
                     o_ref, w1b_ref, w3b_ref, w2b_ref, h_ref, *, base, final_norm):
    j = pl.program_id(1)

    @pl.when(j == 0)
    def _():
        h_ref[...] = _mod_norm(x_ref[...], g_ref[...], mod_ref, base).astype(BF16)
        o_ref[...] = jnp.zeros_like(o_ref)

    w1 = w1_ref[0].astype(BF16)
    w3 = w3_ref[0].astype(BF16)
    w2 = w2_ref[0].astype(BF16)
    w1b_ref[0] = w1
    w3b_ref[0] = w3
    w2b_ref[...] = w2
    o_ref[...] += _swiglu_tile(h_ref[...], w1[None], w3[None], w2)

    @pl.when(j == pl.num_programs(1) - 1)
    def _():
        _ffn_finish(x_ref, mod_ref, gf_ref, o_ref, base=base, final_norm=final_norm)


def _ffn_kernel(x_ref, xn_ref, mod_ref, modn_ref, g_ref, w1_hbm, w3_hbm, w2_hbm, gf_ref, *rest,
                base, final_norm, n_steps, head_tiles):
    if head_tiles:
        y0_ref, *rest = rest
    o_ref, h_ref, w1_buf, w3_buf, w2_buf, sem = rest
    i = pl.program_id(0)
    h_slot = i % 2
    first = (i - head_tiles) * n_steps
    groups = FFN_TF // FFN_GROUP
    ahead = FFN_WSLOTS - 1

    def copies(j, slot):
        jj = j % n_steps
        return (pltpu.make_async_copy(w1_hbm.at[pl.ds(jj * groups, groups)], w1_buf.at[slot], sem.at[slot, 0]),
                pltpu.make_async_copy(w3_hbm.at[pl.ds(jj * groups, groups)], w3_buf.at[slot], sem.at[slot, 1]),
                pltpu.make_async_copy(w2_hbm.at[pl.ds(jj * FFN_TF, FFN_TF)], w2_buf.at[slot], sem.at[slot, 2]))

    if head_tiles:
        @pl.when(i < head_tiles)
        def _():
            o_ref[...] = y0_ref[...]

    @pl.when(i >= head_tiles)
    def _():
        @pl.when(i == head_tiles)
        def _():
            for j in range(ahead):
                for c in copies(j, j):
                    c.start()
            h_ref[h_slot] = _mod_norm(x_ref[...], g_ref[...], mod_ref, base).astype(BF16)

        for j in range(n_steps):
            slot = (first + j) % FFN_WSLOTS
            for c in copies(j, slot):
                c.wait()
            if j < FFN_AHEAD_CHUNKS:
                rows = slice(j * FFN_AHEAD_ROWS, (j + 1) * FFN_AHEAD_ROWS)
                h_next = _mod_norm(xn_ref[rows, :], g_ref[...], modn_ref, base).astype(BF16)
            upd = _swiglu_tile(h_ref[h_slot], w1_buf.at[slot], w3_buf.at[slot], w2_buf[slot])
            for c in copies(j + ahead, (first + j + ahead) % FFN_WSLOTS):
                c.start()
            if j == 0:
                o_ref[...] = upd
            else:
                o_ref[...] += upd
            if j < FFN_AHEAD_CHUNKS:
                h_ref[1 - h_slot, rows, :] = h_next
        _ffn_finish(x_ref, mod_ref, gf_ref, o_ref, base=base, final_norm=final_norm)

        @pl.when(i == pl.num_programs(0) - 1)
        def _():
            for j in range(n_steps, n_steps + ahead):
                for c in copies(j, (first + j) % FFN_WSLOTS):
                    c.wait()


def _ffn_head(x2d, mod3, g, w1, w3, w2, g_final, *, base, final_norm, seq):
    t, d = x2d.shape
    dff = w1.shape[2]
    tiles_per_batch = seq // FFN_TM
    const = lambda i, j: (0, 0)
    kw = dict(base=base, final_norm=final_norm)
    tag = "_final" if final_norm else ""
    common = [
        pl.BlockSpec((1, N_MOD, d), lambda i, j: (i // tiles_per_batch, 0, 0)),
        pl.BlockSpec((1, d), const),
    ]
    hf = FFN_HEAD_TF
    head_rows = FFN_HEAD_TILES * FFN_TM
    assert head_rows <= seq
    assert hf == FFN_GROUP
    w13_shape = jax.ShapeDtypeStruct((dff // FFN_GROUP, d, FFN_GROUP), BF16)
    w13_spec = pl.BlockSpec((1, d, FFN_GROUP), lambda i, j: (j, 0, 0))
    return pl.pallas_call(
        functools.partial(_ffn_head_kernel, **kw),
        out_shape=(jax.ShapeDtypeStruct((head_rows, d), F32), w13_shape, w13_shape,
                   jax.ShapeDtypeStruct((dff, d), BF16)),
        grid=(1, dff // hf),
        in_specs=[pl.BlockSpec((head_rows, d), const, pipeline_mode=pl.Buffered(1))] + common + [
            pl.BlockSpec((1, d, hf), lambda i, j: (0, 0, j)),
            pl.BlockSpec((1, d, hf), lambda i, j: (0, 0, j)),
            pl.BlockSpec((1, hf, d), lambda i, j: (0, j, 0)),
            pl.BlockSpec((1, d), const),
        ],
        out_specs=(pl.BlockSpec((head_rows, d), const), w13_spec, w13_spec,
                   pl.BlockSpec((hf, d), lambda i, j: (j, 0))),
        scratch_shapes=[pltpu.VMEM((head_rows, d), BF16)],
        compiler_params=_params(("arbitrary", "arbitrary")),
        name="ffn_head" + tag,
    )(x2d, mod3, g, w1, w3, w2, g_final)


def _ffn(x2d, mod3, g, w1b, w3b, w2b, g_final, y0, *, base, final_norm, seq):
    t, d = x2d.shape
    dff = w2b.shape[0]
    tiles_per_batch = seq // FFN_TM
    kw = dict(base=base, final_norm=final_norm)
    tag = "_final" if final_norm else ""
    head_tiles = 0 if y0 is None else y0.shape[0] // FFN_TM
    n_tiles = t // FFN_TM
    assert dff // FFN_TF >= FFN_AHEAD_CHUNKS and n_tiles > head_tiles
    nxt = lambda i: jnp.minimum(i + 1, n_tiles - 1)
    groups = FFN_TF // FFN_GROUP
    hbm = pl.BlockSpec(memory_space=pl.ANY)
    head_specs, head_args = [], []
    if head_tiles:
        head_specs = [pl.BlockSpec((FFN_TM, d), lambda i: (jnp.minimum(i, head_tiles - 1), 0),
                                   pipeline_mode=pl.Buffered(1))]
        head_args = [y0]
    return pl.pallas_call(
        functools.partial(_ffn_kernel, n_steps=dff // FFN_TF, head_tiles=head_tiles, **kw),
        out_shape=jax.ShapeDtypeStruct((t, d), F32),
        grid=(n_tiles,),
        in_specs=[
            pl.BlockSpec((FFN_TM, d), lambda i: (i, 0)),
            pl.BlockSpec((FFN_TM, d), lambda i: (nxt(i), 0)),
            pl.BlockSpec((1, N_MOD, d), lambda i: (i // tiles_per_batch, 0, 0)),
            pl.BlockSpec((1, N_MOD, d), lambda i: (nxt(i) // tiles_per_batch, 0, 0)),
            pl.BlockSpec((1, d), lambda i: (0, 0)),
            hbm, hbm, hbm,
            pl.BlockSpec((1, d), lambda i: (0, 0)),
            *head_specs,
        ],
        out_specs=pl.BlockSpec((FFN_TM, d), lambda i: (i, 0)),
        scratch_shapes=[pltpu.VMEM((2, FFN_TM, d), BF16),
                        pltpu.VMEM((FFN_WSLOTS, groups, d, FFN_GROUP), BF16),
                        pltpu.VMEM((FFN_WSLOTS, groups, d, FFN_GROUP), BF16),
                        pltpu.VMEM((FFN_WSLOTS, FFN_TF, d), BF16),
                        pltpu.SemaphoreType.DMA((FFN_WSLOTS, 3))],
        compiler_params=_params(("arbitrary",), FFN_VMEM_LIMIT),
        name="ffn" + tag,
    )(x2d, x2d, mod3, mod3, g, w1b, w3b, w2b, g_final, *head_args)


IN_SPLITS = (MLA_Q_RANK, MLA_KV_RANK, MLA_ROPE, GLA_KEY, GLA_KEY, GLA_VAL, GLA_GATE_RANK, GLA_VAL,
             2048, 2048)
MAIN_ORDER = (0, 1, 3, 4, 5, 7, 8, 9)
TAIL_ORDER = (2, 6)
RELAYOUT_ROWS = 1024


def _segments(order):
    starts = [sum(IN_SPLITS[:i]) for i in range(len(IN_SPLITS))]
    segs, dst = [], 0
    for i in order:
        src, n = starts[i], IN_SPLITS[i]
        if segs and segs[-1][0] + segs[-1][2] == src:
            segs[-1] = (segs[-1][0], segs[-1][1], segs[-1][2] + n)
        else:
            segs.append((src, dst, n))
        dst += n
    return segs, dst


def _relayout_kernel(lo_ref, hi_ref, kr_ref, glr_ref, pos_ref, invf_ref, sgn_ref,
                     main_ref, tail_ref, cos_ref, sin_ref):
    ang_t = invf_ref[...] * pos_ref[0].astype(F32)
    lane_reps = LANES // ROPE_HALF
    cos_ref[...] = jnp.tile(jnp.cos(ang_t).T, (1, lane_reps))
    sin_ref[...] = jnp.tile(jnp.sin(ang_t).T, (1, lane_reps)) * sgn_ref[...]

    r = pl.program_id(0)
    rows = RELAYOUT_ROWS
    for src, dst, n in _segments(MAIN_ORDER)[0]:
        shift = src - dst

        @pl.when((r >= dst // rows) & (r < (dst + n) // rows))
        def _(shift=shift):
            main_ref[:rows - shift, :] = lo_ref[shift:, :].astype(BF16)
            if shift:
                main_ref[rows - shift:, :] = hi_ref[:shift, :].astype(BF16)

    tail_ref[:MLA_ROPE, :] = kr_ref[...].astype(BF16)
    tail_ref[MLA_ROPE:MLA_ROPE + GLA_GATE_RANK, :] = glr_ref[...].astype(BF16)
    pad = TAIL_W - MLA_ROPE - GLA_GATE_RANK
    tail_ref[MLA_ROPE + GLA_GATE_RANK:, :] = jnp.zeros((pad, tail_ref.shape[1]), BF16)


def _relayout_w_in(w_in_t, positions, invf, sgn):
    d_in, d = w_in_t.shape
    segs, n_main = _segments(MAIN_ORDER)
    rows = RELAYOUT_ROWS
    steps = n_main // rows
    assert all(dst % rows == 0 and n % rows == 0 and 0 <= src - dst <= LANES and (src - dst) % 16 == 0
               for src, dst, n in segs)
    (kr_src, _, _), (glr_src, _, _) = _segments(TAIL_ORDER)[0]
    assert kr_src % MLA_ROPE == 0 and glr_src % GLA_GATE_RANK == 0
    t = positions.size
    tok = t // steps
    assert tok * steps == t and tok % LANES == 0
    const = lambda a: pl.BlockSpec(a.shape, lambda r: (0, 0))
    return pl.pallas_call(
        _relayout_kernel,
        out_shape=(jax.ShapeDtypeStruct((n_main, d), BF16), jax.ShapeDtypeStruct((TAIL_W, d), BF16),
                   jax.ShapeDtypeStruct((t, LANES), F32), jax.ShapeDtypeStruct((t, LANES), F32)),
        grid=(steps,),
        in_specs=[pl.BlockSpec((rows, d), lambda r: (r, 0)),
                  pl.BlockSpec((LANES, d), lambda r: ((r + 1) * (rows // LANES), 0)),
                  pl.BlockSpec((MLA_ROPE, d), lambda r: (kr_src // MLA_ROPE, 0)),
                  pl.BlockSpec((GLA_GATE_RANK, d), lambda r: (glr_src // GLA_GATE_RANK, 0)),
                  pl.BlockSpec((1, 1, tok), lambda r: (r, 0, 0)),
                  const(invf), const(sgn)],
        out_specs=(pl.BlockSpec((rows, d), lambda r: (r, 0)),
                   pl.BlockSpec((TAIL_W, d), lambda r: (0, 0)),
                   pl.BlockSpec((tok, LANES), lambda r: (r, 0)),
                   pl.BlockSpec((tok, LANES), lambda r: (r, 0))),
        compiler_params=_params(("arbitrary",)),
        name="w_in_layout",
    )(w_in_t, w_in_t, w_in_t, w_in_t, positions.reshape(steps, 1, tok), invf, sgn)


INP_TM = 1024
INP_TN = 2048
INP_VMEM_LIMIT = 60 * 1024 * 1024


def _inproj_kernel(x_ref, mod_ref, g_ref, w_ref, wt_ref, w1f_ref, w3f_ref,
                   o_ref, tail_ref, w1b_ref, w3b_ref, h_ref, *, base):
    for src, dst in ((w1f_ref, w1b_ref), (w3f_ref, w3b_ref)):
        for grp in range(dst.shape[0]):
            dst[grp] = src[0, :, grp * FFN_GROUP:(grp + 1) * FFN_GROUP].astype(BF16)

    @pl.when(pl.program_id(1) == 0)
    def _():
        h = _mod_norm(x_ref[...], g_ref[...], mod_ref, base).astype(BF16)
        h_ref[...] = h
        tail_ref[...] = _dot_t1(h, wt_ref[...])

    o_ref[...] = _dot_t1(h_ref[...], w_ref[...]).astype(BF16)


def _inproj(x2d, mod3, g, w_main, w_tail, w1, w3, *, base, seq):
    t, d = x2d.shape
    n = w_main.shape[0]
    dff = w1.shape[2]
    tiles_per_batch = seq // INP_TM
    n_steps = n // INP_TN
    steps = (t // INP_TM) * n_steps
    step = lambda i, j: i * n_steps + j
    r13, n_groups = d // steps, dff // FFN_GROUP
    assert r13 * steps == d and r13 % 16 == 0
    w13_in = pl.BlockSpec((1, r13, dff), lambda i, j: (0, step(i, j), 0))
    w13_out = pl.BlockSpec((n_groups, r13, FFN_GROUP), lambda i, j: (0, step(i, j), 0))
    w13_shape = jax.ShapeDtypeStruct((n_groups, d, FFN_GROUP), BF16)
    return pl.pallas_call(
        functools.partial(_inproj_kernel, base=base),
        out_shape=(jax.ShapeDtypeStruct((t, n), BF16),
                   jax.ShapeDtypeStruct((t, TAIL_W), F32),
                   w13_shape, w13_shape),
        grid=(t // INP_TM, n_steps),
        in_specs=[
            pl.BlockSpec((INP_TM, d), lambda i, j: (i, 0)),
            pl.BlockSpec((1, N_MOD, d), lambda i, j: (i // tiles_per_batch, 0, 0)),
            pl.BlockSpec((1, d), lambda i, j: (0, 0)),
            pl.BlockSpec((INP_TN, d), lambda i, j: (j, 0)),
            pl.BlockSpec((TAIL_W, d), lambda i, j: (0, 0)),
            w13_in, w13_in,
        ],
        out_specs=(pl.BlockSpec((INP_TM, INP_TN), lambda i, j: (i, j)),
                   pl.BlockSpec((INP_TM, TAIL_W), lambda i, j: (i, 0)),
                   w13_out, w13_out),
        scratch_shapes=[pltpu.VMEM((INP_TM, d), BF16)],
        compiler_params=_params(("arbitrary", "arbitrary"), INP_VMEM_LIMIT),
        name="in_proj",
    )(x2d, mod3, g, w_main, w_tail, w1, w3)


PREP_TM = 1024
ROPE_HALF = MLA_ROPE // 2


def _rope_rotate(x, cos, sin_signed):
    width = x.shape[-1]
    lane = lax.broadcasted_iota(jnp.int32, x.shape, 1)
    first_half = (lane % MLA_ROPE) < ROPE_HALF
    partner = jnp.where(first_half,
                        pltpu.roll(x, width - ROPE_HALF, 1),
                        pltpu.roll(x, ROPE_HALF, 1))
    return x * cos + partner * sin_signed


def _prep_kernel(qlat_ref, kvlat_ref, tail_ref, cos_ref, sin_ref,
                 gql_ref, wuq_ref, gqn_ref, gqr_ref, gkvl_ref, wukv_ref, gkn_ref, gkr_ref,
                 grp_ref, q_ref, k_ref, v_ref):
    cos = cos_ref[...]
    sin_signed = sin_ref[...]
    n_rope = MLA_HEADS * MLA_ROPE
    cos_q = jnp.concatenate([cos] * (n_rope // LANES), axis=-1)
    sin_q = jnp.concatenate([sin_signed] * (n_rope // LANES), axis=-1)

    cq = _rms(qlat_ref[...].astype(F32), gql_ref[...]).astype(BF16)
    ckv = _rms(kvlat_ref[...].astype(F32), gkvl_ref[...]).astype(BF16)
    n_nope = MLA_HEADS * MLA_NOPE
    pair = 2 * MLA_NOPE

    for hp in range(MLA_HEADS // 2):
        cols = slice(hp * pair, (hp + 1) * pair)
        qp = _dot(cq, wuq_ref[:, cols])
        kp = _dot(ckv, wukv_ref[:, cols])
        for hh in range(2):
            h = 2 * hp + hh
            sub = slice(hh * MLA_NOPE, (hh + 1) * MLA_NOPE)
            q_ref[0, h, :, :MLA_NOPE] = _rms(qp[:, sub], gqn_ref[...]).astype(BF16)
            k_ref[0, h, :, :MLA_NOPE] = _rms(kp[:, sub], gkn_ref[...]).astype(BF16)
        v_ref[0, :, cols] = _dot(ckv, wukv_ref[:, n_nope + hp * pair:n_nope + (hp + 1) * pair]).astype(BF16)

    qr = _dot(cq, wuq_ref[:, n_nope:])
    ssq = _split_dot(qr * qr, grp_ref[...])
    qr = qr * lax.rsqrt(ssq * (1.0 / MLA_ROPE) + EPS) * gqr_ref[...]
    qr = _rope_rotate(qr, cos_q, sin_q).astype(BF16)
    kr = tail_ref[:, TAIL_KROPE:TAIL_KROPE + MLA_ROPE]
    kr = _rms(kr, gkr_ref[...])
    kr = _rope_rotate(kr, cos[:, :MLA_ROPE], sin_signed[:, :MLA_ROPE]).astype(BF16)
    for h in range(MLA_HEADS):
        q_ref[0, h, :, MLA_NOPE:] = qr[:, h * MLA_ROPE:(h + 1) * MLA_ROPE]
        k_ref[0, h, :, MLA_NOPE:] = kr


def _mla_prep(proj, tail, cos, sin, grp, weights, *, batch, seq):
    t = proj.shape[0]
    tiles_per_batch = seq // PREP_TM
    row = lambda i: (i, 0)
    const = lambda i: (0, 0)
    (gql, wuq, gqn, gqr, gkvl, wukv, gkn, gkr) = weights
    head_out = pl.BlockSpec((1, MLA_HEADS, PREP_TM, MLA_QK),
                            lambda i: (i // tiles_per_batch, 0, i % tiles_per_batch, 0))
    full = lambda a: pl.BlockSpec(a.shape, const)
    return pl.pallas_call(
        _prep_kernel,
        out_shape=(jax.ShapeDtypeStruct((batch, MLA_HEADS, seq, MLA_QK), BF16),
                   jax.ShapeDtypeStruct((batch, MLA_HEADS, seq, MLA_QK), BF16),
                   jax.ShapeDtypeStruct((batch, seq, MLA_HEADS * MLA_V), BF16)),
        grid=(t // PREP_TM,),
        in_specs=[
            pl.BlockSpec((PREP_TM, MLA_Q_RANK), lambda i: (i, COL_QLAT // MLA_Q_RANK)),
            pl.BlockSpec((PREP_TM, MLA_KV_RANK), lambda i: (i, COL_KVLAT // MLA_KV_RANK)),
            pl.BlockSpec((PREP_TM, TAIL_W), row),
            pl.BlockSpec((PREP_TM, LANES), row),
            pl.BlockSpec((PREP_TM, LANES), row),
            full(gql), full(wuq), full(gqn), full(gqr),
            full(gkvl), full(wukv), full(gkn), full(gkr),
            full(grp),
        ],
        out_specs=(head_out, head_out,
                   pl.BlockSpec((1, PREP_TM, MLA_HEADS * MLA_V),
                                lambda i: (i // tiles_per_batch, i % tiles_per_batch, 0))),
        compiler_params=_params(("parallel",)),
        name="mla_prep",
    )(proj, proj, tail, cos, sin, gql, wuq, gqn, gqr, gkvl, wukv, gkn, gkr, grp)


ATT_T = 1024
ATT_H = ATT_T // 2


def _attn_kernel(q_ref, k_ref, v_ref, *rest):
    n_cast = (len(rest) - 1) // 2
    o_ref = rest[n_cast]
    for src, dst in zip(rest[:n_cast], rest[n_cast + 1:]):
        dst[...] = src[0].astype(BF16)

    n_tiles = k_ref.shape[2] // ATT_T
    ones = jnp.ones((ATT_T, MLA_V), BF16)
    r = lax.broadcasted_iota(jnp.int32, (ATT_H, ATT_H), 0)
    c = lax.broadcasted_iota(jnp.int32, (ATT_H, ATT_H), 1)
    visible = (c // CHUNK) <= (r // CHUNK)
    state = {}

    def update(chain, s, v2):
        m_cur = jnp.max(s, axis=-1, keepdims=True)
        if chain not in state:
            m_next = jnp.broadcast_to(m_cur, (ATT_H, LANES))
        else:
            m_prev, l_prev, acc_prev = state[chain]
            m_next = jnp.maximum(m_prev, m_cur)
            alpha = jnp.exp2(m_prev - m_next)
        p = jnp.exp2(s - jnp.tile(m_next, (1, s.shape[1] // LANES)))
        pv = _dot(p.astype(BF16), v2)
        acc, l = pv[:, :MLA_V], pv[:, MLA_V:]
        if chain in state:
            acc, l = alpha * acc_prev + acc, alpha * l_prev + l
        state[chain] = (m_next, l, acc)

    for t in range(n_tiles):
        rows = slice(t * ATT_T, (t + 1) * ATT_T)
        k = k_ref[0, 0, rows, :]
        v2 = jnp.concatenate([v_ref[0, rows, :], ones], axis=-1)
        for qt in range(t, n_tiles):
            q0 = q_ref[0, 0, qt * ATT_T:qt * ATT_T + ATT_H, :]
            q1 = q_ref[0, 0, qt * ATT_T + ATT_H:(qt + 1) * ATT_T, :]
            if t < qt:
                update((qt, 0), _dot_t1(q0, k), v2)
                update((qt, 1), _dot_t1(q1, k), v2)
            else:
                update((qt, 0), jnp.where(visible, _dot_t1(q0, k[:ATT_H]), -jnp.inf), v2[:ATT_H])
                s1 = _dot_t1(q1, k)
                s1 = jnp.concatenate([s1[:, :ATT_H], jnp.where(visible, s1[:, ATT_H:], -jnp.inf)],
                                     axis=-1)
                update((qt, 1), s1, v2)
                for half in range(2):
                    _, l, acc = state[(qt, half)]
                    lo = qt * ATT_T + half * ATT_H
                    o_ref[0, lo:lo + ATT_H, :] = (acc / l).astype(BF16)


def _mla_attn(q, k, v, cast_weights):
    batch, heads, seq, _ = q.shape
    steps = batch * heads
    step = lambda b, h: b * heads + h
    cast_in, cast_out, cast_shapes = [], [], []
    for w in cast_weights:
        _, rows, cols = w.shape
        blk = rows // steps
        assert blk * steps == rows and blk % 16 == 0
        cast_in.append(pl.BlockSpec((1, blk, cols), lambda b, h: (0, step(b, h), 0)))
        cast_out.append(pl.BlockSpec((blk, cols), lambda b, h: (step(b, h), 0)))
        cast_shapes.append(jax.ShapeDtypeStruct((rows, cols), BF16))
    return pl.pallas_call(
        _attn_kernel,
        out_shape=(jax.ShapeDtypeStruct((batch, seq, heads * MLA_V), BF16), *cast_shapes),
        grid=(batch, heads),
        in_specs=[
            pl.BlockSpec((1, 1, seq, MLA_QK), lambda b, h: (b, h, 0, 0)),
            pl.BlockSpec((1, 1, seq, MLA_QK), lambda b, h: (b, h, 0, 0)),
            pl.BlockSpec((1, seq, MLA_V), lambda b, h: (b, 0, h)),
            *cast_in,
        ],
        out_specs=(pl.BlockSpec((1, seq, MLA_V), lambda b, h: (b, 0, h)), *cast_out),
        compiler_params=_params(("parallel", "parallel")),
        name="mla_attn",
    )(q, k, v, *cast_weights)


GLA_L = 1024
GLA_NC = GLA_L // CHUNK
GLA_SUB = 512


def _gla_head(log_a, q, k, v, go, gg, state_ref, causal, tri):
    sub_nc = GLA_SUB // CHUNK
    q_dec, k_end, o_intra, b_ends = [], [], [], []
    for sb in range(GLA_L // GLA_SUB):
        rows = slice(sb * GLA_SUB, (sb + 1) * GLA_SUB)
        la = log_a[rows]
        la_hi = la.astype(BF16)
        la_lo = (la - la_hi.astype(F32)).astype(BF16)
        cum = _dot(tri, jnp.concatenate([la_hi, la_lo], axis=-1))
        b_cum = cum[:, :GLA_DK] + cum[:, GLA_DK:]
        b3 = b_cum.reshape(sub_nc, CHUNK, GLA_DK)
        b_end = b3[:, CHUNK - 1:CHUNK, :]
        b_tot = jnp.broadcast_to(b_end, b3.shape).reshape(GLA_SUB, GLA_DK)
        b_ends.append(jnp.exp(b_end))

        qf = q[rows].astype(F32) * (GLA_DK ** -0.5)
        kf = k[rows].astype(F32)
        qd = (qf * jnp.exp(b_cum)).astype(BF16)
        kd = (kf * jnp.exp(-b_cum)).astype(BF16)
        q_dec.append(qd)
        k_end.append((kf * jnp.exp(b_tot - b_cum)).astype(BF16))
        attn = jnp.where(causal, _dot_t1(qd, kd), 0.0).astype(BF16)
        o_intra.append(_dot(attn, v[rows]))

    state = state_ref[...]
    outs = []
    for n in range(GLA_NC):
        sb, rows = n // sub_nc, slice((n % sub_nc) * CHUNK, (n % sub_nc + 1) * CHUNK)
        outs.append(o_intra[sb][rows] + _dot_t1(q_dec[sb][rows], state.astype(BF16)))
        v_n = v[n * CHUNK:(n + 1) * CHUNK]
        state = state * b_ends[sb][n % sub_nc] + _dot_t0(v_n, k_end[sb][rows])
    state_ref[...] = state

    o = jnp.concatenate(outs, axis=0)
    return (_rms(o, gg) * _silu(go.astype(F32))).astype(BF16)


def _gla_kernel(q_ref, k_ref, v_ref, go_ref, tail_ref, wg_ref, bg_ref, gg_ref,
                o_ref, state_ref):
    @pl.when(pl.program_id(1) == 0)
    def _():
        state_ref[...] = jnp.zeros_like(state_ref)

    wg = wg_ref[...]
    wg_hi = wg.astype(BF16)
    wg_lo = (wg - wg_hi.astype(F32)).astype(BF16)
    tl = tail_ref[...]
    tl_hi = tl.astype(BF16)
    tl_lo = (tl - tl_hi.astype(F32)).astype(BF16)
    z = _dot(tl_hi, wg_hi) + _dot(tl_lo, wg_hi) + _dot(tl_hi, wg_lo) + bg_ref[...]
    log_a = (jnp.minimum(z, 0.0) - jnp.log1p(jnp.exp(-jnp.abs(z)))) * (1.0 / GLA_GATE_NORMALIZER)

    r = lax.broadcasted_iota(jnp.int32, (GLA_SUB, GLA_SUB), 0)
    c = lax.broadcasted_iota(jnp.int32, (GLA_SUB, GLA_SUB), 1)
    causal = ((r // CHUNK) == (c // CHUNK)) & (c <= r)
    tri = causal.astype(BF16)

    for hh in range(GLA_HEADS):
        dk = slice(hh * GLA_DK, (hh + 1) * GLA_DK)
        dv = slice(hh * GLA_DV, (hh + 1) * GLA_DV)
        o_ref[:, dv] = _gla_head(log_a[:, dk], q_ref[:, dk], k_ref[:, dk], v_ref[:, dv],
                                 go_ref[:, dv], gg_ref[...], state_ref.at[hh], causal, tri)


def _gla(proj, tail, wg_pad, b_gk, g_gla, *, batch, seq):
    t = proj.shape[0]
    steps = seq // GLA_L
    tok = lambda b, i: b * steps + i
    return pl.pallas_call(
        _gla_kernel,
        out_shape=jax.ShapeDtypeStruct((t, GLA_VAL), BF16),
        grid=(batch, steps),
        in_specs=[
            pl.BlockSpec((GLA_L, GLA_KEY), lambda b, i: (tok(b, i), COL_GQ // GLA_KEY)),
            pl.BlockSpec((GLA_L, GLA_KEY), lambda b, i: (tok(b, i), COL_GK // GLA_KEY)),
            pl.BlockSpec((GLA_L, GLA_VAL), lambda b, i: (tok(b, i), COL_GV // GLA_VAL)),
            pl.BlockSpec((GLA_L, GLA_VAL), lambda b, i: (tok(b, i), COL_GOUT // GLA_VAL)),
            pl.BlockSpec((GLA_L, TAIL_W), lambda b, i: (tok(b, i), 0)),
            pl.BlockSpec((TAIL_W, GLA_KEY), lambda b, i: (0, 0)),
            pl.BlockSpec((1, GLA_KEY), lambda b, i: (0, 0)),
            pl.BlockSpec((1, GLA_DV), lambda b, i: (0, 0)),
        ],
        out_specs=pl.BlockSpec((GLA_L, GLA_VAL), lambda b, i: (tok(b, i), 0)),
        scratch_shapes=[pltpu.VMEM((GLA_HEADS, GLA_DV, GLA_DK), F32)],
        compiler_params=_params(("parallel", "arbitrary")),
        name="gla",
    )(proj, proj, proj, proj, tail, wg_pad, b_gk, g_gla)


MRG_TM = 512


def _merge_kernel(x_ref, mod_ref, a_ref, b_ref, ga_ref, gb_ref, wa_ref, wb_ref, wo_ref, o_ref,
                  *, base):
    ya = _dot(a_ref[...], wa_ref[...])
    yb = _dot(b_ref[...], wb_ref[...])
    merged = (jax.nn.sigmoid(ga_ref[...].astype(F32)) * ya
              + jax.nn.sigmoid(gb_ref[...].astype(F32)) * yb).astype(BF16)
    gate = mod_ref[0, base + 2:base + 3, :]
    o_ref[...] = x_ref[...] + gate * _dot(merged, wo_ref[...])


def _merge(x2d, mod3, attn_o, gla_o, proj, wa, wb, wo, *, base, seq):
    t, d = x2d.shape
    tiles_per_batch = seq // MRG_TM
    row = lambda i: (i, 0)
    resident = lambda a: pl.BlockSpec(a.shape, lambda i: (0, 0), pipeline_mode=pl.Buffered(1))
    return pl.pallas_call(
        functools.partial(_merge_kernel, base=base),
        out_shape=jax.ShapeDtypeStruct((t, d), F32),
        grid=(t // MRG_TM,),
        in_specs=[
            pl.BlockSpec((MRG_TM, d), row),
            pl.BlockSpec((1, N_MOD, d), lambda i: (i // tiles_per_batch, 0, 0)),
            pl.BlockSpec((MRG_TM, attn_o.shape[1]), row),
            pl.BlockSpec((MRG_TM, gla_o.shape[1]), row),
            pl.BlockSpec((MRG_TM, d), lambda i: (i, COL_GATE_A // d)),
            pl.BlockSpec((MRG_TM, d), lambda i: (i, COL_GATE_A // d + 1)),
            resident(wa), resident(wb), resident(wo),
        ],
        out_specs=pl.BlockSpec((MRG_TM, d), row),
        compiler_params=_params(("parallel",)),
        name="merge",
    )(x2d, mod3, attn_o, gla_o, proj, proj, wa, wb, wo)


def _layer(x2d, mod3, positions, p, *, batch, seq):
    d = x2d.shape[1]
    inv_freq = ROPE_THETA ** (-jnp.arange(0, MLA_ROPE, 2, dtype=F32) / MLA_ROPE)
    invf = inv_freq.reshape(ROPE_HALF, 1)
    sgn = jnp.tile(jnp.concatenate([-jnp.ones(ROPE_HALF, F32), jnp.ones(ROPE_HALF, F32)]),
                   LANES // MLA_ROPE).reshape(1, LANES)
    n_rope = MLA_HEADS * MLA_ROPE
    lane = jnp.arange(n_rope)
    grp = (lane[:, None] // MLA_ROPE == lane[None, :] // MLA_ROPE).astype(BF16)

    assert IN_SPLITS[-1] == d and sum(IN_SPLITS) == p["w_in"].shape[1]
    w_main, w_tail, cos, sin = _relayout_w_in(p["w_in"].T, positions, invf, sgn)
    wuq = p["w_uq"].reshape(MLA_Q_RANK, MLA_HEADS, MLA_QK)
    wuq = jnp.concatenate([wuq[:, :, :MLA_NOPE].reshape(MLA_Q_RANK, -1),
                           wuq[:, :, MLA_NOPE:].reshape(MLA_Q_RANK, -1)], axis=1).astype(BF16)
    wukv = p["w_ukv"].reshape(MLA_KV_RANK, MLA_HEADS, MLA_NOPE + MLA_V)
    wukv = jnp.concatenate([wukv[:, :, :MLA_NOPE].reshape(MLA_KV_RANK, -1),
                            wukv[:, :, MLA_NOPE:].reshape(MLA_KV_RANK, -1)], axis=1).astype(BF16)
    wg_pad = jnp.zeros((TAIL_W, GLA_KEY), F32).at[TAIL_GLR:TAIL_GLR + GLA_GATE_RANK].set(p["w_gk_up"])

    row = lambda a: a.reshape(1, -1)
    q_scale = MLA_QK ** -0.5 * LOG2E
    ffn1 = dict(base=0, final_norm=False, seq=seq)
    y0, w1b, w3b, w2b = _ffn_head(x2d, mod3, row(p["g_ffn1"]), p["w1_a"][None], p["w3_a"][None],
                                  p["w2_a"][None], row(p["g_final"]), **ffn1)
    x2d = _ffn(x2d, mod3, row(p["g_ffn1"]), w1b, w3b, w2b, row(p["g_final"]), y0, **ffn1)
    proj, tail, w1b, w3b = _inproj(x2d, mod3, row(p["g_mix"]), w_main, w_tail, p["w1_b"][None],
                                   p["w3_b"][None], base=3, seq=seq)
    q, k, v = _mla_prep(
        proj, tail, cos, sin, grp,
        (row(p["g_q_lat"]), wuq, row(p["g_qn"] * q_scale),
         row(jnp.tile(p["g_qr"], MLA_HEADS) * q_scale),
         row(p["g_kv_lat"]), wukv, row(p["g_kn"]), row(p["g_kr"])),
        batch=batch, seq=seq)
    attn_o, wa, wb, wo, w2b = _mla_attn(
        q, k, v, (p["w_proj_a"][None], p["w_proj_b"][None], p["w_out"][None], p["w2_b"][None]))
    attn_o = attn_o.reshape(batch * seq, MLA_HEADS * MLA_V)
    gla_o = _gla(proj, tail, wg_pad, row(p["b_gk"]), row(p["g_gla"]), batch=batch, seq=seq)
    x2d = _merge(x2d, mod3, attn_o, gla_o, proj, wa, wb, wo, base=3, seq=seq)
    x2d = _ffn(x2d, mod3, row(p["g_ffn2"]), w1b, w3b, w2b, row(p["g_final"]), None,
               base=6, final_norm=True, seq=seq)
    return x2d


def kernel(x, c, positions, w_ada, b_ada, g_ffn1, w1_a, w3_a, w2_a, g_mix, w_in, g_q_lat, w_uq, g_qn, g_qr, g_kv_lat, w_ukv, g_kn, g_kr, w_gk_up, b_gk, g_gla, w_proj_a, w_proj_b, w_out, g_ffn2, w1_b, w3_b, w2_b, g_final):
    batch, seq, d = x.shape
    depth = w_ada.shape[0]
    assert depth == 1, "the final norm is fused into the last FFN of a single layer"
    names = ("g_ffn1", "w1_a", "w3_a", "w2_a", "g_mix", "w_in", "g_q_lat", "w_uq", "g_qn", "g_qr",
             "g_kv_lat", "w_ukv", "g_kn", "g_kr", "w_gk_up", "b_gk", "g_gla", "w_proj_a", "w_proj_b",
             "w_out", "g_ffn2", "w1_b", "w3_b", "w2_b", "g_final")
    stacked = (g_ffn1, w1_a, w3_a, w2_a, g_mix, w_in, g_q_lat, w_uq, g_qn, g_qr, g_kv_lat, w_ukv,
               g_kn, g_kr, w_gk_up, b_gk, g_gla, w_proj_a, w_proj_b, w_out, g_ffn2, w1_b, w3_b, w2_b,
               g_final)
    x2d = x.reshape(batch * seq, d)
    p = {n: a[0] for n, a in zip(names, stacked)}
    mod3 = _adaln(c, w_ada[0], b_ada[0]).reshape(batch, N_MOD, d)
    x2d = _layer(x2d, mod3, positions, p, batch=batch, seq=seq)
    return x2d.reshape(batch, seq, d)
```

```python
import functools

import jax
import jax.numpy as jnp
from jax import lax
from jax.experimental import pallas as pl
from jax.experimental.pallas import tpu as pltpu

F32 = jnp.float32
BF16 = jnp.bfloat16

EPS = 1e-6
LOG2E = 1.4426950408889634
CHUNK = 64
MLA_HEADS = 8
MLA_Q_RANK = 512
MLA_KV_RANK = 512
MLA_NOPE = 128
MLA_ROPE = 64
MLA_V = 128
MLA_QK = MLA_NOPE + MLA_ROPE
ROPE_THETA = 10000.0
GLA_HEADS = 4
GLA_DK = 128
GLA_DV = 256
GLA_GATE_RANK = 16
GLA_GATE_NORMALIZER = 16.0
GLA_KEY = GLA_HEADS * GLA_DK
GLA_VAL = GLA_HEADS * GLA_DV
N_MOD = 9

LANES = 128
SUBLANES = 8
VMEM_LIMIT = 56 * 1024 * 1024

COL_QLAT = 0
COL_KVLAT = COL_QLAT + MLA_Q_RANK
COL_GQ = COL_KVLAT + MLA_KV_RANK
COL_GK = COL_GQ + GLA_KEY
COL_GV = COL_GK + GLA_KEY
COL_GOUT = COL_GV + GLA_VAL
COL_GATE_A = COL_GOUT + GLA_VAL
TAIL_W = LANES
TAIL_KROPE = 0
TAIL_GLR = MLA_ROPE


def _params(sem, vmem_limit=VMEM_LIMIT):
    return pltpu.CompilerParams(dimension_semantics=sem, vmem_limit_bytes=vmem_limit)


def _dot(a, b):
    return jnp.dot(a, b, preferred_element_type=F32)


def _dot_t0(a, b):
    return lax.dot_general(a, b, (((0,), (0,)), ((), ())), preferred_element_type=F32)


def _dot_t1(a, b):
    return lax.dot_general(a, b, (((1,), (1,)), ((), ())), preferred_element_type=F32)


def _split_dot(x, w_bf16):
    hi = x.astype(BF16)
    lo = (x - hi.astype(F32)).astype(BF16)
    return _dot(hi, w_bf16) + _dot(lo, w_bf16)


def _silu(x):
    return x * jax.nn.sigmoid(x)


def _rms(x, g=None):
    ms = jnp.mean(x * x, axis=-1, keepdims=True)
    y = x * lax.rsqrt(ms + EPS)
    return y if g is None else y * g


ADA_TN = 2048
ADA_RC = 64


def _adaln_kernel(ct_ref, w_ref, b_ref, o_ref, s_ref):
    d, nb = ct_ref.shape
    tn = w_ref.shape[1]

    @pl.when(pl.program_id(0) == 0)
    def _():
        s = _silu(ct_ref[...])
        for b in range(nb):
            s_ref[b] = jnp.broadcast_to(s[:, b:b + 1], (d, LANES))

    def body(k, acc):
        rows = pl.ds(pl.multiple_of(k * ADA_RC, ADA_RC), ADA_RC)
        out = []
        for b in range(nb):
            sb = s_ref[b, rows, :]
            cols = []
            for col in range(tn // LANES):
                p = w_ref[rows, col * LANES:(col + 1) * LANES] * sb
                cols.append(p.reshape(ADA_RC // SUBLANES, SUBLANES, LANES).sum(axis=0))
            out.append(acc[b] + jnp.concatenate(cols, axis=-1))
        return tuple(out)

    init = tuple(jnp.zeros((SUBLANES, tn), F32) for _ in range(nb))
    acc = lax.fori_loop(0, d // ADA_RC, body, init, unroll=2)
    rows = [jnp.sum(a, axis=0, keepdims=True) for a in acc]
    o_ref[...] = jnp.concatenate(rows, axis=0) + b_ref[...]


def _adaln(c, w_ada, b_ada):
    nb, d = c.shape
    n = w_ada.shape[1]
    return pl.pallas_call(
        _adaln_kernel,
        out_shape=jax.ShapeDtypeStruct((nb, n), F32),
        grid=(n // ADA_TN,),
        in_specs=[
            pl.BlockSpec((d, nb), lambda j: (0, 0)),
            pl.BlockSpec((d, ADA_TN), lambda j: (0, j)),
            pl.BlockSpec((1, ADA_TN), lambda j: (0, j)),
        ],
        out_specs=pl.BlockSpec((nb, ADA_TN), lambda j: (0, j)),
        scratch_shapes=[pltpu.VMEM((nb, d, LANES), F32)],
        compiler_params=_params(("arbitrary",)),
        name="adaln",
    )(c.T, w_ada, b_ada.reshape(1, n))


def _mod_norm(x, g, mod_ref, base):
    shift = mod_ref[0, base:base + 1, :]
    scale = mod_ref[0, base + 1:base + 2, :]
    return _rms(x, g) * (1.0 + scale) + shift


FFN_TM = 512
FFN_TF = 512
FFN_HEAD_TILES = 2
FFN_HEAD_TF = 256
FFN_GROUP = 256
FFN_WSLOTS = 3
FFN_VMEM_LIMIT = 60 * 1024 * 1024
FFN_AHEAD_CHUNKS = 8
FFN_AHEAD_ROWS = FFN_TM // FFN_AHEAD_CHUNKS


def _swiglu_tile(h, w1g, w3g, w2t):
    upd = None
    for s in range(w1g.shape[0]):
        a = _dot(h, w1g[s])
        b = _dot(h, w3g[s])
        p = (_silu(a) * b).astype(BF16)
        part = _dot(p, w2t[s * FFN_GROUP:(s + 1) * FFN_GROUP, :])
        upd = part if upd is None else upd + part
    return upd


def _ffn_finish(x_ref, mod_ref, gf_ref, o_ref, *, base, final_norm):
    gate = mod_ref[0, base + 2:base + 3, :]
    y = x_ref[...] + 0.5 * gate * o_ref[...]
    if final_norm:
        y = _rms(y, gf_ref[...])
    o_ref[...] = y


def _ffn_head_kernel(x_ref, mod_ref, g_ref, w1_ref, w3_ref, w2_ref, gf_ref,
                     o_ref, w1b_ref, w3b_ref, w2b_ref, h_ref, *, base, final_norm):
    j = pl.program_id(1)

    @pl.when(j == 0)
    def _():
        h_ref[...] = _mod_norm(x_ref[...], g_ref[...], mod_ref, base).astype(BF16)
        o_ref[...] = jnp.zeros_like(o_ref)

    w1 = w1_ref[0].astype(BF16)
    w3 = w3_ref[0].astype(BF16)
    w2 = w2_ref[0].astype(BF16)
    w1b_ref[0] = w1
    w3b_ref[0] = w3
    w2b_ref[...] = w2
    o_ref[...] += _swiglu_tile(h_ref[...], w1[None], w3[None], w2)

    @pl.when(j == pl.num_programs(1) - 1)
    def _():
        _ffn_finish(x_ref, mod_ref, gf_ref, o_ref, base=base, final_norm=final_norm)


def _ffn_kernel(x_ref, xn_ref, mod_ref, modn_ref, g_ref, w1_hbm, w3_hbm, w2_hbm, gf_ref, *rest,
                base, final_norm, n_steps, head_tiles):
    if head_tiles:
        y0_ref, *rest = rest
    o_ref, h_ref, w1_buf, w3_buf, w2_buf, sem = rest
    i = pl.program_id(0)
    h_slot = i % 2
    first = (i - head_tiles) * n_steps
    groups = FFN_TF // FFN_GROUP
    ahead = FFN_WSLOTS - 1

    def copies(j, slot):
        jj = j % n_steps
        return (pltpu.make_async_copy(w1_hbm.at[pl.ds(jj * groups, groups)], w1_buf.at[slot], sem.at[slot, 0]),
                pltpu.make_async_copy(w3_hbm.at[pl.ds(jj * groups, groups)], w3_buf.at[slot], sem.at[slot, 1]),
                pltpu.make_async_copy(w2_hbm.at[pl.ds(jj * FFN_TF, FFN_TF)], w2_buf.at[slot], sem.at[slot, 2]))

    if head_tiles:
        @pl.when(i < head_tiles)
        def _():
            o_ref[...] = y0_ref[...]

    @pl.when(i >= head_tiles)
    def _():
        @pl.when(i == head_tiles)
        def _():
            for j in range(ahead):
                for c in copies(j, j):
                    c.start()
            h_ref[h_slot] = _mod_norm(x_ref[...], g_ref[...], mod_ref, base).astype(BF16)

        for j in range(n_steps):
            slot = (first + j) % FFN_WSLOTS
            for c in copies(j, slot):
                c.wait()
            if j < FFN_AHEAD_CHUNKS:
                rows = slice(j * FFN_AHEAD_ROWS, (j + 1) * FFN_AHEAD_ROWS)
                h_next = _mod_norm(xn_ref[rows, :], g_ref[...], modn_ref, base).astype(BF16)
            upd = _swiglu_tile(h_ref[h_slot], w1_buf.at[slot], w3_buf.at[slot], w2_buf[slot])
            for c in copies(j + ahead, (first + j + ahead) % FFN_WSLOTS):
                c.start()
            if j == 0:
                o_ref[...] = upd
            else:
                o_ref[...] += upd
            if j < FFN_AHEAD_CHUNKS:
                h_ref[1 - h_slot, rows, :] = h_next
        _ffn_finish(x_ref, mod_ref, gf_ref, o_ref, base=base, final_norm=final_norm)

        @pl.when(i == pl.num_programs(0) - 1)
        def _():
            for j in range(n_steps, n_steps + ahead):
                for c in copies(j, (first + j) % FFN_WSLOTS):
                    c.wait()


def _ffn_head(x2d, mod3, g, w1, w3, w2, g_final, *, base, final_norm, seq):
    t, d = x2d.shape
    dff = w1.shape[2]
    tiles_per_batch = seq // FFN_TM
    const = lambda i, j: (0, 0)
    kw = dict(base=base, final_norm=final_norm)
    tag = "_final" if final_norm else ""
    common = [
        pl.BlockSpec((1, N_MOD, d), lambda i, j: (i // tiles_per_batch, 0, 0)),
        pl.BlockSpec((1, d), const),
    ]
    hf = FFN_HEAD_TF
    head_rows = FFN_HEAD_TILES * FFN_TM
    assert head_rows <= seq
    assert hf == FFN_GROUP
    w13_shape = jax.ShapeDtypeStruct((dff // FFN_GROUP, d, FFN_GROUP), BF16)
    w13_spec = pl.BlockSpec((1, d, FFN_GROUP), lambda i, j: (j, 0, 0))
    return pl.pallas_call(
        functools.partial(_ffn_head_kernel, **kw),
        out_shape=(jax.ShapeDtypeStruct((head_rows, d), F32), w13_shape, w13_shape,
                   jax.ShapeDtypeStruct((dff, d), BF16)),
        grid=(1, dff // hf),
        in_specs=[pl.BlockSpec((head_rows, d), const, pipeline_mode=pl.Buffered(1))] + common + [
            pl.BlockSpec((1, d, hf), lambda i, j: (0, 0, j)),
            pl.BlockSpec((1, d, hf), lambda i, j: (0, 0, j)),
            pl.BlockSpec((1, hf, d), lambda i, j: (0, j, 0)),
            pl.BlockSpec((1, d), const),
        ],
        out_specs=(pl.BlockSpec((head_rows, d), const), w13_spec, w13_spec,
                   pl.BlockSpec((hf, d), lambda i, j: (j, 0))),
        scratch_shapes=[pltpu.VMEM((head_rows, d), BF16)],
        compiler_params=_params(("arbitrary", "arbitrary")),
        name="ffn_head" + tag,
    )(x2d, mod3, g, w1, w3, w2, g_final)


def _ffn(x2d, mod3, g, w1b, w3b, w2b, g_final, y0, *, base, final_norm, seq):
    t, d = x2d.shape
    dff = w2b.shape[0]
    tiles_per_batch = seq // FFN_TM
    kw = dict(base=base, final_norm=final_norm)
    tag = "_final" if final_norm else ""
    head_tiles = 0 if y0 is None else y0.shape[0] // FFN_TM
    n_tiles = t // FFN_TM
    assert dff // FFN_TF >= FFN_AHEAD_CHUNKS and n_tiles > head_tiles
    nxt = lambda i: jnp.minimum(i + 1, n_tiles - 1)
    groups = FFN_TF // FFN_GROUP
    hbm = pl.BlockSpec(memory_space=pl.ANY)
    head_specs, head_args = [], []
    if head_tiles:
        head_specs = [pl.BlockSpec((FFN_TM, d), lambda i: (jnp.minimum(i, head_tiles - 1), 0),
                                   pipeline_mode=pl.Buffered(1))]
        head_args = [y0]
    return pl.pallas_call(
        functools.partial(_ffn_kernel, n_steps=dff // FFN_TF, head_tiles=head_tiles, **kw),
        out_shape=jax.ShapeDtypeStruct((t, d), F32),
        grid=(n_tiles,),
        in_specs=[
            pl.BlockSpec((FFN_TM, d), lambda i: (i, 0)),
            pl.BlockSpec((FFN_TM, d), lambda i: (nxt(i), 0)),
            pl.BlockSpec((1, N_MOD, d), lambda i: (i // tiles_per_batch, 0, 0)),
            pl.BlockSpec((1, N_MOD, d), lambda i: (nxt(i) // tiles_per_batch, 0, 0)),
            pl.BlockSpec((1, d), lambda i: (0, 0)),
            hbm, hbm, hbm,
            pl.BlockSpec((1, d), lambda i: (0, 0)),
            *head_specs,
        ],
        out_specs=pl.BlockSpec((FFN_TM, d), lambda i: (i, 0)),
        scratch_shapes=[pltpu.VMEM((2, FFN_TM, d), BF16),
                        pltpu.VMEM((FFN_WSLOTS, groups, d, FFN_GROUP), BF16),
                        pltpu.VMEM((FFN_WSLOTS, groups, d, FFN_GROUP), BF16),
                        pltpu.VMEM((FFN_WSLOTS, FFN_TF, d), BF16),
                        pltpu.SemaphoreType.DMA((FFN_WSLOTS, 3))],
        compiler_params=_params(("arbitrary",), FFN_VMEM_LIMIT),
        name="ffn" + tag,
    )(x2d, x2d, mod3, mod3, g, w1b, w3b, w2b, g_final, *head_args)


IN_SPLITS = (MLA_Q_RANK, MLA_KV_RANK, MLA_ROPE, GLA_KEY, GLA_KEY, GLA_VAL, GLA_GATE_RANK, GLA_VAL,
             2048, 2048)
MAIN_ORDER = (0, 1, 3, 4, 5, 7, 8, 9)
TAIL_ORDER = (2, 6)
RELAYOUT_ROWS = 1024


def _segments(order):
    starts = [sum(IN_SPLITS[:i]) for i in range(len(IN_SPLITS))]
    segs, dst = [], 0
    for i in order:
        src, n = starts[i], IN_SPLITS[i]
        if segs and segs[-1][0] + segs[-1][2] == src:
            segs[-1] = (segs[-1][0], segs[-1][1], segs[-1][2] + n)
        else:
            segs.append((src, dst, n))
        dst += n
    return segs, dst


def _relayout_kernel(lo_ref, hi_ref, kr_ref, glr_ref, pos_ref, invf_ref, sgn_ref,
                     main_ref, tail_ref, cos_ref, sin_ref):
    ang_t = invf_ref[...] * pos_ref[0].astype(F32)
    lane_reps = LANES // ROPE_HALF
    cos_ref[...] = jnp.tile(jnp.cos(ang_t).T, (1, lane_reps))
    sin_ref[...] = jnp.tile(jnp.sin(ang_t).T, (1, lane_reps)) * sgn_ref[...]

    r = pl.program_id(0)
    rows = RELAYOUT_ROWS
    for src, dst, n in _segments(MAIN_ORDER)[0]:
        shift = src - dst

        @pl.when((r >= dst // rows) & (r < (dst + n) // rows))
        def _(shift=shift):
            main_ref[:rows - shift, :] = lo_ref[shift:, :].astype(BF16)
            if shift:
                main_ref[rows - shift:, :] = hi_ref[:shift, :].astype(BF16)

    tail_ref[:MLA_ROPE, :] = kr_ref[...].astype(BF16)
    tail_ref[MLA_ROPE:MLA_ROPE + GLA_GATE_RANK, :] = glr_ref[...].astype(BF16)
    pad = TAIL_W - MLA_ROPE - GLA_GATE_RANK
    tail_ref[MLA_ROPE + GLA_GATE_RANK:, :] = jnp.zeros((pad, tail_ref.shape[1]), BF16)


def _relayout_w_in(w_in_t, positions, invf, sgn):
    d_in, d = w_in_t.shape
    segs, n_main = _segments(MAIN_ORDER)
    rows = RELAYOUT_ROWS
    steps = n_main // rows
    assert all(dst % rows == 0 and n % rows == 0 and 0 <= src - dst <= LANES and (src - dst) % 16 == 0
               for src, dst, n in segs)
    (kr_src, _, _), (glr_src, _, _) = _segments(TAIL_ORDER)[0]
    assert kr_src % MLA_ROPE == 0 and glr_src % GLA_GATE_RANK == 0
    t = positions.size
    tok = t // steps
    assert tok * steps == t and tok % LANES == 0
    const = lambda a: pl.BlockSpec(a.shape, lambda r: (0, 0))
    return pl.pallas_call(
        _relayout_kernel,
        out_shape=(jax.ShapeDtypeStruct((n_main, d), BF16), jax.ShapeDtypeStruct((TAIL_W, d), BF16),
                   jax.ShapeDtypeStruct((t, LANES), F32), jax.ShapeDtypeStruct((t, LANES), F32)),
        grid=(steps,),
        in_specs=[pl.BlockSpec((rows, d), lambda r: (r, 0)),
                  pl.BlockSpec((LANES, d), lambda r: ((r + 1) * (rows // LANES), 0)),
                  pl.BlockSpec((MLA_ROPE, d), lambda r: (kr_src // MLA_ROPE, 0)),
                  pl.BlockSpec((GLA_GATE_RANK, d), lambda r: (glr_src // GLA_GATE_RANK, 0)),
                  pl.BlockSpec((1, 1, tok), lambda r: (r, 0, 0)),
                  const(invf), const(sgn)],
        out_specs=(pl.BlockSpec((rows, d), lambda r: (r, 0)),
                   pl.BlockSpec((TAIL_W, d), lambda r: (0, 0)),
                   pl.BlockSpec((tok, LANES), lambda r: (r, 0)),
                   pl.BlockSpec((tok, LANES), lambda r: (r, 0))),
        compiler_params=_params(("arbitrary",)),
        name="w_in_layout",
    )(w_in_t, w_in_t, w_in_t, w_in_t, positions.reshape(steps, 1, tok), invf, sgn)


INP_TM = 1024
INP_TN = 2048
INP_VMEM_LIMIT = 60 * 1024 * 1024


def _inproj_kernel(x_ref, mod_ref, g_ref, w_ref, wt_ref, w1f_ref, w3f_ref,
                   o_ref, tail_ref, w1b_ref, w3b_ref, h_ref, *, base):
    for src, dst in ((w1f_ref, w1b_ref), (w3f_ref, w3b_ref)):
        for grp in range(dst.shape[0]):
            dst[grp] = src[0, :, grp * FFN_GROUP:(grp + 1) * FFN_GROUP].astype(BF16)

    @pl.when(pl.program_id(1) == 0)
    def _():
        h = _mod_norm(x_ref[...], g_ref[...], mod_ref, base).astype(BF16)
        h_ref[...] = h
        tail_ref[...] = _dot_t1(h, wt_ref[...])

    o_ref[...] = _dot_t1(h_ref[...], w_ref[...]).astype(BF16)


def _inproj(x2d, mod3, g, w_main, w_tail, w1, w3, *, base, seq):
    t, d = x2d.shape
    n = w_main.shape[0]
    dff = w1.shape[2]
    tiles_per_batch = seq // INP_TM
    n_steps = n // INP_TN
    steps = (t // INP_TM) * n_steps
    step = lambda i, j: i * n_steps + j
    r13, n_groups = d // steps, dff // FFN_GROUP
    assert r13 * steps == d and r13 % 16 == 0
    w13_in = pl.BlockSpec((1, r13, dff), lambda i, j: (0, step(i, j), 0))
    w13_out = pl.BlockSpec((n_groups, r13, FFN_GROUP), lambda i, j: (0, step(i, j), 0))
    w13_shape = jax.ShapeDtypeStruct((n_groups, d, FFN_GROUP), BF16)
    return pl.pallas_call(
        functools.partial(_inproj_kernel, base=base),
        out_shape=(jax.ShapeDtypeStruct((t, n), BF16),
                   jax.ShapeDtypeStruct((t, TAIL_W), F32),
                   w13_shape, w13_shape),
        grid=(t // INP_TM, n_steps),
        in_specs=[
            pl.BlockSpec((INP_TM, d), lambda i, j: (i, 0)),
            pl.BlockSpec((1, N_MOD, d), lambda i, j: (i // tiles_per_batch, 0, 0)),
            pl.BlockSpec((1, d), lambda i, j: (0, 0)),
            pl.BlockSpec((INP_TN, d), lambda i, j: (j, 0)),
            pl.BlockSpec((TAIL_W, d), lambda i, j: (0, 0)),
            w13_in, w13_in,
        ],
        out_specs=(pl.BlockSpec((INP_TM, INP_TN), lambda i, j: (i, j)),
                   pl.BlockSpec((INP_TM, TAIL_W), lambda i, j: (i, 0)),
                   w13_out, w13_out),
        scratch_shapes=[pltpu.VMEM((INP_TM, d), BF16)],
        compiler_params=_params(("arbitrary", "arbitrary"), INP_VMEM_LIMIT),
        name="in_proj",
    )(x2d, mod3, g, w_main, w_tail, w1, w3)


PREP_TM = 1024
ROPE_HALF = MLA_ROPE // 2


def _rope_rotate(x, cos, sin_signed):
    width = x.shape[-1]
    lane = lax.broadcasted_iota(jnp.int32, x.shape, 1)
    first_half = (lane % MLA_ROPE) < ROPE_HALF
    partner = jnp.where(first_half,
                        pltpu.roll(x, width - ROPE_HALF, 1),
                        pltpu.roll(x, ROPE_HALF, 1))
    return x * cos + partner * sin_signed


def _prep_kernel(qlat_ref, kvlat_ref, tail_ref, cos_ref, sin_ref,
                 gql_ref, wuq_ref, gqn_ref, gqr_ref, gkvl_ref, wukv_ref, gkr_ref,
                 grp_ref, q_ref, k_ref, v_ref):
    cos = cos_ref[...]
    sin_signed = sin_ref[...]
    n_rope = MLA_HEADS * MLA_ROPE
    cos_q = jnp.concatenate([cos] * (n_rope // LANES), axis=-1)
    sin_q = jnp.concatenate([sin_signed] * (n_rope // LANES), axis=-1)

    cq = _rms(qlat_ref[...].astype(F32), gql_ref[...]).astype(BF16)
    ckv = _rms(kvlat_ref[...].astype(F32), gkvl_ref[...]).astype(BF16)
    n_nope = MLA_HEADS * MLA_NOPE
    pair = 2 * MLA_NOPE

    for hp in range(MLA_HEADS // 2):
        cols = slice(hp * pair, (hp + 1) * pair)
        qp = _dot(cq, wuq_ref[:, cols])
        kp = _dot(ckv, wukv_ref[:, cols])
        for hh in range(2):
            h = 2 * hp + hh
            sub = slice(hh * MLA_NOPE, (hh + 1) * MLA_NOPE)
            q_ref[0, h, :, :MLA_NOPE] = _rms(qp[:, sub], gqn_ref[...]).astype(BF16)
            k_ref[0, h, :, :MLA_NOPE] = _rms(kp[:, sub]).astype(BF16)
        v_ref[0, :, cols] = _dot(ckv, wukv_ref[:, n_nope + hp * pair:n_nope + (hp + 1) * pair]).astype(BF16)

    qr = _dot(cq, wuq_ref[:, n_nope:])
    ssq = _split_dot(qr * qr, grp_ref[...])
    qr = qr * lax.rsqrt(ssq * (1.0 / MLA_ROPE) + EPS) * gqr_ref[...]
    qr = _rope_rotate(qr, cos_q, sin_q).astype(BF16)
    kr = tail_ref[:, TAIL_KROPE:TAIL_KROPE + MLA_ROPE]
    kr = _rms(kr, gkr_ref[...])
    kr = _rope_rotate(kr, cos[:, :MLA_ROPE], sin_signed[:, :MLA_ROPE]).astype(BF16)
    for h in range(MLA_HEADS):
        q_ref[0, h, :, MLA_NOPE:] = qr[:, h * MLA_ROPE:(h + 1) * MLA_ROPE]
        k_ref[0, h, :, MLA_NOPE:] = kr


def _mla_prep(proj, tail, cos, sin, grp, weights, *, batch, seq):
    t = proj.shape[0]
    tiles_per_batch = seq // PREP_TM
    row = lambda i: (i, 0)
    const = lambda i: (0, 0)
    (gql, wuq, gqn, gqr, gkvl, wukv, gkr) = weights
    head_out = pl.BlockSpec((1, MLA_HEADS, PREP_TM, MLA_QK),
                            lambda i: (i // tiles_per_batch, 0, i % tiles_per_batch, 0))
    full = lambda a: pl.BlockSpec(a.shape, const)
    return pl.pallas_call(
        _prep_kernel,
        out_shape=(jax.ShapeDtypeStruct((batch, MLA_HEADS, seq, MLA_QK), BF16),
                   jax.ShapeDtypeStruct((batch, MLA_HEADS, seq, MLA_QK), BF16),
                   jax.ShapeDtypeStruct((batch, seq, MLA_HEADS * MLA_V), BF16)),
        grid=(t // PREP_TM,),
        in_specs=[
            pl.BlockSpec((PREP_TM, MLA_Q_RANK), lambda i: (i, COL_QLAT // MLA_Q_RANK)),
            pl.BlockSpec((PREP_TM, MLA_KV_RANK), lambda i: (i, COL_KVLAT // MLA_KV_RANK)),
            pl.BlockSpec((PREP_TM, TAIL_W), row),
            pl.BlockSpec((PREP_TM, LANES), row),
            pl.BlockSpec((PREP_TM, LANES), row),
            full(gql), full(wuq), full(gqn), full(gqr),
            full(gkvl), full(wukv), full(gkr),
            full(grp),
        ],
        out_specs=(head_out, head_out,
                   pl.BlockSpec((1, PREP_TM, MLA_HEADS * MLA_V),
                                lambda i: (i // tiles_per_batch, i % tiles_per_batch, 0))),
        compiler_params=_params(("parallel",)),
        name="mla_prep",
    )(proj, proj, tail, cos, sin, gql, wuq, gqn, gqr, gkvl, wukv, gkr, grp)


ATT_T = 1024
ATT_H = ATT_T // 2


def _attn_kernel(q_ref, k_ref, v_ref, *rest):
    n_cast = (len(rest) - 1) // 2
    o_ref = rest[n_cast]
    for src, dst in zip(rest[:n_cast], rest[n_cast + 1:]):
        dst[...] = src[0].astype(BF16)

    n_tiles = k_ref.shape[2] // ATT_T
    ones = jnp.ones((ATT_T, MLA_V), BF16)
    r = lax.broadcasted_iota(jnp.int32, (ATT_H, ATT_H), 0)
    c = lax.broadcasted_iota(jnp.int32, (ATT_H, ATT_H), 1)
    visible = (c // CHUNK) <= (r // CHUNK)
    state = {}

    def update(chain, s, v2):
        m_cur = jnp.max(s, axis=-1, keepdims=True)
        if chain not in state:
            m_next = jnp.broadcast_to(m_cur, (ATT_H, LANES))
        else:
            m_prev, l_prev, acc_prev = state[chain]
            m_next = jnp.maximum(m_prev, m_cur)
            alpha = jnp.exp2(m_prev - m_next)
        p = jnp.exp2(s - jnp.tile(m_next, (1, s.shape[1] // LANES)))
        pv = _dot(p.astype(BF16), v2)
        acc, l = pv[:, :MLA_V], pv[:, MLA_V:]
        if chain in state:
            acc, l = alpha * acc_prev + acc, alpha * l_prev + l
        state[chain] = (m_next, l, acc)

    for t in range(n_tiles):
        rows = slice(t * ATT_T, (t + 1) * ATT_T)
        k = k_ref[0, 0, rows, :]
        v2 = jnp.concatenate([v_ref[0, rows, :], ones], axis=-1)
        for qt in range(t, n_tiles):
            q0 = q_ref[0, 0, qt * ATT_T:qt * ATT_T + ATT_H, :]
            q1 = q_ref[0, 0, qt * ATT_T + ATT_H:(qt + 1) * ATT_T, :]
            if t < qt:
                update((qt, 0), _dot_t1(q0, k), v2)
                update((qt, 1), _dot_t1(q1, k), v2)
            else:
                update((qt, 0), jnp.where(visible, _dot_t1(q0, k[:ATT_H]), -jnp.inf), v2[:ATT_H])
                s1 = _dot_t1(q1, k)
                s1 = jnp.concatenate([s1[:, :ATT_H], jnp.where(visible, s1[:, ATT_H:], -jnp.inf)],
                                     axis=-1)
                update((qt, 1), s1, v2)
                for half in range(2):
                    _, l, acc = state[(qt, half)]
                    lo = qt * ATT_T + half * ATT_H
                    o_ref[0, lo:lo + ATT_H, :] = (acc / l).astype(BF16)


def _mla_attn(q, k, v, cast_weights):
    batch, heads, seq, _ = q.shape
    steps = batch * heads
    step = lambda b, h: b * heads + h
    cast_in, cast_out, cast_shapes = [], [], []
    for w in cast_weights:
        _, rows, cols = w.shape
        blk = rows // steps
        assert blk * steps == rows and blk % 16 == 0
        cast_in.append(pl.BlockSpec((1, blk, cols), lambda b, h: (0, step(b, h), 0)))
        cast_out.append(pl.BlockSpec((blk, cols), lambda b, h: (step(b, h), 0)))
        cast_shapes.append(jax.ShapeDtypeStruct((rows, cols), BF16))
    return pl.pallas_call(
        _attn_kernel,
        out_shape=(jax.ShapeDtypeStruct((batch, seq, heads * MLA_V), BF16), *cast_shapes),
        grid=(batch, heads),
        in_specs=[
            pl.BlockSpec((1, 1, seq, MLA_QK), lambda b, h: (b, h, 0, 0)),
            pl.BlockSpec((1, 1, seq, MLA_QK), lambda b, h: (b, h, 0, 0)),
            pl.BlockSpec((1, seq, MLA_V), lambda b, h: (b, 0, h)),
            *cast_in,
        ],
        out_specs=(pl.BlockSpec((1, seq, MLA_V), lambda b, h: (b, 0, h)), *cast_out),
        compiler_params=_params(("parallel", "parallel")),
        name="mla_attn",
    )(q, k, v, *cast_weights)


GLA_L = 1024
GLA_NC = GLA_L // CHUNK
GLA_SUB = 512


def _gla_head(log_a, q, k, v, go, gg, state_ref, causal, tri):
    sub_nc = GLA_SUB // CHUNK
    q_dec, k_end, o_intra, b_ends = [], [], [], []
    for sb in range(GLA_L // GLA_SUB):
        rows = slice(sb * GLA_SUB, (sb + 1) * GLA_SUB)
        la = log_a[rows]
        la_hi = la.astype(BF16)
        la_lo = (la - la_hi.astype(F32)).astype(BF16)
        cum = _dot(tri, jnp.concatenate([la_hi, la_lo], axis=-1))
        b_cum = cum[:, :GLA_DK] + cum[:, GLA_DK:]
        b3 = b_cum.reshape(sub_nc, CHUNK, GLA_DK)
        b_end = b3[:, CHUNK - 1:CHUNK, :]
        b_tot = jnp.broadcast_to(b_end, b3.shape).reshape(GLA_SUB, GLA_DK)
        b_ends.append(jnp.exp(b_end))

        qf = q[rows].astype(F32) * (GLA_DK ** -0.5)
        kf = k[rows].astype(F32)
        qd = (qf * jnp.exp(b_cum)).astype(BF16)
        kd = (kf * jnp.exp(-b_cum)).astype(BF16)
        q_dec.append(qd)
        k_end.append((kf * jnp.exp(b_tot - b_cum)).astype(BF16))
        attn = jnp.where(causal, _dot_t1(qd, kd), 0.0).astype(BF16)
        o_intra.append(_dot(attn, v[rows]))

    state = state_ref[...]
    outs = []
    for n in range(GLA_NC):
        sb, rows = n // sub_nc, slice((n % sub_nc) * CHUNK, (n % sub_nc + 1) * CHUNK)
        outs.append(o_intra[sb][rows] + _dot_t1(q_dec[sb][rows], state.astype(BF16)))
        v_n = v[n * CHUNK:(n + 1) * CHUNK]
        state = state * b_ends[sb][n % sub_nc] + _dot_t0(v_n, k_end[sb][rows])
    state_ref[...] = state

    o = jnp.concatenate(outs, axis=0)
    return (_rms(o, gg) * _silu(go.astype(F32))).astype(BF16)


def _gla_kernel(q_ref, k_ref, v_ref, go_ref, tail_ref, wg_ref, bg_ref, gg_ref,
                o_ref, state_ref):
    @pl.when(pl.program_id(1) == 0)
    def _():
        state_ref[...] = jnp.zeros_like(state_ref)

    wg = wg_ref[...]
    wg_hi = wg.astype(BF16)
    wg_lo = (wg - wg_hi.astype(F32)).astype(BF16)
    tl = tail_ref[...]
    tl_hi = tl.astype(BF16)
    tl_lo = (tl - tl_hi.astype(F32)).astype(BF16)
    z = _dot(tl_hi, wg_hi) + _dot(tl_lo, wg_hi) + _dot(tl_hi, wg_lo) + bg_ref[...]
    log_a = (jnp.minimum(z, 0.0) - jnp.log(1.0 + jnp.exp(-jnp.abs(z)))) * (1.0 / GLA_GATE_NORMALIZER)

    r = lax.broadcasted_iota(jnp.int32, (GLA_SUB, GLA_SUB), 0)
    c = lax.broadcasted_iota(jnp.int32, (GLA_SUB, GLA_SUB), 1)
    causal = ((r // CHUNK) == (c // CHUNK)) & (c <= r)
    tri = causal.astype(BF16)

    for hh in range(GLA_HEADS):
        dk = slice(hh * GLA_DK, (hh + 1) * GLA_DK)
        dv = slice(hh * GLA_DV, (hh + 1) * GLA_DV)
        o_ref[:, dv] = _gla_head(log_a[:, dk], q_ref[:, dk], k_ref[:, dk], v_ref[:, dv],
                                 go_ref[:, dv], gg_ref[...], state_ref.at[hh], causal, tri)


def _gla(proj, tail, wg_pad, b_gk, g_gla, *, batch, seq):
    t = proj.shape[0]
    steps = seq // GLA_L
    tok = lambda b, i: b * steps + i
    return pl.pallas_call(
        _gla_kernel,
        out_shape=jax.ShapeDtypeStruct((t, GLA_VAL), BF16),
        grid=(batch, steps),
        in_specs=[
            pl.BlockSpec((GLA_L, GLA_KEY), lambda b, i: (tok(b, i), COL_GQ // GLA_KEY)),
            pl.BlockSpec((GLA_L, GLA_KEY), lambda b, i: (tok(b, i), COL_GK // GLA_KEY)),
            pl.BlockSpec((GLA_L, GLA_VAL), lambda b, i: (tok(b, i), COL_GV // GLA_VAL)),
            pl.BlockSpec((GLA_L, GLA_VAL), lambda b, i: (tok(b, i), COL_GOUT // GLA_VAL)),
            pl.BlockSpec((GLA_L, TAIL_W), lambda b, i: (tok(b, i), 0)),
            pl.BlockSpec((TAIL_W, GLA_KEY), lambda b, i: (0, 0)),
            pl.BlockSpec((1, GLA_KEY), lambda b, i: (0, 0)),
            pl.BlockSpec((1, GLA_DV), lambda b, i: (0, 0)),
        ],
        out_specs=pl.BlockSpec((GLA_L, GLA_VAL), lambda b, i: (tok(b, i), 0)),
        scratch_shapes=[pltpu.VMEM((GLA_HEADS, GLA_DV, GLA_DK), F32)],
        compiler_params=_params(("parallel", "arbitrary")),
        name="gla",
    )(proj, proj, proj, proj, tail, wg_pad, b_gk, g_gla)


MRG_TM = 512


def _merge_kernel(x_ref, mod_ref, a_ref, b_ref, ga_ref, gb_ref, wa_ref, wb_ref, wo_ref, o_ref,
                  *, base):
    ya = _dot(a_ref[...], wa_ref[...])
    yb = _dot(b_ref[...], wb_ref[...])
    merged = (jax.nn.sigmoid(ga_ref[...].astype(F32)) * ya
              + jax.nn.sigmoid(gb_ref[...].astype(F32)) * yb).astype(BF16)
    gate = mod_ref[0, base + 2:base + 3, :]
    o_ref[...] = x_ref[...] + gate * _dot(merged, wo_ref[...])


def _merge(x2d, mod3, attn_o, gla_o, proj, wa, wb, wo, *, base, seq):
    t, d = x2d.shape
    tiles_per_batch = seq // MRG_TM
    row = lambda i: (i, 0)
    resident = lambda a: pl.BlockSpec(a.shape, lambda i: (0, 0), pipeline_mode=pl.Buffered(1))
    return pl.pallas_call(
        functools.partial(_merge_kernel, base=base),
        out_shape=jax.ShapeDtypeStruct((t, d), F32),
        grid=(t // MRG_TM,),
        in_specs=[
            pl.BlockSpec((MRG_TM, d), row),
            pl.BlockSpec((1, N_MOD, d), lambda i: (i // tiles_per_batch, 0, 0)),
            pl.BlockSpec((MRG_TM, attn_o.shape[1]), row),
            pl.BlockSpec((MRG_TM, gla_o.shape[1]), row),
            pl.BlockSpec((MRG_TM, d), lambda i: (i, COL_GATE_A // d)),
            pl.BlockSpec((MRG_TM, d), lambda i: (i, COL_GATE_A // d + 1)),
            resident(wa), resident(wb), resident(wo),
        ],
        out_specs=pl.BlockSpec((MRG_TM, d), row),
        compiler_params=_params(("parallel",)),
        name="merge",
    )(x2d, mod3, attn_o, gla_o, proj, proj, wa, wb, wo)


def _layer(x2d, mod3, positions, p, *, batch, seq):
    d = x2d.shape[1]
    inv_freq = ROPE_THETA ** (-jnp.arange(0, MLA_ROPE, 2, dtype=F32) / MLA_ROPE)
    invf = inv_freq.reshape(ROPE_HALF, 1)
    sgn = jnp.tile(jnp.concatenate([-jnp.ones(ROPE_HALF, F32), jnp.ones(ROPE_HALF, F32)]),
                   LANES // MLA_ROPE).reshape(1, LANES)
    n_rope = MLA_HEADS * MLA_ROPE
    lane = jnp.arange(n_rope)
    grp = (lane[:, None] // MLA_ROPE == lane[None, :] // MLA_ROPE).astype(BF16)

    assert IN_SPLITS[-1] == d and sum(IN_SPLITS) == p["w_in"].shape[1]
    w_main, w_tail, cos, sin = _relayout_w_in(p["w_in"].T, positions, invf, sgn)
    wuq = p["w_uq"].reshape(MLA_Q_RANK, MLA_HEADS, MLA_QK)
    wuq = jnp.concatenate([wuq[:, :, :MLA_NOPE].reshape(MLA_Q_RANK, -1),
                           wuq[:, :, MLA_NOPE:].reshape(MLA_Q_RANK, -1)], axis=1).astype(BF16)
    wukv = p["w_ukv"].reshape(MLA_KV_RANK, MLA_HEADS, MLA_NOPE + MLA_V)
    wukv = jnp.concatenate([wukv[:, :, :MLA_NOPE].reshape(MLA_KV_RANK, -1),
                            wukv[:, :, MLA_NOPE:].reshape(MLA_KV_RANK, -1)], axis=1).astype(BF16)
    wg_pad = jnp.zeros((TAIL_W, GLA_KEY), F32).at[TAIL_GLR:TAIL_GLR + GLA_GATE_RANK].set(p["w_gk_up"])

    row = lambda a: a.reshape(1, -1)
    q_scale = MLA_QK ** -0.5 * LOG2E
    ffn1 = dict(base=0, final_norm=False, seq=seq)
    y0, w1b, w3b, w2b = _ffn_head(x2d, mod3, row(p["g_ffn1"]), p["w1_a"][None], p["w3_a"][None],
                                  p["w2_a"][None], row(p["g_final"]), **ffn1)
    x2d = _ffn(x2d, mod3, row(p["g_ffn1"]), w1b, w3b, w2b, row(p["g_final"]), y0, **ffn1)
    proj, tail, w1b, w3b = _inproj(x2d, mod3, row(p["g_mix"]), w_main, w_tail, p["w1_b"][None],
                                   p["w3_b"][None], base=3, seq=seq)
    q, k, v = _mla_prep(
        proj, tail, cos, sin, grp,
        (row(p["g_q_lat"]), wuq, row(p["g_qn"] * p["g_kn"] * q_scale),
         row(jnp.tile(p["g_qr"], MLA_HEADS) * q_scale),
         row(p["g_kv_lat"]), wukv, row(p["g_kr"])),
        batch=batch, seq=seq)
    attn_o, wa, wb, wo, w2b = _mla_attn(
        q, k, v, (p["w_proj_a"][None], p["w_proj_b"][None], p["w_out"][None], p["w2_b"][None]))
    attn_o = attn_o.reshape(batch * seq, MLA_HEADS * MLA_V)
    gla_o = _gla(proj, tail, wg_pad, row(p["b_gk"]), row(p["g_gla"]), batch=batch, seq=seq)
    x2d = _merge(x2d, mod3, attn_o, gla_o, proj, wa, wb, wo, base=3, seq=seq)
    x2d = _ffn(x2d, mod3, row(p["g_ffn2"]), w1b, w3b, w2b, row(p["g_final"]), None,
               base=6, final_norm=True, seq=seq)
    return x2d


def kernel(x, c, positions, w_ada, b_ada, g_ffn1, w1_a, w3_a, w2_a, g_mix, w_in, g_q_lat, w_uq, g_qn, g_qr, g_kv_lat, w_ukv, g_kn, g_kr, w_gk_up, b_gk, g_gla, w_proj_a, w_proj_b, w_out, g_ffn2, w1_b, w3_b, w2_b, g_final):
    batch, seq, d = x.shape
    depth = w_ada.shape[0]
    assert depth == 1, "the final norm is fused into the last FFN of a single layer"
    names = ("g_ffn1", "w1_a", "w3_a", "w2_a", "g_mix", "w_in", "g_q_lat", "w_uq", "g_qn", "g_qr",
             "g_kv_lat", "w_ukv", "g_kn", "g_kr", "w_gk_up", "b_gk", "g_gla", "w_proj_a", "w_proj_b",
             "w_out", "g_ffn2", "w1_b", "w3_b", "w2_b", "g_final")
    stacked = (g_ffn1, w1_a, w3_a, w2_a, g_mix, w_in, g_q_lat, w_uq, g_qn, g_qr, g_kv_lat, w_ukv,
               g_kn, g_kr, w_gk_up, b_gk, g_gla, w_proj_a, w_proj_b, w_out, g_ffn2, w1_b, w3_b, w2_b,
               g_final)
    x2d = x.reshape(batch * seq, d)
    p = {n: a[0] for n, a in zip(names, stacked)}
    mod3 = _adaln(c, w_ada[0], b_ada[0]).reshape(batch, N_MOD, d)
    x2d = _layer(x2d, mod3, positions, p, batch=batch, seq=seq)
    return x2d.reshape(batch, seq, d)
```
